```python
import math
import jax, jax.numpy as jnp
from jax import lax
import numpy as np

D_MODEL = 1024
BATCH = 8
SEQ = 2048
DEPTH = 2

S5_WIDTH = D_MODEL // 2
S5_GROUP = 16
S5_GROUPS = S5_WIDTH // S5_GROUP
S5_STATE = 64
S5_DT_MIN = 0.001
S5_DT_MAX = 0.1
S5_DIRS = 2
HY_WIDTH = D_MODEL
HY_ORDER = 2
HY_SHORT = 3
HY_BANDS = 16
HY_EMB = 1 + 2 * HY_BANDS
HY_FFN = 64
HY_DIRS = 2
HY_FILTERS = HY_ORDER * HY_DIRS * HY_WIDTH
HY_DECAY_TARGET = 1e-2
HY_FAST_PCT = 0.3
HY_SLOW_PCT = 1.5
HY_SHIFT = 0.05
HY_EPS = 1e-6
N_BRANCH = 2
RMS_EPS = 1e-6
IN_COLS = 2 * S5_WIDTH + (HY_ORDER + 1) * HY_WIDTH + HY_WIDTH + N_BRANCH * D_MODEL
SPLITS = [S5_WIDTH, 2 * S5_WIDTH, 2 * S5_WIDTH + (HY_ORDER + 1) * HY_WIDTH,
          2 * S5_WIDTH + (HY_ORDER + 2) * HY_WIDTH]

kernel_name = "hybrid_s5_hyena_gated_encoder"


def rmsnorm(x, w):
    xf = x.astype(jnp.float32)
    y = xf * lax.rsqrt(jnp.mean(xf * xf, axis=-1, keepdims=True) + RMS_EPS)
    return (y * w.astype(jnp.float32)).astype(x.dtype)


def _cmul(ar, ai, br, bi):
    return ar * br - ai * bi, ar * bi + ai * br


def _s5_combine(e1, e2):
    a1r, a1i, b1r, b1i = e1
    a2r, a2i, b2r, b2i = e2
    ar, ai = _cmul(a1r, a1i, a2r, a2i)
    tr, ti = _cmul(a2r, a2i, b1r, b1i)
    return ar, ai, tr + b2r, ti + b2i


def s5_direction(u, lam_re, lam_im, log_step, b_re, b_im, c_re, c_im, reverse):
    dt = jnp.exp(log_step)[:, None]
    mag = jnp.exp(lam_re * dt)
    ang = lam_im * dt
    lbar_re, lbar_im = mag * jnp.cos(ang), mag * jnp.sin(ang)
    nr, ni = lbar_re - 1.0, lbar_im
    den = lam_re * lam_re + lam_im * lam_im
    kr = (nr * lam_re + ni * lam_im) / den
    ki = (ni * lam_re - nr * lam_im) / den
    bb_re, bb_im = _cmul(kr[..., None], ki[..., None], b_re, b_im)
    bu_re = jnp.einsum('blgh,gph->blgp', u, bb_re)
    bu_im = jnp.einsum('blgh,gph->blgp', u, bb_im)
    L = u.shape[1]
    a_re = jnp.broadcast_to(lbar_re, (1, L) + lbar_re.shape)
    a_im = jnp.broadcast_to(lbar_im, (1, L) + lbar_im.shape)
    _, _, s_re, s_im = lax.associative_scan(_s5_combine, (a_re, a_im, bu_re, bu_im),
                                            reverse=reverse, axis=1)
    return (jnp.einsum('ghp,blgp->blgh', c_re, s_re)
            - jnp.einsum('ghp,blgp->blgh', c_im, s_im))


def s5_mixer(u, gate, lam_re, lam_im, log_step, b_re, b_im, c_re, c_im, d, w_glu, b_glu):
    bsz, L, _ = u.shape
    f32 = jnp.float32
    uf = u.astype(f32).reshape(bsz, L, S5_GROUPS, S5_GROUP)
    y = uf * d.astype(f32).reshape(S5_GROUPS, S5_GROUP)
    for direction, rev in enumerate((False, True)):
        y = y + s5_direction(uf, lam_re[direction].astype(f32), lam_im[direction].astype(f32),
                             log_step[direction].astype(f32), b_re[direction].astype(f32),
                             b_im[direction].astype(f32), c_re[direction].astype(f32),
                             c_im[direction].astype(f32), rev)
    y = y.reshape(bsz, L, S5_WIDTH).astype(u.dtype)
    y = jax.nn.gelu(y)
    y = y * jax.nn.sigmoid(y @ w_glu + b_glu)
    return y * jax.nn.silu(gate)


def hyena_filters(L, w1, b1, freq, w2, b2, w3, decay):
    f32 = jnp.float32
    t = jnp.linspace(0.0, 1.0, L, dtype=f32)[:, None]
    pos = jnp.arange(L, dtype=f32)[:, None]
    bands = jnp.linspace(1e-4, HY_BANDS - 1, HY_BANDS, dtype=f32)[None, :]
    ang = bands * pos * (2.0 * math.pi / L)
    feats = jnp.concatenate([t, jnp.cos(ang), -jnp.sin(ang)], axis=-1)
    fr = freq.astype(f32)
    h = jnp.sin(fr * (feats @ w1.astype(f32) + b1.astype(f32)))
    h = jnp.sin(fr * (h @ w2.astype(f32) + b2.astype(f32)))
    h = h @ w3.astype(f32)
    h = h * (jnp.exp(-t * jnp.abs(decay.astype(f32))) + HY_SHIFT)
    h = h.reshape(L, HY_ORDER, HY_DIRS, HY_WIDTH)
    h_fwd, h_bwd = h[:, :, 0], h[:, :, 1]
    k = jnp.concatenate([h_fwd, jnp.zeros_like(h_fwd[:1]), jnp.flip(h_bwd[1:], axis=0)], axis=0)
    k = k * lax.rsqrt(jnp.sum(k * k, axis=0, keepdims=True) + HY_EPS)
    return jnp.fft.rfft(k, axis=0)


def hyena_mixer(proj, gate, conv_w, conv_b, w1, b1, freq, w2, b2, w3, decay, d):
    f32 = jnp.float32
    bsz, L, _ = proj.shape
    pad = HY_SHORT // 2
    pp = jnp.pad(proj, ((0, 0), (pad, pad), (0, 0)))
    sc = conv_b
    for j in range(HY_SHORT):
        sc = sc + pp[:, j:j + L] * conv_w[j]
    v, x1, x2 = jnp.split(sc, HY_ORDER + 1, axis=-1)
    kf = hyena_filters(L, w1, b1, freq, w2, b2, w3, decay)
    z = v.astype(f32)
    for o, g in enumerate((x1, x2)):
        zf = jnp.fft.rfft(z, n=2 * L, axis=1)
        conv = jnp.fft.irfft(zf * kf[:, o][None], n=2 * L, axis=1)[:, :L]
        z = g.astype(f32) * (conv + d[o].astype(f32) * z)
    return z.astype(proj.dtype) * jax.nn.silu(gate)


def setup_inputs(seed: int = 0) -> dict:
    key = jax.random.key(seed)
    ks = jax.random.split(key, 32)
    f32 = jnp.float32
    nrm = lambda k, shape, s: jax.random.normal(k, shape, f32) * s
    x = nrm(ks[0], (BATCH, SEQ, D_MODEL), 1.0)
    norm_w = 1.0 + nrm(ks[1], (DEPTH, D_MODEL), 0.02)
    w_in = nrm(ks[2], (DEPTH, D_MODEL, IN_COLS), D_MODEL ** -0.5)
    n_idx = jnp.arange(S5_STATE, dtype=f32)
    s5_lam_re = -0.5 + nrm(ks[3], (DEPTH, S5_DIRS, S5_GROUPS, S5_STATE), 0.01)
    s5_lam_im = math.pi * n_idx + nrm(ks[4], (DEPTH, S5_DIRS, S5_GROUPS, S5_STATE), 0.01)
    s5_log_step = (math.log(S5_DT_MIN) + jax.random.uniform(ks[5], (DEPTH, S5_DIRS, S5_GROUPS), f32)
                   * (math.log(S5_DT_MAX) - math.log(S5_DT_MIN)))
    s5_b_re = nrm(ks[6], (DEPTH, S5_DIRS, S5_GROUPS, S5_STATE, S5_GROUP), (2 * S5_GROUP) ** -0.5)
    s5_b_im = nrm(ks[7], (DEPTH, S5_DIRS, S5_GROUPS, S5_STATE, S5_GROUP), (2 * S5_GROUP) ** -0.5)
    s5_c_re = nrm(ks[8], (DEPTH, S5_DIRS, S5_GROUPS, S5_GROUP, S5_STATE), S5_STATE ** -0.5)
    s5_c_im = nrm(ks[9], (DEPTH, S5_DIRS, S5_GROUPS, S5_GROUP, S5_STATE), S5_STATE ** -0.5)
    s5_d = nrm(ks[10], (DEPTH, S5_WIDTH), 1.0)
    s5_w_glu = nrm(ks[11], (DEPTH, S5_WIDTH, S5_WIDTH), S5_WIDTH ** -0.5)
    s5_b_glu = nrm(ks[12], (DEPTH, S5_WIDTH), 0.01)
    hy_conv_w = nrm(ks[13], (DEPTH, HY_SHORT, (HY_ORDER + 1) * HY_WIDTH), HY_SHORT ** -0.5)
    hy_conv_b = nrm(ks[14], (DEPTH, (HY_ORDER + 1) * HY_WIDTH), 0.01)
    hy_w1 = nrm(ks[15], (DEPTH, HY_EMB, HY_FFN), HY_EMB ** -0.5)
    hy_b1 = nrm(ks[16], (DEPTH, HY_FFN), 0.1)
    hy_freq = 1.0 + nrm(ks[17], (DEPTH, HY_FFN), 0.02)
    hy_w2 = nrm(ks[18], (DEPTH, HY_FFN, HY_FFN), HY_FFN ** -0.5)
    hy_b2 = nrm(ks[19], (DEPTH, HY_FFN), 0.1)
    hy_w3 = nrm(ks[20], (DEPTH, HY_FFN, HY_FILTERS), HY_FFN ** -0.5)
    min_decay = math.log(HY_DECAY_TARGET) / HY_SLOW_PCT
    max_decay = math.log(HY_DECAY_TARGET) / HY_FAST_PCT
    base_decay = jnp.tile(jnp.linspace(min_decay, max_decay, HY_WIDTH, dtype=f32), HY_ORDER * HY_DIRS)
    hy_decay = base_decay[None, :] + nrm(ks[21], (DEPTH, HY_FILTERS), 0.01)
    hy_d = nrm(ks[22], (DEPTH, HY_ORDER, HY_WIDTH), 0.5)
    w_branch_s5 = nrm(ks[23], (DEPTH, S5_WIDTH, D_MODEL), S5_WIDTH ** -0.5)
    w_branch_hy = nrm(ks[24], (DEPTH, HY_WIDTH, D_MODEL), HY_WIDTH ** -0.5)
    w_out = nrm(ks[25], (DEPTH, D_MODEL, D_MODEL), D_MODEL ** -0.5)
    final_norm_w = 1.0 + nrm(ks[26], (D_MODEL,), 0.02)
    return {"x": x, "norm_w": norm_w, "w_in": w_in,
            "s5_lam_re": s5_lam_re, "s5_lam_im": s5_lam_im, "s5_log_step": s5_log_step,
            "s5_b_re": s5_b_re, "s5_b_im": s5_b_im, "s5_c_re": s5_c_re, "s5_c_im": s5_c_im,
            "s5_d": s5_d, "s5_w_glu": s5_w_glu, "s5_b_glu": s5_b_glu,
            "hy_conv_w": hy_conv_w, "hy_conv_b": hy_conv_b, "hy_w1": hy_w1, "hy_b1": hy_b1,
            "hy_freq": hy_freq, "hy_w2": hy_w2, "hy_b2": hy_b2, "hy_w3": hy_w3,
            "hy_decay": hy_decay, "hy_d": hy_d,
            "w_branch_s5": w_branch_s5, "w_branch_hy": w_branch_hy, "w_out": w_out,
            "final_norm_w": final_norm_w}


def reference(x, norm_w, w_in, s5_lam_re, s5_lam_im, s5_log_step, s5_b_re, s5_b_im, s5_c_re, s5_c_im,
              s5_d, s5_w_glu, s5_b_glu, hy_conv_w, hy_conv_b, hy_w1, hy_b1, hy_freq, hy_w2, hy_b2,
              hy_w3, hy_decay, hy_d, w_branch_s5, w_branch_hy, w_out, final_norm_w):
    h = x
    for l in range(DEPTH):
        xn = rmsnorm(h, norm_w[l])
        proj = xn @ w_in[l]
        u_s5, g_s5, p_hy, g_hy, m_logits = jnp.split(proj, SPLITS, axis=-1)
        y_s5 = s5_mixer(u_s5, g_s5, s5_lam_re[l], s5_lam_im[l], s5_log_step[l], s5_b_re[l], s5_b_im[l],
                        s5_c_re[l], s5_c_im[l], s5_d[l], s5_w_glu[l], s5_b_glu[l]) @ w_branch_s5[l]
        y_hy = hyena_mixer(p_hy, g_hy, hy_conv_w[l], hy_conv_b[l], hy_w1[l], hy_b1[l], hy_freq[l],
                           hy_w2[l], hy_b2[l], hy_w3[l], hy_decay[l], hy_d[l]) @ w_branch_hy[l]
        m = jax.nn.sigmoid(m_logits)
        merged = m[..., :D_MODEL] * y_s5 + m[..., D_MODEL:] * y_hy
        h = h + merged @ w_out[l]
    return rmsnorm(h, final_norm_w)
```

```python
import functools
import math

import numpy as np
import jax
import jax.numpy as jnp
from jax import lax
from jax.experimental import pallas as pl
from jax.experimental.pallas import tpu as pltpu

F32 = jnp.float32
BF16 = jnp.bfloat16

RMS_EPS = 1e-6
S5_GROUP = 16
S5_STATE = 64
S5_CHUNK = 16
S5_GROUPS_PER_STEP = 2
HY_ORDER = 2
HY_BANDS = 16
HY_SHIFT = 0.05
HY_EPS = 1e-6
VMEM_LIMIT_BYTES = 56 * 1024 * 1024


def _sigmoid(x):
    return 1.0 / (1.0 + jnp.exp(-x))


def _silu(x):
    return x * _sigmoid(x)


def _gelu_tanh(x):
    return 0.5 * x * (1.0 + jnp.tanh(math.sqrt(2.0 / math.pi) * (x + 0.044715 * (x * x * x))))


def _resident(shape):
    zeros = (0,) * len(shape)
    return pl.BlockSpec(shape, lambda *_: zeros, pipeline_mode=pl.Buffered(1))


def _inproj_body(x_ref, nw_ref, w_ref, u_ref, gs_ref, p_ref, gh_ref, m_ref, *, bounds, col_chunk):
    x = x_ref[...]
    ms = jnp.mean(x * x, axis=-1, keepdims=True)
    xb = (x * lax.rsqrt(ms + RMS_EPS) * nw_ref[...]).astype(BF16)
    outs = ((u_ref, None), (gs_ref, _silu), (p_ref, None), (gh_ref, _silu), (m_ref, _sigmoid))
    for (out_ref, act), lo, hi in zip(outs, bounds[:-1], bounds[1:]):
        for c0 in range(lo, hi, col_chunk):
            y = jnp.dot(xb, w_ref[:, c0:c0 + col_chunk], preferred_element_type=F32)
            if act is not None:
                y = act(y)
            out_ref[:, c0 - lo:c0 - lo + col_chunk] = y.astype(out_ref.dtype)


def _inproj(h2d, norm_w, w_in_bf, widths, *, row_tile=256, col_chunk=512):
    rows, d = h2d.shape
    bounds = [0]
    for w in widths:
        bounds.append(bounds[-1] + w)
    body = functools.partial(_inproj_body, bounds=tuple(bounds), col_chunk=col_chunk)
    return pl.pallas_call(
        body,
        grid=(rows // row_tile,),
        in_specs=[pl.BlockSpec((row_tile, d), lambda i: (i, 0)),
                  _resident((1, d)),
                  _resident(w_in_bf.shape)],
        out_specs=[pl.BlockSpec((row_tile, w), lambda i: (i, 0)) for w in widths],
        out_shape=[jax.ShapeDtypeStruct((rows, w), BF16) for w in widths],
        compiler_params=pltpu.CompilerParams(dimension_semantics=("parallel",),
                                             vmem_limit_bytes=VMEM_LIMIT_BYTES),
    )(h2d, norm_w.reshape(1, d), w_in_bf)


def _s5_body(d_ref, wintra_ref, win_ref, wout_ref, a_ref, dv_ref, y_ref, in_scr, sp_scr, *, batch, chunks, sl):
    d = d_ref[0]
    in_scr[...] = jnp.dot(d, win_ref[0], preferred_element_type=F32)

    def lane_bcast(row):
        return jnp.broadcast_to(a_ref[0, row:row + 1, :], (batch, sl))

    def run(direction):
        ar, ai = lane_bcast(2 * direction), lane_bcast(2 * direction + 1)
        c_re, c_im = 2 * direction * sl, (2 * direction + 1) * sl

        def step(i, carry):
            sr, si = carry
            c = i if direction == 0 else chunks - 1 - i
            r0 = pl.multiple_of(c * batch, batch)
            sp_scr[pl.ds(r0, batch), c_re:c_re + sl] = sr
            sp_scr[pl.ds(r0, batch), c_im:c_im + sl] = si
            ir = in_scr[pl.ds(r0, batch), c_re:c_re + sl]
            ii = in_scr[pl.ds(r0, batch), c_im:c_im + sl]
            return ar * sr - ai * si + ir, ar * si + ai * sr + ii

        zero = jnp.zeros((batch, sl), F32)
        lax.fori_loop(0, chunks, step, (zero, zero))

    run(0)
    run(1)
    y = jnp.dot(d, wintra_ref[0], preferred_element_type=F32)
    y = y + jnp.dot(sp_scr[...].astype(BF16), wout_ref[0], preferred_element_type=F32)
    y = y + d.astype(F32) * dv_ref[0]
    y_ref[0] = y.astype(y_ref.dtype)


def _s5_operands(lam_re, lam_im, log_step, b_re, b_im, c_re, c_im, d_skip):
    dirs, groups, p = lam_re.shape
    hh, t, gp = S5_GROUP, S5_CHUNK, S5_GROUPS_PER_STEP
    npair = groups // gp
    dt = jnp.exp(log_step)[..., None]
    lam = lam_re + 1j * lam_im
    lbar = jnp.exp(lam_re * dt) * (jnp.cos(lam_im * dt) + 1j * jnp.sin(lam_im * dt))
    bbar = ((lbar - 1.0) / lam)[..., None] * (b_re + 1j * b_im)
    cc = c_re + 1j * c_im
    ks = jnp.arange(t + 1, dtype=F32)[:, None, None, None]
    powers = jnp.exp(ks * (lam_re * dt)) * (jnp.cos(ks * (lam_im * dt)) + 1j * jnp.sin(ks * (lam_im * dt)))
    kern = jnp.real(jnp.einsum('dghp,kdgp,dgpi->dkghi', cc, powers[:t], bbar))
    tt = jnp.arange(t)
    lag = tt[None, :] - tt[:, None]
    kf = jnp.where((lag >= 0)[..., None, None, None], kern[0][jnp.clip(lag, 0, t - 1)], 0.0)
    kb = jnp.where((lag <= 0)[..., None, None, None], kern[1][jnp.clip(-lag, 0, t - 1)], 0.0)
    toep = jnp.transpose(kf + kb, (2, 0, 4, 1, 3))
    toep = toep.reshape(npair, gp, t * hh, t * hh)
    eye = jnp.eye(gp, dtype=F32)
    wintra = jnp.einsum('nakl,ab->nakbl', toep, eye).reshape(npair, gp * t * hh, gp * t * hh)
    pin = jnp.stack([powers[:t][::-1, 0], powers[:t][:, 1]], axis=0)
    win_c = jnp.einsum('dtgp,dgph->dgthp', pin, bbar)
    win_c = win_c.reshape(2, npair, gp, t * hh, p)
    win_parts = [jnp.real(win_c[0]), jnp.imag(win_c[0]), jnp.real(win_c[1]), jnp.imag(win_c[1])]
    win = jnp.stack([jnp.einsum('nakp,ab->nakbp', w, eye) for w in win_parts], axis=3)
    win = win.reshape(npair, gp * t * hh, 4 * gp * p)
    pout = jnp.stack([powers[1:t + 1, 0], powers[1:t + 1][::-1, 1]], axis=0)
    e = jnp.einsum('dghp,dtgp->dgpth', cc, pout).reshape(2, npair, gp, p, t * hh)
    wout_parts = [jnp.real(e[0]), -jnp.imag(e[0]), jnp.real(e[1]), -jnp.imag(e[1])]
    wout = jnp.stack([jnp.einsum('napk,ab->napbk', w, eye) for w in wout_parts], axis=1)
    wout = wout.reshape(npair, 4 * gp * p, gp * t * hh)
    lt = powers[t].reshape(2, npair, gp * p)
    a = jnp.stack([jnp.real(lt[0]), jnp.imag(lt[0]), jnp.real(lt[1]), jnp.imag(lt[1])], axis=1)
    dvec = jnp.broadcast_to(d_skip.reshape(npair, gp, 1, hh), (npair, gp, t, hh)).reshape(npair, 1, gp * t * hh)
    return wintra.astype(BF16), win.astype(BF16), wout.astype(BF16), a.astype(F32), dvec.astype(F32)


def _s5_scan(u, operands, batch, seq):
    wintra, win, wout, a, dvec = operands
    npair, k, _ = wintra.shape
    sl = a.shape[-1]
    hh, t, gp = S5_GROUP, S5_CHUNK, S5_GROUPS_PER_STEP
    chunks = seq // t
    rows = chunks * batch
    d = u.reshape(batch, chunks, t, npair, gp, hh)
    d = jnp.transpose(d, (3, 1, 0, 4, 2, 5)).reshape(npair, rows, k)
    body = functools.partial(_s5_body, batch=batch, chunks=chunks, sl=sl)
    per_pair = lambda shape: pl.BlockSpec((1,) + shape, lambda i: (i, 0, 0))
    y = pl.pallas_call(
        body,
        grid=(npair,),
        in_specs=[per_pair((rows, k)), per_pair((k, k)), per_pair((k, 4 * sl)), per_pair((4 * sl, k)),
                  per_pair((4, sl)), per_pair((1, k))],
        out_specs=per_pair((rows, k)),
        out_shape=jax.ShapeDtypeStruct((npair, rows, k), F32),
        scratch_shapes=[pltpu.VMEM((rows, 4 * sl), F32), pltpu.VMEM((rows, 4 * sl), F32)],
        compiler_params=pltpu.CompilerParams(dimension_semantics=("parallel",),
                                             vmem_limit_bytes=VMEM_LIMIT_BYTES),
    )(d, wintra, win, wout, a, dvec)
    y = y.reshape(npair, chunks, batch, gp, t, hh)
    return jnp.transpose(y, (2, 1, 4, 0, 3, 5)).reshape(batch * seq, npair * gp * hh)


def _dft_tables(seq):
    n = 2 * seq
    f = np.arange(seq, dtype=np.int64)
    sym = ((2 * f[:, None] + 1) * (2 * f[None, :] + 1)) % (4 * n)
    lagp = ((2 * f[:, None] + 1) * f[None, :]) % (2 * n)
    ang_sym = sym.astype(np.float64) * (2.0 * np.pi / (4 * n))
    ang_lag = lagp.astype(np.float64) * (2.0 * np.pi / (2 * n))
    as_bf = lambda x: jnp.asarray(x.astype(np.float32)).astype(BF16)
    return as_bf(np.cos(ang_sym)), as_bf(np.sin(ang_sym)), as_bf(np.cos(ang_lag)), as_bf(np.sin(ang_lag))


def _position_features(seq, pad_to):
    t = np.linspace(0.0, 1.0, seq)[:, None]
    pos = np.arange(seq, dtype=np.float64)[:, None]
    bands = np.linspace(1e-4, HY_BANDS - 1, HY_BANDS)[None, :]
    ang = bands * pos * (2.0 * math.pi / seq)
    feats = np.concatenate([t, np.cos(ang), -np.sin(ang)], axis=-1)
    out = np.zeros((seq, pad_to), np.float32)
    out[:, :feats.shape[1]] = feats
    return jnp.asarray(out), jnp.asarray(t.astype(np.float32))


def _hy_filter_body(feats_ref, t_ref, w1_ref, b1_ref, fr_ref, w2_ref, b2_ref, w3f_ref, w3b_ref, decf_ref,
                    decb_ref, pc_ref, ps_ref, kr_ref, ki_ref, h2_scr, *, seq):
    hi = lax.Precision.HIGHEST

    @pl.when(pl.program_id(0) == 0)
    def _():
        fr = fr_ref[...]
        h1 = jnp.sin(fr * (jnp.dot(feats_ref[...], w1_ref[...], precision=hi, preferred_element_type=F32)
                           + b1_ref[...]))
        h2_scr[...] = jnp.sin(fr * (jnp.dot(h1, w2_ref[...], precision=hi, preferred_element_type=F32)
                                    + b2_ref[...]))

    h2 = h2_scr[...]
    t = t_ref[...]
    hf = jnp.dot(h2, w3f_ref[0], precision=hi, preferred_element_type=F32)
    hf = hf * (jnp.exp(-t * jnp.abs(decf_ref[0])) + HY_SHIFT)
    hb = jnp.dot(h2, w3b_ref[0], precision=hi, preferred_element_type=F32)
    hb = hb * (jnp.exp(-t * jnp.abs(decb_ref[0])) + HY_SHIFT)
    row = lax.broadcasted_iota(jnp.int32, hb.shape, 0)
    hb = jnp.where(row == 0, 0.0, hb)
    norm = jnp.sum(hf * hf, axis=0, keepdims=True) + jnp.sum(hb * hb, axis=0, keepdims=True)
    scale = lax.rsqrt(norm + HY_EPS) * (1.0 / seq)
    ke = (hf + hb).astype(BF16)
    ko = (hf - hb).astype(BF16)
    kr_ref[0] = jnp.dot(pc_ref[...], ke, preferred_element_type=F32) * scale
    ki_ref[0] = jnp.dot(ps_ref[...], ko, preferred_element_type=F32) * scale


def _hy_filters(seq, w1, b1, freq, w2, b2, w3, decay, width, tables, *, col_block=256):
    _, _, pc, ps = tables
    ffn = w2.shape[0]
    feat_pad = 128
    feats, t = _position_features(seq, feat_pad)
    w1p = jnp.zeros((feat_pad, ffn), F32).at[:w1.shape[0]].set(w1)
    w3r = w3.reshape(ffn, HY_ORDER, 2, width)
    w3f, w3b = jnp.transpose(w3r[:, :, 0], (1, 0, 2)), jnp.transpose(w3r[:, :, 1], (1, 0, 2))
    decr = decay.reshape(HY_ORDER, 2, 1, width)
    decf, decb = decr[:, 0], decr[:, 1]
    nblk = width // col_block
    col = lambda shape: pl.BlockSpec((1,) + shape, lambda i: (i // nblk, 0, i % nblk))
    body = functools.partial(_hy_filter_body, seq=seq)
    row2 = lambda v: v.reshape(1, -1)
    return pl.pallas_call(
        body,
        grid=(HY_ORDER * nblk,),
        in_specs=[_resident(feats.shape), _resident(t.shape), _resident(w1p.shape), _resident((1, ffn)),
                  _resident((1, ffn)), _resident(w2.shape), _resident((1, ffn)),
                  col((ffn, col_block)), col((ffn, col_block)), col((1, col_block)), col((1, col_block)),
                  _resident(pc.shape), _resident(ps.shape)],
        out_specs=[col((seq, col_block)), col((seq, col_block))],
        out_shape=[jax.ShapeDtypeStruct((HY_ORDER, seq, width), F32)] * 2,
        scratch_shapes=[pltpu.VMEM((seq, ffn), F32)],
        compiler_params=pltpu.CompilerParams(dimension_semantics=("arbitrary",),
                                             vmem_limit_bytes=VMEM_LIMIT_BYTES),
    )(feats, t, w1p, row2(b1), row2(freq), w2, row2(b2), w3f, w3b, decf, decb, pc, ps)


def _hy_conv_body(pv_ref, p1_ref, p2_ref, g_ref, cw_ref, cb_ref, kr_ref, ki_ref, d_ref, mc_ref, ms_ref, o_ref,
                  z_scr, g_scr, zb_scr, yr_scr, ys_scr, *, seq, row_block):
    nrb = seq // row_block

    def short_conv(p_ref, part):
        x = p_ref[0].astype(F32)
        row = lax.broadcasted_iota(jnp.int32, x.shape, 0)
        prev = jnp.where(row == 0, 0.0, pltpu.roll(x, 1, 0))
        nxt = jnp.where(row == seq - 1, 0.0, pltpu.roll(x, seq - 1, 0))
        w = cw_ref[:, part, :]
        return cb_ref[part:part + 1, :] + prev * w[0:1] + x * w[1:2] + nxt * w[2:3]

    z_scr[...] = short_conv(pv_ref, 0)
    for o, gate_ref in enumerate((p1_ref, p2_ref)):
        zb_scr[...] = z_scr[...].astype(BF16)

        def fwd(i, _):
            r = pl.ds(pl.multiple_of(i * row_block, row_block), row_block)
            zb = zb_scr[...]
            a = jnp.dot(mc_ref[r, :], zb, preferred_element_type=F32)
            b = jnp.dot(ms_ref[r, :], zb, preferred_element_type=F32)
            kr, ki = kr_ref[o, r, :], ki_ref[o, r, :]
            yr_scr[r, :] = (a * kr - b * ki).astype(BF16)
            ys_scr[r, :] = (a * ki + b * kr).astype(BF16)
            return 0

        lax.fori_loop(0, nrb, fwd, 0)
        g_scr[...] = short_conv(gate_ref, o + 1)
        dsk = d_ref[o:o + 1, :]
        last = o == HY_ORDER - 1

        def inv(i, _):
            r = pl.ds(pl.multiple_of(i * row_block, row_block), row_block)
            conv = (jnp.dot(mc_ref[r, :], yr_scr[...], preferred_element_type=F32)
                    + jnp.dot(ms_ref[r, :], ys_scr[...], preferred_element_type=F32))
            z = g_scr[r, :] * (conv + dsk * z_scr[r, :])
            if last:
                o_ref[0, r, :] = (z * g_ref[0, r, :].astype(F32)).astype(o_ref.dtype)
            else:
                z_scr[r, :] = z
            return 0

        lax.fori_loop(0, nrb, inv, 0)


def _hy_conv(p_hy, gate_silu, conv_w, conv_b, kr, ki, d_skip, tables, batch, seq, width, *, col_block=256,
             row_block=512):
    mc, ms, _, _ = tables
    nblk = width // col_block
    part = lambda k: pl.BlockSpec((1, seq, col_block), lambda c, b: (b, 0, k * nblk + c))
    cw = conv_w.reshape(conv_w.shape[0], HY_ORDER + 1, width)
    cb = conv_b.reshape(HY_ORDER + 1, width)
    filt = pl.BlockSpec((HY_ORDER, seq, col_block), lambda c, b: (0, 0, c), pipeline_mode=pl.Buffered(1))
    body = functools.partial(_hy_conv_body, seq=seq, row_block=min(row_block, seq))
    return pl.pallas_call(
        body,
        grid=(nblk, batch),
        in_specs=[part(0), part(1), part(2),
                  pl.BlockSpec((1, seq, col_block), lambda c, b: (b, 0, c)),
                  pl.BlockSpec((cw.shape[0], HY_ORDER + 1, col_block), lambda c, b: (0, 0, c)),
                  pl.BlockSpec((HY_ORDER + 1, col_block), lambda c, b: (0, c)),
                  filt, filt,
                  pl.BlockSpec((HY_ORDER, col_block), lambda c, b: (0, c)),
                  _resident(mc.shape), _resident(ms.shape)],
        out_specs=pl.BlockSpec((1, seq, col_block), lambda c, b: (b, 0, c)),
        out_shape=jax.ShapeDtypeStruct((batch, seq, width), BF16),
        scratch_shapes=[pltpu.VMEM((seq, col_block), F32), pltpu.VMEM((seq, col_block), F32),
                        pltpu.VMEM((seq, col_block), BF16), pltpu.VMEM((seq, col_block), BF16),
                        pltpu.VMEM((seq, col_block), BF16)],
        compiler_params=pltpu.CompilerParams(dimension_semantics=("parallel", "parallel"),
                                             vmem_limit_bytes=VMEM_LIMIT_BYTES),
    )(p_hy, p_hy, p_hy, gate_silu, cw, cb, kr, ki, d_skip, mc, ms)


def _merge_body(ys_ref, gs_ref, yh_ref, m_ref, h_ref, wg_ref, bg_ref, wbs_ref, wbh_ref, wo_ref, fw_ref, o_ref,
                *, d_model, final_norm):
    y = _gelu_tanh(ys_ref[...])
    glu = y * _sigmoid(jnp.dot(y.astype(BF16), wg_ref[...], preferred_element_type=F32) + bg_ref[...])
    s5 = (glu * gs_ref[...].astype(F32)).astype(BF16)
    y_s5 = jnp.dot(s5, wbs_ref[...], preferred_element_type=F32)
    y_hy = jnp.dot(yh_ref[...], wbh_ref[...], preferred_element_type=F32)
    merged = m_ref[:, :d_model].astype(F32) * y_s5 + m_ref[:, d_model:].astype(F32) * y_hy
    h = h_ref[...] + jnp.dot(merged.astype(BF16), wo_ref[...], preferred_element_type=F32)
    if final_norm:
        ms = jnp.mean(h * h, axis=-1, keepdims=True)
        h = h * lax.rsqrt(ms + RMS_EPS) * fw_ref[...]
    o_ref[...] = h


def _merge(y_s5pre, gs, y_hypre, m, h2d, w_glu, b_glu, w_bs, w_bh, w_out, final_w, final_norm, *, row_tile=256):
    rows, d = h2d.shape
    sw = y_s5pre.shape[1]
    hw = y_hypre.shape[1]
    tile = lambda w: pl.BlockSpec((row_tile, w), lambda i: (i, 0))
    body = functools.partial(_merge_body, d_model=d, final_norm=final_norm)
    return pl.pallas_call(
        body,
        grid=(rows // row_tile,),
        in_specs=[tile(sw), tile(sw), tile(hw), tile(2 * d), tile(d),
                  _resident(w_glu.shape), _resident((1, sw)), _resident(w_bs.shape), _resident(w_bh.shape),
                  _resident(w_out.shape), _resident((1, d))],
        out_specs=tile(d),
        out_shape=jax.ShapeDtypeStruct((rows, d), F32),
        compiler_params=pltpu.CompilerParams(dimension_semantics=("parallel",),
                                             vmem_limit_bytes=VMEM_LIMIT_BYTES),
    )(y_s5pre, gs, y_hypre, m, h2d, w_glu, b_glu.reshape(1, sw), w_bs, w_bh, w_out, final_w.reshape(1, d))


def kernel(x, norm_w, w_in, s5_lam_re, s5_lam_im, s5_log_step, s5_b_re, s5_b_im, s5_c_re, s5_c_im, s5_d, s5_w_glu, s5_b_glu, hy_conv_w, hy_conv_b, hy_w1, hy_b1, hy_freq, hy_w2, hy_b2, hy_w3, hy_decay, hy_d, w_branch_s5, w_branch_hy, w_out, final_norm_w):
    batch, seq, d_model = x.shape
    depth = w_in.shape[0]
    s5_width = s5_d.shape[1]
    hy_width = hy_d.shape[2]
    widths = (s5_width, s5_width, (HY_ORDER + 1) * hy_width, hy_width, 2 * d_model)
    tables = _dft_tables(seq)
    h = x.reshape(batch * seq, d_model)
    for l in range(depth):
        u, gs, p_hy, gh, m = _inproj(h, norm_w[l], w_in[l].astype(BF16), widths)
        s5_ops = _s5_operands(s5_lam_re[l], s5_lam_im[l], s5_log_step[l], s5_b_re[l], s5_b_im[l],
                              s5_c_re[l], s5_c_im[l], s5_d[l])
        y_s5pre = _s5_scan(u, s5_ops, batch, seq)
        kr, ki = _hy_filters(seq, hy_w1[l], hy_b1[l], hy_freq[l], hy_w2[l], hy_b2[l], hy_w3[l], hy_decay[l],
                             hy_width, tables)
        y_hypre = _hy_conv(p_hy.reshape(batch, seq, -1), gh.reshape(batch, seq, -1), hy_conv_w[l], hy_conv_b[l],
                           kr, ki, hy_d[l], tables, batch, seq, hy_width)
        h = _merge(y_s5pre, gs, y_hypre.reshape(batch * seq, hy_width), m, h,
                   s5_w_glu[l].astype(BF16), s5_b_glu[l], w_branch_s5[l].astype(BF16),
                   w_branch_hy[l].astype(BF16), w_out[l].astype(BF16), final_norm_w, l == depth - 1)
    return h.reshape(batch, seq, d_model)
```

```python
import functools
import math

import numpy as np
import jax
import jax.numpy as jnp
from jax import lax
from jax.experimental import pallas as pl
from jax.experimental.pallas import tpu as pltpu

F32 = jnp.float32
BF16 = jnp.bfloat16

RMS_EPS = 1e-6
S5_GROUP = 16
S5_CHUNK = 16
S5_GROUPS_PER_PAIR = 2
LANES = 128
HY_ORDER = 2
HY_BANDS = 16
HY_SHIFT = 0.05
HY_EPS = 1e-6
VMEM_LIMIT_BYTES = 56 * 1024 * 1024

_NT = (((1,), (1,)), ((), ()))


def _sigmoid(x):
    return 1.0 / (1.0 + jnp.exp(-x))


def _silu(x):
    return x * _sigmoid(x)


def _gelu_tanh(x):
    return 0.5 * x * (1.0 + jnp.tanh(math.sqrt(2.0 / math.pi) * (x + 0.044715 * (x * x * x))))


def _resident(shape):
    zeros = (0,) * len(shape)
    return pl.BlockSpec(shape, lambda *_: zeros, pipeline_mode=pl.Buffered(1))


def _chunk_major_spec(row_tile, seq, width):
    tiles_per_seq = seq // row_tile
    return pl.BlockSpec((row_tile // S5_CHUNK, None, S5_CHUNK, width),
                        lambda i: (i % tiles_per_seq, i // tiles_per_seq, 0, 0))


def _inproj_body(x_ref, nw_ref, w_ref, u_ref, gs_ref, p_ref, gh_ref, m_ref, *, bounds, col_chunk):
    x = x_ref[...]
    ms = jnp.mean(x * x, axis=-1, keepdims=True)
    xb = (x * lax.rsqrt(ms + RMS_EPS) * nw_ref[...]).astype(BF16)
    outs = ((u_ref, None), (gs_ref, _silu), (p_ref, None), (gh_ref, _silu), (m_ref, _sigmoid))
    for (out_ref, act), lo, hi in zip(outs, bounds[:-1], bounds[1:]):
        for c0 in range(lo, hi, col_chunk):
            y = jnp.dot(xb, w_ref[:, c0:c0 + col_chunk], preferred_element_type=F32)
            if act is not None:
                y = act(y)
            if out_ref is u_ref:
                out_ref[:, :, c0 - lo:c0 - lo + col_chunk] = y.reshape(
                    y.shape[0] // S5_CHUNK, S5_CHUNK, col_chunk).astype(out_ref.dtype)
            else:
                out_ref[:, c0 - lo:c0 - lo + col_chunk] = y.astype(out_ref.dtype)


def _inproj(h2d, norm_w, w_in_bf, widths, batch, seq, *, row_tile=256, col_chunk=512):
    rows, d = h2d.shape
    bounds = [0]
    for w in widths:
        bounds.append(bounds[-1] + w)
    body = functools.partial(_inproj_body, bounds=tuple(bounds), col_chunk=col_chunk)
    tile = lambda w: pl.BlockSpec((row_tile, w), lambda i: (i, 0))
    return pl.pallas_call(
        body,
        grid=(rows // row_tile,),
        in_specs=[tile(d), _resident((1, d)), _resident(w_in_bf.shape)],
        out_specs=[_chunk_major_spec(row_tile, seq, widths[0])] + [tile(w) for w in widths[1:]],
        out_shape=[jax.ShapeDtypeStruct((seq // S5_CHUNK, batch, S5_CHUNK, widths[0]), BF16)]
        + [jax.ShapeDtypeStruct((rows, w), BF16) for w in widths[1:]],
        compiler_params=pltpu.CompilerParams(dimension_semantics=("parallel",),
                                             vmem_limit_bytes=VMEM_LIMIT_BYTES),
    )(h2d, norm_w.reshape(1, d), w_in_bf)


def _s5_body(*refs, batch, chunks, pairs, pw, sl):
    t_steps = S5_CHUNK
    x_refs = refs[:t_steps]
    win_ref, woutt_ref, a_ref, sc_ref, dv_ref, y_ref, d_scr, st_scr = refs[t_steps:]
    k = t_steps * pw
    rows = batch * chunks
    assert k == 4 * sl

    for j in range(pairs):
        dj = jnp.concatenate([x_refs[t][:, j * pw:(j + 1) * pw] for t in range(t_steps)], axis=1)
        d_scr[:, j * k:(j + 1) * k] = dj
        st_scr[:, j * k:(j + 1) * k] = jnp.dot(dj, win_ref[j].astype(BF16), preferred_element_type=F32)

    coef = [[jnp.broadcast_to(a_ref[j, c:c + 1, :], (batch, sl)) for c in range(4)] for j in range(pairs)]

    def step(i, carry):
        row_f = pl.ds(pl.multiple_of(i * batch, batch), batch)
        row_b = pl.ds(pl.multiple_of((chunks - 1 - i) * batch, batch), batch)
        out = []
        for j in range(pairs):
            for direction, r in ((0, row_f), (1, row_b)):
                sr, si = carry[4 * j + 2 * direction], carry[4 * j + 2 * direction + 1]
                ar, ai = coef[j][2 * direction], coef[j][2 * direction + 1]
                c_re = j * k + 2 * direction * sl
                c_im = c_re + sl
                ir, ii = st_scr[r, c_re:c_re + sl], st_scr[r, c_im:c_im + sl]
                st_scr[r, c_re:c_re + sl] = sr
                st_scr[r, c_im:c_im + sl] = si
                out += [ar * sr - ai * si + ir, ar * si + ai * sr + ii]
        return tuple(out)

    zero = jnp.zeros((batch, sl), F32)
    lax.fori_loop(0, chunks, step, (zero,) * (4 * pairs))

    in_step = lax.broadcasted_iota(jnp.int32, (k, k), 0) // pw
    out_step = lax.broadcasted_iota(jnp.int32, (k, k), 1) // pw
    for j in range(pairs):
        w, sc, wt = win_ref[j], sc_ref[j], woutt_ref[j]
        kern = None
        for direction in range(2):
            lo = 2 * direction * sl
            wr, wi = w[:, lo:lo + sl], w[:, lo + sl:lo + 2 * sl]
            pr, pi = sc[2 * direction:2 * direction + 1, :], sc[2 * direction + 1:2 * direction + 2, :]
            wp = jnp.concatenate([wr * pr - wi * pi, wr * pi + wi * pr], axis=1).astype(BF16)
            kd = lax.dot_general(wp, wt[:, lo:lo + 2 * sl], _NT, preferred_element_type=F32)
            kd = jnp.where(out_step >= in_step if direction == 0 else out_step <= in_step, kd, 0.0)
            kern = kd if kern is None else kern + kd
        dj = d_scr[:, j * k:(j + 1) * k]
        y = jnp.dot(dj, kern.astype(BF16), preferred_element_type=F32)
        y = y + lax.dot_general(st_scr[:, j * k:(j + 1) * k].astype(BF16), wt, _NT, preferred_element_type=F32)
        y = y + dj.astype(F32) * dv_ref[j]
        st_scr[:, j * k:(j + 1) * k] = y
    for t in range(t_steps):
        y_ref[pl.ds(t, rows, stride=t_steps), :] = jnp.concatenate(
            [st_scr[:, j * k + t * pw:j * k + (t + 1) * pw] for j in range(pairs)], axis=1)


def _s5_operands(lam_re, lam_im, log_step, b_re, b_im, c_re, c_im, d_skip):
    _, groups, p = lam_re.shape
    hh, t, gp = S5_GROUP, S5_CHUNK, S5_GROUPS_PER_PAIR
    npair = groups // gp
    dt = jnp.exp(log_step)[..., None]
    xr, xi = lam_re * dt, lam_im * dt
    lbr, lbi = jnp.exp(xr) * jnp.cos(xi), jnp.exp(xr) * jnp.sin(xi)
    nr, ni = lbr - 1.0, lbi
    den = lam_re * lam_re + lam_im * lam_im
    zr, zi = (nr * lam_re + ni * lam_im) / den, (ni * lam_re - nr * lam_im) / den
    btr, bti = jnp.swapaxes(b_re, -1, -2), jnp.swapaxes(b_im, -1, -2)
    bbr = zr[:, :, None, :] * btr - zi[:, :, None, :] * bti
    bbi = zr[:, :, None, :] * bti + zi[:, :, None, :] * btr

    def powers(e):
        kk = e[:, None, :, None]
        mag, ang = jnp.exp(kk * xr[:, :, None, :]), kk * xi[:, :, None, :]
        return (mag * jnp.cos(ang))[:, :, :, None, :], (mag * jnp.sin(ang))[:, :, :, None, :]

    tt = jnp.arange(t, dtype=F32)
    pir, pii = powers(jnp.stack([t - 1 - tt, tt]))
    por, poi = powers(jnp.stack([tt + 1, t - tt]))
    b5r, b5i = bbr[:, :, None], bbi[:, :, None]
    c5r, c5i = c_re[:, :, None], c_im[:, :, None]
    win_r, win_i = pir * b5r - pii * b5i, pir * b5i + pii * b5r
    out_r, out_i = por * c5r - poi * c5i, por * c5i + poi * c5r

    def pair_layout(x_re, x_im):
        comp = jnp.stack([x_re[0], x_im[0], x_re[1], x_im[1]]).reshape(4, npair, gp, t, hh, p)
        comp = jnp.transpose(comp, (1, 3, 2, 4, 0, 5))
        eye = jnp.eye(gp, dtype=F32)
        full = comp[:, :, :, :, :, None, :] * eye[None, None, :, None, None, :, None]
        return full.reshape(npair, t * gp * hh, 4 * gp * p)

    def pair_vec(e):
        mag, ang = jnp.exp(e * xr), e * xi
        v = jnp.stack([(mag * jnp.cos(ang))[0], (mag * jnp.sin(ang))[0],
                       (mag * jnp.cos(ang))[1], (mag * jnp.sin(ang))[1]])
        return jnp.transpose(v.reshape(4, npair, gp * p), (1, 0, 2))

    dvec = jnp.broadcast_to(d_skip.reshape(npair, 1, gp, hh), (npair, t, gp, hh)).reshape(npair, 1, t * gp * hh)
    return (pair_layout(win_r, win_i), pair_layout(out_r, -out_i).astype(BF16),
            pair_vec(float(t)), pair_vec(-float(t)), dvec)


def _s5_scan(u_cm, operands, batch, seq):
    win, woutt, a, sc, dvec = operands
    npair, k, sl4 = win.shape
    sl = sl4 // 4
    t = S5_CHUNK
    pw = k // t
    pairs = LANES // pw
    chunks = seq // t
    rows = chunks * batch
    width = npair * pw
    nblk = width // LANES
    body = functools.partial(_s5_body, batch=batch, chunks=chunks, pairs=pairs, pw=pw, sl=sl)
    per_blk = lambda shape, **kw: pl.BlockSpec((pairs,) + shape, lambda q: (q, 0, 0), **kw)
    x2d = u_cm.reshape(rows, t * width)
    y = pl.pallas_call(
        body,
        grid=(nblk,),
        in_specs=[pl.BlockSpec((rows, LANES), functools.partial(lambda q, s: (0, s * nblk + q), s=s))
                  for s in range(t)]
        + [per_blk((k, sl4), pipeline_mode=pl.Buffered(1)), per_blk((k, sl4), pipeline_mode=pl.Buffered(1)),
           per_blk((4, sl)), per_blk((4, sl)), per_blk((1, k))],
        out_specs=pl.BlockSpec((rows * t, LANES), lambda q: (0, q)),
        out_shape=jax.ShapeDtypeStruct((rows * t, width), F32),
        scratch_shapes=[pltpu.VMEM((rows, pairs * k), BF16), pltpu.VMEM((rows, pairs * k), F32)],
        compiler_params=pltpu.CompilerParams(dimension_semantics=("parallel",),
                                             vmem_limit_bytes=VMEM_LIMIT_BYTES),
    )(*([x2d] * t), win, woutt, a, sc, dvec)
    return y.reshape(chunks, batch, t, width)


def _dft_tables(seq):
    n = 2 * seq
    f = np.arange(seq, dtype=np.int64)
    sym = ((2 * f[:, None] + 1) * (2 * f[None, :] + 1)) % (4 * n)
    lagp = ((2 * f[:, None] + 1) * f[None, :]) % (2 * n)
    ang_sym = sym.astype(np.float64) * (2.0 * np.pi / (4 * n))
    ang_lag = lagp.astype(np.float64) * (2.0 * np.pi / (2 * n))
    as_bf = lambda x: jnp.asarray(x.astype(np.float32)).astype(BF16)
    return as_bf(np.cos(ang_sym)), as_bf(np.sin(ang_sym)), as_bf(np.cos(ang_lag)), as_bf(np.sin(ang_lag))


def _position_features(seq, pad_to):
    t = np.linspace(0.0, 1.0, seq)[:, None]
    pos = np.arange(seq, dtype=np.float64)[:, None]
    bands = np.linspace(1e-4, HY_BANDS - 1, HY_BANDS)[None, :]
    ang = bands * pos * (2.0 * math.pi / seq)
    feats = np.concatenate([t, np.cos(ang), -np.sin(ang)], axis=-1)
    out = np.zeros((seq, pad_to), np.float32)
    out[:, :feats.shape[1]] = feats
    return out, t.astype(np.float32)


def _hy_filter_body(feats_ref, t_ref, w1_ref, b1_ref, fr_ref, w2_ref, b2_ref, w3f_ref, w3b_ref, decf_ref,
                    decb_ref, pc_ref, ps_ref, kr_ref, ki_ref, h2_scr, *, seq):
    hi = lax.Precision.HIGHEST

    @pl.when(pl.program_id(0) == 0)
    def _():
        fr = fr_ref[...]
        h1 = jnp.sin(fr * (jnp.dot(feats_ref[...], w1_ref[...], precision=hi, preferred_element_type=F32)
                           + b1_ref[...]))
        h2_scr[...] = jnp.sin(fr * (jnp.dot(h1, w2_ref[...], precision=hi, preferred_element_type=F32)
                                    + b2_ref[...]))

    h2 = h2_scr[...]
    t = t_ref[...]
    hf = jnp.dot(h2, w3f_ref[...], precision=hi, preferred_element_type=F32)
    hf = hf * (jnp.exp(-t * jnp.abs(decf_ref[...])) + HY_SHIFT)
    hb = jnp.dot(h2, w3b_ref[...], precision=hi, preferred_element_type=F32)
    hb = hb * (jnp.exp(-t * jnp.abs(decb_ref[...])) + HY_SHIFT)
    row = lax.broadcasted_iota(jnp.int32, hb.shape, 0)
    hb = jnp.where(row == 0, 0.0, hb)
    norm = jnp.sum(hf * hf, axis=0, keepdims=True) + jnp.sum(hb * hb, axis=0, keepdims=True)
    scale = lax.rsqrt(norm + HY_EPS) * (1.0 / seq)
    ke = (hf + hb).astype(BF16)
    ko = (hf - hb).astype(BF16)
    kr_ref[0] = jnp.dot(pc_ref[...], ke, preferred_element_type=F32) * scale
    ki_ref[0] = jnp.dot(ps_ref[...], ko, preferred_element_type=F32) * scale


def _hy_filters(seq, w1, b1, freq, w2, b2, w3, decay, width, tables, *, col_block=256):
    _, _, pc, ps = tables
    ffn = w2.shape[0]
    feat_pad = LANES
    feats, t = _position_features(seq, feat_pad)
    w1p = jnp.zeros((feat_pad, ffn), F32).at[:w1.shape[0]].set(w1)
    nblk = width // col_block
    side = lambda rows, direction: pl.BlockSpec(
        (rows, col_block), lambda i: (0, (2 * (i // nblk) + direction) * nblk + i % nblk))
    out = pl.BlockSpec((1, seq, col_block), lambda i: (i // nblk, 0, i % nblk))
    body = functools.partial(_hy_filter_body, seq=seq)
    row2 = lambda v: v.reshape(1, -1)
    return pl.pallas_call(
        body,
        grid=(HY_ORDER * nblk,),
        in_specs=[_resident(feats.shape), _resident(t.shape), _resident(w1p.shape), _resident((1, ffn)),
                  _resident((1, ffn)), _resident(w2.shape), _resident((1, ffn)),
                  side(ffn, 0), side(ffn, 1), side(1, 0), side(1, 1),
                  _resident(pc.shape), _resident(ps.shape)],
        out_specs=[out, out],
        out_shape=[jax.ShapeDtypeStruct((HY_ORDER, seq, width), F32)] * 2,
        scratch_shapes=[pltpu.VMEM((seq, ffn), F32)],
        compiler_params=pltpu.CompilerParams(dimension_semantics=("arbitrary",),
                                             vmem_limit_bytes=VMEM_LIMIT_BYTES),
    )(feats, t, w1p, row2(b1), row2(freq), w2, row2(b2), w3, w3, row2(decay), row2(decay), pc, ps)


def _hy_conv_body(pv_ref, p1_ref, p2_ref, g_ref, cw_ref, cb_ref, kr_ref, ki_ref, d_ref, mc_ref, ms_ref, o_ref,
                  z_scr, g_scr, zb_scr, yr_scr, ys_scr, *, seq, row_block):
    nrb = seq // row_block

    def short_conv(p_ref, part):
        x = p_ref[0].astype(F32)
        row = lax.broadcasted_iota(jnp.int32, x.shape, 0)
        prev = jnp.where(row == 0, 0.0, pltpu.roll(x, 1, 0))
        nxt = jnp.where(row == seq - 1, 0.0, pltpu.roll(x, seq - 1, 0))
        w = cw_ref[:, part, :]
        return cb_ref[part:part + 1, :] + prev * w[0:1] + x * w[1:2] + nxt * w[2:3]

    z_scr[...] = short_conv(pv_ref, 0)
    for o, gate_ref in enumerate((p1_ref, p2_ref)):
        zb_scr[...] = z_scr[...].astype(BF16)

        def fwd(i, _):
            r = pl.ds(pl.multiple_of(i * row_block, row_block), row_block)
            zb = zb_scr[...]
            a = jnp.dot(mc_ref[r, :], zb, preferred_element_type=F32)
            b = jnp.dot(ms_ref[r, :], zb, preferred_element_type=F32)
            kr, ki = kr_ref[o, r, :], ki_ref[o, r, :]
            yr_scr[r, :] = (a * kr - b * ki).astype(BF16)
            ys_scr[r, :] = (a * ki + b * kr).astype(BF16)
            return 0

        lax.fori_loop(0, nrb, fwd, 0)
        g_scr[...] = short_conv(gate_ref, o + 1)
        dsk = d_ref[o:o + 1, :]
        last = o == HY_ORDER - 1

        def inv(i, _):
            r = pl.ds(pl.multiple_of(i * row_block, row_block), row_block)
            conv = (jnp.dot(mc_ref[r, :], yr_scr[...], preferred_element_type=F32)
                    + jnp.dot(ms_ref[r, :], ys_scr[...], preferred_element_type=F32))
            z = g_scr[r, :] * (conv + dsk * z_scr[r, :])
            if last:
                o_ref[0, r, :] = (z * g_ref[0, r, :].astype(F32)).astype(o_ref.dtype)
            else:
                z_scr[r, :] = z
            return 0

        lax.fori_loop(0, nrb, inv, 0)


def _hy_conv(p_hy, gate_silu, conv_w, conv_b, kr, ki, d_skip, tables, batch, seq, width, *, col_block=256,
             row_block=512):
    mc, ms, _, _ = tables
    nblk = width // col_block
    part = lambda k: pl.BlockSpec((1, seq, col_block), lambda c, b: (b, 0, k * nblk + c))
    cw = conv_w.reshape(conv_w.shape[0], HY_ORDER + 1, width)
    cb = conv_b.reshape(HY_ORDER + 1, width)
    filt = pl.BlockSpec((HY_ORDER, seq, col_block), lambda c, b: (0, 0, c), pipeline_mode=pl.Buffered(1))
    body = functools.partial(_hy_conv_body, seq=seq, row_block=min(row_block, seq))
    return pl.pallas_call(
        body,
        grid=(nblk, batch),
        in_specs=[part(0), part(1), part(2),
                  pl.BlockSpec((1, seq, col_block), lambda c, b: (b, 0, c)),
                  pl.BlockSpec((cw.shape[0], HY_ORDER + 1, col_block), lambda c, b: (0, 0, c)),
                  pl.BlockSpec((HY_ORDER + 1, col_block), lambda c, b: (0, c)),
                  filt, filt,
                  pl.BlockSpec((HY_ORDER, col_block), lambda c, b: (0, c)),
                  _resident(mc.shape), _resident(ms.shape)],
        out_specs=pl.BlockSpec((1, seq, col_block), lambda c, b: (b, 0, c)),
        out_shape=jax.ShapeDtypeStruct((batch, seq, width), BF16),
        scratch_shapes=[pltpu.VMEM((seq, col_block), F32), pltpu.VMEM((seq, col_block), F32),
                        pltpu.VMEM((seq, col_block), BF16), pltpu.VMEM((seq, col_block), BF16),
                        pltpu.VMEM((seq, col_block), BF16)],
        compiler_params=pltpu.CompilerParams(dimension_semantics=("parallel", "parallel"),
                                             vmem_limit_bytes=VMEM_LIMIT_BYTES),
    )(p_hy, p_hy, p_hy, gate_silu, cw, cb, kr, ki, d_skip, mc, ms)


def _merge_body(ys_ref, gs_ref, yh_ref, m_ref, h_ref, wg_ref, bg_ref, wbs_ref, wbh_ref, wo_ref, fw_ref, o_ref,
                *, d_model, final_norm):
    ys = ys_ref[...]
    y = _gelu_tanh(ys.reshape(ys.shape[0] * ys.shape[1], ys.shape[2]))
    glu = y * _sigmoid(jnp.dot(y.astype(BF16), wg_ref[...], preferred_element_type=F32) + bg_ref[...])
    s5 = (glu * gs_ref[...].astype(F32)).astype(BF16)
    y_s5 = jnp.dot(s5, wbs_ref[...], preferred_element_type=F32)
    y_hy = jnp.dot(yh_ref[...], wbh_ref[...], preferred_element_type=F32)
    merged = m_ref[:, :d_model].astype(F32) * y_s5 + m_ref[:, d_model:].astype(F32) * y_hy
    h = h_ref[...] + jnp.dot(merged.astype(BF16), wo_ref[...], preferred_element_type=F32)
    if final_norm:
        ms = jnp.mean(h * h, axis=-1, keepdims=True)
        h = h * lax.rsqrt(ms + RMS_EPS) * fw_ref[...]
    o_ref[...] = h


def _merge(y_s5_cm, gs, y_hypre, m, h2d, w_glu, b_glu, w_bs, w_bh, w_out, final_w, final_norm, seq, *,
           row_tile=256):
    rows, d = h2d.shape
    sw = y_s5_cm.shape[-1]
    hw = y_hypre.shape[1]
    tile = lambda w: pl.BlockSpec((row_tile, w), lambda i: (i, 0))
    body = functools.partial(_merge_body, d_model=d, final_norm=final_norm)
    return pl.pallas_call(
        body,
        grid=(rows // row_tile,),
        in_specs=[_chunk_major_spec(row_tile, seq, sw), tile(sw), tile(hw), tile(2 * d), tile(d),
                  _resident(w_glu.shape), _resident((1, sw)), _resident(w_bs.shape), _resident(w_bh.shape),
                  _resident(w_out.shape), _resident((1, d))],
        out_specs=tile(d),
        out_shape=jax.ShapeDtypeStruct((rows, d), F32),
        compiler_params=pltpu.CompilerParams(dimension_semantics=("parallel",),
                                             vmem_limit_bytes=VMEM_LIMIT_BYTES),
    )(y_s5_cm, gs, y_hypre, m, h2d, w_glu, b_glu.reshape(1, sw), w_bs, w_bh, w_out, final_w.reshape(1, d))


def kernel(x, norm_w, w_in, s5_lam_re, s5_lam_im, s5_log_step, s5_b_re, s5_b_im, s5_c_re, s5_c_im, s5_d, s5_w_glu, s5_b_glu, hy_conv_w, hy_conv_b, hy_w1, hy_b1, hy_freq, hy_w2, hy_b2, hy_w3, hy_decay, hy_d, w_branch_s5, w_branch_hy, w_out, final_norm_w):
    batch, seq, d_model = x.shape
    depth = w_in.shape[0]
    s5_width = s5_d.shape[1]
    hy_width = hy_d.shape[2]
    widths = (s5_width, s5_width, (HY_ORDER + 1) * hy_width, hy_width, 2 * d_model)
    tables = _dft_tables(seq)
    h = x.reshape(batch * seq, d_model)
    for l in range(depth):
        u, gs, p_hy, gh, m = _inproj(h, norm_w[l], w_in[l].astype(BF16), widths, batch, seq)
        s5_ops = _s5_operands(s5_lam_re[l], s5_lam_im[l], s5_log_step[l], s5_b_re[l], s5_b_im[l],
                              s5_c_re[l], s5_c_im[l], s5_d[l])
        y_s5 = _s5_scan(u, s5_ops, batch, seq)
        kr, ki = _hy_filters(seq, hy_w1[l], hy_b1[l], hy_freq[l], hy_w2[l], hy_b2[l], hy_w3[l], hy_decay[l],
                             hy_width, tables)
        y_hypre = _hy_conv(p_hy.reshape(batch, seq, -1), gh.reshape(batch, seq, -1), hy_conv_w[l], hy_conv_b[l],
                           kr, ki, hy_d[l], tables, batch, seq, hy_width)
        h = _merge(y_s5, gs, y_hypre.reshape(batch * seq, hy_width), m, h,
                   s5_w_glu[l].astype(BF16), s5_b_glu[l], w_branch_s5[l].astype(BF16),
                   w_branch_hy[l].astype(BF16), w_out[l].astype(BF16), final_norm_w, l == depth - 1, seq)
    return h.reshape(batch, seq, d_model)
```

```python
import functools
import math

import numpy as np
import jax
import jax.numpy as jnp
from jax import lax
from jax.experimental import pallas as pl
from jax.experimental.pallas import tpu as pltpu

F32 = jnp.float32
BF16 = jnp.bfloat16

RMS_EPS = 1e-6
S5_GROUP = 16
S5_CHUNK = 16
S5_GROUPS_PER_PAIR = 2
LANES = 128
HY_ORDER = 2
HY_BANDS = 16
HY_SHIFT = 0.05
HY_EPS = 1e-6
HY_TIME_BLOCKS = 4
VMEM_LIMIT_BYTES = 56 * 1024 * 1024

_NT = (((1,), (1,)), ((), ()))


def _sigmoid(x):
    return 1.0 / (1.0 + jnp.exp(-x))


def _silu(x):
    return x * _sigmoid(x)


def _gelu_tanh(x):
    return 0.5 * x * (1.0 + jnp.tanh(math.sqrt(2.0 / math.pi) * (x + 0.044715 * (x * x * x))))


def _resident(shape):
    zeros = (0,) * len(shape)
    return pl.BlockSpec(shape, lambda *_: zeros, pipeline_mode=pl.Buffered(1))


def _chunk_major_spec(row_tile, seq, width):
    tiles_per_seq = seq // row_tile
    return pl.BlockSpec((row_tile // S5_CHUNK, None, S5_CHUNK, width),
                        lambda i: (i % tiles_per_seq, i // tiles_per_seq, 0, 0))


def _inproj_body(x_ref, nw_ref, w_ref, u_ref, gs_ref, p_ref, gh_ref, m_ref, *, bounds, col_chunk):
    x = x_ref[...]
    ms = jnp.mean(x * x, axis=-1, keepdims=True)
    xb = (x * lax.rsqrt(ms + RMS_EPS) * nw_ref[...]).astype(BF16)
    outs = ((u_ref, None), (gs_ref, _silu), (p_ref, None), (gh_ref, _silu), (m_ref, _sigmoid))
    for (out_ref, act), lo, hi in zip(outs, bounds[:-1], bounds[1:]):
        for c0 in range(lo, hi, col_chunk):
            y = jnp.dot(xb, w_ref[:, c0:c0 + col_chunk], preferred_element_type=F32)
            if act is not None:
                y = act(y)
            if out_ref is u_ref:
                out_ref[:, :, c0 - lo:c0 - lo + col_chunk] = y.reshape(
                    y.shape[0] // S5_CHUNK, S5_CHUNK, col_chunk).astype(out_ref.dtype)
            else:
                out_ref[:, c0 - lo:c0 - lo + col_chunk] = y.astype(out_ref.dtype)


def _inproj(h2d, norm_w, w_in_bf, widths, batch, seq, *, row_tile=256, col_chunk=512):
    rows, d = h2d.shape
    bounds = [0]
    for w in widths:
        bounds.append(bounds[-1] + w)
    body = functools.partial(_inproj_body, bounds=tuple(bounds), col_chunk=col_chunk)
    tile = lambda w: pl.BlockSpec((row_tile, w), lambda i: (i, 0))
    return pl.pallas_call(
        body,
        grid=(rows // row_tile,),
        in_specs=[tile(d), _resident((1, d)), _resident(w_in_bf.shape)],
        out_specs=[_chunk_major_spec(row_tile, seq, widths[0])] + [tile(w) for w in widths[1:]],
        out_shape=[jax.ShapeDtypeStruct((seq // S5_CHUNK, batch, S5_CHUNK, widths[0]), BF16)]
        + [jax.ShapeDtypeStruct((rows, w), BF16) for w in widths[1:]],
        compiler_params=pltpu.CompilerParams(dimension_semantics=("parallel",),
                                             vmem_limit_bytes=VMEM_LIMIT_BYTES),
    )(h2d, norm_w.reshape(1, d), w_in_bf)


def _s5_body(*refs, batch, chunks, pairs, pw, sl):
    t_steps = S5_CHUNK
    x_refs = refs[:t_steps]
    win_ref, woutt_ref, a_ref, sc_ref, dv_ref, y_ref, d_scr, st_scr = refs[t_steps:]
    k = t_steps * pw
    rows = batch * chunks
    assert k == 4 * sl

    for j in range(pairs):
        dj = jnp.concatenate([x_refs[t][:, j * pw:(j + 1) * pw] for t in range(t_steps)], axis=1)
        d_scr[:, j * k:(j + 1) * k] = dj
        win = jnp.concatenate([win_ref[c, j] for c in range(4)], axis=1).astype(BF16)
        st_scr[:, j * k:(j + 1) * k] = jnp.dot(dj, win, preferred_element_type=F32)

    coef = [[jnp.broadcast_to(a_ref[j, c:c + 1, :], (batch, sl)) for c in range(4)] for j in range(pairs)]

    def step(i, carry):
        row_f = pl.ds(pl.multiple_of(i * batch, batch), batch)
        row_b = pl.ds(pl.multiple_of((chunks - 1 - i) * batch, batch), batch)
        out = []
        for j in range(pairs):
            for direction, r in ((0, row_f), (1, row_b)):
                sr, si = carry[4 * j + 2 * direction], carry[4 * j + 2 * direction + 1]
                ar, ai = coef[j][2 * direction], coef[j][2 * direction + 1]
                c_re = j * k + 2 * direction * sl
                c_im = c_re + sl
                ir, ii = st_scr[r, c_re:c_re + sl], st_scr[r, c_im:c_im + sl]
                st_scr[r, c_re:c_re + sl] = sr
                st_scr[r, c_im:c_im + sl] = si
                out += [ar * sr - ai * si + ir, ar * si + ai * sr + ii]
        return tuple(out)

    zero = jnp.zeros((batch, sl), F32)
    lax.fori_loop(0, chunks, step, (zero,) * (4 * pairs))

    in_step = lax.broadcasted_iota(jnp.int32, (k, k), 0) // pw
    out_step = lax.broadcasted_iota(jnp.int32, (k, k), 1) // pw
    for j in range(pairs):
        sc = sc_ref[j]
        kern = None
        for direction in range(2):
            wr, wi = win_ref[2 * direction, j], win_ref[2 * direction + 1, j]
            pr, pi = sc[2 * direction:2 * direction + 1, :], sc[2 * direction + 1:2 * direction + 2, :]
            wp = jnp.concatenate([wr * pr - wi * pi, wr * pi + wi * pr], axis=1).astype(BF16)
            wt = jnp.concatenate([woutt_ref[2 * direction, j], woutt_ref[2 * direction + 1, j]], axis=1)
            kd = lax.dot_general(wp, wt, _NT, preferred_element_type=F32)
            kd = jnp.where(out_step >= in_step if direction == 0 else out_step <= in_step, kd, 0.0)
            kern = kd if kern is None else kern + kd
        dj = d_scr[:, j * k:(j + 1) * k]
        wt = jnp.concatenate([woutt_ref[c, j] for c in range(4)], axis=1)
        y = jnp.dot(dj, kern.astype(BF16), preferred_element_type=F32)
        y = y + lax.dot_general(st_scr[:, j * k:(j + 1) * k].astype(BF16), wt, _NT, preferred_element_type=F32)
        y = y + dj.astype(F32) * dv_ref[j]
        st_scr[:, j * k:(j + 1) * k] = y
    for t in range(t_steps):
        y_ref[pl.ds(t, rows, stride=t_steps), :] = jnp.concatenate(
            [st_scr[:, j * k + t * pw:j * k + (t + 1) * pw] for j in range(pairs)], axis=1)


def _s5_operands(lam_re, lam_im, log_step, b_re, b_im, c_re, c_im, d_skip):
    _, groups, p = lam_re.shape
    hh, t, gp = S5_GROUP, S5_CHUNK, S5_GROUPS_PER_PAIR
    npair = groups // gp
    rep = lambda v: jnp.concatenate([v] * gp, axis=-1)
    own = (jnp.arange(gp * p)[None, :] // p == jnp.arange(groups)[:, None] % gp).astype(F32)
    lam_re, lam_im = rep(lam_re), rep(lam_im)
    dt = jnp.exp(log_step)[..., None]
    xr, xi = lam_re * dt, lam_im * dt
    lbr, lbi = jnp.exp(xr) * jnp.cos(xi), jnp.exp(xr) * jnp.sin(xi)
    nr, ni = lbr - 1.0, lbi
    den = lam_re * lam_re + lam_im * lam_im
    zr, zi = (nr * lam_re + ni * lam_im) / den * own, (ni * lam_re - nr * lam_im) / den * own
    btr, bti = rep(jnp.swapaxes(b_re, -1, -2)), rep(jnp.swapaxes(b_im, -1, -2))
    bbr = zr[:, :, None, :] * btr - zi[:, :, None, :] * bti
    bbi = zr[:, :, None, :] * bti + zi[:, :, None, :] * btr
    ccr, cci = rep(c_re) * own[None, :, None, :], rep(c_im) * own[None, :, None, :]

    def powers(e):
        kk = e[:, None, :, None]
        mag, ang = jnp.exp(kk * xr[:, :, None, :]), kk * xi[:, :, None, :]
        return (mag * jnp.cos(ang))[:, :, :, None, :], (mag * jnp.sin(ang))[:, :, :, None, :]

    tt = jnp.arange(t, dtype=F32)
    pir, pii = powers(jnp.stack([t - 1 - tt, tt]))
    por, poi = powers(jnp.stack([tt + 1, t - tt]))
    b5r, b5i = bbr[:, :, None], bbi[:, :, None]
    c5r, c5i = ccr[:, :, None], cci[:, :, None]
    win_r, win_i = pir * b5r - pii * b5i, pir * b5i + pii * b5r
    out_r, out_i = por * c5r - poi * c5i, por * c5i + poi * c5r

    def pair_layout(x_re, x_im):
        comp = jnp.stack([x_re[0], x_im[0], x_re[1], x_im[1]]).reshape(4, npair, gp, t, hh, gp * p)
        return jnp.transpose(comp, (0, 1, 3, 2, 4, 5)).reshape(4, npair, t * gp * hh, gp * p)

    def pair_vec(e):
        mag, ang = jnp.exp(e * xr) * own, e * xi
        v = jnp.stack([(mag * jnp.cos(ang))[0], (mag * jnp.sin(ang))[0],
                       (mag * jnp.cos(ang))[1], (mag * jnp.sin(ang))[1]])
        return jnp.transpose(v.reshape(4, npair, gp, gp * p).sum(axis=2), (1, 0, 2))

    dvec = jnp.broadcast_to(d_skip.reshape(npair, 1, gp, hh), (npair, t, gp, hh)).reshape(npair, 1, t * gp * hh)
    return (pair_layout(win_r, win_i), pair_layout(out_r, -out_i).astype(BF16),
            pair_vec(float(t)), pair_vec(-float(t)), dvec)


def _s5_scan(u_cm, operands, batch, seq):
    win, woutt, a, sc, dvec = operands
    _, npair, k, sl = win.shape
    t = S5_CHUNK
    pw = k // t
    pairs = LANES // pw
    chunks = seq // t
    rows = chunks * batch
    width = npair * pw
    nblk = width // LANES
    body = functools.partial(_s5_body, batch=batch, chunks=chunks, pairs=pairs, pw=pw, sl=sl)
    per_blk = lambda shape: pl.BlockSpec((pairs,) + shape, lambda q: (q, 0, 0))
    comp_blk = pl.BlockSpec((4, pairs, k, sl), lambda q: (0, q, 0, 0), pipeline_mode=pl.Buffered(1))
    x2d = u_cm.reshape(rows, t * width)
    y = pl.pallas_call(
        body,
        grid=(nblk,),
        in_specs=[pl.BlockSpec((rows, LANES), functools.partial(lambda q, s: (0, s * nblk + q), s=s))
                  for s in range(t)]
        + [comp_blk, comp_blk, per_blk((4, sl)), per_blk((4, sl)), per_blk((1, k))],
        out_specs=pl.BlockSpec((rows * t, LANES), lambda q: (0, q), pipeline_mode=pl.Buffered(1)),
        out_shape=jax.ShapeDtypeStruct((rows * t, width), F32),
        scratch_shapes=[pltpu.VMEM((rows, pairs * k), BF16), pltpu.VMEM((rows, pairs * k), F32)],
        compiler_params=pltpu.CompilerParams(dimension_semantics=("parallel",),
                                             vmem_limit_bytes=VMEM_LIMIT_BYTES),
    )(*([x2d] * t), win, woutt, a, sc, dvec)
    return y.reshape(chunks, batch, t, width)


def _dft_tables(blk):
    n = 2 * blk
    f = np.arange(blk, dtype=np.int64)
    sym = ((2 * f[:, None] + 1) * (2 * f[None, :] + 1)) % (4 * n)
    ang_sym = sym.astype(np.float64) * (2.0 * np.pi / (4 * n))
    lag = np.arange(2 * blk, dtype=np.int64) - blk
    ang_lag = (((2 * f[:, None] + 1) * lag[None, :]) % (2 * n)).astype(np.float64) * (2.0 * np.pi / (2 * n))
    tc, ts = np.cos(ang_lag), np.sin(ang_lag)
    tc[:, 0] = 0.0
    ts[:, 0] = 0.0
    as_bf = lambda x: jnp.asarray(x.astype(np.float32)).astype(BF16)
    return as_bf(np.cos(ang_sym)), as_bf(np.sin(ang_sym)), as_bf(tc), as_bf(ts)


def _position_features(seq, pad_to):
    t = np.linspace(0.0, 1.0, seq)[:, None]
    pos = np.arange(seq, dtype=np.float64)[:, None]
    bands = np.linspace(1e-4, HY_BANDS - 1, HY_BANDS)[None, :]
    ang = bands * pos * (2.0 * math.pi / seq)
    feats = np.zeros((seq, pad_to), np.float64)
    feats[:, :1 + 2 * HY_BANDS] = np.concatenate([t, np.cos(ang), -np.sin(ang)], axis=-1)
    mirror = lambda x: np.roll(x[::-1], 1, axis=0)
    f32 = lambda x: jnp.asarray(x.astype(np.float32))
    return f32(feats), f32(t), f32(mirror(feats)), f32(mirror(t))


def _hy_filter_body(feats_ref, t_ref, featsm_ref, tm_ref, w1_ref, b1_ref, fr_ref, w2_ref, b2_ref, w3f_ref,
                    w3b_ref, decf_ref, decb_ref, tc_ref, ts_ref, kr_ref, ki_ref, h2_scr, h2m_scr, k2_scr,
                    *, seq, nb):
    hi = lax.Precision.HIGHEST
    blk = seq // nb

    @pl.when(pl.program_id(0) == 0)
    def _():
        fr = fr_ref[...]

        def mlp(feats):
            h1 = jnp.sin(fr * (jnp.dot(feats, w1_ref[...], precision=hi, preferred_element_type=F32) + b1_ref[...]))
            return jnp.sin(fr * (jnp.dot(h1, w2_ref[...], precision=hi, preferred_element_type=F32) + b2_ref[...]))

        h2_scr[...] = mlp(feats_ref[...])
        h2m_scr[...] = mlp(featsm_ref[...])

    hf = jnp.dot(h2_scr[...], w3f_ref[...], precision=hi, preferred_element_type=F32)
    hf = hf * (jnp.exp(-t_ref[...] * jnp.abs(decf_ref[...])) + HY_SHIFT)
    hb = jnp.dot(h2m_scr[...], w3b_ref[...], precision=hi, preferred_element_type=F32)
    hb = hb * (jnp.exp(-tm_ref[...] * jnp.abs(decb_ref[...])) + HY_SHIFT)
    row = lax.broadcasted_iota(jnp.int32, hb.shape, 0)
    hb = jnp.where(row == 0, 0.0, hb)
    norm = jnp.sum(hf * hf, axis=0, keepdims=True) + jnp.sum(hb * hb, axis=0, keepdims=True)
    scale = lax.rsqrt(norm + HY_EPS) * (1.0 / blk)
    k2_scr[0:seq, :] = hb.astype(BF16)
    k2_scr[seq:2 * seq, :] = hf.astype(BF16)
    for d in range(2 * nb - 1):
        seg = k2_scr[d * blk:(d + 2) * blk, :]
        kr_ref[0, d] = jnp.dot(tc_ref[...], seg, preferred_element_type=F32) * scale
        ki_ref[0, d] = jnp.dot(ts_ref[...], seg, preferred_element_type=F32) * scale


def _hy_filters(seq, w1, b1, freq, w2, b2, w3, decay, width, tables, nb, *, col_block=256):
    _, _, tc, ts = tables
    blk = seq // nb
    ffn = w2.shape[0]
    feats, t, featsm, tm = _position_features(seq, LANES)
    w1p = jnp.zeros((LANES, ffn), F32).at[:w1.shape[0]].set(w1)
    nblk = width // col_block
    side = lambda rows, direction: pl.BlockSpec(
        (rows, col_block), lambda i: (0, (2 * (i // nblk) + direction) * nblk + i % nblk))
    out = pl.BlockSpec((1, 2 * nb - 1, blk, col_block), lambda i: (i // nblk, 0, 0, i % nblk))
    body = functools.partial(_hy_filter_body, seq=seq, nb=nb)
    row2 = lambda v: v.reshape(1, -1)
    return pl.pallas_call(
        body,
        grid=(HY_ORDER * nblk,),
        in_specs=[_resident(feats.shape), _resident(t.shape), _resident(feats.shape), _resident(t.shape),
                  _resident(w1p.shape), _resident((1, ffn)), _resident((1, ffn)), _resident(w2.shape),
                  _resident((1, ffn)), side(ffn, 0), side(ffn, 1), side(1, 0), side(1, 1),
                  _resident(tc.shape), _resident(ts.shape)],
        out_specs=[out, out],
        out_shape=[jax.ShapeDtypeStruct((HY_ORDER, 2 * nb - 1, blk, width), F32)] * 2,
        scratch_shapes=[pltpu.VMEM((seq, ffn), F32), pltpu.VMEM((seq, ffn), F32),
                        pltpu.VMEM((2 * seq, col_block), BF16)],
        compiler_params=pltpu.CompilerParams(dimension_semantics=("arbitrary",),
                                             vmem_limit_bytes=VMEM_LIMIT_BYTES),
    )(feats, t, featsm, tm, w1p, row2(b1), row2(freq), w2, row2(b2), w3, w3, row2(decay), row2(decay), tc, ts)


def _hy_conv_body(pv_ref, p1_ref, p2_ref, g_ref, cw_ref, cb_ref, kr_ref, ki_ref, d_ref, mc_ref, ms_ref, o_ref,
                  z_scr, g_scr, a_scr, b_scr, yr_scr, ys_scr, *, seq, nb, cw, row_chunk):
    blk = seq // nb

    def short_conv(p_ref, part, dst):
        x = p_ref[0].astype(F32)
        row = lax.broadcasted_iota(jnp.int32, x.shape, 0)
        prev = jnp.where(row == 0, 0.0, pltpu.roll(x, 1, 0))
        nxt = jnp.where(row == seq - 1, 0.0, pltpu.roll(x, seq - 1, 0))
        w = cw_ref[:, part, :]
        y = cb_ref[part:part + 1, :] + prev * w[0:1] + x * w[1:2] + nxt * w[2:3]
        for j in range(nb):
            dst[:, j * cw:(j + 1) * cw] = y[j * blk:(j + 1) * blk, :]

    short_conv(pv_ref, 0, z_scr)
    for o, gate_ref in enumerate((p1_ref, p2_ref)):
        zb = z_scr[...].astype(BF16)
        a_scr[...] = jnp.dot(mc_ref[...], zb, preferred_element_type=F32)
        b_scr[...] = jnp.dot(ms_ref[...], zb, preferred_element_type=F32)

        def pointwise(c, _):
            r = pl.ds(pl.multiple_of(c * row_chunk, row_chunk), row_chunk)
            a = [a_scr[r, j * cw:(j + 1) * cw] for j in range(nb)]
            b = [b_scr[r, j * cw:(j + 1) * cw] for j in range(nb)]
            for i in range(nb):
                yr = ys = None
                for j in range(nb):
                    kr, ki = kr_ref[o, i - j + nb - 1, r, :], ki_ref[o, i - j + nb - 1, r, :]
                    tr, ts = a[j] * kr - b[j] * ki, a[j] * ki + b[j] * kr
                    yr, ys = (tr, ts) if yr is None else (yr + tr, ys + ts)
                yr_scr[r, i * cw:(i + 1) * cw] = yr.astype(BF16)
                ys_scr[r, i * cw:(i + 1) * cw] = ys.astype(BF16)
            return 0

        lax.fori_loop(0, blk // row_chunk, pointwise, 0)
        conv = (jnp.dot(mc_ref[...], yr_scr[...], preferred_element_type=F32)
                + jnp.dot(ms_ref[...], ys_scr[...], preferred_element_type=F32))
        short_conv(gate_ref, o + 1, g_scr)
        dsk = jnp.concatenate([d_ref[o:o + 1, :]] * nb, axis=1)
        z = g_scr[...] * (conv + dsk * z_scr[...])
        if o == HY_ORDER - 1:
            for j in range(nb):
                rows = slice(j * blk, (j + 1) * blk)
                o_ref[0, rows, :] = (z[:, j * cw:(j + 1) * cw] * g_ref[0, rows, :].astype(F32)).astype(o_ref.dtype)
        else:
            z_scr[...] = z


def _hy_conv(p_hy, gate_silu, conv_w, conv_b, kr, ki, d_skip, tables, batch, seq, width, nb, *, col_block=LANES,
             row_chunk=16):
    mc, ms, _, _ = tables
    blk = seq // nb
    nblk = width // col_block
    part = lambda k: pl.BlockSpec((1, seq, col_block), lambda c, b: (b, 0, k * nblk + c))
    cw = conv_w.reshape(conv_w.shape[0], HY_ORDER + 1, width)
    cb = conv_b.reshape(HY_ORDER + 1, width)
    filt = pl.BlockSpec((HY_ORDER, 2 * nb - 1, blk, col_block), lambda c, b: (0, 0, 0, c),
                        pipeline_mode=pl.Buffered(1))
    body = functools.partial(_hy_conv_body, seq=seq, nb=nb, cw=col_block, row_chunk=row_chunk)
    wide = nb * col_block
    return pl.pallas_call(
        body,
        grid=(nblk, batch),
        in_specs=[part(0), part(1), part(2),
                  pl.BlockSpec((1, seq, col_block), lambda c, b: (b, 0, c)),
                  pl.BlockSpec((cw.shape[0], HY_ORDER + 1, col_block), lambda c, b: (0, 0, c)),
                  pl.BlockSpec((HY_ORDER + 1, col_block), lambda c, b: (0, c)),
                  filt, filt,
                  pl.BlockSpec((HY_ORDER, col_block), lambda c, b: (0, c)),
                  _resident(mc.shape), _resident(ms.shape)],
        out_specs=pl.BlockSpec((1, seq, col_block), lambda c, b: (b, 0, c)),
        out_shape=jax.ShapeDtypeStruct((batch, seq, width), BF16),
        scratch_shapes=[pltpu.VMEM((blk, wide), F32), pltpu.VMEM((blk, wide), F32),
                        pltpu.VMEM((blk, wide), F32), pltpu.VMEM((blk, wide), F32),
                        pltpu.VMEM((blk, wide), BF16), pltpu.VMEM((blk, wide), BF16)],
        compiler_params=pltpu.CompilerParams(dimension_semantics=("parallel", "parallel"),
                                             vmem_limit_bytes=VMEM_LIMIT_BYTES),
    )(p_hy, p_hy, p_hy, gate_silu, cw, cb, kr, ki, d_skip, mc, ms)


def _merge_body(ys_ref, gs_ref, yh_ref, m_ref, h_ref, wg_ref, bg_ref, wbs_ref, wbh_ref, wo_ref, fw_ref, o_ref,
                *, d_model, final_norm):
    ys = ys_ref[...]
    y = _gelu_tanh(ys.reshape(ys.shape[0] * ys.shape[1], ys.shape[2]))
    glu = y * _sigmoid(jnp.dot(y.astype(BF16), wg_ref[...], preferred_element_type=F32) + bg_ref[...])
    s5 = (glu * gs_ref[...].astype(F32)).astype(BF16)
    y_s5 = jnp.dot(s5, wbs_ref[...], preferred_element_type=F32)
    y_hy = jnp.dot(yh_ref[...], wbh_ref[...], preferred_element_type=F32)
    merged = m_ref[:, :d_model].astype(F32) * y_s5 + m_ref[:, d_model:].astype(F32) * y_hy
    h = h_ref[...] + jnp.dot(merged.astype(BF16), wo_ref[...], preferred_element_type=F32)
    if final_norm:
        ms = jnp.mean(h * h, axis=-1, keepdims=True)
        h = h * lax.rsqrt(ms + RMS_EPS) * fw_ref[...]
    o_ref[...] = h


def _merge(y_s5_cm, gs, y_hypre, m, h2d, w_glu, b_glu, w_bs, w_bh, w_out, final_w, final_norm, seq, *,
           row_tile=256):
    rows, d = h2d.shape
    sw = y_s5_cm.shape[-1]
    hw = y_hypre.shape[1]
    tile = lambda w: pl.BlockSpec((row_tile, w), lambda i: (i, 0))
    body = functools.partial(_merge_body, d_model=d, final_norm=final_norm)
    return pl.pallas_call(
        body,
        grid=(rows // row_tile,),
        in_specs=[_chunk_major_spec(row_tile, seq, sw), tile(sw), tile(hw), tile(2 * d), tile(d),
                  _resident(w_glu.shape), _resident((1, sw)), _resident(w_bs.shape), _resident(w_bh.shape),
                  _resident(w_out.shape), _resident((1, d))],
        out_specs=tile(d),
        out_shape=jax.ShapeDtypeStruct((rows, d), F32),
        compiler_params=pltpu.CompilerParams(dimension_semantics=("parallel",),
                                             vmem_limit_bytes=VMEM_LIMIT_BYTES),
    )(y_s5_cm, gs, y_hypre, m, h2d, w_glu, b_glu.reshape(1, sw), w_bs, w_bh, w_out, final_w.reshape(1, d))


def kernel(x, norm_w, w_in, s5_lam_re, s5_lam_im, s5_log_step, s5_b_re, s5_b_im, s5_c_re, s5_c_im, s5_d, s5_w_glu, s5_b_glu, hy_conv_w, hy_conv_b, hy_w1, hy_b1, hy_freq, hy_w2, hy_b2, hy_w3, hy_decay, hy_d, w_branch_s5, w_branch_hy, w_out, final_norm_w):
    batch, seq, d_model = x.shape
    depth = w_in.shape[0]
    s5_width = s5_d.shape[1]
    hy_width = hy_d.shape[2]
    widths = (s5_width, s5_width, (HY_ORDER + 1) * hy_width, hy_width, 2 * d_model)
    nb = HY_TIME_BLOCKS
    tables = _dft_tables(seq // nb)
    h = x.reshape(batch * seq, d_model)
    for l in range(depth):
        u, gs, p_hy, gh, m = _inproj(h, norm_w[l], w_in[l].astype(BF16), widths, batch, seq)
        s5_ops = _s5_operands(s5_lam_re[l], s5_lam_im[l], s5_log_step[l], s5_b_re[l], s5_b_im[l],
                              s5_c_re[l], s5_c_im[l], s5_d[l])
        y_s5 = _s5_scan(u, s5_ops, batch, seq)
        kr, ki = _hy_filters(seq, hy_w1[l], hy_b1[l], hy_freq[l], hy_w2[l], hy_b2[l], hy_w3[l], hy_decay[l],
                             hy_width, tables, nb)
        y_hypre = _hy_conv(p_hy.reshape(batch, seq, -1), gh.reshape(batch, seq, -1), hy_conv_w[l], hy_conv_b[l],
                           kr, ki, hy_d[l], tables, batch, seq, hy_width, nb)
        h = _merge(y_s5, gs, y_hypre.reshape(batch * seq, hy_width), m, h,
                   s5_w_glu[l].astype(BF16), s5_b_glu[l], w_branch_s5[l].astype(BF16),
                   w_branch_hy[l].astype(BF16), w_out[l].astype(BF16), final_norm_w, l == depth - 1, seq)
    return h.reshape(batch, seq, d_model)
```

```python
import functools
import math

import numpy as np
import jax
import jax.numpy as jnp
from jax import lax
from jax.experimental import pallas as pl
from jax.experimental.pallas import tpu as pltpu

F32 = jnp.float32
BF16 = jnp.bfloat16

RMS_EPS = 1e-6
S5_GROUP = 16
S5_CHUNK = 16
S5_GROUPS_PER_PAIR = 2
LANES = 128
HY_ORDER = 2
HY_BANDS = 16
HY_SHIFT = 0.05
HY_EPS = 1e-6
HY_TIME_BLOCKS = 4
VMEM_LIMIT_BYTES = 56 * 1024 * 1024

_NT = (((1,), (1,)), ((), ()))


def _sigmoid(x):
    return 1.0 / (1.0 + jnp.exp(-x))


def _silu(x):
    return x * _sigmoid(x)


def _gelu_tanh(x):
    return 0.5 * x * (1.0 + jnp.tanh(math.sqrt(2.0 / math.pi) * (x + 0.044715 * (x * x * x))))


def _resident(shape):
    zeros = (0,) * len(shape)
    return pl.BlockSpec(shape, lambda *_: zeros, pipeline_mode=pl.Buffered(1))


def _chunk_major_spec(row_tile, seq, width):
    tiles_per_seq = seq // row_tile
    return pl.BlockSpec((row_tile // S5_CHUNK, None, S5_CHUNK, width),
                        lambda i: (i % tiles_per_seq, i // tiles_per_seq, 0, 0))


def _inproj_body(x_ref, nw_ref, w_ref, u_ref, gs_ref, p_ref, gh_ref, m_ref, *, bounds, col_chunk):
    x = x_ref[...]
    ms = jnp.mean(x * x, axis=-1, keepdims=True)
    xb = (x * lax.rsqrt(ms + RMS_EPS) * nw_ref[...]).astype(BF16)
    outs = ((u_ref, None), (gs_ref, _silu), (p_ref, None), (gh_ref, _silu), (m_ref, _sigmoid))
    for (out_ref, act), lo, hi in zip(outs, bounds[:-1], bounds[1:]):
        for c0 in range(lo, hi, col_chunk):
            y = jnp.dot(xb, w_ref[:, c0:c0 + col_chunk], preferred_element_type=F32)
            if act is not None:
                y = act(y)
            if out_ref is u_ref:
                out_ref[:, :, c0 - lo:c0 - lo + col_chunk] = y.astype(out_ref.dtype).reshape(
                    y.shape[0] // S5_CHUNK, S5_CHUNK, col_chunk)
            else:
                out_ref[:, c0 - lo:c0 - lo + col_chunk] = y.astype(out_ref.dtype)


def _inproj(h2d, norm_w, w_in_bf, widths, batch, seq, *, row_tile=512, col_chunk=512):
    rows, d = h2d.shape
    bounds = [0]
    for w in widths:
        bounds.append(bounds[-1] + w)
    body = functools.partial(_inproj_body, bounds=tuple(bounds), col_chunk=col_chunk)
    tile = lambda w: pl.BlockSpec((row_tile, w), lambda i: (i, 0))
    return pl.pallas_call(
        body,
        grid=(rows // row_tile,),
        in_specs=[tile(d), _resident((1, d)), _resident(w_in_bf.shape)],
        out_specs=[_chunk_major_spec(row_tile, seq, widths[0])] + [tile(w) for w in widths[1:]],
        out_shape=[jax.ShapeDtypeStruct((seq // S5_CHUNK, batch, S5_CHUNK, widths[0]), BF16)]
        + [jax.ShapeDtypeStruct((rows, w), BF16) for w in widths[1:]],
        compiler_params=pltpu.CompilerParams(dimension_semantics=("parallel",),
                                             vmem_limit_bytes=VMEM_LIMIT_BYTES),
    )(h2d, norm_w.reshape(1, d), w_in_bf)


def _s5_body(*refs, batch, chunks, pairs, pw, sl):
    t_steps = S5_CHUNK
    x_refs = refs[:t_steps]
    win_ref, woutt_ref, a_ref, sc_ref, dv_ref, y_ref, d_scr, st_scr = refs[t_steps:]
    k = t_steps * pw
    rows = batch * chunks
    assert k == 4 * sl

    for j in range(pairs):
        dj = jnp.concatenate([x_refs[t][:, j * pw:(j + 1) * pw] for t in range(t_steps)], axis=1)
        d_scr[:, j * k:(j + 1) * k] = dj
        win = jnp.concatenate([win_ref[c, j] for c in range(4)], axis=1).astype(BF16)
        st_scr[:, j * k:(j + 1) * k] = jnp.dot(dj, win, preferred_element_type=F32)

    coef = [[jnp.broadcast_to(a_ref[j, c:c + 1, :], (batch, sl)) for c in range(4)] for j in range(pairs)]

    def step(i, carry):
        row_f = pl.ds(pl.multiple_of(i * batch, batch), batch)
        row_b = pl.ds(pl.multiple_of((chunks - 1 - i) * batch, batch), batch)
        out = []
        for j in range(pairs):
            for direction, r in ((0, row_f), (1, row_b)):
                sr, si = carry[4 * j + 2 * direction], carry[4 * j + 2 * direction + 1]
                ar, ai = coef[j][2 * direction], coef[j][2 * direction + 1]
                c_re = j * k + 2 * direction * sl
                c_im = c_re + sl
                ir, ii = st_scr[r, c_re:c_re + sl], st_scr[r, c_im:c_im + sl]
                st_scr[r, c_re:c_re + sl] = sr
                st_scr[r, c_im:c_im + sl] = si
                out += [ar * sr - ai * si + ir, ar * si + ai * sr + ii]
        return tuple(out)

    zero = jnp.zeros((batch, sl), F32)
    lax.fori_loop(0, chunks, step, (zero,) * (4 * pairs))

    in_step = lax.broadcasted_iota(jnp.int32, (k, k), 0) // pw
    out_step = lax.broadcasted_iota(jnp.int32, (k, k), 1) // pw
    for j in range(pairs):
        sc = sc_ref[j]
        kern = None
        for direction in range(2):
            wr, wi = win_ref[2 * direction, j], win_ref[2 * direction + 1, j]
            pr, pi = sc[2 * direction:2 * direction + 1, :], sc[2 * direction + 1:2 * direction + 2, :]
            wp = jnp.concatenate([wr * pr - wi * pi, wr * pi + wi * pr], axis=1).astype(BF16)
            wt = jnp.concatenate([woutt_ref[2 * direction, j], woutt_ref[2 * direction + 1, j]], axis=1)
            kd = lax.dot_general(wp, wt, _NT, preferred_element_type=F32)
            kd = jnp.where(out_step >= in_step if direction == 0 else out_step <= in_step, kd, 0.0)
            kern = kd if kern is None else kern + kd
        dj = d_scr[:, j * k:(j + 1) * k]
        wt = jnp.concatenate([woutt_ref[c, j] for c in range(4)], axis=1)
        y = jnp.dot(dj, kern.astype(BF16), preferred_element_type=F32)
        y = y + lax.dot_general(st_scr[:, j * k:(j + 1) * k].astype(BF16), wt, _NT, preferred_element_type=F32)
        y = y + dj.astype(F32) * dv_ref[j]
        st_scr[:, j * k:(j + 1) * k] = y
    for t in range(t_steps):
        y_ref[pl.ds(t, rows, stride=t_steps), :] = jnp.concatenate(
            [st_scr[:, j * k + t * pw:j * k + (t + 1) * pw] for j in range(pairs)], axis=1)


def _s5_operands(lam_re, lam_im, log_step, b_re, b_im, c_re, c_im, d_skip):
    _, groups, p = lam_re.shape
    hh, t, gp = S5_GROUP, S5_CHUNK, S5_GROUPS_PER_PAIR
    npair = groups // gp
    rep = lambda v: jnp.concatenate([v] * gp, axis=-1)
    own = (jnp.arange(gp * p)[None, :] // p == jnp.arange(groups)[:, None] % gp).astype(F32)
    lam_re, lam_im = rep(lam_re), rep(lam_im)
    dt = jnp.exp(log_step)[..., None]
    xr, xi = lam_re * dt, lam_im * dt
    lbr, lbi = jnp.exp(xr) * jnp.cos(xi), jnp.exp(xr) * jnp.sin(xi)
    nr, ni = lbr - 1.0, lbi
    den = lam_re * lam_re + lam_im * lam_im
    zr, zi = (nr * lam_re + ni * lam_im) / den * own, (ni * lam_re - nr * lam_im) / den * own
    btr, bti = rep(jnp.swapaxes(b_re, -1, -2)), rep(jnp.swapaxes(b_im, -1, -2))
    bbr = zr[:, :, None, :] * btr - zi[:, :, None, :] * bti
    bbi = zr[:, :, None, :] * bti + zi[:, :, None, :] * btr
    ccr, cci = rep(c_re) * own[None, :, None, :], rep(c_im) * own[None, :, None, :]

    def powers(e):
        kk = e[:, None, :, None]
        mag, ang = jnp.exp(kk * xr[:, :, None, :]), kk * xi[:, :, None, :]
        return (mag * jnp.cos(ang))[:, :, :, None, :], (mag * jnp.sin(ang))[:, :, :, None, :]

    tt = jnp.arange(t, dtype=F32)
    pir, pii = powers(jnp.stack([t - 1 - tt, tt]))
    por, poi = powers(jnp.stack([tt + 1, t - tt]))
    b5r, b5i = bbr[:, :, None], bbi[:, :, None]
    c5r, c5i = ccr[:, :, None], cci[:, :, None]
    win_r, win_i = pir * b5r - pii * b5i, pir * b5i + pii * b5r
    out_r, out_i = por * c5r - poi * c5i, por * c5i + poi * c5r

    def pair_layout(x_re, x_im):
        comp = jnp.stack([x_re[0], x_im[0], x_re[1], x_im[1]]).reshape(4, npair, gp, t, hh, gp * p)
        return jnp.transpose(comp, (0, 1, 3, 2, 4, 5)).reshape(4, npair, t * gp * hh, gp * p)

    def pair_vec(e):
        mag, ang = jnp.exp(e * xr) * own, e * xi
        v = jnp.stack([(mag * jnp.cos(ang))[0], (mag * jnp.sin(ang))[0],
                       (mag * jnp.cos(ang))[1], (mag * jnp.sin(ang))[1]])
        return jnp.transpose(v.reshape(4, npair, gp, gp * p).sum(axis=2), (1, 0, 2))

    dvec = jnp.broadcast_to(d_skip.reshape(npair, 1, gp, hh), (npair, t, gp, hh)).reshape(npair, 1, t * gp * hh)
    return (pair_layout(win_r, win_i), pair_layout(out_r, -out_i).astype(BF16),
            pair_vec(float(t)), pair_vec(-float(t)), dvec)


def _s5_scan(u_cm, operands, batch, seq):
    win, woutt, a, sc, dvec = operands
    _, npair, k, sl = win.shape
    t = S5_CHUNK
    pw = k // t
    pairs = LANES // pw
    chunks = seq // t
    rows = chunks * batch
    width = npair * pw
    nblk = width // LANES
    body = functools.partial(_s5_body, batch=batch, chunks=chunks, pairs=pairs, pw=pw, sl=sl)
    per_blk = lambda shape: pl.BlockSpec((pairs,) + shape, lambda q: (q, 0, 0))
    comp_blk = pl.BlockSpec((4, pairs, k, sl), lambda q: (0, q, 0, 0), pipeline_mode=pl.Buffered(1))
    x2d = u_cm.reshape(rows, t * width)
    y = pl.pallas_call(
        body,
        grid=(nblk,),
        in_specs=[pl.BlockSpec((rows, LANES), functools.partial(lambda q, s: (0, s * nblk + q), s=s))
                  for s in range(t)]
        + [comp_blk, comp_blk, per_blk((4, sl)), per_blk((4, sl)), per_blk((1, k))],
        out_specs=pl.BlockSpec((rows * t, LANES), lambda q: (0, q), pipeline_mode=pl.Buffered(1)),
        out_shape=jax.ShapeDtypeStruct((rows * t, width), F32),
        scratch_shapes=[pltpu.VMEM((rows, pairs * k), BF16), pltpu.VMEM((rows, pairs * k), F32)],
        compiler_params=pltpu.CompilerParams(dimension_semantics=("parallel",),
                                             vmem_limit_bytes=VMEM_LIMIT_BYTES),
    )(*([x2d] * t), win, woutt, a, sc, dvec)
    return y.reshape(chunks, batch, t, width)


def _dft_tables(blk):
    n = 2 * blk
    f = np.arange(blk, dtype=np.int64)
    sym = ((2 * f[:, None] + 1) * (2 * f[None, :] + 1)) % (4 * n)
    ang_sym = sym.astype(np.float64) * (2.0 * np.pi / (4 * n))
    lag = np.arange(2 * blk, dtype=np.int64) - blk
    ang_lag = (((2 * f[:, None] + 1) * lag[None, :]) % (2 * n)).astype(np.float64) * (2.0 * np.pi / (2 * n))
    tc, ts = np.cos(ang_lag), np.sin(ang_lag)
    tc[:, 0] = 0.0
    ts[:, 0] = 0.0
    as_bf = lambda x: jnp.asarray(x.astype(np.float32)).astype(BF16)
    return as_bf(np.cos(ang_sym)), as_bf(np.sin(ang_sym)), as_bf(tc), as_bf(ts)


def _position_features(seq, pad_to):
    t = np.linspace(0.0, 1.0, seq)[:, None]
    pos = np.arange(seq, dtype=np.float64)[:, None]
    bands = np.linspace(1e-4, HY_BANDS - 1, HY_BANDS)[None, :]
    ang = bands * pos * (2.0 * math.pi / seq)
    feats = np.zeros((seq, pad_to), np.float64)
    feats[:, :1 + 2 * HY_BANDS] = np.concatenate([t, np.cos(ang), -np.sin(ang)], axis=-1)
    mirror = lambda x: np.roll(x[::-1], 1, axis=0)
    f32 = lambda x: jnp.asarray(x.astype(np.float32))
    return f32(feats), f32(t), f32(mirror(feats)), f32(mirror(t))


def _hy_filter_body(feats_ref, t_ref, featsm_ref, tm_ref, w1_ref, b1_ref, fr_ref, w2_ref, b2_ref, w3f_ref,
                    w3b_ref, decf_ref, decb_ref, tc_ref, ts_ref, kr_ref, ki_ref, h2_scr, h2m_scr, k2_scr,
                    *, seq, nb):
    hi = lax.Precision.HIGHEST
    blk = seq // nb

    @pl.when(pl.program_id(0) == 0)
    def _():
        fr = fr_ref[...]

        def mlp(feats):
            h1 = jnp.sin(fr * (jnp.dot(feats, w1_ref[...], precision=hi, preferred_element_type=F32) + b1_ref[...]))
            return jnp.sin(fr * (jnp.dot(h1, w2_ref[...], precision=hi, preferred_element_type=F32) + b2_ref[...]))

        h2_scr[...] = mlp(feats_ref[...])
        h2m_scr[...] = mlp(featsm_ref[...])

    hf = jnp.dot(h2_scr[...], w3f_ref[...], precision=hi, preferred_element_type=F32)
    hf = hf * (jnp.exp(-t_ref[...] * jnp.abs(decf_ref[...])) + HY_SHIFT)
    hb = jnp.dot(h2m_scr[...], w3b_ref[...], precision=hi, preferred_element_type=F32)
    hb = hb * (jnp.exp(-tm_ref[...] * jnp.abs(decb_ref[...])) + HY_SHIFT)
    row = lax.broadcasted_iota(jnp.int32, hb.shape, 0)
    hb = jnp.where(row == 0, 0.0, hb)
    norm = jnp.sum(hf * hf, axis=0, keepdims=True) + jnp.sum(hb * hb, axis=0, keepdims=True)
    scale = lax.rsqrt(norm + HY_EPS) * (1.0 / blk)
    k2_scr[0:seq, :] = hb.astype(BF16)
    k2_scr[seq:2 * seq, :] = hf.astype(BF16)
    for d in range(2 * nb - 1):
        seg = k2_scr[d * blk:(d + 2) * blk, :]
        kr_ref[0, d] = jnp.dot(tc_ref[...], seg, preferred_element_type=F32) * scale
        ki_ref[0, d] = jnp.dot(ts_ref[...], seg, preferred_element_type=F32) * scale


def _hy_filters(seq, w1, b1, freq, w2, b2, w3, decay, width, tables, nb, *, col_block=256):
    _, _, tc, ts = tables
    blk = seq // nb
    ffn = w2.shape[0]
    feats, t, featsm, tm = _position_features(seq, LANES)
    w1p = jnp.zeros((LANES, ffn), F32).at[:w1.shape[0]].set(w1)
    nblk = width // col_block
    side = lambda rows, direction: pl.BlockSpec(
        (rows, col_block), lambda i: (0, (2 * (i // nblk) + direction) * nblk + i % nblk))
    out = pl.BlockSpec((1, 2 * nb - 1, blk, col_block), lambda i: (i // nblk, 0, 0, i % nblk))
    body = functools.partial(_hy_filter_body, seq=seq, nb=nb)
    row2 = lambda v: v.reshape(1, -1)
    return pl.pallas_call(
        body,
        grid=(HY_ORDER * nblk,),
        in_specs=[_resident(feats.shape), _resident(t.shape), _resident(feats.shape), _resident(t.shape),
                  _resident(w1p.shape), _resident((1, ffn)), _resident((1, ffn)), _resident(w2.shape),
                  _resident((1, ffn)), side(ffn, 0), side(ffn, 1), side(1, 0), side(1, 1),
                  _resident(tc.shape), _resident(ts.shape)],
        out_specs=[out, out],
        out_shape=[jax.ShapeDtypeStruct((HY_ORDER, 2 * nb - 1, blk, width), F32)] * 2,
        scratch_shapes=[pltpu.VMEM((seq, ffn), F32), pltpu.VMEM((seq, ffn), F32),
                        pltpu.VMEM((2 * seq, col_block), BF16)],
        compiler_params=pltpu.CompilerParams(dimension_semantics=("arbitrary",),
                                             vmem_limit_bytes=VMEM_LIMIT_BYTES),
    )(feats, t, featsm, tm, w1p, row2(b1), row2(freq), w2, row2(b2), w3, w3, row2(decay), row2(decay), tc, ts)


def _hy_conv_body(pv_ref, p1_ref, p2_ref, g_ref, cw_ref, cb_ref, kr_ref, ki_ref, d_ref, mc_ref, ms_ref, o_ref,
                  z_scr, g_scr, a_scr, b_scr, yr_scr, ys_scr, *, seq, nb, cw, row_chunk, problems):
    blk = seq // nb
    gate_refs = (p1_ref, p2_ref)

    def short_conv(k, p_ref, part, dst):
        x = p_ref[k].astype(F32)
        row = lax.broadcasted_iota(jnp.int32, x.shape, 0)
        prev = jnp.where(row == 0, 0.0, pltpu.roll(x, 1, 0))
        nxt = jnp.where(row == seq - 1, 0.0, pltpu.roll(x, seq - 1, 0))
        w = cw_ref[:, part, :]
        y = cb_ref[part:part + 1, :] + prev * w[0:1] + x * w[1:2] + nxt * w[2:3]
        for j in range(nb):
            dst[k, :, j * cw:(j + 1) * cw] = y[j * blk:(j + 1) * blk, :]

    def forward(k, o):
        zb = z_scr[k].astype(BF16)
        a_scr[k] = jnp.dot(mc_ref[...], zb, preferred_element_type=F32)
        b_scr[k] = jnp.dot(ms_ref[...], zb, preferred_element_type=F32)

    def pointwise(k, o):
        for c in range(blk // row_chunk):
            r = slice(c * row_chunk, (c + 1) * row_chunk)
            a = [a_scr[k, r, j * cw:(j + 1) * cw] for j in range(nb)]
            b = [b_scr[k, r, j * cw:(j + 1) * cw] for j in range(nb)]
            for i in range(nb):
                yr = ys = None
                for j in range(nb):
                    kr, ki = kr_ref[o, i - j + nb - 1, r, :], ki_ref[o, i - j + nb - 1, r, :]
                    tr, ts = a[j] * kr - b[j] * ki, a[j] * ki + b[j] * kr
                    yr, ys = (tr, ts) if yr is None else (yr + tr, ys + ts)
                yr_scr[k, r, i * cw:(i + 1) * cw] = yr.astype(BF16)
                ys_scr[k, r, i * cw:(i + 1) * cw] = ys.astype(BF16)

    def inverse(k, o):
        conv = (jnp.dot(mc_ref[...], yr_scr[k], preferred_element_type=F32)
                + jnp.dot(ms_ref[...], ys_scr[k], preferred_element_type=F32))
        short_conv(k, gate_refs[o], o + 1, g_scr)
        dsk = jnp.concatenate([d_ref[o:o + 1, :]] * nb, axis=1)
        z = g_scr[k] * (conv + dsk * z_scr[k])
        if o == HY_ORDER - 1:
            for j in range(nb):
                rows = slice(j * blk, (j + 1) * blk)
                o_ref[k, rows, :] = (z[:, j * cw:(j + 1) * cw] * g_ref[k, rows, :].astype(F32)).astype(o_ref.dtype)
        else:
            z_scr[k] = z

    stages = [lambda k: short_conv(k, pv_ref, 0, z_scr)]
    for o in range(HY_ORDER):
        stages += [functools.partial(forward, o=o), functools.partial(pointwise, o=o),
                   functools.partial(inverse, o=o)]
    for s in range(len(stages) + problems - 1):
        for k in range(problems):
            if 0 <= s - k < len(stages):
                stages[s - k](k)


def _hy_conv(p_hy, gate_silu, conv_w, conv_b, kr, ki, d_skip, tables, batch, seq, width, nb, *, col_block=LANES,
             row_chunk=16, problems=2):
    mc, ms, _, _ = tables
    blk = seq // nb
    nblk = width // col_block
    part = lambda k: pl.BlockSpec((problems, seq, col_block), lambda c, b: (b, 0, k * nblk + c))
    cw = conv_w.reshape(conv_w.shape[0], HY_ORDER + 1, width)
    cb = conv_b.reshape(HY_ORDER + 1, width)
    filt = pl.BlockSpec((HY_ORDER, 2 * nb - 1, blk, col_block), lambda c, b: (0, 0, 0, c),
                        pipeline_mode=pl.Buffered(1))
    body = functools.partial(_hy_conv_body, seq=seq, nb=nb, cw=col_block, row_chunk=row_chunk, problems=problems)
    wide = nb * col_block
    scratch = lambda dtype: pltpu.VMEM((problems, blk, wide), dtype)
    return pl.pallas_call(
        body,
        grid=(nblk, batch // problems),
        in_specs=[part(0), part(1), part(2),
                  pl.BlockSpec((problems, seq, col_block), lambda c, b: (b, 0, c)),
                  pl.BlockSpec((cw.shape[0], HY_ORDER + 1, col_block), lambda c, b: (0, 0, c)),
                  pl.BlockSpec((HY_ORDER + 1, col_block), lambda c, b: (0, c)),
                  filt, filt,
                  pl.BlockSpec((HY_ORDER, col_block), lambda c, b: (0, c)),
                  _resident(mc.shape), _resident(ms.shape)],
        out_specs=pl.BlockSpec((problems, seq, col_block), lambda c, b: (b, 0, c)),
        out_shape=jax.ShapeDtypeStruct((batch, seq, width), BF16),
        scratch_shapes=[scratch(F32), scratch(F32), scratch(F32), scratch(F32), scratch(BF16), scratch(BF16)],
        compiler_params=pltpu.CompilerParams(dimension_semantics=("parallel", "parallel"),
                                             vmem_limit_bytes=VMEM_LIMIT_BYTES),
    )(p_hy, p_hy, p_hy, gate_silu, cw, cb, kr, ki, d_skip, mc, ms)


def _merge_body(ys_ref, gs_ref, yh_ref, m_ref, h_ref, wg_ref, bg_ref, wbs_ref, wbh_ref, wo_ref, fw_ref, o_ref,
                *, d_model, final_norm):
    ys = ys_ref[...]
    y = _gelu_tanh(ys.reshape(ys.shape[0] * ys.shape[1], ys.shape[2]))
    glu = y * _sigmoid(jnp.dot(y.astype(BF16), wg_ref[...], preferred_element_type=F32) + bg_ref[...])
    s5 = (glu * gs_ref[...].astype(F32)).astype(BF16)
    y_s5 = jnp.dot(s5, wbs_ref[...], preferred_element_type=F32)
    y_hy = jnp.dot(yh_ref[...], wbh_ref[...], preferred_element_type=F32)
    merged = m_ref[:, :d_model].astype(F32) * y_s5 + m_ref[:, d_model:].astype(F32) * y_hy
    h = h_ref[...] + jnp.dot(merged.astype(BF16), wo_ref[...], preferred_element_type=F32)
    if final_norm:
        ms = jnp.mean(h * h, axis=-1, keepdims=True)
        h = h * lax.rsqrt(ms + RMS_EPS) * fw_ref[...]
    o_ref[...] = h


def _merge(y_s5_cm, gs, y_hypre, m, h2d, w_glu, b_glu, w_bs, w_bh, w_out, final_w, final_norm, seq, *,
           row_tile=512):
    rows, d = h2d.shape
    sw = y_s5_cm.shape[-1]
    hw = y_hypre.shape[1]
    tile = lambda w: pl.BlockSpec((row_tile, w), lambda i: (i, 0))
    body = functools.partial(_merge_body, d_model=d, final_norm=final_norm)
    return pl.pallas_call(
        body,
        grid=(rows // row_tile,),
        in_specs=[_chunk_major_spec(row_tile, seq, sw), tile(sw), tile(hw), tile(2 * d), tile(d),
                  _resident(w_glu.shape), _resident((1, sw)), _resident(w_bs.shape), _resident(w_bh.shape),
                  _resident(w_out.shape), _resident((1, d))],
        out_specs=tile(d),
        out_shape=jax.ShapeDtypeStruct((rows, d), F32),
        compiler_params=pltpu.CompilerParams(dimension_semantics=("parallel",),
                                             vmem_limit_bytes=VMEM_LIMIT_BYTES),
    )(y_s5_cm, gs, y_hypre, m, h2d, w_glu, b_glu.reshape(1, sw), w_bs, w_bh, w_out, final_w.reshape(1, d))


def kernel(x, norm_w, w_in, s5_lam_re, s5_lam_im, s5_log_step, s5_b_re, s5_b_im, s5_c_re, s5_c_im, s5_d, s5_w_glu, s5_b_glu, hy_conv_w, hy_conv_b, hy_w1, hy_b1, hy_freq, hy_w2, hy_b2, hy_w3, hy_decay, hy_d, w_branch_s5, w_branch_hy, w_out, final_norm_w):
    batch, seq, d_model = x.shape
    depth = w_in.shape[0]
    s5_width = s5_d.shape[1]
    hy_width = hy_d.shape[2]
    widths = (s5_width, s5_width, (HY_ORDER + 1) * hy_width, hy_width, 2 * d_model)
    nb = HY_TIME_BLOCKS
    tables = _dft_tables(seq // nb)
    h = x.reshape(batch * seq, d_model)
    for l in range(depth):
        u, gs, p_hy, gh, m = _inproj(h, norm_w[l], w_in[l].astype(BF16), widths, batch, seq)
        s5_ops = _s5_operands(s5_lam_re[l], s5_lam_im[l], s5_log_step[l], s5_b_re[l], s5_b_im[l],
                              s5_c_re[l], s5_c_im[l], s5_d[l])
        y_s5 = _s5_scan(u, s5_ops, batch, seq)
        kr, ki = _hy_filters(seq, hy_w1[l], hy_b1[l], hy_freq[l], hy_w2[l], hy_b2[l], hy_w3[l], hy_decay[l],
                             hy_width, tables, nb)
        y_hypre = _hy_conv(p_hy.reshape(batch, seq, -1), gh.reshape(batch, seq, -1), hy_conv_w[l], hy_conv_b[l],
                           kr, ki, hy_d[l], tables, batch, seq, hy_width, nb)
        h = _merge(y_s5, gs, y_hypre.reshape(batch * seq, hy_width), m, h,
                   s5_w_glu[l].astype(BF16), s5_b_glu[l], w_branch_s5[l].astype(BF16),
                   w_branch_hy[l].astype(BF16), w_out[l].astype(BF16), final_norm_w, l == depth - 1, seq)
    return h.reshape(batch, seq, d_model)
```

```python
import functools
import math

import numpy as np
import jax
import jax.numpy as jnp
from jax import lax
from jax.experimental import pallas as pl
from jax.experimental.pallas import tpu as pltpu

F32 = jnp.float32
BF16 = jnp.bfloat16

RMS_EPS = 1e-6
S5_GROUP = 16
S5_CHUNK = 16
S5_GROUPS_PER_PAIR = 2
LANES = 128
HY_ORDER = 2
HY_BANDS = 16
HY_SHIFT = 0.05
HY_EPS = 1e-6
HY_TIME_BLOCKS = 4
VMEM_LIMIT_BYTES = 56 * 1024 * 1024

_NT = (((1,), (1,)), ((), ()))


def _sigmoid(x):
    return 1.0 / (1.0 + jnp.exp(-x))


def _silu(x):
    return x * _sigmoid(x)


def _gelu_tanh(x):
    return 0.5 * x * (1.0 + jnp.tanh(math.sqrt(2.0 / math.pi) * (x + 0.044715 * (x * x * x))))


def _resident(shape):
    zeros = (0,) * len(shape)
    return pl.BlockSpec(shape, lambda *_: zeros, pipeline_mode=pl.Buffered(1))


def _chunk_major_spec(row_tile, seq, width):
    tiles_per_seq = seq // row_tile
    return pl.BlockSpec((width // LANES, row_tile // S5_CHUNK, None, S5_CHUNK, LANES),
                        lambda i: (0, i % tiles_per_seq, i // tiles_per_seq, 0, 0))


def _lane_blocked_spec(row_tile, width):
    return pl.BlockSpec((width // LANES, row_tile, LANES), lambda i: (0, i, 0))


def _inproj_body(x_ref, nw_ref, w_ref, u_ref, gs_ref, p_ref, gh_ref, m_ref, *, bounds, col_chunk):
    x = x_ref[...]
    ms = jnp.mean(x * x, axis=-1, keepdims=True)
    xb = (x * lax.rsqrt(ms + RMS_EPS) * nw_ref[...]).astype(BF16)
    outs = ((u_ref, None), (gs_ref, _silu), (p_ref, None), (gh_ref, _silu), (m_ref, _sigmoid))
    for (out_ref, act), lo, hi in zip(outs, bounds[:-1], bounds[1:]):
        for c0 in range(lo, hi, col_chunk):
            y = jnp.dot(xb, w_ref[:, c0:c0 + col_chunk], preferred_element_type=F32)
            if act is not None:
                y = act(y)
            y = y.astype(out_ref.dtype)
            if out_ref is gs_ref or out_ref is m_ref:
                out_ref[:, c0 - lo:c0 - lo + col_chunk] = y
                continue
            for s in range(0, col_chunk, LANES):
                slab = y[:, s:s + LANES]
                if out_ref is u_ref:
                    slab = slab.reshape(slab.shape[0] // S5_CHUNK, S5_CHUNK, LANES)
                out_ref[(c0 - lo + s) // LANES] = slab


def _inproj(h2d, norm_w, w_in_bf, widths, batch, seq, *, row_tile=256, col_chunk=512):
    rows, d = h2d.shape
    bounds = [0]
    for w in widths:
        bounds.append(bounds[-1] + w)
    body = functools.partial(_inproj_body, bounds=tuple(bounds), col_chunk=col_chunk)
    tile = lambda w: pl.BlockSpec((row_tile, w), lambda i: (i, 0))
    return pl.pallas_call(
        body,
        grid=(rows // row_tile,),
        in_specs=[tile(d), _resident((1, d)), _resident(w_in_bf.shape)],
        out_specs=[_chunk_major_spec(row_tile, seq, widths[0]), tile(widths[1]),
                   _lane_blocked_spec(row_tile, widths[2]), _lane_blocked_spec(row_tile, widths[3]),
                   tile(widths[4])],
        out_shape=[jax.ShapeDtypeStruct((widths[0] // LANES, seq // S5_CHUNK, batch, S5_CHUNK, LANES), BF16),
                   jax.ShapeDtypeStruct((rows, widths[1]), BF16),
                   jax.ShapeDtypeStruct((widths[2] // LANES, rows, LANES), BF16),
                   jax.ShapeDtypeStruct((widths[3] // LANES, rows, LANES), BF16),
                   jax.ShapeDtypeStruct((rows, widths[4]), BF16)],
        compiler_params=pltpu.CompilerParams(dimension_semantics=("parallel",),
                                             vmem_limit_bytes=VMEM_LIMIT_BYTES),
    )(h2d, norm_w.reshape(1, d), w_in_bf)


def _s5_body(*refs, batch, chunks, pairs, pw, sl):
    t_steps = S5_CHUNK
    x_ref, win_ref, woutt_ref, a_ref, sc_ref, dv_ref, y_ref, d_scr, st_scr = refs
    k = t_steps * pw
    rows = batch * chunks
    assert k == 4 * sl

    for j in range(pairs):
        dj = jnp.concatenate([x_ref[:, t * LANES + j * pw:t * LANES + (j + 1) * pw] for t in range(t_steps)],
                             axis=1)
        d_scr[:, j * k:(j + 1) * k] = dj
        win = jnp.concatenate([win_ref[c, j] for c in range(4)], axis=1).astype(BF16)
        st_scr[:, j * k:(j + 1) * k] = jnp.dot(dj, win, preferred_element_type=F32)

    coef = [[jnp.broadcast_to(a_ref[j, c:c + 1, :], (batch, sl)) for c in range(4)] for j in range(pairs)]

    def step(i, carry):
        row_f = pl.ds(pl.multiple_of(i * batch, batch), batch)
        row_b = pl.ds(pl.multiple_of((chunks - 1 - i) * batch, batch), batch)
        out = []
        for j in range(pairs):
            for direction, r in ((0, row_f), (1, row_b)):
                sr, si = carry[4 * j + 2 * direction], carry[4 * j + 2 * direction + 1]
                ar, ai = coef[j][2 * direction], coef[j][2 * direction + 1]
                c_re = j * k + 2 * direction * sl
                c_im = c_re + sl
                ir, ii = st_scr[r, c_re:c_re + sl], st_scr[r, c_im:c_im + sl]
                st_scr[r, c_re:c_re + sl] = sr
                st_scr[r, c_im:c_im + sl] = si
                out += [ar * sr - ai * si + ir, ar * si + ai * sr + ii]
        return tuple(out)

    zero = jnp.zeros((batch, sl), F32)
    lax.fori_loop(0, chunks, step, (zero,) * (4 * pairs))

    in_step = lax.broadcasted_iota(jnp.int32, (k, k), 0) // pw
    out_step = lax.broadcasted_iota(jnp.int32, (k, k), 1) // pw
    for j in range(pairs):
        sc = sc_ref[j]
        kern = None
        for direction in range(2):
            wr, wi = win_ref[2 * direction, j], win_ref[2 * direction + 1, j]
            pr, pi = sc[2 * direction:2 * direction + 1, :], sc[2 * direction + 1:2 * direction + 2, :]
            wp = jnp.concatenate([wr * pr - wi * pi, wr * pi + wi * pr], axis=1).astype(BF16)
            wt = jnp.concatenate([woutt_ref[2 * direction, j], woutt_ref[2 * direction + 1, j]], axis=1)
            kd = lax.dot_general(wp, wt, _NT, preferred_element_type=F32)
            kd = jnp.where(out_step >= in_step if direction == 0 else out_step <= in_step, kd, 0.0)
            kern = kd if kern is None else kern + kd
        dj = d_scr[:, j * k:(j + 1) * k]
        wt = jnp.concatenate([woutt_ref[c, j] for c in range(4)], axis=1)
        y = jnp.dot(dj, kern.astype(BF16), preferred_element_type=F32)
        y = y + lax.dot_general(st_scr[:, j * k:(j + 1) * k].astype(BF16), wt, _NT, preferred_element_type=F32)
        y = y + dj.astype(F32) * dv_ref[j]
        st_scr[:, j * k:(j + 1) * k] = y
    for t in range(t_steps):
        y_ref[pl.ds(t, rows, stride=t_steps), :] = jnp.concatenate(
            [st_scr[:, j * k + t * pw:j * k + (t + 1) * pw] for j in range(pairs)], axis=1)


def _s5_operands(lam_re, lam_im, log_step, b_re, b_im, c_re, c_im, d_skip):
    _, groups, p = lam_re.shape
    hh, t, gp = S5_GROUP, S5_CHUNK, S5_GROUPS_PER_PAIR
    npair = groups // gp
    rep = lambda v: jnp.concatenate([v] * gp, axis=-1)
    own = (jnp.arange(gp * p)[None, :] // p == jnp.arange(groups)[:, None] % gp).astype(F32)
    lam_re, lam_im = rep(lam_re), rep(lam_im)
    dt = jnp.exp(log_step)[..., None]
    xr, xi = lam_re * dt, lam_im * dt
    lbr, lbi = jnp.exp(xr) * jnp.cos(xi), jnp.exp(xr) * jnp.sin(xi)
    nr, ni = lbr - 1.0, lbi
    den = lam_re * lam_re + lam_im * lam_im
    zr, zi = (nr * lam_re + ni * lam_im) / den * own, (ni * lam_re - nr * lam_im) / den * own
    btr, bti = rep(jnp.swapaxes(b_re, -1, -2)), rep(jnp.swapaxes(b_im, -1, -2))
    bbr = zr[:, :, None, :] * btr - zi[:, :, None, :] * bti
    bbi = zr[:, :, None, :] * bti + zi[:, :, None, :] * btr
    ccr, cci = rep(c_re) * own[None, :, None, :], rep(c_im) * own[None, :, None, :]

    def powers(e):
        kk = e[:, None, :, None]
        mag, ang = jnp.exp(kk * xr[:, :, None, :]), kk * xi[:, :, None, :]
        return (mag * jnp.cos(ang))[:, :, :, None, :], (mag * jnp.sin(ang))[:, :, :, None, :]

    tt = jnp.arange(t, dtype=F32)
    pir, pii = powers(jnp.stack([t - 1 - tt, tt]))
    por, poi = powers(jnp.stack([tt + 1, t - tt]))
    b5r, b5i = bbr[:, :, None], bbi[:, :, None]
    c5r, c5i = ccr[:, :, None], cci[:, :, None]
    win_r, win_i = pir * b5r - pii * b5i, pir * b5i + pii * b5r
    out_r, out_i = por * c5r - poi * c5i, por * c5i + poi * c5r

    def pair_layout(x_re, x_im):
        comp = jnp.stack([x_re[0], x_im[0], x_re[1], x_im[1]]).reshape(4, npair, gp, t, hh, gp * p)
        return jnp.transpose(comp, (0, 1, 3, 2, 4, 5)).reshape(4, npair, t * gp * hh, gp * p)

    def pair_vec(e):
        mag, ang = jnp.exp(e * xr) * own, e * xi
        v = jnp.stack([(mag * jnp.cos(ang))[0], (mag * jnp.sin(ang))[0],
                       (mag * jnp.cos(ang))[1], (mag * jnp.sin(ang))[1]])
        return jnp.transpose(v.reshape(4, npair, gp, gp * p).sum(axis=2), (1, 0, 2))

    dvec = jnp.broadcast_to(d_skip.reshape(npair, 1, gp, hh), (npair, t, gp, hh)).reshape(npair, 1, t * gp * hh)
    return (pair_layout(win_r, win_i), pair_layout(out_r, -out_i).astype(BF16),
            pair_vec(float(t)), pair_vec(-float(t)), dvec)


def _s5_scan(u_cm, operands, batch, seq):
    win, woutt, a, sc, dvec = operands
    _, npair, k, sl = win.shape
    t = S5_CHUNK
    pw = k // t
    pairs = LANES // pw
    chunks = seq // t
    rows = chunks * batch
    width = npair * pw
    nblk = width // LANES
    body = functools.partial(_s5_body, batch=batch, chunks=chunks, pairs=pairs, pw=pw, sl=sl)
    per_blk = lambda shape: pl.BlockSpec((pairs,) + shape, lambda q: (q, 0, 0))
    comp_blk = pl.BlockSpec((4, pairs, k, sl), lambda q: (0, q, 0, 0), pipeline_mode=pl.Buffered(1))
    y = pl.pallas_call(
        body,
        grid=(nblk,),
        in_specs=[pl.BlockSpec((None, rows, t * LANES), lambda q: (q, 0, 0)),
                  comp_blk, comp_blk, per_blk((4, sl)), per_blk((4, sl)), per_blk((1, k))],
        out_specs=pl.BlockSpec((None, rows * t, LANES), lambda q: (q, 0, 0), pipeline_mode=pl.Buffered(1)),
        out_shape=jax.ShapeDtypeStruct((nblk, rows * t, LANES), F32),
        scratch_shapes=[pltpu.VMEM((rows, pairs * k), BF16), pltpu.VMEM((rows, pairs * k), F32)],
        compiler_params=pltpu.CompilerParams(dimension_semantics=("parallel",),
                                             vmem_limit_bytes=VMEM_LIMIT_BYTES),
    )(u_cm.reshape(nblk, rows, t * LANES), win, woutt, a, sc, dvec)
    return y.reshape(nblk, chunks, batch, t, LANES)


def _dft_tables(blk):
    n = 2 * blk
    f = np.arange(blk, dtype=np.int64)
    sym = ((2 * f[:, None] + 1) * (2 * f[None, :] + 1)) % (4 * n)
    ang_sym = sym.astype(np.float64) * (2.0 * np.pi / (4 * n))
    lag = np.arange(2 * blk, dtype=np.int64) - blk
    ang_lag = (((2 * f[:, None] + 1) * lag[None, :]) % (2 * n)).astype(np.float64) * (2.0 * np.pi / (2 * n))
    tc, ts = np.cos(ang_lag), np.sin(ang_lag)
    tc[:, 0] = 0.0
    ts[:, 0] = 0.0
    as_bf = lambda x: jnp.asarray(x.astype(np.float32)).astype(BF16)
    return as_bf(np.cos(ang_sym)), as_bf(np.sin(ang_sym)), as_bf(tc), as_bf(ts)


def _position_features(seq, pad_to):
    t = np.linspace(0.0, 1.0, seq)[:, None]
    pos = np.arange(seq, dtype=np.float64)[:, None]
    bands = np.linspace(1e-4, HY_BANDS - 1, HY_BANDS)[None, :]
    ang = bands * pos * (2.0 * math.pi / seq)
    feats = np.zeros((seq, pad_to), np.float64)
    feats[:, :1 + 2 * HY_BANDS] = np.concatenate([t, np.cos(ang), -np.sin(ang)], axis=-1)
    mirror = lambda x: np.roll(x[::-1], 1, axis=0)
    f32 = lambda x: jnp.asarray(x.astype(np.float32))
    return f32(feats), f32(t), f32(mirror(feats)), f32(mirror(t))


def _hy_filter_body(feats_ref, t_ref, featsm_ref, tm_ref, w1_ref, b1_ref, fr_ref, w2_ref, b2_ref, w3f_ref,
                    w3b_ref, decf_ref, decb_ref, tc_ref, ts_ref, kr_ref, ki_ref, h2_scr, h2m_scr, k2_scr,
                    *, seq, nb):
    hi = lax.Precision.HIGHEST
    blk = seq // nb

    @pl.when(pl.program_id(0) == 0)
    def _():
        fr = fr_ref[...]

        def mlp(feats):
            h1 = jnp.sin(fr * (jnp.dot(feats, w1_ref[...], precision=hi, preferred_element_type=F32) + b1_ref[...]))
            return jnp.sin(fr * (jnp.dot(h1, w2_ref[...], precision=hi, preferred_element_type=F32) + b2_ref[...]))

        h2_scr[...] = mlp(feats_ref[...])
        h2m_scr[...] = mlp(featsm_ref[...])

    hf = jnp.dot(h2_scr[...], w3f_ref[...], precision=hi, preferred_element_type=F32)
    hf = hf * (jnp.exp(-t_ref[...] * jnp.abs(decf_ref[...])) + HY_SHIFT)
    hb = jnp.dot(h2m_scr[...], w3b_ref[...], precision=hi, preferred_element_type=F32)
    hb = hb * (jnp.exp(-tm_ref[...] * jnp.abs(decb_ref[...])) + HY_SHIFT)
    row = lax.broadcasted_iota(jnp.int32, hb.shape, 0)
    hb = jnp.where(row == 0, 0.0, hb)
    norm = jnp.sum(hf * hf, axis=0, keepdims=True) + jnp.sum(hb * hb, axis=0, keepdims=True)
    scale = lax.rsqrt(norm + HY_EPS) * (1.0 / blk)
    k2_scr[0:seq, :] = hb.astype(BF16)
    k2_scr[seq:2 * seq, :] = hf.astype(BF16)
    for d in range(2 * nb - 1):
        seg = k2_scr[d * blk:(d + 2) * blk, :]
        kr_ref[0, d] = jnp.dot(tc_ref[...], seg, preferred_element_type=F32) * scale
        ki_ref[0, d] = jnp.dot(ts_ref[...], seg, preferred_element_type=F32) * scale


def _hy_filters(seq, w1, b1, freq, w2, b2, w3, decay, width, tables, nb, *, col_block=256):
    _, _, tc, ts = tables
    blk = seq // nb
    ffn = w2.shape[0]
    feats, t, featsm, tm = _position_features(seq, LANES)
    w1p = jnp.zeros((LANES, ffn), F32).at[:w1.shape[0]].set(w1)
    nblk = width // col_block
    side = lambda rows, direction: pl.BlockSpec(
        (rows, col_block), lambda i: (0, (2 * (i // nblk) + direction) * nblk + i % nblk))
    out = pl.BlockSpec((1, 2 * nb - 1, blk, col_block), lambda i: (i // nblk, 0, 0, i % nblk))
    body = functools.partial(_hy_filter_body, seq=seq, nb=nb)
    row2 = lambda v: v.reshape(1, -1)
    return pl.pallas_call(
        body,
        grid=(HY_ORDER * nblk,),
        in_specs=[_resident(feats.shape), _resident(t.shape), _resident(feats.shape), _resident(t.shape),
                  _resident(w1p.shape), _resident((1, ffn)), _resident((1, ffn)), _resident(w2.shape),
                  _resident((1, ffn)), side(ffn, 0), side(ffn, 1), side(1, 0), side(1, 1),
                  _resident(tc.shape), _resident(ts.shape)],
        out_specs=[out, out],
        out_shape=[jax.ShapeDtypeStruct((HY_ORDER, 2 * nb - 1, blk, width), F32)] * 2,
        scratch_shapes=[pltpu.VMEM((seq, ffn), F32), pltpu.VMEM((seq, ffn), F32),
                        pltpu.VMEM((2 * seq, col_block), BF16)],
        compiler_params=pltpu.CompilerParams(dimension_semantics=("arbitrary",),
                                             vmem_limit_bytes=VMEM_LIMIT_BYTES),
    )(feats, t, featsm, tm, w1p, row2(b1), row2(freq), w2, row2(b2), w3, w3, row2(decay), row2(decay), tc, ts)


def _hy_conv_body(pv_ref, p1_ref, p2_ref, g_ref, cw_ref, cb_ref, kr_ref, ki_ref, d_ref, mc_ref, ms_ref, o_ref,
                  z_scr, g_scr, a_scr, b_scr, yr_scr, ys_scr, *, seq, nb, cw, row_chunk, problems):
    blk = seq // nb
    gate_refs = (p1_ref, p2_ref)

    def short_conv(k, p_ref, part, dst):
        x = p_ref[k].astype(F32)
        row = lax.broadcasted_iota(jnp.int32, x.shape, 0)
        prev = jnp.where(row == 0, 0.0, pltpu.roll(x, 1, 0))
        nxt = jnp.where(row == seq - 1, 0.0, pltpu.roll(x, seq - 1, 0))
        w = cw_ref[:, part, :]
        y = cb_ref[part:part + 1, :] + prev * w[0:1] + x * w[1:2] + nxt * w[2:3]
        for j in range(nb):
            dst[k, :, j * cw:(j + 1) * cw] = y[j * blk:(j + 1) * blk, :]

    def forward(k, o):
        zb = z_scr[k].astype(BF16)
        a_scr[k] = jnp.dot(mc_ref[...], zb, preferred_element_type=F32)
        b_scr[k] = jnp.dot(ms_ref[...], zb, preferred_element_type=F32)

    def pointwise(k, o):
        for c in range(blk // row_chunk):
            r = slice(c * row_chunk, (c + 1) * row_chunk)
            a = [a_scr[k, r, j * cw:(j + 1) * cw] for j in range(nb)]
            b = [b_scr[k, r, j * cw:(j + 1) * cw] for j in range(nb)]
            for i in range(nb):
                yr = ys = None
                for j in range(nb):
                    kr, ki = kr_ref[o, i - j + nb - 1, r, :], ki_ref[o, i - j + nb - 1, r, :]
                    tr, ts = a[j] * kr - b[j] * ki, a[j] * ki + b[j] * kr
                    yr, ys = (tr, ts) if yr is None else (yr + tr, ys + ts)
                yr_scr[k, r, i * cw:(i + 1) * cw] = yr.astype(BF16)
                ys_scr[k, r, i * cw:(i + 1) * cw] = ys.astype(BF16)

    def inverse(k, o):
        conv = (jnp.dot(mc_ref[...], yr_scr[k], preferred_element_type=F32)
                + jnp.dot(ms_ref[...], ys_scr[k], preferred_element_type=F32))
        short_conv(k, gate_refs[o], o + 1, g_scr)
        dsk = jnp.concatenate([d_ref[o:o + 1, :]] * nb, axis=1)
        z = g_scr[k] * (conv + dsk * z_scr[k])
        if o == HY_ORDER - 1:
            for j in range(nb):
                rows = slice(j * blk, (j + 1) * blk)
                o_ref[k, rows, :] = (z[:, j * cw:(j + 1) * cw] * g_ref[k, rows, :].astype(F32)).astype(o_ref.dtype)
        else:
            z_scr[k] = z

    stages = [lambda k: short_conv(k, pv_ref, 0, z_scr)]
    for o in range(HY_ORDER):
        stages += [functools.partial(forward, o=o), functools.partial(pointwise, o=o),
                   functools.partial(inverse, o=o)]
    for s in range(len(stages) + problems - 1):
        for k in range(problems):
            if 0 <= s - k < len(stages):
                stages[s - k](k)


def _hy_conv(p_hy, gate_silu, conv_w, conv_b, kr, ki, d_skip, tables, batch, seq, width, nb, *, col_block=LANES,
             row_chunk=16, problems=2):
    assert col_block == LANES
    mc, ms, _, _ = tables
    blk = seq // nb
    nblk = width // col_block
    part = lambda k: pl.BlockSpec((None, problems, seq, col_block), lambda c, b: (k * nblk + c, b, 0, 0))
    cw = conv_w.reshape(conv_w.shape[0], HY_ORDER + 1, width)
    cb = conv_b.reshape(HY_ORDER + 1, width)
    filt = pl.BlockSpec((HY_ORDER, 2 * nb - 1, blk, col_block), lambda c, b: (0, 0, 0, c),
                        pipeline_mode=pl.Buffered(1))
    body = functools.partial(_hy_conv_body, seq=seq, nb=nb, cw=col_block, row_chunk=row_chunk, problems=problems)
    wide = nb * col_block
    scratch = lambda dtype: pltpu.VMEM((problems, blk, wide), dtype)
    return pl.pallas_call(
        body,
        grid=(nblk, batch // problems),
        in_specs=[part(0), part(1), part(2), part(0),
                  pl.BlockSpec((cw.shape[0], HY_ORDER + 1, col_block), lambda c, b: (0, 0, c)),
                  pl.BlockSpec((HY_ORDER + 1, col_block), lambda c, b: (0, c)),
                  filt, filt,
                  pl.BlockSpec((HY_ORDER, col_block), lambda c, b: (0, c)),
                  _resident(mc.shape), _resident(ms.shape)],
        out_specs=part(0),
        out_shape=jax.ShapeDtypeStruct((nblk, batch, seq, col_block), BF16),
        scratch_shapes=[scratch(F32), scratch(F32), scratch(F32), scratch(F32), scratch(BF16), scratch(BF16)],
        compiler_params=pltpu.CompilerParams(dimension_semantics=("parallel", "parallel"),
                                             vmem_limit_bytes=VMEM_LIMIT_BYTES),
    )(p_hy, p_hy, p_hy, gate_silu, cw, cb, kr, ki, d_skip, mc, ms)


def _merge_body(ys_ref, gs_ref, yh_ref, m_ref, h_ref, wg_ref, bg_ref, wbs_ref, wbh_ref, wo_ref, fw_ref, o_ref,
                *, d_model, final_norm):
    tile_rows = h_ref.shape[0]
    ys = jnp.concatenate([ys_ref[q].reshape(tile_rows, LANES) for q in range(ys_ref.shape[0])], axis=1)
    yh = jnp.concatenate([yh_ref[q] for q in range(yh_ref.shape[0])], axis=1)
    y = _gelu_tanh(ys)
    glu = y * _sigmoid(jnp.dot(y.astype(BF16), wg_ref[...], preferred_element_type=F32) + bg_ref[...])
    s5 = (glu * gs_ref[...].astype(F32)).astype(BF16)
    y_s5 = jnp.dot(s5, wbs_ref[...], preferred_element_type=F32)
    y_hy = jnp.dot(yh, wbh_ref[...], preferred_element_type=F32)
    merged = m_ref[:, :d_model].astype(F32) * y_s5 + m_ref[:, d_model:].astype(F32) * y_hy
    h = h_ref[...] + jnp.dot(merged.astype(BF16), wo_ref[...], preferred_element_type=F32)
    if final_norm:
        ms = jnp.mean(h * h, axis=-1, keepdims=True)
        h = h * lax.rsqrt(ms + RMS_EPS) * fw_ref[...]
    o_ref[...] = h


def _merge(y_s5_cm, gs, y_hypre, m, h2d, w_glu, b_glu, w_bs, w_bh, w_out, final_w, final_norm, seq, *,
           row_tile=512):
    rows, d = h2d.shape
    sw = y_s5_cm.shape[0] * LANES
    hw = y_hypre.shape[0] * LANES
    tile = lambda w: pl.BlockSpec((row_tile, w), lambda i: (i, 0))
    body = functools.partial(_merge_body, d_model=d, final_norm=final_norm)
    return pl.pallas_call(
        body,
        grid=(rows // row_tile,),
        in_specs=[_chunk_major_spec(row_tile, seq, sw), tile(sw), _lane_blocked_spec(row_tile, hw), tile(2 * d),
                  tile(d),
                  _resident(w_glu.shape), _resident((1, sw)), _resident(w_bs.shape), _resident(w_bh.shape),
                  _resident(w_out.shape), _resident((1, d))],
        out_specs=tile(d),
        out_shape=jax.ShapeDtypeStruct((rows, d), F32),
        compiler_params=pltpu.CompilerParams(dimension_semantics=("parallel",),
                                             vmem_limit_bytes=VMEM_LIMIT_BYTES),
    )(y_s5_cm, gs, y_hypre, m, h2d, w_glu, b_glu.reshape(1, sw), w_bs, w_bh, w_out, final_w.reshape(1, d))


def kernel(x, norm_w, w_in, s5_lam_re, s5_lam_im, s5_log_step, s5_b_re, s5_b_im, s5_c_re, s5_c_im, s5_d, s5_w_glu, s5_b_glu, hy_conv_w, hy_conv_b, hy_w1, hy_b1, hy_freq, hy_w2, hy_b2, hy_w3, hy_decay, hy_d, w_branch_s5, w_branch_hy, w_out, final_norm_w):
    batch, seq, d_model = x.shape
    depth = w_in.shape[0]
    s5_width = s5_d.shape[1]
    hy_width = hy_d.shape[2]
    widths = (s5_width, s5_width, (HY_ORDER + 1) * hy_width, hy_width, 2 * d_model)
    nb = HY_TIME_BLOCKS
    tables = _dft_tables(seq // nb)
    h = x.reshape(batch * seq, d_model)
    for l in range(depth):
        u, gs, p_hy, gh, m = _inproj(h, norm_w[l], w_in[l].astype(BF16), widths, batch, seq)
        s5_ops = _s5_operands(s5_lam_re[l], s5_lam_im[l], s5_log_step[l], s5_b_re[l], s5_b_im[l],
                              s5_c_re[l], s5_c_im[l], s5_d[l])
        y_s5 = _s5_scan(u, s5_ops, batch, seq)
        kr, ki = _hy_filters(seq, hy_w1[l], hy_b1[l], hy_freq[l], hy_w2[l], hy_b2[l], hy_w3[l], hy_decay[l],
                             hy_width, tables, nb)
        y_hypre = _hy_conv(p_hy.reshape(-1, batch, seq, LANES), gh.reshape(-1, batch, seq, LANES), hy_conv_w[l],
                           hy_conv_b[l], kr, ki, hy_d[l], tables, batch, seq, hy_width, nb)
        h = _merge(y_s5, gs, y_hypre.reshape(-1, batch * seq, LANES), m, h,
                   s5_w_glu[l].astype(BF16), s5_b_glu[l], w_branch_s5[l].astype(BF16),
                   w_branch_hy[l].astype(BF16), w_out[l].astype(BF16), final_norm_w, l == depth - 1, seq)
    return h.reshape(batch, seq, d_model)
```

```python
import functools
import math

import numpy as np
import jax
import jax.numpy as jnp
from jax import lax
from jax.experimental import pallas as pl
from jax.experimental.pallas import tpu as pltpu

F32 = jnp.float32
BF16 = jnp.bfloat16

RMS_EPS = 1e-6
S5_GROUP = 16
S5_CHUNK = 16
S5_GROUPS_PER_PAIR = 2
LANES = 128
SUBLANES = 8
HY_ORDER = 2
HY_BANDS = 16
HY_SHIFT = 0.05
HY_EPS = 1e-6
HY_TIME_BLOCKS = 4
VMEM_LIMIT_BYTES = 56 * 1024 * 1024

_NT = (((1,), (1,)), ((), ()))


def _sigmoid(x):
    return 1.0 / (1.0 + jnp.exp(-x))


def _silu(x):
    return x * _sigmoid(x)


def _gelu_tanh(x):
    return 0.5 * x * (1.0 + jnp.tanh(math.sqrt(2.0 / math.pi) * (x + 0.044715 * (x * x * x))))


def _resident(shape):
    zeros = (0,) * len(shape)
    return pl.BlockSpec(shape, lambda *_: zeros, pipeline_mode=pl.Buffered(1))


def _chunk_major_spec(row_tile, seq, width):
    tiles_per_seq = seq // row_tile
    return pl.BlockSpec((width // LANES, row_tile // S5_CHUNK, None, S5_CHUNK, LANES),
                        lambda i: (0, i % tiles_per_seq, i // tiles_per_seq, 0, 0))


def _lane_blocked_spec(row_tile, width):
    return pl.BlockSpec((width // LANES, row_tile, LANES), lambda i: (0, i, 0))


def _inproj_body(xp_ref, x_ref, xn_ref, nw_ref, w_ref, cw_ref, cb_ref, u_ref, gs_ref, p_ref, gh_ref, m_ref, *,
                 bounds, col_chunk, tiles_per_seq):
    tm, halo = x_ref.shape[0], xp_ref.shape[0]
    pos = pl.program_id(0) % tiles_per_seq

    def norm(x):
        ms = jnp.mean(x * x, axis=-1, keepdims=True)
        return x * lax.rsqrt(ms + RMS_EPS) * nw_ref[...]

    xn = norm(x_ref[...])
    xb = xn.astype(BF16)
    before = jnp.where(pos == 0, 0.0, norm(xp_ref[...]))
    after = jnp.where(pos == tiles_per_seq - 1, 0.0, norm(xn_ref[...]))
    xb_halo = jnp.concatenate([before, xn, after], axis=0).astype(BF16)
    outs = ((u_ref, None), (gs_ref, _silu), (p_ref, None), (gh_ref, _silu), (m_ref, _sigmoid))
    for (out_ref, act), lo, hi in zip(outs, bounds[:-1], bounds[1:]):
        for c0 in range(lo, hi, col_chunk):
            w = w_ref[:, c0:c0 + col_chunk]
            if out_ref is p_ref:
                ye = jnp.dot(xb_halo, w, preferred_element_type=F32)
                cw = cw_ref[:, c0 - lo:c0 - lo + col_chunk]
                y = (cb_ref[:, c0 - lo:c0 - lo + col_chunk]
                     + pltpu.roll(ye, 1, 0)[halo:halo + tm] * cw[0:1]
                     + ye[halo:halo + tm] * cw[1:2]
                     + pltpu.roll(ye, tm + 2 * halo - 1, 0)[halo:halo + tm] * cw[2:3])
            else:
                y = jnp.dot(xb, w, preferred_element_type=F32)
            if act is not None:
                y = act(y)
            y = y.astype(out_ref.dtype)
            if out_ref is gs_ref or out_ref is m_ref:
                out_ref[:, c0 - lo:c0 - lo + col_chunk] = y
                continue
            for s in range(0, col_chunk, LANES):
                slab = y[:, s:s + LANES]
                if out_ref is u_ref:
                    slab = slab.reshape(slab.shape[0] // S5_CHUNK, S5_CHUNK, LANES)
                out_ref[(c0 - lo + s) // LANES] = slab


def _inproj(h2d, norm_w, w_in_bf, conv_w, conv_b, widths, batch, seq, *, row_tile=256, col_chunk=512):
    rows, d = h2d.shape
    bounds = [0]
    for w in widths:
        bounds.append(bounds[-1] + w)
    body = functools.partial(_inproj_body, bounds=tuple(bounds), col_chunk=col_chunk,
                             tiles_per_seq=seq // row_tile)
    tile = lambda w: pl.BlockSpec((row_tile, w), lambda i: (i, 0))
    per_tile = row_tile // SUBLANES
    halo_before = pl.BlockSpec((SUBLANES, d), lambda i: (jnp.maximum(i * per_tile - 1, 0), 0))
    halo_after = pl.BlockSpec((SUBLANES, d), lambda i: (jnp.minimum((i + 1) * per_tile, rows // SUBLANES - 1), 0))
    return pl.pallas_call(
        body,
        grid=(rows // row_tile,),
        in_specs=[halo_before, tile(d), halo_after, _resident((1, d)), _resident(w_in_bf.shape),
                  _resident(conv_w.shape), _resident((1, widths[2]))],
        out_specs=[_chunk_major_spec(row_tile, seq, widths[0]), tile(widths[1]),
                   _lane_blocked_spec(row_tile, widths[2]), _lane_blocked_spec(row_tile, widths[3]),
                   tile(widths[4])],
        out_shape=[jax.ShapeDtypeStruct((widths[0] // LANES, seq // S5_CHUNK, batch, S5_CHUNK, LANES), BF16),
                   jax.ShapeDtypeStruct((rows, widths[1]), BF16),
                   jax.ShapeDtypeStruct((widths[2] // LANES, rows, LANES), BF16),
                   jax.ShapeDtypeStruct((widths[3] // LANES, rows, LANES), BF16),
                   jax.ShapeDtypeStruct((rows, widths[4]), BF16)],
        compiler_params=pltpu.CompilerParams(dimension_semantics=("parallel",),
                                             vmem_limit_bytes=VMEM_LIMIT_BYTES),
    )(h2d, h2d, h2d, norm_w.reshape(1, d), w_in_bf, conv_w, conv_b.reshape(1, -1))


def _s5_body(*refs, batch, chunks, pairs, pw, sl):
    t_steps = S5_CHUNK
    x_ref, win_ref, woutt_ref, a_ref, sc_ref, dv_ref, y_ref, d_scr, st_scr = refs
    k = t_steps * pw
    rows = batch * chunks
    assert k == 4 * sl

    for j in range(pairs):
        dj = jnp.concatenate([x_ref[:, t * LANES + j * pw:t * LANES + (j + 1) * pw] for t in range(t_steps)],
                             axis=1)
        d_scr[:, j * k:(j + 1) * k] = dj
        win = jnp.concatenate([win_ref[c, j] for c in range(4)], axis=1).astype(BF16)
        st_scr[:, j * k:(j + 1) * k] = jnp.dot(dj, win, preferred_element_type=F32)

    coef = [[jnp.broadcast_to(a_ref[j, c:c + 1, :], (batch, sl)) for c in range(4)] for j in range(pairs)]

    def step(i, carry):
        row_f = pl.ds(pl.multiple_of(i * batch, batch), batch)
        row_b = pl.ds(pl.multiple_of((chunks - 1 - i) * batch, batch), batch)
        out = []
        for j in range(pairs):
            for direction, r in ((0, row_f), (1, row_b)):
                sr, si = carry[4 * j + 2 * direction], carry[4 * j + 2 * direction + 1]
                ar, ai = coef[j][2 * direction], coef[j][2 * direction + 1]
                c_re = j * k + 2 * direction * sl
                c_im = c_re + sl
                ir, ii = st_scr[r, c_re:c_re + sl], st_scr[r, c_im:c_im + sl]
                st_scr[r, c_re:c_re + sl] = sr
                st_scr[r, c_im:c_im + sl] = si
                out += [ar * sr - ai * si + ir, ar * si + ai * sr + ii]
        return tuple(out)

    zero = jnp.zeros((batch, sl), F32)
    lax.fori_loop(0, chunks, step, (zero,) * (4 * pairs))

    in_step = lax.broadcasted_iota(jnp.int32, (k, k), 0) // pw
    out_step = lax.broadcasted_iota(jnp.int32, (k, k), 1) // pw
    for j in range(pairs):
        sc = sc_ref[j]
        kern = None
        for direction in range(2):
            wr, wi = win_ref[2 * direction, j], win_ref[2 * direction + 1, j]
            pr, pi = sc[2 * direction:2 * direction + 1, :], sc[2 * direction + 1:2 * direction + 2, :]
            wp = jnp.concatenate([wr * pr - wi * pi, wr * pi + wi * pr], axis=1).astype(BF16)
            wt = jnp.concatenate([woutt_ref[2 * direction, j], woutt_ref[2 * direction + 1, j]], axis=1)
            kd = lax.dot_general(wp, wt, _NT, preferred_element_type=F32)
            kd = jnp.where(out_step >= in_step if direction == 0 else out_step <= in_step, kd, 0.0)
            kern = kd if kern is None else kern + kd
        dj = d_scr[:, j * k:(j + 1) * k]
        wt = jnp.concatenate([woutt_ref[c, j] for c in range(4)], axis=1)
        y = jnp.dot(dj, kern.astype(BF16), preferred_element_type=F32)
        y = y + lax.dot_general(st_scr[:, j * k:(j + 1) * k].astype(BF16), wt, _NT, preferred_element_type=F32)
        y = y + dj.astype(F32) * dv_ref[j]
        st_scr[:, j * k:(j + 1) * k] = y
    for t in range(t_steps):
        y_ref[pl.ds(t, rows, stride=t_steps), :] = jnp.concatenate(
            [st_scr[:, j * k + t * pw:j * k + (t + 1) * pw] for j in range(pairs)], axis=1)


def _s5_operands(lam_re, lam_im, log_step, b_re, b_im, c_re, c_im, d_skip):
    _, groups, p = lam_re.shape
    hh, t, gp = S5_GROUP, S5_CHUNK, S5_GROUPS_PER_PAIR
    npair = groups // gp
    rep = lambda v: jnp.concatenate([v] * gp, axis=-1)
    own = (jnp.arange(gp * p)[None, :] // p == jnp.arange(groups)[:, None] % gp).astype(F32)
    lam_re, lam_im = rep(lam_re), rep(lam_im)
    dt = jnp.exp(log_step)[..., None]
    xr, xi = lam_re * dt, lam_im * dt
    lbr, lbi = jnp.exp(xr) * jnp.cos(xi), jnp.exp(xr) * jnp.sin(xi)
    nr, ni = lbr - 1.0, lbi
    den = lam_re * lam_re + lam_im * lam_im
    zr, zi = (nr * lam_re + ni * lam_im) / den * own, (ni * lam_re - nr * lam_im) / den * own
    btr, bti = rep(jnp.swapaxes(b_re, -1, -2)), rep(jnp.swapaxes(b_im, -1, -2))
    bbr = zr[:, :, None, :] * btr - zi[:, :, None, :] * bti
    bbi = zr[:, :, None, :] * bti + zi[:, :, None, :] * btr
    ccr, cci = rep(c_re) * own[None, :, None, :], rep(c_im) * own[None, :, None, :]

    def powers(e):
        kk = e[:, None, :, None]
        mag, ang = jnp.exp(kk * xr[:, :, None, :]), kk * xi[:, :, None, :]
        return (mag * jnp.cos(ang))[:, :, :, None, :], (mag * jnp.sin(ang))[:, :, :, None, :]

    tt = jnp.arange(t, dtype=F32)
    pir, pii = powers(jnp.stack([t - 1 - tt, tt]))
    por, poi = powers(jnp.stack([tt + 1, t - tt]))
    b5r, b5i = bbr[:, :, None], bbi[:, :, None]
    c5r, c5i = ccr[:, :, None], cci[:, :, None]
    win_r, win_i = pir * b5r - pii * b5i, pir * b5i + pii * b5r
    out_r, out_i = por * c5r - poi * c5i, por * c5i + poi * c5r

    def pair_layout(x_re, x_im):
        comp = jnp.stack([x_re[0], x_im[0], x_re[1], x_im[1]]).reshape(4, npair, gp, t, hh, gp * p)
        return jnp.transpose(comp, (0, 1, 3, 2, 4, 5)).reshape(4, npair, t * gp * hh, gp * p)

    def pair_vec(e):
        mag, ang = jnp.exp(e * xr) * own, e * xi
        v = jnp.stack([(mag * jnp.cos(ang))[0], (mag * jnp.sin(ang))[0],
                       (mag * jnp.cos(ang))[1], (mag * jnp.sin(ang))[1]])
        return jnp.transpose(v.reshape(4, npair, gp, gp * p).sum(axis=2), (1, 0, 2))

    dvec = jnp.broadcast_to(d_skip.reshape(npair, 1, gp, hh), (npair, t, gp, hh)).reshape(npair, 1, t * gp * hh)
    return (pair_layout(win_r, win_i), pair_layout(out_r, -out_i).astype(BF16),
            pair_vec(float(t)), pair_vec(-float(t)), dvec)


def _s5_scan(u_cm, operands, batch, seq):
    win, woutt, a, sc, dvec = operands
    _, npair, k, sl = win.shape
    t = S5_CHUNK
    pw = k // t
    pairs = LANES // pw
    chunks = seq // t
    rows = chunks * batch
    width = npair * pw
    nblk = width // LANES
    body = functools.partial(_s5_body, batch=batch, chunks=chunks, pairs=pairs, pw=pw, sl=sl)
    per_blk = lambda shape: pl.BlockSpec((pairs,) + shape, lambda q: (q, 0, 0))
    comp_blk = pl.BlockSpec((4, pairs, k, sl), lambda q: (0, q, 0, 0), pipeline_mode=pl.Buffered(1))
    y = pl.pallas_call(
        body,
        grid=(nblk,),
        in_specs=[pl.BlockSpec((None, rows, t * LANES), lambda q: (q, 0, 0)),
                  comp_blk, comp_blk, per_blk((4, sl)), per_blk((4, sl)), per_blk((1, k))],
        out_specs=pl.BlockSpec((None, rows * t, LANES), lambda q: (q, 0, 0), pipeline_mode=pl.Buffered(1)),
        out_shape=jax.ShapeDtypeStruct((nblk, rows * t, LANES), F32),
        scratch_shapes=[pltpu.VMEM((rows, pairs * k), BF16), pltpu.VMEM((rows, pairs * k), F32)],
        compiler_params=pltpu.CompilerParams(dimension_semantics=("parallel",),
                                             vmem_limit_bytes=VMEM_LIMIT_BYTES),
    )(u_cm.reshape(nblk, rows, t * LANES), win, woutt, a, sc, dvec)
    return y.reshape(nblk, chunks, batch, t, LANES)


def _dft_tables(blk):
    n = 2 * blk
    f = np.arange(blk, dtype=np.int64)
    sym = ((2 * f[:, None] + 1) * (2 * f[None, :] + 1)) % (4 * n)
    ang_sym = sym.astype(np.float64) * (2.0 * np.pi / (4 * n))
    lag = np.arange(2 * blk, dtype=np.int64) - blk
    ang_lag = (((2 * f[:, None] + 1) * lag[None, :]) % (2 * n)).astype(np.float64) * (2.0 * np.pi / (2 * n))
    tc, ts = np.cos(ang_lag), np.sin(ang_lag)
    tc[:, 0] = 0.0
    ts[:, 0] = 0.0
    as_bf = lambda x: jnp.asarray(x.astype(np.float32)).astype(BF16)
    return as_bf(np.cos(ang_sym)), as_bf(np.sin(ang_sym)), as_bf(tc), as_bf(ts)


def _position_features(seq, pad_to):
    t = np.linspace(0.0, 1.0, seq)[:, None]
    pos = np.arange(seq, dtype=np.float64)[:, None]
    bands = np.linspace(1e-4, HY_BANDS - 1, HY_BANDS)[None, :]
    ang = bands * pos * (2.0 * math.pi / seq)
    feats = np.zeros((seq, pad_to), np.float64)
    feats[:, :1 + 2 * HY_BANDS] = np.concatenate([t, np.cos(ang), -np.sin(ang)], axis=-1)
    mirror = lambda x: np.roll(x[::-1], 1, axis=0)
    f32 = lambda x: jnp.asarray(x.astype(np.float32))
    return f32(feats), f32(t), f32(mirror(feats)), f32(mirror(t))


def _hy_filter_body(feats_ref, t_ref, featsm_ref, tm_ref, w1_ref, b1_ref, fr_ref, w2_ref, b2_ref, w3f_ref,
                    w3b_ref, decf_ref, decb_ref, tc_ref, ts_ref, kr_ref, ki_ref, h2_scr, h2m_scr, k2_scr,
                    *, seq, nb):
    hi = lax.Precision.HIGHEST
    blk = seq // nb

    @pl.when(pl.program_id(0) == 0)
    def _():
        fr = fr_ref[...]

        def mlp(feats):
            h1 = jnp.sin(fr * (jnp.dot(feats, w1_ref[...], precision=hi, preferred_element_type=F32) + b1_ref[...]))
            return jnp.sin(fr * (jnp.dot(h1, w2_ref[...], precision=hi, preferred_element_type=F32) + b2_ref[...]))

        h2_scr[...] = mlp(feats_ref[...])
        h2m_scr[...] = mlp(featsm_ref[...])

    hf = jnp.dot(h2_scr[...], w3f_ref[...], precision=hi, preferred_element_type=F32)
    hf = hf * (jnp.exp(-t_ref[...] * jnp.abs(decf_ref[...])) + HY_SHIFT)
    hb = jnp.dot(h2m_scr[...], w3b_ref[...], precision=hi, preferred_element_type=F32)
    hb = hb * (jnp.exp(-tm_ref[...] * jnp.abs(decb_ref[...])) + HY_SHIFT)
    row = lax.broadcasted_iota(jnp.int32, hb.shape, 0)
    hb = jnp.where(row == 0, 0.0, hb)
    norm = jnp.sum(hf * hf, axis=0, keepdims=True) + jnp.sum(hb * hb, axis=0, keepdims=True)
    scale = lax.rsqrt(norm + HY_EPS) * (1.0 / blk)
    k2_scr[0:seq, :] = hb.astype(BF16)
    k2_scr[seq:2 * seq, :] = hf.astype(BF16)
    for d in range(2 * nb - 1):
        seg = k2_scr[d * blk:(d + 2) * blk, :]
        kr_ref[0, d] = jnp.dot(tc_ref[...], seg, preferred_element_type=F32) * scale
        ki_ref[0, d] = jnp.dot(ts_ref[...], seg, preferred_element_type=F32) * scale


def _hy_filters(seq, w1, b1, freq, w2, b2, w3, decay, width, tables, nb, *, col_block=256):
    _, _, tc, ts = tables
    blk = seq // nb
    ffn = w2.shape[0]
    feats, t, featsm, tm = _position_features(seq, LANES)
    w1p = jnp.zeros((LANES, ffn), F32).at[:w1.shape[0]].set(w1)
    nblk = width // col_block
    side = lambda rows, direction: pl.BlockSpec(
        (rows, col_block), lambda i: (0, (2 * (i // nblk) + direction) * nblk + i % nblk))
    out = pl.BlockSpec((1, 2 * nb - 1, blk, col_block), lambda i: (i // nblk, 0, 0, i % nblk))
    body = functools.partial(_hy_filter_body, seq=seq, nb=nb)
    row2 = lambda v: v.reshape(1, -1)
    return pl.pallas_call(
        body,
        grid=(HY_ORDER * nblk,),
        in_specs=[_resident(feats.shape), _resident(t.shape), _resident(feats.shape), _resident(t.shape),
                  _resident(w1p.shape), _resident((1, ffn)), _resident((1, ffn)), _resident(w2.shape),
                  _resident((1, ffn)), side(ffn, 0), side(ffn, 1), side(1, 0), side(1, 1),
                  _resident(tc.shape), _resident(ts.shape)],
        out_specs=[out, out],
        out_shape=[jax.ShapeDtypeStruct((HY_ORDER, 2 * nb - 1, blk, width), F32)] * 2,
        scratch_shapes=[pltpu.VMEM((seq, ffn), F32), pltpu.VMEM((seq, ffn), F32),
                        pltpu.VMEM((2 * seq, col_block), BF16)],
        compiler_params=pltpu.CompilerParams(dimension_semantics=("arbitrary",),
                                             vmem_limit_bytes=VMEM_LIMIT_BYTES),
    )(feats, t, featsm, tm, w1p, row2(b1), row2(freq), w2, row2(b2), w3, w3, row2(decay), row2(decay), tc, ts)


def _hy_conv_body(v_ref, x1_ref, x2_ref, g_ref, kr_ref, ki_ref, d_ref, mc_ref, ms_ref, o_ref,
                  z_scr, a_scr, b_scr, yr_scr, ys_scr, *, seq, nb, cw, row_chunk, problems):
    blk = seq // nb
    gate_refs = (x1_ref, x2_ref)
    block_rows = lambda j: slice(j * blk, (j + 1) * blk)

    def forward(k, o):
        if o == 0:
            zb = jnp.concatenate([v_ref[k, block_rows(j), :] for j in range(nb)], axis=1)
        else:
            zb = z_scr[k].astype(BF16)
        a_scr[k] = jnp.dot(mc_ref[...], zb, preferred_element_type=F32)
        b_scr[k] = jnp.dot(ms_ref[...], zb, preferred_element_type=F32)

    def pointwise(k, o):
        for c in range(blk // row_chunk):
            r = slice(c * row_chunk, (c + 1) * row_chunk)
            a = [a_scr[k, r, j * cw:(j + 1) * cw] for j in range(nb)]
            b = [b_scr[k, r, j * cw:(j + 1) * cw] for j in range(nb)]
            for i in range(nb):
                yr = ys = None
                for j in range(nb):
                    kr, ki = kr_ref[o, i - j + nb - 1, r, :], ki_ref[o, i - j + nb - 1, r, :]
                    tr, ts = a[j] * kr - b[j] * ki, a[j] * ki + b[j] * kr
                    yr, ys = (tr, ts) if yr is None else (yr + tr, ys + ts)
                yr_scr[k, r, i * cw:(i + 1) * cw] = yr.astype(BF16)
                ys_scr[k, r, i * cw:(i + 1) * cw] = ys.astype(BF16)

    def inverse(k, o):
        conv = (jnp.dot(mc_ref[...], yr_scr[k], preferred_element_type=F32)
                + jnp.dot(ms_ref[...], ys_scr[k], preferred_element_type=F32))
        dsk = d_ref[o:o + 1, :]
        for j in range(nb):
            rows, lanes = block_rows(j), slice(j * cw, (j + 1) * cw)
            z_old = v_ref[k, rows, :].astype(F32) if o == 0 else z_scr[k, :, lanes]
            z = gate_refs[o][k, rows, :].astype(F32) * (conv[:, lanes] + dsk * z_old)
            if o == HY_ORDER - 1:
                o_ref[k, rows, :] = (z * g_ref[k, rows, :].astype(F32)).astype(o_ref.dtype)
            else:
                z_scr[k, :, lanes] = z

    stages = []
    for o in range(HY_ORDER):
        stages += [functools.partial(forward, o=o), functools.partial(pointwise, o=o),
                   functools.partial(inverse, o=o)]
    for s in range(len(stages) + problems - 1):
        for k in range(problems):
            if 0 <= s - k < len(stages):
                stages[s - k](k)


def _hy_conv(p_hy, gate_silu, kr, ki, d_skip, tables, batch, seq, width, nb, *, col_block=LANES,
             row_chunk=16, problems=2):
    assert col_block == LANES
    mc, ms, _, _ = tables
    blk = seq // nb
    nblk = width // col_block
    part = lambda k: pl.BlockSpec((None, problems, seq, col_block), lambda c, b: (k * nblk + c, b, 0, 0))
    filt = pl.BlockSpec((HY_ORDER, 2 * nb - 1, blk, col_block), lambda c, b: (0, 0, 0, c),
                        pipeline_mode=pl.Buffered(1))
    body = functools.partial(_hy_conv_body, seq=seq, nb=nb, cw=col_block, row_chunk=row_chunk, problems=problems)
    wide = nb * col_block
    scratch = lambda dtype: pltpu.VMEM((problems, blk, wide), dtype)
    return pl.pallas_call(
        body,
        grid=(nblk, batch // problems),
        in_specs=[part(0), part(1), part(2), part(0), filt, filt,
                  pl.BlockSpec((HY_ORDER, col_block), lambda c, b: (0, c)),
                  _resident(mc.shape), _resident(ms.shape)],
        out_specs=part(0),
        out_shape=jax.ShapeDtypeStruct((nblk, batch, seq, col_block), BF16),
        scratch_shapes=[scratch(F32), scratch(F32), scratch(F32), scratch(BF16), scratch(BF16)],
        compiler_params=pltpu.CompilerParams(dimension_semantics=("parallel", "parallel"),
                                             vmem_limit_bytes=VMEM_LIMIT_BYTES),
    )(p_hy, p_hy, p_hy, gate_silu, kr, ki, d_skip, mc, ms)


def _merge_body(ys_ref, gs_ref, yh_ref, m_ref, h_ref, wg_ref, bg_ref, wbs_ref, wbh_ref, wo_ref, fw_ref, o_ref,
                *, d_model, final_norm):
    tile_rows = h_ref.shape[0]
    ys = jnp.concatenate([ys_ref[q].reshape(tile_rows, LANES) for q in range(ys_ref.shape[0])], axis=1)
    yh = jnp.concatenate([yh_ref[q] for q in range(yh_ref.shape[0])], axis=1)
    y = _gelu_tanh(ys)
    glu = y * _sigmoid(jnp.dot(y.astype(BF16), wg_ref[...], preferred_element_type=F32) + bg_ref[...])
    s5 = (glu * gs_ref[...].astype(F32)).astype(BF16)
    y_s5 = jnp.dot(s5, wbs_ref[...], preferred_element_type=F32)
    y_hy = jnp.dot(yh, wbh_ref[...], preferred_element_type=F32)
    merged = m_ref[:, :d_model].astype(F32) * y_s5 + m_ref[:, d_model:].astype(F32) * y_hy
    h = h_ref[...] + jnp.dot(merged.astype(BF16), wo_ref[...], preferred_element_type=F32)
    if final_norm:
        ms = jnp.mean(h * h, axis=-1, keepdims=True)
        h = h * lax.rsqrt(ms + RMS_EPS) * fw_ref[...]
    o_ref[...] = h


def _merge(y_s5_cm, gs, y_hypre, m, h2d, w_glu, b_glu, w_bs, w_bh, w_out, final_w, final_norm, seq, *,
           row_tile=512):
    rows, d = h2d.shape
    sw = y_s5_cm.shape[0] * LANES
    hw = y_hypre.shape[0] * LANES
    tile = lambda w: pl.BlockSpec((row_tile, w), lambda i: (i, 0))
    body = functools.partial(_merge_body, d_model=d, final_norm=final_norm)
    return pl.pallas_call(
        body,
        grid=(rows // row_tile,),
        in_specs=[_chunk_major_spec(row_tile, seq, sw), tile(sw), _lane_blocked_spec(row_tile, hw), tile(2 * d),
                  tile(d),
                  _resident(w_glu.shape), _resident((1, sw)), _resident(w_bs.shape), _resident(w_bh.shape),
                  _resident(w_out.shape), _resident((1, d))],
        out_specs=tile(d),
        out_shape=jax.ShapeDtypeStruct((rows, d), F32),
        compiler_params=pltpu.CompilerParams(dimension_semantics=("parallel",),
                                             vmem_limit_bytes=VMEM_LIMIT_BYTES),
    )(y_s5_cm, gs, y_hypre, m, h2d, w_glu, b_glu.reshape(1, sw), w_bs, w_bh, w_out, final_w.reshape(1, d))


def kernel(x, norm_w, w_in, s5_lam_re, s5_lam_im, s5_log_step, s5_b_re, s5_b_im, s5_c_re, s5_c_im, s5_d, s5_w_glu, s5_b_glu, hy_conv_w, hy_conv_b, hy_w1, hy_b1, hy_freq, hy_w2, hy_b2, hy_w3, hy_decay, hy_d, w_branch_s5, w_branch_hy, w_out, final_norm_w):
    batch, seq, d_model = x.shape
    depth = w_in.shape[0]
    s5_width = s5_d.shape[1]
    hy_width = hy_d.shape[2]
    widths = (s5_width, s5_width, (HY_ORDER + 1) * hy_width, hy_width, 2 * d_model)
    nb = HY_TIME_BLOCKS
    tables = _dft_tables(seq // nb)
    h = x.reshape(batch * seq, d_model)
    for l in range(depth):
        u, gs, p_hy, gh, m = _inproj(h, norm_w[l], w_in[l].astype(BF16), hy_conv_w[l], hy_conv_b[l], widths, batch,
                                     seq)
        s5_ops = _s5_operands(s5_lam_re[l], s5_lam_im[l], s5_log_step[l], s5_b_re[l], s5_b_im[l],
                              s5_c_re[l], s5_c_im[l], s5_d[l])
        y_s5 = _s5_scan(u, s5_ops, batch, seq)
        kr, ki = _hy_filters(seq, hy_w1[l], hy_b1[l], hy_freq[l], hy_w2[l], hy_b2[l], hy_w3[l], hy_decay[l],
                             hy_width, tables, nb)
        y_hypre = _hy_conv(p_hy.reshape(-1, batch, seq, LANES), gh.reshape(-1, batch, seq, LANES), kr, ki, hy_d[l],
                           tables, batch, seq, hy_width, nb)
        h = _merge(y_s5, gs, y_hypre.reshape(-1, batch * seq, LANES), m, h,
                   s5_w_glu[l].astype(BF16), s5_b_glu[l], w_branch_s5[l].astype(BF16),
                   w_branch_hy[l].astype(BF16), w_out[l].astype(BF16), final_norm_w, l == depth - 1, seq)
    return h.reshape(batch, seq, d_model)
```

```python
import functools
import math

import numpy as np
import jax
import jax.numpy as jnp
from jax import lax
from jax.experimental import pallas as pl
from jax.experimental.pallas import tpu as pltpu

F32 = jnp.float32
BF16 = jnp.bfloat16

RMS_EPS = 1e-6
S5_GROUP = 16
S5_CHUNK = 16
S5_GROUPS_PER_PAIR = 2
LANES = 128
SUBLANES = 8
HY_ORDER = 2
HY_BANDS = 16
HY_SHIFT = 0.05
HY_EPS = 1e-6
HY_TIME_BLOCKS = 4
HY_LAGS = 2 * HY_TIME_BLOCKS - 1
HY_COMBOS = 9
VMEM_LIMIT_BYTES = 56 * 1024 * 1024

_NT = (((1,), (1,)), ((), ()))


def _sigmoid(x):
    return 1.0 / (1.0 + jnp.exp(-x))


def _silu(x):
    return x * _sigmoid(x)


def _gelu_tanh(x):
    return 0.5 * x * (1.0 + jnp.tanh(math.sqrt(2.0 / math.pi) * (x + 0.044715 * (x * x * x))))


def _resident(shape):
    zeros = (0,) * len(shape)
    return pl.BlockSpec(shape, lambda *_: zeros, pipeline_mode=pl.Buffered(1))


def _chunk_major_spec(row_tile, seq, width):
    tiles_per_seq = seq // row_tile
    return pl.BlockSpec((row_tile // S5_CHUNK, None, S5_CHUNK, width),
                        lambda i: (i % tiles_per_seq, i // tiles_per_seq, 0, 0))


def _inproj_body(x_ref, nw_ref, w_ref, u_ref, gs_ref, p_ref, gh_ref, m_ref, *, bounds, col_chunk):
    x = x_ref[...]
    ms = jnp.mean(x * x, axis=-1, keepdims=True)
    xb = (x * lax.rsqrt(ms + RMS_EPS) * nw_ref[...]).astype(BF16)
    outs = ((u_ref, None), (gs_ref, _silu), (p_ref, None), (gh_ref, _silu), (m_ref, _sigmoid))
    for (out_ref, act), lo, hi in zip(outs, bounds[:-1], bounds[1:]):
        for c0 in range(lo, hi, col_chunk):
            y = jnp.dot(xb, w_ref[:, c0:c0 + col_chunk], preferred_element_type=F32)
            if act is not None:
                y = act(y)
            if out_ref is u_ref:
                out_ref[:, :, c0 - lo:c0 - lo + col_chunk] = y.astype(out_ref.dtype).reshape(
                    y.shape[0] // S5_CHUNK, S5_CHUNK, col_chunk)
            else:
                out_ref[:, c0 - lo:c0 - lo + col_chunk] = y.astype(out_ref.dtype)


def _inproj(h2d, norm_w, w_in_bf, widths, batch, seq, *, row_tile=256, col_chunk=512):
    rows, d = h2d.shape
    bounds = [0]
    for w in widths:
        bounds.append(bounds[-1] + w)
    body = functools.partial(_inproj_body, bounds=tuple(bounds), col_chunk=col_chunk)
    tile = lambda w: pl.BlockSpec((row_tile, w), lambda i: (i, 0))
    return pl.pallas_call(
        body,
        grid=(rows // row_tile,),
        in_specs=[tile(d), _resident((1, d)), _resident(w_in_bf.shape)],
        out_specs=[_chunk_major_spec(row_tile, seq, widths[0])] + [tile(w) for w in widths[1:]],
        out_shape=[jax.ShapeDtypeStruct((seq // S5_CHUNK, batch, S5_CHUNK, widths[0]), BF16)]
        + [jax.ShapeDtypeStruct((rows, w), BF16) for w in widths[1:]],
        compiler_params=pltpu.CompilerParams(dimension_semantics=("parallel",),
                                             vmem_limit_bytes=VMEM_LIMIT_BYTES),
    )(h2d, norm_w.reshape(1, d), w_in_bf)


def _s5_body(*refs, batch, chunks, pairs, pw, sl):
    t_steps = S5_CHUNK
    x_refs = refs[:t_steps]
    win_ref, woutt_ref, a_ref, sc_ref, dv_ref, y_ref, d_scr, st_scr = refs[t_steps:]
    k = t_steps * pw
    rows = batch * chunks
    assert k == 4 * sl

    for j in range(pairs):
        dj = jnp.concatenate([x_refs[t][:, j * pw:(j + 1) * pw] for t in range(t_steps)], axis=1)
        d_scr[:, j * k:(j + 1) * k] = dj
        win = jnp.concatenate([win_ref[c, j] for c in range(4)], axis=1).astype(BF16)
        st_scr[:, j * k:(j + 1) * k] = jnp.dot(dj, win, preferred_element_type=F32)

    coef = [[jnp.broadcast_to(a_ref[j, c:c + 1, :], (batch, sl)) for c in range(4)] for j in range(pairs)]

    def step(i, carry):
        row_f = pl.ds(pl.multiple_of(i * batch, batch), batch)
        row_b = pl.ds(pl.multiple_of((chunks - 1 - i) * batch, batch), batch)
        out = []
        for j in range(pairs):
            for direction, r in ((0, row_f), (1, row_b)):
                sr, si = carry[4 * j + 2 * direction], carry[4 * j + 2 * direction + 1]
                ar, ai = coef[j][2 * direction], coef[j][2 * direction + 1]
                c_re = j * k + 2 * direction * sl
                c_im = c_re + sl
                ir, ii = st_scr[r, c_re:c_re + sl], st_scr[r, c_im:c_im + sl]
                st_scr[r, c_re:c_re + sl] = sr
                st_scr[r, c_im:c_im + sl] = si
                out += [ar * sr - ai * si + ir, ar * si + ai * sr + ii]
        return tuple(out)

    zero = jnp.zeros((batch, sl), F32)
    lax.fori_loop(0, chunks, step, (zero,) * (4 * pairs))

    in_step = lax.broadcasted_iota(jnp.int32, (k, k), 0) // pw
    out_step = lax.broadcasted_iota(jnp.int32, (k, k), 1) // pw
    for j in range(pairs):
        sc = sc_ref[j]
        kern = None
        for direction in range(2):
            wr, wi = win_ref[2 * direction, j], win_ref[2 * direction + 1, j]
            pr, pi = sc[2 * direction:2 * direction + 1, :], sc[2 * direction + 1:2 * direction + 2, :]
            wp = jnp.concatenate([wr * pr - wi * pi, wr * pi + wi * pr], axis=1).astype(BF16)
            wt = jnp.concatenate([woutt_ref[2 * direction, j], woutt_ref[2 * direction + 1, j]], axis=1)
            kd = lax.dot_general(wp, wt, _NT, preferred_element_type=F32)
            kd = jnp.where(out_step >= in_step if direction == 0 else out_step <= in_step, kd, 0.0)
            kern = kd if kern is None else kern + kd
        dj = d_scr[:, j * k:(j + 1) * k]
        wt = jnp.concatenate([woutt_ref[c, j] for c in range(4)], axis=1)
        y = jnp.dot(dj, kern.astype(BF16), preferred_element_type=F32)
        y = y + lax.dot_general(st_scr[:, j * k:(j + 1) * k].astype(BF16), wt, _NT, preferred_element_type=F32)
        y = y + dj.astype(F32) * dv_ref[j]
        st_scr[:, j * k:(j + 1) * k] = y
    for t in range(t_steps):
        y_ref[pl.ds(t, rows, stride=t_steps), :] = jnp.concatenate(
            [st_scr[:, j * k + t * pw:j * k + (t + 1) * pw] for j in range(pairs)], axis=1)


def _s5_operands(lam_re, lam_im, log_step, b_re, b_im, c_re, c_im, d_skip):
    _, groups, p = lam_re.shape
    hh, t, gp = S5_GROUP, S5_CHUNK, S5_GROUPS_PER_PAIR
    npair = groups // gp
    rep = lambda v: jnp.concatenate([v] * gp, axis=-1)
    own = (jnp.arange(gp * p)[None, :] // p == jnp.arange(groups)[:, None] % gp).astype(F32)
    lam_re, lam_im = rep(lam_re), rep(lam_im)
    dt = jnp.exp(log_step)[..., None]
    xr, xi = lam_re * dt, lam_im * dt
    lbr, lbi = jnp.exp(xr) * jnp.cos(xi), jnp.exp(xr) * jnp.sin(xi)
    nr, ni = lbr - 1.0, lbi
    den = lam_re * lam_re + lam_im * lam_im
    zr, zi = (nr * lam_re + ni * lam_im) / den * own, (ni * lam_re - nr * lam_im) / den * own
    btr, bti = rep(jnp.swapaxes(b_re, -1, -2)), rep(jnp.swapaxes(b_im, -1, -2))
    bbr = zr[:, :, None, :] * btr - zi[:, :, None, :] * bti
    bbi = zr[:, :, None, :] * bti + zi[:, :, None, :] * btr
    ccr, cci = rep(c_re) * own[None, :, None, :], rep(c_im) * own[None, :, None, :]

    def powers(e):
        kk = e[:, None, :, None]
        mag, ang = jnp.exp(kk * xr[:, :, None, :]), kk * xi[:, :, None, :]
        return (mag * jnp.cos(ang))[:, :, :, None, :], (mag * jnp.sin(ang))[:, :, :, None, :]

    tt = jnp.arange(t, dtype=F32)
    pir, pii = powers(jnp.stack([t - 1 - tt, tt]))
    por, poi = powers(jnp.stack([tt + 1, t - tt]))
    b5r, b5i = bbr[:, :, None], bbi[:, :, None]
    c5r, c5i = ccr[:, :, None], cci[:, :, None]
    win_r, win_i = pir * b5r - pii * b5i, pir * b5i + pii * b5r
    out_r, out_i = por * c5r - poi * c5i, por * c5i + poi * c5r

    def pair_layout(x_re, x_im):
        comp = jnp.stack([x_re[0], x_im[0], x_re[1], x_im[1]]).reshape(4, npair, gp, t, hh, gp * p)
        return jnp.transpose(comp, (0, 1, 3, 2, 4, 5)).reshape(4, npair, t * gp * hh, gp * p)

    def pair_vec(e):
        mag, ang = jnp.exp(e * xr) * own, e * xi
        v = jnp.stack([(mag * jnp.cos(ang))[0], (mag * jnp.sin(ang))[0],
                       (mag * jnp.cos(ang))[1], (mag * jnp.sin(ang))[1]])
        return jnp.transpose(v.reshape(4, npair, gp, gp * p).sum(axis=2), (1, 0, 2))

    dvec = jnp.broadcast_to(d_skip.reshape(npair, 1, gp, hh), (npair, t, gp, hh)).reshape(npair, 1, t * gp * hh)
    return (pair_layout(win_r, win_i), pair_layout(out_r, -out_i).astype(BF16),
            pair_vec(float(t)), pair_vec(-float(t)), dvec)


def _s5_scan(u_cm, operands, batch, seq):
    win, woutt, a, sc, dvec = operands
    _, npair, k, sl = win.shape
    t = S5_CHUNK
    pw = k // t
    pairs = LANES // pw
    chunks = seq // t
    rows = chunks * batch
    width = npair * pw
    nblk = width // LANES
    body = functools.partial(_s5_body, batch=batch, chunks=chunks, pairs=pairs, pw=pw, sl=sl)
    per_blk = lambda shape: pl.BlockSpec((pairs,) + shape, lambda q: (q, 0, 0))
    comp_blk = pl.BlockSpec((4, pairs, k, sl), lambda q: (0, q, 0, 0), pipeline_mode=pl.Buffered(1))
    x2d = u_cm.reshape(rows, t * width)
    y = pl.pallas_call(
        body,
        grid=(nblk,),
        in_specs=[pl.BlockSpec((rows, LANES), functools.partial(lambda q, s: (0, s * nblk + q), s=s))
                  for s in range(t)]
        + [comp_blk, comp_blk, per_blk((4, sl)), per_blk((4, sl)), per_blk((1, k))],
        out_specs=pl.BlockSpec((rows * t, LANES), lambda q: (0, q), pipeline_mode=pl.Buffered(1)),
        out_shape=jax.ShapeDtypeStruct((rows * t, width), F32),
        scratch_shapes=[pltpu.VMEM((rows, pairs * k), BF16), pltpu.VMEM((rows, pairs * k), F32)],
        compiler_params=pltpu.CompilerParams(dimension_semantics=("parallel",),
                                             vmem_limit_bytes=VMEM_LIMIT_BYTES),
    )(*([x2d] * t), win, woutt, a, sc, dvec)
    return y.reshape(chunks, batch, t, width)


def _dft_tables(blk):
    n = 2 * blk
    f = np.arange(blk, dtype=np.int64)
    sym = ((2 * f[:, None] + 1) * (2 * f[None, :] + 1)) % (4 * n)
    ang_sym = sym.astype(np.float64) * (2.0 * np.pi / (4 * n))
    lag = np.arange(2 * blk, dtype=np.int64) - blk
    ang_lag = (((2 * f[:, None] + 1) * lag[None, :]) % (2 * n)).astype(np.float64) * (2.0 * np.pi / (2 * n))
    tc, ts = np.cos(ang_lag), np.sin(ang_lag)
    tc[:, 0] = 0.0
    ts[:, 0] = 0.0
    as_bf = lambda x: jnp.asarray(x.astype(np.float32)).astype(BF16)
    return as_bf(np.cos(ang_sym)), as_bf(np.sin(ang_sym)), as_bf(tc), as_bf(ts)


def _position_features(seq, pad_to):
    t = np.linspace(0.0, 1.0, seq)[:, None]
    pos = np.arange(seq, dtype=np.float64)[:, None]
    bands = np.linspace(1e-4, HY_BANDS - 1, HY_BANDS)[None, :]
    ang = bands * pos * (2.0 * math.pi / seq)
    feats = np.zeros((seq, pad_to), np.float64)
    feats[:, :1 + 2 * HY_BANDS] = np.concatenate([t, np.cos(ang), -np.sin(ang)], axis=-1)
    mirror = lambda x: np.roll(x[::-1], 1, axis=0)
    f32 = lambda x: jnp.asarray(x.astype(np.float32))
    return f32(feats), f32(t), f32(mirror(feats)), f32(mirror(t))


_HY_COMBO_TERMS = (
    ((0, 1),), ((-1, 1), (0, -1)), ((1, 1), (0, -1)),
    ((-2, 1), (0, -1)), ((-3, 1), (-1, -1), (-2, -1), (0, 1)), ((-1, 1), (1, -1), (-2, -1), (0, 1)),
    ((2, 1), (0, -1)), ((1, 1), (-1, -1), (2, -1), (0, 1)), ((3, 1), (1, -1), (2, -1), (0, 1)),
)


def _hy_filter_body(feats_ref, t_ref, featsm_ref, tm_ref, w1_ref, b1_ref, fr_ref, w2_ref, b2_ref, w3f_ref,
                    w3b_ref, decf_ref, decb_ref, tc_ref, ts_ref, gr_ref, gi_ref, h2_scr, h2m_scr, k2_scr,
                    kr_scr, ki_scr, *, seq, nb):
    hi = lax.Precision.HIGHEST
    blk = seq // nb

    @pl.when(pl.program_id(0) == 0)
    def _():
        fr = fr_ref[...]

        def mlp(feats):
            h1 = jnp.sin(fr * (jnp.dot(feats, w1_ref[...], precision=hi, preferred_element_type=F32) + b1_ref[...]))
            return jnp.sin(fr * (jnp.dot(h1, w2_ref[...], precision=hi, preferred_element_type=F32) + b2_ref[...]))

        h2_scr[...] = mlp(feats_ref[...])
        h2m_scr[...] = mlp(featsm_ref[...])

    hf = jnp.dot(h2_scr[...], w3f_ref[...], precision=hi, preferred_element_type=F32)
    hf = hf * (jnp.exp(-t_ref[...] * jnp.abs(decf_ref[...])) + HY_SHIFT)
    hb = jnp.dot(h2m_scr[...], w3b_ref[...], precision=hi, preferred_element_type=F32)
    hb = hb * (jnp.exp(-tm_ref[...] * jnp.abs(decb_ref[...])) + HY_SHIFT)
    row = lax.broadcasted_iota(jnp.int32, hb.shape, 0)
    hb = jnp.where(row == 0, 0.0, hb)
    norm = jnp.sum(hf * hf, axis=0, keepdims=True) + jnp.sum(hb * hb, axis=0, keepdims=True)
    scale = lax.rsqrt(norm + HY_EPS) * (1.0 / blk)
    k2_scr[0:seq, :] = hb.astype(BF16)
    k2_scr[seq:2 * seq, :] = hf.astype(BF16)
    for d in range(2 * nb - 1):
        seg = k2_scr[d * blk:(d + 2) * blk, :]
        kr_scr[d] = jnp.dot(tc_ref[...], seg, preferred_element_type=F32) * scale
        ki_scr[d] = jnp.dot(ts_ref[...], seg, preferred_element_type=F32) * scale
    for idx, terms in enumerate(_HY_COMBO_TERMS):
        for src, dst in ((kr_scr, gr_ref), (ki_scr, gi_ref)):
            acc = None
            for lag, weight in terms:
                term = src[lag + nb - 1]
                acc = (term if weight > 0 else -term) if acc is None else (acc + term if weight > 0 else acc - term)
            dst[0, idx] = acc


def _hy_filters(seq, w1, b1, freq, w2, b2, w3, decay, width, tables, nb, *, col_block=256):
    _, _, tc, ts = tables
    blk = seq // nb
    ffn = w2.shape[0]
    feats, t, featsm, tm = _position_features(seq, LANES)
    w1p = jnp.zeros((LANES, ffn), F32).at[:w1.shape[0]].set(w1)
    nblk = width // col_block
    side = lambda rows, direction: pl.BlockSpec(
        (rows, col_block), lambda i: (0, (2 * (i // nblk) + direction) * nblk + i % nblk))
    out = pl.BlockSpec((1, HY_COMBOS, blk, col_block), lambda i: (i // nblk, 0, 0, i % nblk))
    body = functools.partial(_hy_filter_body, seq=seq, nb=nb)
    row2 = lambda v: v.reshape(1, -1)
    return pl.pallas_call(
        body,
        grid=(HY_ORDER * nblk,),
        in_specs=[_resident(feats.shape), _resident(t.shape), _resident(feats.shape), _resident(t.shape),
                  _resident(w1p.shape), _resident((1, ffn)), _resident((1, ffn)), _resident(w2.shape),
                  _resident((1, ffn)), side(ffn, 0), side(ffn, 1), side(1, 0), side(1, 1),
                  _resident(tc.shape), _resident(ts.shape)],
        out_specs=[out, out],
        out_shape=[jax.ShapeDtypeStruct((HY_ORDER, HY_COMBOS, blk, width), F32)] * 2,
        scratch_shapes=[pltpu.VMEM((seq, ffn), F32), pltpu.VMEM((seq, ffn), F32),
                        pltpu.VMEM((2 * seq, col_block), BF16),
                        pltpu.VMEM((2 * nb - 1, blk, col_block), F32), pltpu.VMEM((2 * nb - 1, blk, col_block), F32)],
        compiler_params=pltpu.CompilerParams(dimension_semantics=("arbitrary",),
                                             vmem_limit_bytes=VMEM_LIMIT_BYTES),
    )(feats, t, featsm, tm, w1p, row2(b1), row2(freq), w2, row2(b2), w3, w3, row2(decay), row2(decay), tc, ts)


def _hy_conv_body(pv_ref, p1_ref, p2_ref, g_ref, cw_ref, cb_ref, gr_ref, gi_ref, d_ref, mc_ref, ms_ref, o_ref,
                  z_scr, g_scr, a_scr, b_scr, yr_scr, ys_scr, *, seq, nb, cw, row_chunk, problems):
    blk = seq // nb
    gate_refs = (p1_ref, p2_ref)
    lanes = lambda j: slice(j * cw, (j + 1) * cw)
    cadd = lambda x, y: (x[0] + y[0], x[1] + y[1])
    cmul = lambda x, y: (x[0] * y[0] - x[1] * y[1], x[0] * y[1] + x[1] * y[0])

    def short_conv(k, p_ref, part, dst):
        x = p_ref[k].astype(F32)
        row = lax.broadcasted_iota(jnp.int32, x.shape, 0)
        prev = jnp.where(row == 0, 0.0, pltpu.roll(x, 1, 0))
        nxt = jnp.where(row == seq - 1, 0.0, pltpu.roll(x, seq - 1, 0))
        w = cw_ref[:, part, :]
        y = cb_ref[part:part + 1, :] + prev * w[0:1] + x * w[1:2] + nxt * w[2:3]
        for j in range(nb):
            dst[k, :, lanes(j)] = y[j * blk:(j + 1) * blk, :]

    def forward(k, o):
        zb = z_scr[k].astype(BF16)
        a_scr[k] = jnp.dot(mc_ref[...], zb, preferred_element_type=F32)
        b_scr[k] = jnp.dot(ms_ref[...], zb, preferred_element_type=F32)

    def pointwise(k, o):
        def toeplitz2(base, x0, x1, r):
            g = lambda idx: (gr_ref[o, idx, r, :], gi_ref[o, idx, r, :])
            q = cmul(g(base), cadd(x0, x1))
            return cadd(q, cmul(g(base + 1), x1)), cadd(q, cmul(g(base + 2), x0))

        for c in range(blk // row_chunk):
            halves = []
            for s in range(row_chunk // SUBLANES):
                r = slice(c * row_chunk + s * SUBLANES, c * row_chunk + (s + 1) * SUBLANES)
                z = [(a_scr[k, r, lanes(j)], b_scr[k, r, lanes(j)]) for j in range(nb)]
                p1 = toeplitz2(0, cadd(z[0], z[2]), cadd(z[1], z[3]), r)
                p2 = toeplitz2(3, z[2], z[3], r)
                p3 = toeplitz2(6, z[0], z[1], r)
                halves.append((cadd(p1[0], p2[0]), cadd(p1[1], p2[1]), cadd(p1[0], p3[0]), cadd(p1[1], p3[1])))
            r = slice(c * row_chunk, (c + 1) * row_chunk)
            for i in range(nb):
                yr_scr[k, r, lanes(i)] = jnp.concatenate([h[i][0] for h in halves], axis=0).astype(BF16)
                ys_scr[k, r, lanes(i)] = jnp.concatenate([h[i][1] for h in halves], axis=0).astype(BF16)

    def inverse(k, o):
        conv = (jnp.dot(mc_ref[...], yr_scr[k], preferred_element_type=F32)
                + jnp.dot(ms_ref[...], ys_scr[k], preferred_element_type=F32))
        short_conv(k, gate_refs[o], o + 1, g_scr)
        dsk = jnp.concatenate([d_ref[o:o + 1, :]] * nb, axis=1)
        z = g_scr[k] * (conv + dsk * z_scr[k])
        if o == HY_ORDER - 1:
            for j in range(nb):
                rows = slice(j * blk, (j + 1) * blk)
                o_ref[k, rows, :] = (z[:, lanes(j)] * g_ref[k, rows, :].astype(F32)).astype(o_ref.dtype)
        else:
            z_scr[k] = z

    stages = [lambda k: short_conv(k, pv_ref, 0, z_scr)]
    for o in range(HY_ORDER):
        stages += [functools.partial(forward, o=o), functools.partial(pointwise, o=o),
                   functools.partial(inverse, o=o)]
    for s in range(len(stages) + problems - 1):
        for k in range(problems):
            if 0 <= s - k < len(stages):
                stages[s - k](k)


def _hy_conv(p_hy, gate_silu, conv_w, conv_b, gr, gi, d_skip, tables, batch, seq, width, nb, *, col_block=LANES,
             row_chunk=16, problems=2):
    assert nb == HY_TIME_BLOCKS
    mc, ms, _, _ = tables
    blk = seq // nb
    nblk = width // col_block
    part = lambda k: pl.BlockSpec((problems, seq, col_block), lambda c, b: (b, 0, k * nblk + c))
    cw = conv_w.reshape(conv_w.shape[0], HY_ORDER + 1, width)
    cb = conv_b.reshape(HY_ORDER + 1, width)
    filt = pl.BlockSpec((HY_ORDER, HY_COMBOS, blk, col_block), lambda c, b: (0, 0, 0, c),
                        pipeline_mode=pl.Buffered(1))
    body = functools.partial(_hy_conv_body, seq=seq, nb=nb, cw=col_block, row_chunk=row_chunk, problems=problems)
    wide = nb * col_block
    scratch = lambda dtype: pltpu.VMEM((problems, blk, wide), dtype)
    return pl.pallas_call(
        body,
        grid=(nblk, batch // problems),
        in_specs=[part(0), part(1), part(2),
                  pl.BlockSpec((problems, seq, col_block), lambda c, b: (b, 0, c)),
                  pl.BlockSpec((cw.shape[0], HY_ORDER + 1, col_block), lambda c, b: (0, 0, c)),
                  pl.BlockSpec((HY_ORDER + 1, col_block), lambda c, b: (0, c)),
                  filt, filt,
                  pl.BlockSpec((HY_ORDER, col_block), lambda c, b: (0, c)),
                  _resident(mc.shape), _resident(ms.shape)],
        out_specs=pl.BlockSpec((problems, seq, col_block), lambda c, b: (b, 0, c)),
        out_shape=jax.ShapeDtypeStruct((batch, seq, width), BF16),
        scratch_shapes=[scratch(F32), scratch(F32), scratch(F32), scratch(F32), scratch(BF16), scratch(BF16)],
        compiler_params=pltpu.CompilerParams(dimension_semantics=("parallel", "parallel"),
                                             vmem_limit_bytes=VMEM_LIMIT_BYTES),
    )(p_hy, p_hy, p_hy, gate_silu, cw, cb, gr, gi, d_skip, mc, ms)


def _merge_body(ys_ref, gs_ref, yh_ref, m_ref, h_ref, wg_ref, bg_ref, wbs_ref, wbh_ref, wo_ref, fw_ref, o_ref,
                *, d_model, final_norm):
    ys = ys_ref[...]
    y = _gelu_tanh(ys.reshape(ys.shape[0] * ys.shape[1], ys.shape[2]))
    glu = y * _sigmoid(jnp.dot(y.astype(BF16), wg_ref[...], preferred_element_type=F32) + bg_ref[...])
    s5 = (glu * gs_ref[...].astype(F32)).astype(BF16)
    y_s5 = jnp.dot(s5, wbs_ref[...], preferred_element_type=F32)
    y_hy = jnp.dot(yh_ref[...], wbh_ref[...], preferred_element_type=F32)
    merged = m_ref[:, :d_model].astype(F32) * y_s5 + m_ref[:, d_model:].astype(F32) * y_hy
    h = h_ref[...] + jnp.dot(merged.astype(BF16), wo_ref[...], preferred_element_type=F32)
    if final_norm:
        ms = jnp.mean(h * h, axis=-1, keepdims=True)
        h = h * lax.rsqrt(ms + RMS_EPS) * fw_ref[...]
    o_ref[...] = h


def _merge(y_s5_cm, gs, y_hypre, m, h2d, w_glu, b_glu, w_bs, w_bh, w_out, final_w, final_norm, seq, *,
           row_tile=512):
    rows, d = h2d.shape
    sw = y_s5_cm.shape[-1]
    hw = y_hypre.shape[1]
    tile = lambda w: pl.BlockSpec((row_tile, w), lambda i: (i, 0))
    body = functools.partial(_merge_body, d_model=d, final_norm=final_norm)
    return pl.pallas_call(
        body,
        grid=(rows // row_tile,),
        in_specs=[_chunk_major_spec(row_tile, seq, sw), tile(sw), tile(hw), tile(2 * d), tile(d),
                  _resident(w_glu.shape), _resident((1, sw)), _resident(w_bs.shape), _resident(w_bh.shape),
                  _resident(w_out.shape), _resident((1, d))],
        out_specs=tile(d),
        out_shape=jax.ShapeDtypeStruct((rows, d), F32),
        compiler_params=pltpu.CompilerParams(dimension_semantics=("parallel",),
                                             vmem_limit_bytes=VMEM_LIMIT_BYTES),
    )(y_s5_cm, gs, y_hypre, m, h2d, w_glu, b_glu.reshape(1, sw), w_bs, w_bh, w_out, final_w.reshape(1, d))


def kernel(x, norm_w, w_in, s5_lam_re, s5_lam_im, s5_log_step, s5_b_re, s5_b_im, s5_c_re, s5_c_im, s5_d, s5_w_glu, s5_b_glu, hy_conv_w, hy_conv_b, hy_w1, hy_b1, hy_freq, hy_w2, hy_b2, hy_w3, hy_decay, hy_d, w_branch_s5, w_branch_hy, w_out, final_norm_w):
    batch, seq, d_model = x.shape
    depth = w_in.shape[0]
    s5_width = s5_d.shape[1]
    hy_width = hy_d.shape[2]
    widths = (s5_width, s5_width, (HY_ORDER + 1) * hy_width, hy_width, 2 * d_model)
    nb = HY_TIME_BLOCKS
    tables = _dft_tables(seq // nb)
    h = x.reshape(batch * seq, d_model)
    for l in range(depth):
        u, gs, p_hy, gh, m = _inproj(h, norm_w[l], w_in[l].astype(BF16), widths, batch, seq)
        s5_ops = _s5_operands(s5_lam_re[l], s5_lam_im[l], s5_log_step[l], s5_b_re[l], s5_b_im[l],
                              s5_c_re[l], s5_c_im[l], s5_d[l])
        y_s5 = _s5_scan(u, s5_ops, batch, seq)
        gr, gi = _hy_filters(seq, hy_w1[l], hy_b1[l], hy_freq[l], hy_w2[l], hy_b2[l], hy_w3[l], hy_decay[l],
                             hy_width, tables, nb)
        y_hypre = _hy_conv(p_hy.reshape(batch, seq, -1), gh.reshape(batch, seq, -1), hy_conv_w[l], hy_conv_b[l],
                           gr, gi, hy_d[l], tables, batch, seq, hy_width, nb)
        h = _merge(y_s5, gs, y_hypre.reshape(batch * seq, hy_width), m, h,
                   s5_w_glu[l].astype(BF16), s5_b_glu[l], w_branch_s5[l].astype(BF16),
                   w_branch_hy[l].astype(BF16), w_out[l].astype(BF16), final_norm_w, l == depth - 1, seq)
    return h.reshape(batch, seq, d_model)
```

```python
import functools
import math

import numpy as np
import jax
import jax.numpy as jnp
from jax import lax
from jax.experimental import pallas as pl
from jax.experimental.pallas import tpu as pltpu

F32 = jnp.float32
BF16 = jnp.bfloat16

RMS_EPS = 1e-6
S5_GROUP = 16
S5_CHUNK = 16
S5_GROUPS_PER_PAIR = 2
LANES = 128
SUBLANES = 8
HY_ORDER = 2
HY_BANDS = 16
HY_SHIFT = 0.05
HY_EPS = 1e-6
HY_TIME_BLOCKS = 4
HY_LAGS = 2 * HY_TIME_BLOCKS - 1
HY_COMBOS = 9
VMEM_LIMIT_BYTES = 56 * 1024 * 1024

_NT = (((1,), (1,)), ((), ()))


def _sigmoid(x):
    return 1.0 / (1.0 + jnp.exp(-x))


def _silu(x):
    return x * _sigmoid(x)


def _gelu_tanh(x):
    return 0.5 * x * (1.0 + jnp.tanh(math.sqrt(2.0 / math.pi) * (x + 0.044715 * (x * x * x))))


def _resident(shape):
    zeros = (0,) * len(shape)
    return pl.BlockSpec(shape, lambda *_: zeros, pipeline_mode=pl.Buffered(1))


def _chunk_major_spec(row_tile, seq, width):
    tiles_per_seq = seq // row_tile
    return pl.BlockSpec((row_tile // S5_CHUNK, None, S5_CHUNK, width),
                        lambda i: (i % tiles_per_seq, i // tiles_per_seq, 0, 0))


def _inproj_body(x_ref, nw_ref, w_ref, u_ref, gs_ref, p_ref, gh_ref, m_ref, *, bounds, col_chunk):
    x = x_ref[...]
    ms = jnp.mean(x * x, axis=-1, keepdims=True)
    xb = (x * lax.rsqrt(ms + RMS_EPS) * nw_ref[...]).astype(BF16)
    outs = ((u_ref, None), (gs_ref, _silu), (p_ref, None), (gh_ref, _silu), (m_ref, _sigmoid))
    for (out_ref, act), lo, hi in zip(outs, bounds[:-1], bounds[1:]):
        for c0 in range(lo, hi, col_chunk):
            y = jnp.dot(xb, w_ref[:, c0:c0 + col_chunk], preferred_element_type=F32)
            if act is not None:
                y = act(y)
            if out_ref is u_ref:
                out_ref[:, :, c0 - lo:c0 - lo + col_chunk] = y.astype(out_ref.dtype).reshape(
                    y.shape[0] // S5_CHUNK, S5_CHUNK, col_chunk)
            else:
                out_ref[:, c0 - lo:c0 - lo + col_chunk] = y.astype(out_ref.dtype)


def _inproj(h2d, norm_w, w_in_bf, widths, batch, seq, *, row_tile=256, col_chunk=512):
    rows, d = h2d.shape
    bounds = [0]
    for w in widths:
        bounds.append(bounds[-1] + w)
    body = functools.partial(_inproj_body, bounds=tuple(bounds), col_chunk=col_chunk)
    tile = lambda w: pl.BlockSpec((row_tile, w), lambda i: (i, 0))
    return pl.pallas_call(
        body,
        grid=(rows // row_tile,),
        in_specs=[tile(d), _resident((1, d)), _resident(w_in_bf.shape)],
        out_specs=[_chunk_major_spec(row_tile, seq, widths[0])] + [tile(w) for w in widths[1:]],
        out_shape=[jax.ShapeDtypeStruct((seq // S5_CHUNK, batch, S5_CHUNK, widths[0]), BF16)]
        + [jax.ShapeDtypeStruct((rows, w), BF16) for w in widths[1:]],
        compiler_params=pltpu.CompilerParams(dimension_semantics=("parallel",),
                                             vmem_limit_bytes=VMEM_LIMIT_BYTES),
    )(h2d, norm_w.reshape(1, d), w_in_bf)


def _s5_body(*refs, batch, chunks, pairs, pw, sl):
    t_steps = S5_CHUNK
    x_refs = refs[:t_steps]
    win_ref, woutt_ref, a_ref, sc_ref, dv_ref, y_ref, d_scr, st_scr = refs[t_steps:]
    k = t_steps * pw
    rows = batch * chunks
    assert k == 4 * sl

    for j in range(pairs):
        dj = jnp.concatenate([x_refs[t][:, j * pw:(j + 1) * pw] for t in range(t_steps)], axis=1)
        d_scr[:, j * k:(j + 1) * k] = dj
        win = jnp.concatenate([win_ref[c, j] for c in range(4)], axis=1).astype(BF16)
        st_scr[:, j * k:(j + 1) * k] = jnp.dot(dj, win, preferred_element_type=F32)

    coef = [[jnp.broadcast_to(a_ref[j, c:c + 1, :], (batch, sl)) for c in range(4)] for j in range(pairs)]

    def step(i, carry):
        row_f = pl.ds(pl.multiple_of(i * batch, batch), batch)
        row_b = pl.ds(pl.multiple_of((chunks - 1 - i) * batch, batch), batch)
        out = []
        for j in range(pairs):
            for direction, r in ((0, row_f), (1, row_b)):
                sr, si = carry[4 * j + 2 * direction], carry[4 * j + 2 * direction + 1]
                ar, ai = coef[j][2 * direction], coef[j][2 * direction + 1]
                c_re = j * k + 2 * direction * sl
                c_im = c_re + sl
                ir, ii = st_scr[r, c_re:c_re + sl], st_scr[r, c_im:c_im + sl]
                st_scr[r, c_re:c_re + sl] = sr
                st_scr[r, c_im:c_im + sl] = si
                out += [ar * sr - ai * si + ir, ar * si + ai * sr + ii]
        return tuple(out)

    zero = jnp.zeros((batch, sl), F32)
    lax.fori_loop(0, chunks, step, (zero,) * (4 * pairs))

    in_step = lax.broadcasted_iota(jnp.int32, (k, k), 0) // pw
    out_step = lax.broadcasted_iota(jnp.int32, (k, k), 1) // pw
    for j in range(pairs):
        sc = sc_ref[j]
        kern = None
        for direction in range(2):
            wr, wi = win_ref[2 * direction, j], win_ref[2 * direction + 1, j]
            pr, pi = sc[2 * direction:2 * direction + 1, :], sc[2 * direction + 1:2 * direction + 2, :]
            wp = jnp.concatenate([wr * pr - wi * pi, wr * pi + wi * pr], axis=1).astype(BF16)
            wt = jnp.concatenate([woutt_ref[2 * direction, j], woutt_ref[2 * direction + 1, j]], axis=1)
            kd = lax.dot_general(wp, wt, _NT, preferred_element_type=F32)
            kd = jnp.where(out_step >= in_step if direction == 0 else out_step <= in_step, kd, 0.0)
            kern = kd if kern is None else kern + kd
        dj = d_scr[:, j * k:(j + 1) * k]
        wt = jnp.concatenate([woutt_ref[c, j] for c in range(4)], axis=1)
        y = jnp.dot(dj, kern.astype(BF16), preferred_element_type=F32)
        y = y + lax.dot_general(st_scr[:, j * k:(j + 1) * k].astype(BF16), wt, _NT, preferred_element_type=F32)
        y = y + dj.astype(F32) * dv_ref[j]
        st_scr[:, j * k:(j + 1) * k] = y
    for t in range(t_steps):
        y_ref[pl.ds(t, rows, stride=t_steps), :] = jnp.concatenate(
            [st_scr[:, j * k + t * pw:j * k + (t + 1) * pw] for j in range(pairs)], axis=1)


def _s5_operands(lam_re, lam_im, log_step, b_re, b_im, c_re, c_im, d_skip):
    _, groups, p = lam_re.shape
    hh, t, gp = S5_GROUP, S5_CHUNK, S5_GROUPS_PER_PAIR
    npair = groups // gp
    rep = lambda v: jnp.concatenate([v] * gp, axis=-1)
    own = (jnp.arange(gp * p)[None, :] // p == jnp.arange(groups)[:, None] % gp).astype(F32)
    lam_re, lam_im = rep(lam_re), rep(lam_im)
    dt = jnp.exp(log_step)[..., None]
    xr, xi = lam_re * dt, lam_im * dt
    lbr, lbi = jnp.exp(xr) * jnp.cos(xi), jnp.exp(xr) * jnp.sin(xi)
    nr, ni = lbr - 1.0, lbi
    den = lam_re * lam_re + lam_im * lam_im
    zr, zi = (nr * lam_re + ni * lam_im) / den * own, (ni * lam_re - nr * lam_im) / den * own
    btr, bti = rep(jnp.swapaxes(b_re, -1, -2)), rep(jnp.swapaxes(b_im, -1, -2))
    bbr = zr[:, :, None, :] * btr - zi[:, :, None, :] * bti
    bbi = zr[:, :, None, :] * bti + zi[:, :, None, :] * btr
    ccr, cci = rep(c_re) * own[None, :, None, :], rep(c_im) * own[None, :, None, :]

    def powers(e):
        kk = e[:, None, :, None]
        mag, ang = jnp.exp(kk * xr[:, :, None, :]), kk * xi[:, :, None, :]
        return (mag * jnp.cos(ang))[:, :, :, None, :], (mag * jnp.sin(ang))[:, :, :, None, :]

    tt = jnp.arange(t, dtype=F32)
    pir, pii = powers(jnp.stack([t - 1 - tt, tt]))
    por, poi = powers(jnp.stack([tt + 1, t - tt]))
    b5r, b5i = bbr[:, :, None], bbi[:, :, None]
    c5r, c5i = ccr[:, :, None], cci[:, :, None]
    win_r, win_i = pir * b5r - pii * b5i, pir * b5i + pii * b5r
    out_r, out_i = por * c5r - poi * c5i, por * c5i + poi * c5r

    def pair_layout(x_re, x_im):
        comp = jnp.stack([x_re[0], x_im[0], x_re[1], x_im[1]]).reshape(4, npair, gp, t, hh, gp * p)
        return jnp.transpose(comp, (0, 1, 3, 2, 4, 5)).reshape(4, npair, t * gp * hh, gp * p)

    def pair_vec(e):
        mag, ang = jnp.exp(e * xr) * own, e * xi
        v = jnp.stack([(mag * jnp.cos(ang))[0], (mag * jnp.sin(ang))[0],
                       (mag * jnp.cos(ang))[1], (mag * jnp.sin(ang))[1]])
        return jnp.transpose(v.reshape(4, npair, gp, gp * p).sum(axis=2), (1, 0, 2))

    dvec = jnp.broadcast_to(d_skip.reshape(npair, 1, gp, hh), (npair, t, gp, hh)).reshape(npair, 1, t * gp * hh)
    return (pair_layout(win_r, win_i), pair_layout(out_r, -out_i).astype(BF16),
            pair_vec(float(t)), pair_vec(-float(t)), dvec)


def _s5_scan(u_cm, operands, batch, seq):
    win, woutt, a, sc, dvec = operands
    _, npair, k, sl = win.shape
    t = S5_CHUNK
    pw = k // t
    pairs = LANES // pw
    chunks = seq // t
    rows = chunks * batch
    width = npair * pw
    nblk = width // LANES
    body = functools.partial(_s5_body, batch=batch, chunks=chunks, pairs=pairs, pw=pw, sl=sl)
    per_blk = lambda shape: pl.BlockSpec((pairs,) + shape, lambda q: (q, 0, 0))
    comp_blk = pl.BlockSpec((4, pairs, k, sl), lambda q: (0, q, 0, 0), pipeline_mode=pl.Buffered(1))
    x2d = u_cm.reshape(rows, t * width)
    y = pl.pallas_call(
        body,
        grid=(nblk,),
        in_specs=[pl.BlockSpec((rows, LANES), functools.partial(lambda q, s: (0, s * nblk + q), s=s))
                  for s in range(t)]
        + [comp_blk, comp_blk, per_blk((4, sl)), per_blk((4, sl)), per_blk((1, k))],
        out_specs=pl.BlockSpec((rows * t, LANES), lambda q: (0, q), pipeline_mode=pl.Buffered(1)),
        out_shape=jax.ShapeDtypeStruct((rows * t, width), F32),
        scratch_shapes=[pltpu.VMEM((rows, pairs * k), BF16), pltpu.VMEM((rows, pairs * k), F32)],
        compiler_params=pltpu.CompilerParams(dimension_semantics=("parallel",),
                                             vmem_limit_bytes=VMEM_LIMIT_BYTES),
    )(*([x2d] * t), win, woutt, a, sc, dvec)
    return y.reshape(chunks, batch, t, width)


def _dft_tables(blk):
    n = 2 * blk
    f = np.arange(blk, dtype=np.int64)
    sym = ((2 * f[:, None] + 1) * (2 * f[None, :] + 1)) % (4 * n)
    ang_sym = sym.astype(np.float64) * (2.0 * np.pi / (4 * n))
    lag = np.arange(2 * blk, dtype=np.int64) - blk
    ang_lag = (((2 * f[:, None] + 1) * lag[None, :]) % (2 * n)).astype(np.float64) * (2.0 * np.pi / (2 * n))
    tc, ts = np.cos(ang_lag), np.sin(ang_lag)
    tc[:, 0] = 0.0
    ts[:, 0] = 0.0
    as_bf = lambda x: jnp.asarray(x.astype(np.float32)).astype(BF16)
    return as_bf(np.cos(ang_sym)), as_bf(np.sin(ang_sym)), as_bf(tc), as_bf(ts)


def _position_features(seq, pad_to):
    t = np.linspace(0.0, 1.0, seq)[:, None]
    pos = np.arange(seq, dtype=np.float64)[:, None]
    bands = np.linspace(1e-4, HY_BANDS - 1, HY_BANDS)[None, :]
    ang = bands * pos * (2.0 * math.pi / seq)
    feats = np.zeros((seq, pad_to), np.float64)
    feats[:, :1 + 2 * HY_BANDS] = np.concatenate([t, np.cos(ang), -np.sin(ang)], axis=-1)
    mirror = lambda x: np.roll(x[::-1], 1, axis=0)
    f32 = lambda x: jnp.asarray(x.astype(np.float32))
    return f32(feats), f32(t), f32(mirror(feats)), f32(mirror(t))


_HY_COMBO_TERMS = (
    ((0, 1),), ((-1, 1), (0, -1)), ((1, 1), (0, -1)),
    ((-2, 1), (0, -1)), ((-3, 1), (-1, -1), (-2, -1), (0, 1)), ((-1, 1), (1, -1), (-2, -1), (0, 1)),
    ((2, 1), (0, -1)), ((1, 1), (-1, -1), (2, -1), (0, 1)), ((3, 1), (1, -1), (2, -1), (0, 1)),
)


def _hy_filter_body(feats_ref, t_ref, featsm_ref, tm_ref, w1_ref, b1_ref, fr_ref, w2_ref, b2_ref, w3f_ref,
                    w3b_ref, decf_ref, decb_ref, tc_ref, ts_ref, gr_ref, gi_ref, h2_scr, h2m_scr, k2_scr,
                    kr_scr, ki_scr, *, seq, nb):
    hi = lax.Precision.HIGHEST
    blk = seq // nb

    @pl.when(pl.program_id(0) == 0)
    def _():
        fr = fr_ref[...]

        def mlp(feats):
            h1 = jnp.sin(fr * (jnp.dot(feats, w1_ref[...], precision=hi, preferred_element_type=F32) + b1_ref[...]))
            return jnp.sin(fr * (jnp.dot(h1, w2_ref[...], precision=hi, preferred_element_type=F32) + b2_ref[...]))

        def split(h):
            top = h.astype(BF16).astype(F32)
            return jnp.concatenate([top, h - top, top, h - top], axis=1).astype(BF16)

        h2_scr[...] = split(mlp(feats_ref[...]))
        h2m_scr[...] = split(mlp(featsm_ref[...]))

    def split_rows(w):
        top = w.astype(BF16).astype(F32)
        return jnp.concatenate([top, top, w - top, w - top], axis=0).astype(BF16)

    hf = jnp.dot(h2_scr[...], split_rows(w3f_ref[...]), preferred_element_type=F32)
    hf = hf * (jnp.exp(-t_ref[...] * jnp.abs(decf_ref[...])) + HY_SHIFT)
    hb = jnp.dot(h2m_scr[...], split_rows(w3b_ref[...]), preferred_element_type=F32)
    hb = hb * (jnp.exp(-tm_ref[...] * jnp.abs(decb_ref[...])) + HY_SHIFT)
    row = lax.broadcasted_iota(jnp.int32, hb.shape, 0)
    hb = jnp.where(row == 0, 0.0, hb)
    norm = jnp.sum(hf * hf, axis=0, keepdims=True) + jnp.sum(hb * hb, axis=0, keepdims=True)
    scale = lax.rsqrt(norm + HY_EPS) * (1.0 / blk)
    k2_scr[0:seq, :] = hb.astype(BF16)
    k2_scr[seq:2 * seq, :] = hf.astype(BF16)
    for d in range(2 * nb - 1):
        seg = k2_scr[d * blk:(d + 2) * blk, :]
        kr_scr[d] = jnp.dot(tc_ref[...], seg, preferred_element_type=F32) * scale
        ki_scr[d] = jnp.dot(ts_ref[...], seg, preferred_element_type=F32) * scale
    for idx, terms in enumerate(_HY_COMBO_TERMS):
        for src, dst in ((kr_scr, gr_ref), (ki_scr, gi_ref)):
            acc = None
            for lag, weight in terms:
                term = src[lag + nb - 1]
                acc = (term if weight > 0 else -term) if acc is None else (acc + term if weight > 0 else acc - term)
            dst[0, idx] = acc


def _hy_filters(seq, w1, b1, freq, w2, b2, w3, decay, width, tables, nb, *, col_block=256):
    _, _, tc, ts = tables
    blk = seq // nb
    ffn = w2.shape[0]
    feats, t, featsm, tm = _position_features(seq, LANES)
    w1p = jnp.zeros((LANES, ffn), F32).at[:w1.shape[0]].set(w1)
    nblk = width // col_block
    side = lambda rows, direction: pl.BlockSpec(
        (rows, col_block), lambda i: (0, (2 * (i // nblk) + direction) * nblk + i % nblk))
    out = pl.BlockSpec((1, HY_COMBOS, blk, col_block), lambda i: (i // nblk, 0, 0, i % nblk))
    body = functools.partial(_hy_filter_body, seq=seq, nb=nb)
    row2 = lambda v: v.reshape(1, -1)
    return pl.pallas_call(
        body,
        grid=(HY_ORDER * nblk,),
        in_specs=[_resident(feats.shape), _resident(t.shape), _resident(feats.shape), _resident(t.shape),
                  _resident(w1p.shape), _resident((1, ffn)), _resident((1, ffn)), _resident(w2.shape),
                  _resident((1, ffn)), side(ffn, 0), side(ffn, 1), side(1, 0), side(1, 1),
                  _resident(tc.shape), _resident(ts.shape)],
        out_specs=[out, out],
        out_shape=[jax.ShapeDtypeStruct((HY_ORDER, HY_COMBOS, blk, width), F32)] * 2,
        scratch_shapes=[pltpu.VMEM((seq, 4 * ffn), BF16), pltpu.VMEM((seq, 4 * ffn), BF16),
                        pltpu.VMEM((2 * seq, col_block), BF16),
                        pltpu.VMEM((2 * nb - 1, blk, col_block), F32), pltpu.VMEM((2 * nb - 1, blk, col_block), F32)],
        compiler_params=pltpu.CompilerParams(dimension_semantics=("arbitrary",),
                                             vmem_limit_bytes=VMEM_LIMIT_BYTES),
    )(feats, t, featsm, tm, w1p, row2(b1), row2(freq), w2, row2(b2), w3, w3, row2(decay), row2(decay), tc, ts)


def _hy_conv_body(pv_ref, p1_ref, p2_ref, g_ref, cw_ref, cb_ref, gr_ref, gi_ref, d_ref, mc_ref, ms_ref, o_ref,
                  z_scr, g_scr, a_scr, b_scr, *, seq, nb, cw, row_chunk, problems):
    blk = seq // nb
    gate_refs = (p1_ref, p2_ref)
    lanes = lambda j: slice(j * cw, (j + 1) * cw)
    cadd = lambda x, y: (x[0] + y[0], x[1] + y[1])
    cmul = lambda x, y: (x[0] * y[0] - x[1] * y[1], x[0] * y[1] + x[1] * y[0])

    def short_conv(k, p_ref, part, dst):
        x = p_ref[k].astype(F32)
        row = lax.broadcasted_iota(jnp.int32, x.shape, 0)
        prev = jnp.where(row == 0, 0.0, pltpu.roll(x, 1, 0))
        nxt = jnp.where(row == seq - 1, 0.0, pltpu.roll(x, seq - 1, 0))
        w = cw_ref[:, part, :]
        y = cb_ref[part:part + 1, :] + prev * w[0:1] + x * w[1:2] + nxt * w[2:3]
        for j in range(nb):
            dst[k, :, lanes(j)] = y[j * blk:(j + 1) * blk, :]

    def forward(k, o):
        zb = z_scr[k].astype(BF16)
        a_scr[k] = jnp.dot(mc_ref[...], zb, preferred_element_type=F32)
        b_scr[k] = jnp.dot(ms_ref[...], zb, preferred_element_type=F32)

    def pointwise(k, o):
        def toeplitz2(base, x0, x1, r):
            g = lambda idx: (gr_ref[o, idx, r, :], gi_ref[o, idx, r, :])
            q = cmul(g(base), cadd(x0, x1))
            return cadd(q, cmul(g(base + 1), x1)), cadd(q, cmul(g(base + 2), x0))

        for c in range(blk // row_chunk):
            r = slice(c * row_chunk, (c + 1) * row_chunk)
            z = [(a_scr[k, r, lanes(j)], b_scr[k, r, lanes(j)]) for j in range(nb)]
            p1 = toeplitz2(0, cadd(z[0], z[2]), cadd(z[1], z[3]), r)
            p2 = toeplitz2(3, z[2], z[3], r)
            p3 = toeplitz2(6, z[0], z[1], r)
            for i, y in enumerate((cadd(p1[0], p2[0]), cadd(p1[1], p2[1]), cadd(p1[0], p3[0]), cadd(p1[1], p3[1]))):
                a_scr[k, r, lanes(i)], b_scr[k, r, lanes(i)] = y

    def inverse(k, o):
        conv = (jnp.dot(mc_ref[...], a_scr[k].astype(BF16), preferred_element_type=F32)
                + jnp.dot(ms_ref[...], b_scr[k].astype(BF16), preferred_element_type=F32))
        short_conv(k, gate_refs[o], o + 1, g_scr)
        dsk = jnp.concatenate([d_ref[o:o + 1, :]] * nb, axis=1)
        z = g_scr[k] * (conv + dsk * z_scr[k])
        if o == HY_ORDER - 1:
            for j in range(nb):
                rows = slice(j * blk, (j + 1) * blk)
                o_ref[k, rows, :] = (z[:, lanes(j)] * g_ref[k, rows, :].astype(F32)).astype(o_ref.dtype)
        else:
            z_scr[k] = z

    stages = [lambda k: short_conv(k, pv_ref, 0, z_scr)]
    for o in range(HY_ORDER):
        stages += [functools.partial(forward, o=o), functools.partial(pointwise, o=o),
                   functools.partial(inverse, o=o)]
    for s in range(len(stages) + problems - 1):
        for k in range(problems):
            if 0 <= s - k < len(stages):
                stages[s - k](k)


def _hy_conv(p_hy, gate_silu, conv_w, conv_b, gr, gi, d_skip, tables, batch, seq, width, nb, *, col_block=LANES,
             row_chunk=SUBLANES, problems=2):
    assert nb == HY_TIME_BLOCKS
    mc, ms, _, _ = tables
    blk = seq // nb
    nblk = width // col_block
    part = lambda k: pl.BlockSpec((problems, seq, col_block), lambda c, b: (b, 0, k * nblk + c))
    cw = conv_w.reshape(conv_w.shape[0], HY_ORDER + 1, width)
    cb = conv_b.reshape(HY_ORDER + 1, width)
    filt = pl.BlockSpec((HY_ORDER, HY_COMBOS, blk, col_block), lambda c, b: (0, 0, 0, c),
                        pipeline_mode=pl.Buffered(1))
    body = functools.partial(_hy_conv_body, seq=seq, nb=nb, cw=col_block, row_chunk=row_chunk, problems=problems)
    wide = nb * col_block
    scratch = lambda dtype: pltpu.VMEM((problems, blk, wide), dtype)
    return pl.pallas_call(
        body,
        grid=(nblk, batch // problems),
        in_specs=[part(0), part(1), part(2),
                  pl.BlockSpec((problems, seq, col_block), lambda c, b: (b, 0, c)),
                  pl.BlockSpec((cw.shape[0], HY_ORDER + 1, col_block), lambda c, b: (0, 0, c)),
                  pl.BlockSpec((HY_ORDER + 1, col_block), lambda c, b: (0, c)),
                  filt, filt,
                  pl.BlockSpec((HY_ORDER, col_block), lambda c, b: (0, c)),
                  _resident(mc.shape), _resident(ms.shape)],
        out_specs=pl.BlockSpec((problems, seq, col_block), lambda c, b: (b, 0, c)),
        out_shape=jax.ShapeDtypeStruct((batch, seq, width), BF16),
        scratch_shapes=[scratch(F32), scratch(F32), scratch(F32), scratch(F32)],
        compiler_params=pltpu.CompilerParams(dimension_semantics=("parallel", "parallel"),
                                             vmem_limit_bytes=VMEM_LIMIT_BYTES),
    )(p_hy, p_hy, p_hy, gate_silu, cw, cb, gr, gi, d_skip, mc, ms)


def _merge_body(ys_ref, gs_ref, yh_ref, m_ref, h_ref, wg_ref, bg_ref, wbs_ref, wbh_ref, wo_ref, fw_ref, o_ref,
                *, d_model, final_norm):
    ys = ys_ref[...]
    y = _gelu_tanh(ys.reshape(ys.shape[0] * ys.shape[1], ys.shape[2]))
    glu = y * _sigmoid(jnp.dot(y.astype(BF16), wg_ref[...], preferred_element_type=F32) + bg_ref[...])
    s5 = (glu * gs_ref[...].astype(F32)).astype(BF16)
    y_s5 = jnp.dot(s5, wbs_ref[...], preferred_element_type=F32)
    y_hy = jnp.dot(yh_ref[...], wbh_ref[...], preferred_element_type=F32)
    merged = m_ref[:, :d_model].astype(F32) * y_s5 + m_ref[:, d_model:].astype(F32) * y_hy
    h = h_ref[...] + jnp.dot(merged.astype(BF16), wo_ref[...], preferred_element_type=F32)
    if final_norm:
        ms = jnp.mean(h * h, axis=-1, keepdims=True)
        h = h * lax.rsqrt(ms + RMS_EPS) * fw_ref[...]
    o_ref[...] = h


def _merge(y_s5_cm, gs, y_hypre, m, h2d, w_glu, b_glu, w_bs, w_bh, w_out, final_w, final_norm, seq, *,
           row_tile=512):
    rows, d = h2d.shape
    sw = y_s5_cm.shape[-1]
    hw = y_hypre.shape[1]
    tile = lambda w: pl.BlockSpec((row_tile, w), lambda i: (i, 0))
    body = functools.partial(_merge_body, d_model=d, final_norm=final_norm)
    return pl.pallas_call(
        body,
        grid=(rows // row_tile,),
        in_specs=[_chunk_major_spec(row_tile, seq, sw), tile(sw), tile(hw), tile(2 * d), tile(d),
                  _resident(w_glu.shape), _resident((1, sw)), _resident(w_bs.shape), _resident(w_bh.shape),
                  _resident(w_out.shape), _resident((1, d))],
        out_specs=tile(d),
        out_shape=jax.ShapeDtypeStruct((rows, d), F32),
        compiler_params=pltpu.CompilerParams(dimension_semantics=("parallel",),
                                             vmem_limit_bytes=VMEM_LIMIT_BYTES),
    )(y_s5_cm, gs, y_hypre, m, h2d, w_glu, b_glu.reshape(1, sw), w_bs, w_bh, w_out, final_w.reshape(1, d))


def kernel(x, norm_w, w_in, s5_lam_re, s5_lam_im, s5_log_step, s5_b_re, s5_b_im, s5_c_re, s5_c_im, s5_d, s5_w_glu, s5_b_glu, hy_conv_w, hy_conv_b, hy_w1, hy_b1, hy_freq, hy_w2, hy_b2, hy_w3, hy_decay, hy_d, w_branch_s5, w_branch_hy, w_out, final_norm_w):
    batch, seq, d_model = x.shape
    depth = w_in.shape[0]
    s5_width = s5_d.shape[1]
    hy_width = hy_d.shape[2]
    widths = (s5_width, s5_width, (HY_ORDER + 1) * hy_width, hy_width, 2 * d_model)
    nb = HY_TIME_BLOCKS
    tables = _dft_tables(seq // nb)
    h = x.reshape(batch * seq, d_model)
    for l in range(depth):
        u, gs, p_hy, gh, m = _inproj(h, norm_w[l], w_in[l].astype(BF16), widths, batch, seq)
        s5_ops = _s5_operands(s5_lam_re[l], s5_lam_im[l], s5_log_step[l], s5_b_re[l], s5_b_im[l],
                              s5_c_re[l], s5_c_im[l], s5_d[l])
        y_s5 = _s5_scan(u, s5_ops, batch, seq)
        gr, gi = _hy_filters(seq, hy_w1[l], hy_b1[l], hy_freq[l], hy_w2[l], hy_b2[l], hy_w3[l], hy_decay[l],
                             hy_width, tables, nb)
        y_hypre = _hy_conv(p_hy.reshape(batch, seq, -1), gh.reshape(batch, seq, -1), hy_conv_w[l], hy_conv_b[l],
                           gr, gi, hy_d[l], tables, batch, seq, hy_width, nb)
        h = _merge(y_s5, gs, y_hypre.reshape(batch * seq, hy_width), m, h,
                   s5_w_glu[l].astype(BF16), s5_b_glu[l], w_branch_s5[l].astype(BF16),
                   w_branch_hy[l].astype(BF16), w_out[l].astype(BF16), final_norm_w, l == depth - 1, seq)
    return h.reshape(batch, seq, d_model)
```

```python
import functools
import math

import numpy as np
import jax
import jax.numpy as jnp
from jax import lax
from jax.experimental import pallas as pl
from jax.experimental.pallas import tpu as pltpu

F32 = jnp.float32
BF16 = jnp.bfloat16

RMS_EPS = 1e-6
S5_GROUP = 16
S5_CHUNK = 16
S5_GROUPS_PER_PAIR = 2
LANES = 128
SUBLANES = 8
HY_ORDER = 2
HY_BANDS = 16
HY_SHIFT = 0.05
HY_EPS = 1e-6
HY_TIME_BLOCKS = 4
HY_LAGS = 2 * HY_TIME_BLOCKS - 1
HY_COMBOS = 9
VMEM_LIMIT_BYTES = 56 * 1024 * 1024

_NT = (((1,), (1,)), ((), ()))


def _sigmoid(x):
    return 1.0 / (1.0 + jnp.exp(-x))


def _silu(x):
    return x * _sigmoid(x)


def _gelu_tanh(x):
    return 0.5 * x * (1.0 + jnp.tanh(math.sqrt(2.0 / math.pi) * (x + 0.044715 * (x * x * x))))


def _resident(shape):
    zeros = (0,) * len(shape)
    return pl.BlockSpec(shape, lambda *_: zeros, pipeline_mode=pl.Buffered(1))


def _resident_layer(stacked_shape, layer):
    zeros = (0,) * (len(stacked_shape) - 1)
    return pl.BlockSpec((None,) + tuple(stacked_shape[1:]), lambda *_: (layer,) + zeros,
                        pipeline_mode=pl.Buffered(1))


def _chunk_major_spec(row_tile, seq, width):
    tiles_per_seq = seq // row_tile
    return pl.BlockSpec((row_tile // S5_CHUNK, None, S5_CHUNK, width),
                        lambda i: (i % tiles_per_seq, i // tiles_per_seq, 0, 0))


def _inproj_body(x_ref, nw_ref, w_ref, u_ref, gs_ref, p_ref, gh_ref, m_ref, *, bounds, col_chunk):
    x = x_ref[...]
    ms = jnp.mean(x * x, axis=-1, keepdims=True)
    xb = (x * lax.rsqrt(ms + RMS_EPS) * nw_ref[...]).astype(BF16)
    outs = ((u_ref, None), (gs_ref, _silu), (p_ref, None), (gh_ref, _silu), (m_ref, _sigmoid))
    for (out_ref, act), lo, hi in zip(outs, bounds[:-1], bounds[1:]):
        for c0 in range(lo, hi, col_chunk):
            y = jnp.dot(xb, w_ref[:, c0:c0 + col_chunk], preferred_element_type=F32)
            if act is not None:
                y = act(y)
            if out_ref is u_ref:
                out_ref[:, :, c0 - lo:c0 - lo + col_chunk] = y.astype(out_ref.dtype).reshape(
                    y.shape[0] // S5_CHUNK, S5_CHUNK, col_chunk)
            else:
                out_ref[:, c0 - lo:c0 - lo + col_chunk] = y.astype(out_ref.dtype)


def _inproj(h2d, norm_w, w_in_bf, layer, widths, batch, seq, *, row_tile=256, col_chunk=512):
    rows, d = h2d.shape
    bounds = [0]
    for w in widths:
        bounds.append(bounds[-1] + w)
    body = functools.partial(_inproj_body, bounds=tuple(bounds), col_chunk=col_chunk)
    tile = lambda w: pl.BlockSpec((row_tile, w), lambda i: (i, 0))
    return pl.pallas_call(
        body,
        grid=(rows // row_tile,),
        in_specs=[tile(d), _resident_layer((norm_w.shape[0], 1, d), layer), _resident_layer(w_in_bf.shape, layer)],
        out_specs=[_chunk_major_spec(row_tile, seq, widths[0])] + [tile(w) for w in widths[1:]],
        out_shape=[jax.ShapeDtypeStruct((seq // S5_CHUNK, batch, S5_CHUNK, widths[0]), BF16)]
        + [jax.ShapeDtypeStruct((rows, w), BF16) for w in widths[1:]],
        compiler_params=pltpu.CompilerParams(dimension_semantics=("parallel",),
                                             vmem_limit_bytes=VMEM_LIMIT_BYTES),
    )(h2d, norm_w.reshape(-1, 1, d), w_in_bf)


def _s5_body(*refs, batch, chunks, pairs, pw, sl):
    t_steps = S5_CHUNK
    x_refs = refs[:t_steps]
    win_ref, woutt_ref, a_ref, sc_ref, dv_ref, y_ref, d_scr, st_scr = refs[t_steps:]
    k = t_steps * pw
    rows = batch * chunks
    assert k == 4 * sl

    for j in range(pairs):
        dj = jnp.concatenate([x_refs[t][:, j * pw:(j + 1) * pw] for t in range(t_steps)], axis=1)
        d_scr[:, j * k:(j + 1) * k] = dj
        win = jnp.concatenate([win_ref[c, j] for c in range(4)], axis=1).astype(BF16)
        st_scr[:, j * k:(j + 1) * k] = jnp.dot(dj, win, preferred_element_type=F32)

    coef = [[jnp.broadcast_to(a_ref[j, c:c + 1, :], (batch, sl)) for c in range(4)] for j in range(pairs)]

    def step(i, carry):
        row_f = pl.ds(pl.multiple_of(i * batch, batch), batch)
        row_b = pl.ds(pl.multiple_of((chunks - 1 - i) * batch, batch), batch)
        out = []
        for j in range(pairs):
            for direction, r in ((0, row_f), (1, row_b)):
                sr, si = carry[4 * j + 2 * direction], carry[4 * j + 2 * direction + 1]
                ar, ai = coef[j][2 * direction], coef[j][2 * direction + 1]
                c_re = j * k + 2 * direction * sl
                c_im = c_re + sl
                ir, ii = st_scr[r, c_re:c_re + sl], st_scr[r, c_im:c_im + sl]
                st_scr[r, c_re:c_re + sl] = sr
                st_scr[r, c_im:c_im + sl] = si
                out += [ar * sr - ai * si + ir, ar * si + ai * sr + ii]
        return tuple(out)

    zero = jnp.zeros((batch, sl), F32)
    lax.fori_loop(0, chunks, step, (zero,) * (4 * pairs))

    in_step = lax.broadcasted_iota(jnp.int32, (k, k), 0) // pw
    out_step = lax.broadcasted_iota(jnp.int32, (k, k), 1) // pw
    for j in range(pairs):
        sc = sc_ref[j]
        kern = None
        for direction in range(2):
            wr, wi = win_ref[2 * direction, j], win_ref[2 * direction + 1, j]
            pr, pi = sc[2 * direction:2 * direction + 1, :], sc[2 * direction + 1:2 * direction + 2, :]
            wp = jnp.concatenate([wr * pr - wi * pi, wr * pi + wi * pr], axis=1).astype(BF16)
            wt = jnp.concatenate([woutt_ref[2 * direction, j], woutt_ref[2 * direction + 1, j]], axis=1)
            kd = lax.dot_general(wp, wt, _NT, preferred_element_type=F32)
            kd = jnp.where(out_step >= in_step if direction == 0 else out_step <= in_step, kd, 0.0)
            kern = kd if kern is None else kern + kd
        dj = d_scr[:, j * k:(j + 1) * k]
        wt = jnp.concatenate([woutt_ref[c, j] for c in range(4)], axis=1)
        y = jnp.dot(dj, kern.astype(BF16), preferred_element_type=F32)
        y = y + lax.dot_general(st_scr[:, j * k:(j + 1) * k].astype(BF16), wt, _NT, preferred_element_type=F32)
        y = y + dj.astype(F32) * dv_ref[j]
        st_scr[:, j * k:(j + 1) * k] = y
    for t in range(t_steps):
        y_ref[pl.ds(t, rows, stride=t_steps), :] = jnp.concatenate(
            [st_scr[:, j * k + t * pw:j * k + (t + 1) * pw] for j in range(pairs)], axis=1)


def _s5_operands(lam_re, lam_im, log_step, b_re, b_im, c_re, c_im, d_skip):
    _, groups, p = lam_re.shape
    hh, t, gp = S5_GROUP, S5_CHUNK, S5_GROUPS_PER_PAIR
    npair = groups // gp
    rep = lambda v: jnp.concatenate([v] * gp, axis=-1)
    own = (jnp.arange(gp * p)[None, :] // p == jnp.arange(groups)[:, None] % gp).astype(F32)
    lam_re, lam_im = rep(lam_re), rep(lam_im)
    dt = jnp.exp(log_step)[..., None]
    xr, xi = lam_re * dt, lam_im * dt
    lbr, lbi = jnp.exp(xr) * jnp.cos(xi), jnp.exp(xr) * jnp.sin(xi)
    nr, ni = lbr - 1.0, lbi
    den = lam_re * lam_re + lam_im * lam_im
    zr, zi = (nr * lam_re + ni * lam_im) / den * own, (ni * lam_re - nr * lam_im) / den * own
    btr, bti = rep(jnp.swapaxes(b_re, -1, -2)), rep(jnp.swapaxes(b_im, -1, -2))
    bbr = zr[:, :, None, :] * btr - zi[:, :, None, :] * bti
    bbi = zr[:, :, None, :] * bti + zi[:, :, None, :] * btr
    ccr, cci = rep(c_re) * own[None, :, None, :], rep(c_im) * own[None, :, None, :]

    def powers(e):
        kk = e[:, None, :, None]
        mag, ang = jnp.exp(kk * xr[:, :, None, :]), kk * xi[:, :, None, :]
        return (mag * jnp.cos(ang))[:, :, :, None, :], (mag * jnp.sin(ang))[:, :, :, None, :]

    tt = jnp.arange(t, dtype=F32)
    pir, pii = powers(jnp.stack([t - 1 - tt, tt]))
    por, poi = powers(jnp.stack([tt + 1, t - tt]))
    b5r, b5i = bbr[:, :, None], bbi[:, :, None]
    c5r, c5i = ccr[:, :, None], cci[:, :, None]
    win_r, win_i = pir * b5r - pii * b5i, pir * b5i + pii * b5r
    out_r, out_i = por * c5r - poi * c5i, por * c5i + poi * c5r

    def pair_layout(x_re, x_im):
        comp = jnp.stack([x_re[0], x_im[0], x_re[1], x_im[1]]).reshape(4, npair, gp, t, hh, gp * p)
        return jnp.transpose(comp, (0, 1, 3, 2, 4, 5)).reshape(4, npair, t * gp * hh, gp * p)

    def pair_vec(e):
        mag, ang = jnp.exp(e * xr) * own, e * xi
        v = jnp.stack([(mag * jnp.cos(ang))[0], (mag * jnp.sin(ang))[0],
                       (mag * jnp.cos(ang))[1], (mag * jnp.sin(ang))[1]])
        return jnp.transpose(v.reshape(4, npair, gp, gp * p).sum(axis=2), (1, 0, 2))

    dvec = jnp.broadcast_to(d_skip.reshape(npair, 1, gp, hh), (npair, t, gp, hh)).reshape(npair, 1, t * gp * hh)
    return (pair_layout(win_r, win_i), pair_layout(out_r, -out_i).astype(BF16),
            pair_vec(float(t)), pair_vec(-float(t)), dvec)


def _s5_scan(u_cm, operands, layer, batch, seq):
    win, woutt, a, sc, dvec = operands
    _, _, npair, k, sl = win.shape
    t = S5_CHUNK
    pw = k // t
    pairs = LANES // pw
    chunks = seq // t
    rows = chunks * batch
    width = npair * pw
    nblk = width // LANES
    body = functools.partial(_s5_body, batch=batch, chunks=chunks, pairs=pairs, pw=pw, sl=sl)
    per_blk = lambda shape: pl.BlockSpec((None, pairs) + shape, lambda q: (layer, q, 0, 0))
    comp_blk = pl.BlockSpec((None, 4, pairs, k, sl), lambda q: (layer, 0, q, 0, 0), pipeline_mode=pl.Buffered(1))
    x2d = u_cm.reshape(rows, t * width)
    y = pl.pallas_call(
        body,
        grid=(nblk,),
        in_specs=[pl.BlockSpec((rows, LANES), functools.partial(lambda q, s: (0, s * nblk + q), s=s))
                  for s in range(t)]
        + [comp_blk, comp_blk, per_blk((4, sl)), per_blk((4, sl)), per_blk((1, k))],
        out_specs=pl.BlockSpec((rows * t, LANES), lambda q: (0, q), pipeline_mode=pl.Buffered(1)),
        out_shape=jax.ShapeDtypeStruct((rows * t, width), F32),
        scratch_shapes=[pltpu.VMEM((rows, pairs * k), BF16), pltpu.VMEM((rows, pairs * k), F32)],
        compiler_params=pltpu.CompilerParams(dimension_semantics=("parallel",),
                                             vmem_limit_bytes=VMEM_LIMIT_BYTES),
    )(*([x2d] * t), win, woutt, a, sc, dvec)
    return y.reshape(chunks, batch, t, width)


def _dft_tables(blk):
    n = 2 * blk
    f = np.arange(blk, dtype=np.int64)
    sym = ((2 * f[:, None] + 1) * (2 * f[None, :] + 1)) % (4 * n)
    ang_sym = sym.astype(np.float64) * (2.0 * np.pi / (4 * n))
    lag = np.arange(2 * blk, dtype=np.int64) - blk
    ang_lag = (((2 * f[:, None] + 1) * lag[None, :]) % (2 * n)).astype(np.float64) * (2.0 * np.pi / (2 * n))
    tc, ts = np.cos(ang_lag), np.sin(ang_lag)
    tc[:, 0] = 0.0
    ts[:, 0] = 0.0
    as_bf = lambda x: jnp.asarray(x.astype(np.float32)).astype(BF16)
    return as_bf(np.cos(ang_sym)), as_bf(np.sin(ang_sym)), as_bf(tc), as_bf(ts)


def _position_features(seq, pad_to):
    t = np.linspace(0.0, 1.0, seq)[:, None]
    pos = np.arange(seq, dtype=np.float64)[:, None]
    bands = np.linspace(1e-4, HY_BANDS - 1, HY_BANDS)[None, :]
    ang = bands * pos * (2.0 * math.pi / seq)
    feats = np.zeros((seq, pad_to), np.float64)
    feats[:, :1 + 2 * HY_BANDS] = np.concatenate([t, np.cos(ang), -np.sin(ang)], axis=-1)
    mirror = lambda x: np.roll(x[::-1], 1, axis=0)
    f32 = lambda x: jnp.asarray(x.astype(np.float32))
    return f32(feats), f32(t), f32(mirror(feats)), f32(mirror(t))


_HY_COMBO_TERMS = (
    ((0, 1),), ((-1, 1), (0, -1)), ((1, 1), (0, -1)),
    ((-2, 1), (0, -1)), ((-3, 1), (-1, -1), (-2, -1), (0, 1)), ((-1, 1), (1, -1), (-2, -1), (0, 1)),
    ((2, 1), (0, -1)), ((1, 1), (-1, -1), (2, -1), (0, 1)), ((3, 1), (1, -1), (2, -1), (0, 1)),
)


def _hy_filter_body(feats_ref, t_ref, featsm_ref, tm_ref, w1_ref, b1_ref, fr_ref, w2_ref, b2_ref, w3f_ref,
                    w3b_ref, decf_ref, decb_ref, tc_ref, ts_ref, gr_ref, gi_ref, h2_scr, h2m_scr, k2_scr,
                    kr_scr, ki_scr, *, seq, nb):
    hi = lax.Precision.HIGHEST
    blk = seq // nb

    @pl.when(pl.program_id(0) == 0)
    def _():
        fr = fr_ref[...]

        def mlp(feats):
            h1 = jnp.sin(fr * (jnp.dot(feats, w1_ref[...], precision=hi, preferred_element_type=F32) + b1_ref[...]))
            return jnp.sin(fr * (jnp.dot(h1, w2_ref[...], precision=hi, preferred_element_type=F32) + b2_ref[...]))

        def split(h):
            top = h.astype(BF16).astype(F32)
            return jnp.concatenate([top, h - top, top, h - top], axis=1).astype(BF16)

        h2_scr[...] = split(mlp(feats_ref[...]))
        h2m_scr[...] = split(mlp(featsm_ref[...]))

    def split_rows(w):
        top = w.astype(BF16).astype(F32)
        return jnp.concatenate([top, top, w - top, w - top], axis=0).astype(BF16)

    hf = jnp.dot(h2_scr[...], split_rows(w3f_ref[...]), preferred_element_type=F32)
    hf = hf * (jnp.exp(-t_ref[...] * jnp.abs(decf_ref[...])) + HY_SHIFT)
    hb = jnp.dot(h2m_scr[...], split_rows(w3b_ref[...]), preferred_element_type=F32)
    hb = hb * (jnp.exp(-tm_ref[...] * jnp.abs(decb_ref[...])) + HY_SHIFT)
    row = lax.broadcasted_iota(jnp.int32, hb.shape, 0)
    hb = jnp.where(row == 0, 0.0, hb)
    norm = jnp.sum(hf * hf, axis=0, keepdims=True) + jnp.sum(hb * hb, axis=0, keepdims=True)
    scale = lax.rsqrt(norm + HY_EPS) * (1.0 / blk)
    k2_scr[0:seq, :] = hb.astype(BF16)
    k2_scr[seq:2 * seq, :] = hf.astype(BF16)
    for d in range(2 * nb - 1):
        seg = k2_scr[d * blk:(d + 2) * blk, :]
        kr_scr[d] = jnp.dot(tc_ref[...], seg, preferred_element_type=F32) * scale
        ki_scr[d] = jnp.dot(ts_ref[...], seg, preferred_element_type=F32) * scale
    for idx, terms in enumerate(_HY_COMBO_TERMS):
        for src, dst in ((kr_scr, gr_ref), (ki_scr, gi_ref)):
            acc = None
            for lag, weight in terms:
                term = src[lag + nb - 1]
                acc = (term if weight > 0 else -term) if acc is None else (acc + term if weight > 0 else acc - term)
            dst[0, idx] = acc


def _hy_filters(seq, w1, b1, freq, w2, b2, w3, decay, width, tables, nb, *, col_block=256):
    _, _, tc, ts = tables
    blk = seq // nb
    ffn = w2.shape[0]
    feats, t, featsm, tm = _position_features(seq, LANES)
    w1p = jnp.zeros((LANES, ffn), F32).at[:w1.shape[0]].set(w1)
    nblk = width // col_block
    side = lambda rows, direction: pl.BlockSpec(
        (rows, col_block), lambda i: (0, (2 * (i // nblk) + direction) * nblk + i % nblk))
    out = pl.BlockSpec((1, HY_COMBOS, blk, col_block), lambda i: (i // nblk, 0, 0, i % nblk))
    body = functools.partial(_hy_filter_body, seq=seq, nb=nb)
    row2 = lambda v: v.reshape(1, -1)
    return pl.pallas_call(
        body,
        grid=(HY_ORDER * nblk,),
        in_specs=[_resident(feats.shape), _resident(t.shape), _resident(feats.shape), _resident(t.shape),
                  _resident(w1p.shape), _resident((1, ffn)), _resident((1, ffn)), _resident(w2.shape),
                  _resident((1, ffn)), side(ffn, 0), side(ffn, 1), side(1, 0), side(1, 1),
                  _resident(tc.shape), _resident(ts.shape)],
        out_specs=[out, out],
        out_shape=[jax.ShapeDtypeStruct((HY_ORDER, HY_COMBOS, blk, width), F32)] * 2,
        scratch_shapes=[pltpu.VMEM((seq, 4 * ffn), BF16), pltpu.VMEM((seq, 4 * ffn), BF16),
                        pltpu.VMEM((2 * seq, col_block), BF16),
                        pltpu.VMEM((2 * nb - 1, blk, col_block), F32), pltpu.VMEM((2 * nb - 1, blk, col_block), F32)],
        compiler_params=pltpu.CompilerParams(dimension_semantics=("arbitrary",),
                                             vmem_limit_bytes=VMEM_LIMIT_BYTES),
    )(feats, t, featsm, tm, w1p, row2(b1), row2(freq), w2, row2(b2), w3, w3, row2(decay), row2(decay), tc, ts)


def _hy_conv_body(pv_ref, p1_ref, p2_ref, g_ref, cw_ref, cb_ref, gr_ref, gi_ref, d_ref, mc_ref, ms_ref, o_ref,
                  z_scr, a_scr, b_scr, *, seq, nb, cw, row_chunk, problems):
    blk = seq // nb
    gate_refs = (p1_ref, p2_ref)
    lanes = lambda j: slice(j * cw, (j + 1) * cw)
    cadd = lambda x, y: (x[0] + y[0], x[1] + y[1])
    cmul = lambda x, y: (x[0] * y[0] - x[1] * y[1], x[0] * y[1] + x[1] * y[0])

    def short_conv(k, p_ref, part):
        x = p_ref[k].astype(F32)
        row = lax.broadcasted_iota(jnp.int32, x.shape, 0)
        prev = jnp.where(row == 0, 0.0, pltpu.roll(x, 1, 0))
        nxt = jnp.where(row == seq - 1, 0.0, pltpu.roll(x, seq - 1, 0))
        w = cw_ref[:, part, :]
        return cb_ref[part:part + 1, :] + prev * w[0:1] + x * w[1:2] + nxt * w[2:3]

    def load_v(k):
        v = short_conv(k, pv_ref, 0)
        for j in range(nb):
            z_scr[k, :, lanes(j)] = v[j * blk:(j + 1) * blk, :]

    def forward(k, o):
        zb = z_scr[k].astype(BF16)
        a_scr[k] = jnp.dot(mc_ref[...], zb, preferred_element_type=F32)
        b_scr[k] = jnp.dot(ms_ref[...], zb, preferred_element_type=F32)

    def pointwise(k, o):
        def toeplitz2(base, x0, x1, r):
            g = lambda idx: (gr_ref[o, idx, r, :], gi_ref[o, idx, r, :])
            q = cmul(g(base), cadd(x0, x1))
            return cadd(q, cmul(g(base + 1), x1)), cadd(q, cmul(g(base + 2), x0))

        for c in range(blk // row_chunk):
            r = slice(c * row_chunk, (c + 1) * row_chunk)
            z = [(a_scr[k, r, lanes(j)], b_scr[k, r, lanes(j)]) for j in range(nb)]
            p1 = toeplitz2(0, cadd(z[0], z[2]), cadd(z[1], z[3]), r)
            p2 = toeplitz2(3, z[2], z[3], r)
            p3 = toeplitz2(6, z[0], z[1], r)
            for i, y in enumerate((cadd(p1[0], p2[0]), cadd(p1[1], p2[1]), cadd(p1[0], p3[0]), cadd(p1[1], p3[1]))):
                a_scr[k, r, lanes(i)], b_scr[k, r, lanes(i)] = y

    def inverse(k, o):
        conv = (jnp.dot(mc_ref[...], a_scr[k].astype(BF16), preferred_element_type=F32)
                + jnp.dot(ms_ref[...], b_scr[k].astype(BF16), preferred_element_type=F32))
        gate = short_conv(k, gate_refs[o], o + 1)
        dsk = d_ref[o:o + 1, :]
        for j in range(nb):
            rows = slice(j * blk, (j + 1) * blk)
            z = gate[rows, :] * (conv[:, lanes(j)] + dsk * z_scr[k, :, lanes(j)])
            if o == HY_ORDER - 1:
                o_ref[k, rows, :] = (z * g_ref[k, rows, :].astype(F32)).astype(o_ref.dtype)
            else:
                z_scr[k, :, lanes(j)] = z

    stages = [load_v]
    for o in range(HY_ORDER):
        stages += [functools.partial(forward, o=o), functools.partial(pointwise, o=o),
                   functools.partial(inverse, o=o)]
    for s in range(len(stages) + problems - 1):
        for k in range(problems):
            if 0 <= s - k < len(stages):
                stages[s - k](k)


def _hy_conv(p_hy, gate_silu, conv_w, conv_b, gr, gi, d_skip, tables, batch, seq, width, nb, *, col_block=LANES,
             row_chunk=SUBLANES, problems=2):
    assert nb == HY_TIME_BLOCKS
    mc, ms, _, _ = tables
    blk = seq // nb
    nblk = width // col_block
    part = lambda k: pl.BlockSpec((problems, seq, col_block), lambda c, b: (b, 0, k * nblk + c))
    cw = conv_w.reshape(conv_w.shape[0], HY_ORDER + 1, width)
    cb = conv_b.reshape(HY_ORDER + 1, width)
    filt = pl.BlockSpec((HY_ORDER, HY_COMBOS, blk, col_block), lambda c, b: (0, 0, 0, c),
                        pipeline_mode=pl.Buffered(1))
    body = functools.partial(_hy_conv_body, seq=seq, nb=nb, cw=col_block, row_chunk=row_chunk, problems=problems)
    wide = nb * col_block
    scratch = lambda dtype: pltpu.VMEM((problems, blk, wide), dtype)
    return pl.pallas_call(
        body,
        grid=(nblk, batch // problems),
        in_specs=[part(0), part(1), part(2),
                  pl.BlockSpec((problems, seq, col_block), lambda c, b: (b, 0, c)),
                  pl.BlockSpec((cw.shape[0], HY_ORDER + 1, col_block), lambda c, b: (0, 0, c)),
                  pl.BlockSpec((HY_ORDER + 1, col_block), lambda c, b: (0, c)),
                  filt, filt,
                  pl.BlockSpec((HY_ORDER, col_block), lambda c, b: (0, c)),
                  _resident(mc.shape), _resident(ms.shape)],
        out_specs=pl.BlockSpec((problems, seq, col_block), lambda c, b: (b, 0, c)),
        out_shape=jax.ShapeDtypeStruct((batch, seq, width), BF16),
        scratch_shapes=[scratch(F32), scratch(F32), scratch(F32)],
        compiler_params=pltpu.CompilerParams(dimension_semantics=("parallel", "parallel"),
                                             vmem_limit_bytes=VMEM_LIMIT_BYTES),
    )(p_hy, p_hy, p_hy, gate_silu, cw, cb, gr, gi, d_skip, mc, ms)


def _merge_body(ys_ref, gs_ref, yh_ref, m_ref, h_ref, wg_ref, bg_ref, wbs_ref, wbh_ref, wo_ref, fw_ref, o_ref,
                *, d_model, final_norm):
    ys = ys_ref[...]
    y = _gelu_tanh(ys.reshape(ys.shape[0] * ys.shape[1], ys.shape[2]))
    glu = y * _sigmoid(jnp.dot(y.astype(BF16), wg_ref[...], preferred_element_type=F32) + bg_ref[...])
    s5 = (glu * gs_ref[...].astype(F32)).astype(BF16)
    y_s5 = jnp.dot(s5, wbs_ref[...], preferred_element_type=F32)
    y_hy = jnp.dot(yh_ref[...], wbh_ref[...], preferred_element_type=F32)
    merged = m_ref[:, :d_model].astype(F32) * y_s5 + m_ref[:, d_model:].astype(F32) * y_hy
    h = h_ref[...] + jnp.dot(merged.astype(BF16), wo_ref[...], preferred_element_type=F32)
    if final_norm:
        ms = jnp.mean(h * h, axis=-1, keepdims=True)
        h = h * lax.rsqrt(ms + RMS_EPS) * fw_ref[...]
    o_ref[...] = h


def _merge(y_s5_cm, gs, y_hypre, m, h2d, w_glu, b_glu, w_bs, w_bh, w_out, layer, final_w, final_norm, seq, *,
           row_tile=512):
    rows, d = h2d.shape
    sw = y_s5_cm.shape[-1]
    hw = y_hypre.shape[1]
    tile = lambda w: pl.BlockSpec((row_tile, w), lambda i: (i, 0))
    body = functools.partial(_merge_body, d_model=d, final_norm=final_norm)
    return pl.pallas_call(
        body,
        grid=(rows // row_tile,),
        in_specs=[_chunk_major_spec(row_tile, seq, sw), tile(sw), tile(hw), tile(2 * d), tile(d),
                  _resident_layer(w_glu.shape, layer), _resident_layer((b_glu.shape[0], 1, sw), layer),
                  _resident_layer(w_bs.shape, layer), _resident_layer(w_bh.shape, layer),
                  _resident_layer(w_out.shape, layer), _resident((1, d))],
        out_specs=tile(d),
        out_shape=jax.ShapeDtypeStruct((rows, d), F32),
        compiler_params=pltpu.CompilerParams(dimension_semantics=("parallel",),
                                             vmem_limit_bytes=VMEM_LIMIT_BYTES),
    )(y_s5_cm, gs, y_hypre, m, h2d, w_glu, b_glu.reshape(-1, 1, sw), w_bs, w_bh, w_out, final_w.reshape(1, d))


def kernel(x, norm_w, w_in, s5_lam_re, s5_lam_im, s5_log_step, s5_b_re, s5_b_im, s5_c_re, s5_c_im, s5_d, s5_w_glu, s5_b_glu, hy_conv_w, hy_conv_b, hy_w1, hy_b1, hy_freq, hy_w2, hy_b2, hy_w3, hy_decay, hy_d, w_branch_s5, w_branch_hy, w_out, final_norm_w):
    batch, seq, d_model = x.shape
    depth = w_in.shape[0]
    s5_width = s5_d.shape[1]
    hy_width = hy_d.shape[2]
    widths = (s5_width, s5_width, (HY_ORDER + 1) * hy_width, hy_width, 2 * d_model)
    nb = HY_TIME_BLOCKS
    tables = _dft_tables(seq // nb)
    h = x.reshape(batch * seq, d_model)
    w_in_bf, w_glu_bf, w_bs_bf, w_bh_bf, w_out_bf = (w.astype(BF16) for w in (w_in, s5_w_glu, w_branch_s5,
                                                                              w_branch_hy, w_out))
    s5_ops = jax.vmap(_s5_operands)(s5_lam_re, s5_lam_im, s5_log_step, s5_b_re, s5_b_im, s5_c_re, s5_c_im, s5_d)
    for l in range(depth):
        u, gs, p_hy, gh, m = _inproj(h, norm_w, w_in_bf, l, widths, batch, seq)
        y_s5 = _s5_scan(u, s5_ops, l, batch, seq)
        gr, gi = _hy_filters(seq, hy_w1[l], hy_b1[l], hy_freq[l], hy_w2[l], hy_b2[l], hy_w3[l], hy_decay[l],
                             hy_width, tables, nb)
        y_hypre = _hy_conv(p_hy.reshape(batch, seq, -1), gh.reshape(batch, seq, -1), hy_conv_w[l], hy_conv_b[l],
                           gr, gi, hy_d[l], tables, batch, seq, hy_width, nb)
        h = _merge(y_s5, gs, y_hypre.reshape(batch * seq, hy_width), m, h, w_glu_bf, s5_b_glu, w_bs_bf, w_bh_bf,
                   w_out_bf, l, final_norm_w, l == depth - 1, seq)
    return h.reshape(batch, seq, d_model)
```

```python
import functools
import math

import numpy as np
import jax
import jax.numpy as jnp
from jax import lax
from jax.experimental import pallas as pl
from jax.experimental.pallas import tpu as pltpu

F32 = jnp.float32
BF16 = jnp.bfloat16

RMS_EPS = 1e-6
S5_GROUP = 16
S5_CHUNK = 16
S5_GROUPS_PER_PAIR = 2
LANES = 128
SUBLANES = 8
HY_ORDER = 2
HY_BANDS = 16
HY_SHIFT = 0.05
HY_EPS = 1e-6
HY_TIME_BLOCKS = 4
HY_LAGS = 2 * HY_TIME_BLOCKS - 1
HY_COMBOS = 9
VMEM_LIMIT_BYTES = 56 * 1024 * 1024

_NT = (((1,), (1,)), ((), ()))


def _sigmoid(x):
    return 1.0 / (1.0 + jnp.exp(-x))


def _silu(x):
    return x * _sigmoid(x)


def _gelu_tanh(x):
    return 0.5 * x * (1.0 + jnp.tanh(math.sqrt(2.0 / math.pi) * (x + 0.044715 * (x * x * x))))


def _resident(shape):
    zeros = (0,) * len(shape)
    return pl.BlockSpec(shape, lambda *_: zeros, pipeline_mode=pl.Buffered(1))


def _resident_layer(stacked_shape, layer):
    zeros = (0,) * (len(stacked_shape) - 1)
    return pl.BlockSpec((None,) + tuple(stacked_shape[1:]), lambda *_: (layer,) + zeros,
                        pipeline_mode=pl.Buffered(1))


def _chunk_major_spec(row_tile, seq, width):
    tiles_per_seq = seq // row_tile
    return pl.BlockSpec((row_tile // S5_CHUNK, None, S5_CHUNK, width),
                        lambda i: (i % tiles_per_seq, i // tiles_per_seq, 0, 0))


def _inproj_body(x_ref, nw_ref, w_ref, u_ref, gs_ref, p_ref, gh_ref, m_ref, *, bounds, col_chunk):
    x = x_ref[...]
    ms = jnp.mean(x * x, axis=-1, keepdims=True)
    xb = (x * lax.rsqrt(ms + RMS_EPS) * nw_ref[...]).astype(BF16)
    outs = ((u_ref, None), (gs_ref, _silu), (p_ref, None), (gh_ref, _silu), (m_ref, _sigmoid))
    for (out_ref, act), lo, hi in zip(outs, bounds[:-1], bounds[1:]):
        for c0 in range(lo, hi, col_chunk):
            y = jnp.dot(xb, w_ref[:, c0:c0 + col_chunk], preferred_element_type=F32)
            if act is not None:
                y = act(y)
            if out_ref is u_ref:
                out_ref[:, :, c0 - lo:c0 - lo + col_chunk] = y.astype(out_ref.dtype).reshape(
                    y.shape[0] // S5_CHUNK, S5_CHUNK, col_chunk)
            else:
                out_ref[:, c0 - lo:c0 - lo + col_chunk] = y.astype(out_ref.dtype)


def _inproj(h2d, norm_w, w_in_bf, layer, widths, batch, seq, *, row_tile=256, col_chunk=512):
    rows, d = h2d.shape
    bounds = [0]
    for w in widths:
        bounds.append(bounds[-1] + w)
    body = functools.partial(_inproj_body, bounds=tuple(bounds), col_chunk=col_chunk)
    tile = lambda w: pl.BlockSpec((row_tile, w), lambda i: (i, 0))
    return pl.pallas_call(
        body,
        grid=(rows // row_tile,),
        in_specs=[tile(d), _resident_layer((norm_w.shape[0], 1, d), layer), _resident_layer(w_in_bf.shape, layer)],
        out_specs=[_chunk_major_spec(row_tile, seq, widths[0])] + [tile(w) for w in widths[1:]],
        out_shape=[jax.ShapeDtypeStruct((seq // S5_CHUNK, batch, S5_CHUNK, widths[0]), BF16)]
        + [jax.ShapeDtypeStruct((rows, w), BF16) for w in widths[1:]],
        compiler_params=pltpu.CompilerParams(dimension_semantics=("parallel",),
                                             vmem_limit_bytes=VMEM_LIMIT_BYTES),
    )(h2d, norm_w.reshape(-1, 1, d), w_in_bf)


def _s5_body(*refs, batch, chunks, pairs, pw, sl):
    t_steps = S5_CHUNK
    x_refs = refs[:t_steps]
    win_ref, woutt_ref, a_ref, sc_ref, dv_ref, y_ref, d_scr, st_scr = refs[t_steps:]
    k = t_steps * pw
    rows = batch * chunks
    assert k == 4 * sl

    for j in range(pairs):
        dj = jnp.concatenate([x_refs[t][:, j * pw:(j + 1) * pw] for t in range(t_steps)], axis=1)
        d_scr[:, j * k:(j + 1) * k] = dj
        win = jnp.concatenate([win_ref[c, j] for c in range(4)], axis=1).astype(BF16)
        st_scr[:, j * k:(j + 1) * k] = jnp.dot(dj, win, preferred_element_type=F32)

    coef = [[jnp.broadcast_to(a_ref[j, c:c + 1, :], (batch, sl)) for c in range(4)] for j in range(pairs)]

    def step(i, carry):
        row_f = pl.ds(pl.multiple_of(i * batch, batch), batch)
        row_b = pl.ds(pl.multiple_of((chunks - 1 - i) * batch, batch), batch)
        out = []
        for j in range(pairs):
            for direction, r in ((0, row_f), (1, row_b)):
                sr, si = carry[4 * j + 2 * direction], carry[4 * j + 2 * direction + 1]
                ar, ai = coef[j][2 * direction], coef[j][2 * direction + 1]
                c_re = j * k + 2 * direction * sl
                c_im = c_re + sl
                ir, ii = st_scr[r, c_re:c_re + sl], st_scr[r, c_im:c_im + sl]
                st_scr[r, c_re:c_re + sl] = sr
                st_scr[r, c_im:c_im + sl] = si
                out += [ar * sr - ai * si + ir, ar * si + ai * sr + ii]
        return tuple(out)

    zero = jnp.zeros((batch, sl), F32)
    lax.fori_loop(0, chunks, step, (zero,) * (4 * pairs), unroll=2)

    in_step = lax.broadcasted_iota(jnp.int32, (k, k), 0) // pw
    out_step = lax.broadcasted_iota(jnp.int32, (k, k), 1) // pw
    for j in range(pairs):
        sc = sc_ref[j]
        kern = None
        for direction in range(2):
            wr, wi = win_ref[2 * direction, j], win_ref[2 * direction + 1, j]
            pr, pi = sc[2 * direction:2 * direction + 1, :], sc[2 * direction + 1:2 * direction + 2, :]
            wp = jnp.concatenate([wr * pr - wi * pi, wr * pi + wi * pr], axis=1).astype(BF16)
            wt = jnp.concatenate([woutt_ref[2 * direction, j], woutt_ref[2 * direction + 1, j]], axis=1)
            kd = lax.dot_general(wp, wt, _NT, preferred_element_type=F32)
            kd = jnp.where(out_step >= in_step if direction == 0 else out_step <= in_step, kd, 0.0)
            kern = kd if kern is None else kern + kd
        dj = d_scr[:, j * k:(j + 1) * k]
        wt = jnp.concatenate([woutt_ref[c, j] for c in range(4)], axis=1)
        y = jnp.dot(dj, kern.astype(BF16), preferred_element_type=F32)
        y = y + lax.dot_general(st_scr[:, j * k:(j + 1) * k].astype(BF16), wt, _NT, preferred_element_type=F32)
        y = y + dj.astype(F32) * dv_ref[j]
        st_scr[:, j * k:(j + 1) * k] = y
    for t in range(t_steps):
        y_ref[pl.ds(t, rows, stride=t_steps), :] = jnp.concatenate(
            [st_scr[:, j * k + t * pw:j * k + (t + 1) * pw] for j in range(pairs)], axis=1)


def _s5_operands(lam_re, lam_im, log_step, b_re, b_im, c_re, c_im, d_skip):
    _, groups, p = lam_re.shape
    hh, t, gp = S5_GROUP, S5_CHUNK, S5_GROUPS_PER_PAIR
    npair = groups // gp
    rep = lambda v: jnp.concatenate([v] * gp, axis=-1)
    own = (jnp.arange(gp * p)[None, :] // p == jnp.arange(groups)[:, None] % gp).astype(F32)
    lam_re, lam_im = rep(lam_re), rep(lam_im)
    dt = jnp.exp(log_step)[..., None]
    xr, xi = lam_re * dt, lam_im * dt
    lbr, lbi = jnp.exp(xr) * jnp.cos(xi), jnp.exp(xr) * jnp.sin(xi)
    nr, ni = lbr - 1.0, lbi
    den = lam_re * lam_re + lam_im * lam_im
    zr, zi = (nr * lam_re + ni * lam_im) / den * own, (ni * lam_re - nr * lam_im) / den * own
    btr, bti = rep(jnp.swapaxes(b_re, -1, -2)), rep(jnp.swapaxes(b_im, -1, -2))
    bbr = zr[:, :, None, :] * btr - zi[:, :, None, :] * bti
    bbi = zr[:, :, None, :] * bti + zi[:, :, None, :] * btr
    ccr, cci = rep(c_re) * own[None, :, None, :], rep(c_im) * own[None, :, None, :]

    def powers(e):
        kk = e[:, None, :, None]
        mag, ang = jnp.exp(kk * xr[:, :, None, :]), kk * xi[:, :, None, :]
        return (mag * jnp.cos(ang))[:, :, :, None, :], (mag * jnp.sin(ang))[:, :, :, None, :]

    tt = jnp.arange(t, dtype=F32)
    pir, pii = powers(jnp.stack([t - 1 - tt, tt]))
    por, poi = powers(jnp.stack([tt + 1, t - tt]))
    b5r, b5i = bbr[:, :, None], bbi[:, :, None]
    c5r, c5i = ccr[:, :, None], cci[:, :, None]
    win_r, win_i = pir * b5r - pii * b5i, pir * b5i + pii * b5r
    out_r, out_i = por * c5r - poi * c5i, por * c5i + poi * c5r

    def pair_layout(x_re, x_im):
        comp = jnp.stack([x_re[0], x_im[0], x_re[1], x_im[1]]).reshape(4, npair, gp, t, hh, gp * p)
        return jnp.transpose(comp, (0, 1, 3, 2, 4, 5)).reshape(4, npair, t * gp * hh, gp * p)

    def pair_vec(e):
        mag, ang = jnp.exp(e * xr) * own, e * xi
        v = jnp.stack([(mag * jnp.cos(ang))[0], (mag * jnp.sin(ang))[0],
                       (mag * jnp.cos(ang))[1], (mag * jnp.sin(ang))[1]])
        return jnp.transpose(v.reshape(4, npair, gp, gp * p).sum(axis=2), (1, 0, 2))

    dvec = jnp.broadcast_to(d_skip.reshape(npair, 1, gp, hh), (npair, t, gp, hh)).reshape(npair, 1, t * gp * hh)
    return (pair_layout(win_r, win_i), pair_layout(out_r, -out_i).astype(BF16),
            pair_vec(float(t)), pair_vec(-float(t)), dvec)


def _s5_scan(u_cm, operands, layer, batch, seq):
    win, woutt, a, sc, dvec = operands
    _, _, npair, k, sl = win.shape
    t = S5_CHUNK
    pw = k // t
    pairs = LANES // pw
    chunks = seq // t
    rows = chunks * batch
    width = npair * pw
    nblk = width // LANES
    body = functools.partial(_s5_body, batch=batch, chunks=chunks, pairs=pairs, pw=pw, sl=sl)
    per_blk = lambda shape: pl.BlockSpec((None, pairs) + shape, lambda q: (layer, q, 0, 0))
    comp_blk = pl.BlockSpec((None, 4, pairs, k, sl), lambda q: (layer, 0, q, 0, 0), pipeline_mode=pl.Buffered(1))
    x2d = u_cm.reshape(rows, t * width)
    y = pl.pallas_call(
        body,
        grid=(nblk,),
        in_specs=[pl.BlockSpec((rows, LANES), functools.partial(lambda q, s: (0, s * nblk + q), s=s))
                  for s in range(t)]
        + [comp_blk, comp_blk, per_blk((4, sl)), per_blk((4, sl)), per_blk((1, k))],
        out_specs=pl.BlockSpec((rows * t, LANES), lambda q: (0, q), pipeline_mode=pl.Buffered(1)),
        out_shape=jax.ShapeDtypeStruct((rows * t, width), F32),
        scratch_shapes=[pltpu.VMEM((rows, pairs * k), BF16), pltpu.VMEM((rows, pairs * k), F32)],
        compiler_params=pltpu.CompilerParams(dimension_semantics=("parallel",),
                                             vmem_limit_bytes=VMEM_LIMIT_BYTES),
    )(*([x2d] * t), win, woutt, a, sc, dvec)
    return y.reshape(chunks, batch, t, width)


def _dft_tables(blk):
    n = 2 * blk
    f = np.arange(blk, dtype=np.int64)
    sym = ((2 * f[:, None] + 1) * (2 * f[None, :] + 1)) % (4 * n)
    ang_sym = sym.astype(np.float64) * (2.0 * np.pi / (4 * n))
    lag = np.arange(2 * blk, dtype=np.int64) - blk
    ang_lag = (((2 * f[:, None] + 1) * lag[None, :]) % (2 * n)).astype(np.float64) * (2.0 * np.pi / (2 * n))
    tc, ts = np.cos(ang_lag), np.sin(ang_lag)
    tc[:, 0] = 0.0
    ts[:, 0] = 0.0
    as_bf = lambda x: jnp.asarray(x.astype(np.float32)).astype(BF16)
    return as_bf(np.cos(ang_sym)), as_bf(np.sin(ang_sym)), as_bf(tc), as_bf(ts)


def _position_features(seq, pad_to):
    t = np.linspace(0.0, 1.0, seq)[:, None]
    pos = np.arange(seq, dtype=np.float64)[:, None]
    bands = np.linspace(1e-4, HY_BANDS - 1, HY_BANDS)[None, :]
    ang = bands * pos * (2.0 * math.pi / seq)
    feats = np.zeros((seq, pad_to), np.float64)
    feats[:, :1 + 2 * HY_BANDS] = np.concatenate([t, np.cos(ang), -np.sin(ang)], axis=-1)
    mirror = lambda x: np.roll(x[::-1], 1, axis=0)
    f32 = lambda x: jnp.asarray(x.astype(np.float32))
    return f32(feats), f32(t), f32(mirror(feats)), f32(mirror(t))


_HY_COMBO_TERMS = (
    ((0, 1),), ((-1, 1), (0, -1)), ((1, 1), (0, -1)),
    ((-2, 1), (0, -1)), ((-3, 1), (-1, -1), (-2, -1), (0, 1)), ((-1, 1), (1, -1), (-2, -1), (0, 1)),
    ((2, 1), (0, -1)), ((1, 1), (-1, -1), (2, -1), (0, 1)), ((3, 1), (1, -1), (2, -1), (0, 1)),
)


def _hy_filter_body(feats_ref, t_ref, featsm_ref, tm_ref, w1_ref, b1_ref, fr_ref, w2_ref, b2_ref, w3f_ref,
                    w3b_ref, decf_ref, decb_ref, tc_ref, ts_ref, gr_ref, gi_ref, h2_scr, h2m_scr, k2_scr,
                    kr_scr, ki_scr, *, seq, nb):
    hi = lax.Precision.HIGHEST
    blk = seq // nb

    @pl.when(pl.program_id(0) == 0)
    def _():
        fr = fr_ref[...]

        def mlp(feats):
            h1 = jnp.sin(fr * (jnp.dot(feats, w1_ref[...], precision=hi, preferred_element_type=F32) + b1_ref[...]))
            return jnp.sin(fr * (jnp.dot(h1, w2_ref[...], precision=hi, preferred_element_type=F32) + b2_ref[...]))

        def split(h):
            top = h.astype(BF16).astype(F32)
            return jnp.concatenate([top, h - top, top, h - top], axis=1).astype(BF16)

        h2_scr[...] = split(mlp(feats_ref[...]))
        h2m_scr[...] = split(mlp(featsm_ref[...]))

    def split_rows(w):
        top = w.astype(BF16).astype(F32)
        return jnp.concatenate([top, top, w - top, w - top], axis=0).astype(BF16)

    hf = jnp.dot(h2_scr[...], split_rows(w3f_ref[...]), preferred_element_type=F32)
    hf = hf * (jnp.exp(-t_ref[...] * jnp.abs(decf_ref[...])) + HY_SHIFT)
    hb = jnp.dot(h2m_scr[...], split_rows(w3b_ref[...]), preferred_element_type=F32)
    hb = hb * (jnp.exp(-tm_ref[...] * jnp.abs(decb_ref[...])) + HY_SHIFT)
    row = lax.broadcasted_iota(jnp.int32, hb.shape, 0)
    hb = jnp.where(row == 0, 0.0, hb)
    norm = jnp.sum(hf * hf, axis=0, keepdims=True) + jnp.sum(hb * hb, axis=0, keepdims=True)
    scale = lax.rsqrt(norm + HY_EPS) * (1.0 / blk)
    k2_scr[0:seq, :] = hb.astype(BF16)
    k2_scr[seq:2 * seq, :] = hf.astype(BF16)
    for d in range(2 * nb - 1):
        seg = k2_scr[d * blk:(d + 2) * blk, :]
        kr_scr[d] = jnp.dot(tc_ref[...], seg, preferred_element_type=F32) * scale
        ki_scr[d] = jnp.dot(ts_ref[...], seg, preferred_element_type=F32) * scale
    for idx, terms in enumerate(_HY_COMBO_TERMS):
        for src, dst in ((kr_scr, gr_ref), (ki_scr, gi_ref)):
            acc = None
            for lag, weight in terms:
                term = src[lag + nb - 1]
                acc = (term if weight > 0 else -term) if acc is None else (acc + term if weight > 0 else acc - term)
            dst[0, idx] = acc


def _hy_filters(seq, w1, b1, freq, w2, b2, w3, decay, width, tables, nb, *, col_block=256):
    _, _, tc, ts = tables
    blk = seq // nb
    ffn = w2.shape[0]
    feats, t, featsm, tm = _position_features(seq, LANES)
    w1p = jnp.zeros((LANES, ffn), F32).at[:w1.shape[0]].set(w1)
    nblk = width // col_block
    side = lambda rows, direction: pl.BlockSpec(
        (rows, col_block), lambda i: (0, (2 * (i // nblk) + direction) * nblk + i % nblk))
    out = pl.BlockSpec((1, HY_COMBOS, blk, col_block), lambda i: (i // nblk, 0, 0, i % nblk))
    body = functools.partial(_hy_filter_body, seq=seq, nb=nb)
    row2 = lambda v: v.reshape(1, -1)
    return pl.pallas_call(
        body,
        grid=(HY_ORDER * nblk,),
        in_specs=[_resident(feats.shape), _resident(t.shape), _resident(feats.shape), _resident(t.shape),
                  _resident(w1p.shape), _resident((1, ffn)), _resident((1, ffn)), _resident(w2.shape),
                  _resident((1, ffn)), side(ffn, 0), side(ffn, 1), side(1, 0), side(1, 1),
                  _resident(tc.shape), _resident(ts.shape)],
        out_specs=[out, out],
        out_shape=[jax.ShapeDtypeStruct((HY_ORDER, HY_COMBOS, blk, width), F32)] * 2,
        scratch_shapes=[pltpu.VMEM((seq, 4 * ffn), BF16), pltpu.VMEM((seq, 4 * ffn), BF16),
                        pltpu.VMEM((2 * seq, col_block), BF16),
                        pltpu.VMEM((2 * nb - 1, blk, col_block), F32), pltpu.VMEM((2 * nb - 1, blk, col_block), F32)],
        compiler_params=pltpu.CompilerParams(dimension_semantics=("arbitrary",),
                                             vmem_limit_bytes=VMEM_LIMIT_BYTES),
    )(feats, t, featsm, tm, w1p, row2(b1), row2(freq), w2, row2(b2), w3, w3, row2(decay), row2(decay), tc, ts)


def _hy_conv_body(pv_ref, p1_ref, p2_ref, g_ref, cw_ref, cb_ref, gr_ref, gi_ref, d_ref, mc_ref, ms_ref, o_ref,
                  z_scr, a_scr, b_scr, xs_scr, *, seq, nb, cw, row_chunk, problems):
    blk = seq // nb
    gate_refs = (p1_ref, p2_ref)
    lanes = lambda j: slice(j * cw, (j + 1) * cw)
    cadd = lambda x, y: (x[0] + y[0], x[1] + y[1])
    cmul = lambda x, y: (x[0] * y[0] - x[1] * y[1], x[0] * y[1] + x[1] * y[0])

    def short_conv(k, p_ref, part):
        x = p_ref[k].astype(F32)
        pad = xs_scr.at[k, part]
        pad[0:SUBLANES, :] = jnp.zeros((SUBLANES, cw), F32)
        pad[SUBLANES + seq:2 * SUBLANES + seq, :] = jnp.zeros((SUBLANES, cw), F32)
        pad[SUBLANES:SUBLANES + seq, :] = x
        prev, nxt = pad[pl.ds(SUBLANES - 1, seq), :], pad[pl.ds(SUBLANES + 1, seq), :]
        w = cw_ref[:, part, :]
        return cb_ref[part:part + 1, :] + prev * w[0:1] + x * w[1:2] + nxt * w[2:3]

    def load_v(k):
        v = short_conv(k, pv_ref, 0)
        for j in range(nb):
            z_scr[k, :, lanes(j)] = v[j * blk:(j + 1) * blk, :]

    def forward(k, o):
        zb = z_scr[k].astype(BF16)
        a_scr[k] = jnp.dot(mc_ref[...], zb, preferred_element_type=F32)
        b_scr[k] = jnp.dot(ms_ref[...], zb, preferred_element_type=F32)

    def pointwise(k, o):
        def toeplitz2(base, x0, x1, r):
            g = lambda idx: (gr_ref[o, idx, r, :], gi_ref[o, idx, r, :])
            q = cmul(g(base), cadd(x0, x1))
            return cadd(q, cmul(g(base + 1), x1)), cadd(q, cmul(g(base + 2), x0))

        for c in range(blk // row_chunk):
            r = slice(c * row_chunk, (c + 1) * row_chunk)
            z = [(a_scr[k, r, lanes(j)], b_scr[k, r, lanes(j)]) for j in range(nb)]
            p1 = toeplitz2(0, cadd(z[0], z[2]), cadd(z[1], z[3]), r)
            p2 = toeplitz2(3, z[2], z[3], r)
            p3 = toeplitz2(6, z[0], z[1], r)
            for i, y in enumerate((cadd(p1[0], p2[0]), cadd(p1[1], p2[1]), cadd(p1[0], p3[0]), cadd(p1[1], p3[1]))):
                a_scr[k, r, lanes(i)], b_scr[k, r, lanes(i)] = y

    def inverse(k, o):
        conv = (jnp.dot(mc_ref[...], a_scr[k].astype(BF16), preferred_element_type=F32)
                + jnp.dot(ms_ref[...], b_scr[k].astype(BF16), preferred_element_type=F32))
        gate = short_conv(k, gate_refs[o], o + 1)
        dsk = d_ref[o:o + 1, :]
        for j in range(nb):
            rows = slice(j * blk, (j + 1) * blk)
            z = gate[rows, :] * (conv[:, lanes(j)] + dsk * z_scr[k, :, lanes(j)])
            if o == HY_ORDER - 1:
                o_ref[k, rows, :] = (z * g_ref[k, rows, :].astype(F32)).astype(o_ref.dtype)
            else:
                z_scr[k, :, lanes(j)] = z

    stages = [load_v]
    for o in range(HY_ORDER):
        stages += [functools.partial(forward, o=o), functools.partial(pointwise, o=o),
                   functools.partial(inverse, o=o)]
    for s in range(len(stages) + problems - 1):
        for k in range(problems):
            if 0 <= s - k < len(stages):
                stages[s - k](k)


def _hy_conv(p_hy, gate_silu, conv_w, conv_b, gr, gi, d_skip, tables, batch, seq, width, nb, *, col_block=LANES,
             row_chunk=SUBLANES, problems=2):
    assert nb == HY_TIME_BLOCKS
    mc, ms, _, _ = tables
    blk = seq // nb
    nblk = width // col_block
    part = lambda k: pl.BlockSpec((problems, seq, col_block), lambda c, b: (b, 0, k * nblk + c))
    cw = conv_w.reshape(conv_w.shape[0], HY_ORDER + 1, width)
    cb = conv_b.reshape(HY_ORDER + 1, width)
    filt = pl.BlockSpec((HY_ORDER, HY_COMBOS, blk, col_block), lambda c, b: (0, 0, 0, c),
                        pipeline_mode=pl.Buffered(1))
    body = functools.partial(_hy_conv_body, seq=seq, nb=nb, cw=col_block, row_chunk=row_chunk, problems=problems)
    wide = nb * col_block
    scratch = lambda dtype: pltpu.VMEM((problems, blk, wide), dtype)
    return pl.pallas_call(
        body,
        grid=(nblk, batch // problems),
        in_specs=[part(0), part(1), part(2),
                  pl.BlockSpec((problems, seq, col_block), lambda c, b: (b, 0, c)),
                  pl.BlockSpec((cw.shape[0], HY_ORDER + 1, col_block), lambda c, b: (0, 0, c)),
                  pl.BlockSpec((HY_ORDER + 1, col_block), lambda c, b: (0, c)),
                  filt, filt,
                  pl.BlockSpec((HY_ORDER, col_block), lambda c, b: (0, c)),
                  _resident(mc.shape), _resident(ms.shape)],
        out_specs=pl.BlockSpec((problems, seq, col_block), lambda c, b: (b, 0, c)),
        out_shape=jax.ShapeDtypeStruct((batch, seq, width), BF16),
        scratch_shapes=[scratch(F32), scratch(F32), scratch(F32),
                        pltpu.VMEM((problems, HY_ORDER + 1, seq + 2 * SUBLANES, col_block), F32)],
        compiler_params=pltpu.CompilerParams(dimension_semantics=("parallel", "parallel"),
                                             vmem_limit_bytes=VMEM_LIMIT_BYTES),
    )(p_hy, p_hy, p_hy, gate_silu, cw, cb, gr, gi, d_skip, mc, ms)


def _merge_body(ys_ref, gs_ref, yh_ref, m_ref, h_ref, wg_ref, bg_ref, wbs_ref, wbh_ref, wo_ref, fw_ref, o_ref,
                *, d_model, final_norm):
    ys = ys_ref[...]
    y = _gelu_tanh(ys.reshape(ys.shape[0] * ys.shape[1], ys.shape[2]))
    glu = y * _sigmoid(jnp.dot(y.astype(BF16), wg_ref[...], preferred_element_type=F32) + bg_ref[...])
    s5 = (glu * gs_ref[...].astype(F32)).astype(BF16)
    y_s5 = jnp.dot(s5, wbs_ref[...], preferred_element_type=F32)
    y_hy = jnp.dot(yh_ref[...], wbh_ref[...], preferred_element_type=F32)
    merged = m_ref[:, :d_model].astype(F32) * y_s5 + m_ref[:, d_model:].astype(F32) * y_hy
    h = h_ref[...] + jnp.dot(merged.astype(BF16), wo_ref[...], preferred_element_type=F32)
    if final_norm:
        ms = jnp.mean(h * h, axis=-1, keepdims=True)
        h = h * lax.rsqrt(ms + RMS_EPS) * fw_ref[...]
    o_ref[...] = h


def _merge(y_s5_cm, gs, y_hypre, m, h2d, w_glu, b_glu, w_bs, w_bh, w_out, layer, final_w, final_norm, seq, *,
           row_tile=512):
    rows, d = h2d.shape
    sw = y_s5_cm.shape[-1]
    hw = y_hypre.shape[1]
    tile = lambda w: pl.BlockSpec((row_tile, w), lambda i: (i, 0))
    body = functools.partial(_merge_body, d_model=d, final_norm=final_norm)
    return pl.pallas_call(
        body,
        grid=(rows // row_tile,),
        in_specs=[_chunk_major_spec(row_tile, seq, sw), tile(sw), tile(hw), tile(2 * d), tile(d),
                  _resident_layer(w_glu.shape, layer), _resident_layer((b_glu.shape[0], 1, sw), layer),
                  _resident_layer(w_bs.shape, layer), _resident_layer(w_bh.shape, layer),
                  _resident_layer(w_out.shape, layer), _resident((1, d))],
        out_specs=tile(d),
        out_shape=jax.ShapeDtypeStruct((rows, d), F32),
        compiler_params=pltpu.CompilerParams(dimension_semantics=("parallel",),
                                             vmem_limit_bytes=VMEM_LIMIT_BYTES),
    )(y_s5_cm, gs, y_hypre, m, h2d, w_glu, b_glu.reshape(-1, 1, sw), w_bs, w_bh, w_out, final_w.reshape(1, d))


def kernel(x, norm_w, w_in, s5_lam_re, s5_lam_im, s5_log_step, s5_b_re, s5_b_im, s5_c_re, s5_c_im, s5_d, s5_w_glu, s5_b_glu, hy_conv_w, hy_conv_b, hy_w1, hy_b1, hy_freq, hy_w2, hy_b2, hy_w3, hy_decay, hy_d, w_branch_s5, w_branch_hy, w_out, final_norm_w):
    batch, seq, d_model = x.shape
    depth = w_in.shape[0]
    s5_width = s5_d.shape[1]
    hy_width = hy_d.shape[2]
    widths = (s5_width, s5_width, (HY_ORDER + 1) * hy_width, hy_width, 2 * d_model)
    nb = HY_TIME_BLOCKS
    tables = _dft_tables(seq // nb)
    h = x.reshape(batch * seq, d_model)
    w_in_bf, w_glu_bf, w_bs_bf, w_bh_bf, w_out_bf = (w.astype(BF16) for w in (w_in, s5_w_glu, w_branch_s5,
                                                                              w_branch_hy, w_out))
    s5_ops = jax.vmap(_s5_operands)(s5_lam_re, s5_lam_im, s5_log_step, s5_b_re, s5_b_im, s5_c_re, s5_c_im, s5_d)
    for l in range(depth):
        u, gs, p_hy, gh, m = _inproj(h, norm_w, w_in_bf, l, widths, batch, seq)
        y_s5 = _s5_scan(u, s5_ops, l, batch, seq)
        gr, gi = _hy_filters(seq, hy_w1[l], hy_b1[l], hy_freq[l], hy_w2[l], hy_b2[l], hy_w3[l], hy_decay[l],
                             hy_width, tables, nb)
        y_hypre = _hy_conv(p_hy.reshape(batch, seq, -1), gh.reshape(batch, seq, -1), hy_conv_w[l], hy_conv_b[l],
                           gr, gi, hy_d[l], tables, batch, seq, hy_width, nb)
        h = _merge(y_s5, gs, y_hypre.reshape(batch * seq, hy_width), m, h, w_glu_bf, s5_b_glu, w_bs_bf, w_bh_bf,
                   w_out_bf, l, final_norm_w, l == depth - 1, seq)
    return h.reshape(batch, seq, d_model)
```

```python
import functools
import math

import numpy as np
import jax
import jax.numpy as jnp
from jax import lax
from jax.experimental import pallas as pl
from jax.experimental.pallas import tpu as pltpu

F32 = jnp.float32
BF16 = jnp.bfloat16

RMS_EPS = 1e-6
S5_GROUP = 16
S5_CHUNK = 16
S5_GROUPS_PER_PAIR = 2
LANES = 128
SUBLANES = 8
HY_ORDER = 2
HY_BANDS = 16
HY_SHIFT = 0.05
HY_EPS = 1e-6
HY_TIME_BLOCKS = 4
HY_LAGS = 2 * HY_TIME_BLOCKS - 1
HY_COMBOS = 9
VMEM_LIMIT_BYTES = 56 * 1024 * 1024

_NT = (((1,), (1,)), ((), ()))


def _sigmoid(x):
    return 1.0 / (1.0 + jnp.exp(-x))


def _silu(x):
    return x * _sigmoid(x)


def _gelu_tanh(x):
    return 0.5 * x * (1.0 + jnp.tanh(math.sqrt(2.0 / math.pi) * (x + 0.044715 * (x * x * x))))


def _resident(shape):
    zeros = (0,) * len(shape)
    return pl.BlockSpec(shape, lambda *_: zeros, pipeline_mode=pl.Buffered(1))


def _resident_layer(stacked_shape, layer):
    zeros = (0,) * (len(stacked_shape) - 1)
    return pl.BlockSpec((None,) + tuple(stacked_shape[1:]), lambda *_: (layer,) + zeros,
                        pipeline_mode=pl.Buffered(1))


def _chunk_major_spec(row_tile, seq, width):
    tiles_per_seq = seq // row_tile
    return pl.BlockSpec((row_tile // S5_CHUNK, None, S5_CHUNK, width),
                        lambda i: (i % tiles_per_seq, i // tiles_per_seq, 0, 0))


def _inproj_body(x_ref, nw_ref, w_ref, u_ref, gs_ref, p_ref, gh_ref, m_ref, *, bounds, col_chunk):
    x = x_ref[...]
    ms = jnp.mean(x * x, axis=-1, keepdims=True)
    xb = (x * lax.rsqrt(ms + RMS_EPS) * nw_ref[...]).astype(BF16)
    outs = ((u_ref, None), (gs_ref, _silu), (p_ref, None), (gh_ref, _silu), (m_ref, _sigmoid))
    for (out_ref, act), lo, hi in zip(outs, bounds[:-1], bounds[1:]):
        for c0 in range(lo, hi, col_chunk):
            y = jnp.dot(xb, w_ref[:, c0:c0 + col_chunk], preferred_element_type=F32)
            if act is not None:
                y = act(y)
            if out_ref is u_ref:
                out_ref[:, :, c0 - lo:c0 - lo + col_chunk] = y.astype(out_ref.dtype).reshape(
                    y.shape[0] // S5_CHUNK, S5_CHUNK, col_chunk)
            else:
                out_ref[:, c0 - lo:c0 - lo + col_chunk] = y.astype(out_ref.dtype)


def _inproj(h2d, norm_w, w_in_bf, layer, widths, batch, seq, *, row_tile=256, col_chunk=512):
    rows, d = h2d.shape
    bounds = [0]
    for w in widths:
        bounds.append(bounds[-1] + w)
    body = functools.partial(_inproj_body, bounds=tuple(bounds), col_chunk=col_chunk)
    tile = lambda w: pl.BlockSpec((row_tile, w), lambda i: (i, 0))
    return pl.pallas_call(
        body,
        grid=(rows // row_tile,),
        in_specs=[tile(d), _resident_layer((norm_w.shape[0], 1, d), layer), _resident_layer(w_in_bf.shape, layer)],
        out_specs=[_chunk_major_spec(row_tile, seq, widths[0])] + [tile(w) for w in widths[1:]],
        out_shape=[jax.ShapeDtypeStruct((seq // S5_CHUNK, batch, S5_CHUNK, widths[0]), BF16)]
        + [jax.ShapeDtypeStruct((rows, w), BF16) for w in widths[1:]],
        compiler_params=pltpu.CompilerParams(dimension_semantics=("parallel",),
                                             vmem_limit_bytes=VMEM_LIMIT_BYTES),
    )(h2d, norm_w.reshape(-1, 1, d), w_in_bf)


def _s5_body(*refs, batch, chunks, pairs, pw, sl):
    t_steps = S5_CHUNK
    x_refs = refs[:t_steps]
    win_ref, woutt_ref, a_ref, sc_ref, dv_ref, y_ref, d_scr, st_scr = refs[t_steps:]
    k = t_steps * pw
    rows = batch * chunks
    assert k == 4 * sl

    assert pairs == 4 and pairs * pw == LANES and t_steps % pairs == 0

    def piece_transpose(v):
        lane = lax.broadcasted_iota(jnp.int32, v[0].shape, 1)
        upper, odd = lane >= 2 * pw, (lane // pw) % 2 == 1
        t0 = jnp.where(upper, pltpu.roll(v[2], 2 * pw, 1), v[0])
        t2 = jnp.where(upper, v[2], pltpu.roll(v[0], 2 * pw, 1))
        t1 = jnp.where(upper, pltpu.roll(v[3], 2 * pw, 1), v[1])
        t3 = jnp.where(upper, v[3], pltpu.roll(v[1], 2 * pw, 1))
        return (jnp.where(odd, pltpu.roll(t1, pw, 1), t0), jnp.where(odd, t1, pltpu.roll(t0, LANES - pw, 1)),
                jnp.where(odd, pltpu.roll(t3, pw, 1), t2), jnp.where(odd, t3, pltpu.roll(t2, LANES - pw, 1)))

    def pair_rows(ref, c, j):
        g = [ref[c, S5_GROUPS_PER_PAIR * j + a] for a in range(S5_GROUPS_PER_PAIR)]
        return jnp.concatenate([ga[t * S5_GROUP:(t + 1) * S5_GROUP] for t in range(t_steps) for ga in g], axis=0)

    for j in range(pairs):
        d_scr[:, j * k:(j + 1) * k] = jnp.concatenate(
            [x_refs[t][:, j * pw:(j + 1) * pw] for t in range(t_steps)], axis=1)
        win = jnp.concatenate([pair_rows(win_ref, c, j) for c in range(4)], axis=1).astype(BF16)
        st_scr[:, j * k:(j + 1) * k] = jnp.dot(d_scr[:, j * k:(j + 1) * k], win, preferred_element_type=F32)

    coef = [[jnp.broadcast_to(a_ref[j, c:c + 1, :], (batch, sl)) for c in range(4)] for j in range(pairs)]

    def step(i, carry):
        row_f = pl.ds(pl.multiple_of(i * batch, batch), batch)
        row_b = pl.ds(pl.multiple_of((chunks - 1 - i) * batch, batch), batch)
        out = []
        for j in range(pairs):
            for direction, r in ((0, row_f), (1, row_b)):
                sr, si = carry[4 * j + 2 * direction], carry[4 * j + 2 * direction + 1]
                ar, ai = coef[j][2 * direction], coef[j][2 * direction + 1]
                c_re = j * k + 2 * direction * sl
                c_im = c_re + sl
                ir, ii = st_scr[r, c_re:c_re + sl], st_scr[r, c_im:c_im + sl]
                st_scr[r, c_re:c_re + sl] = sr
                st_scr[r, c_im:c_im + sl] = si
                out += [ar * sr - ai * si + ir, ar * si + ai * sr + ii]
        return tuple(out)

    zero = jnp.zeros((batch, sl), F32)
    lax.fori_loop(0, chunks, step, (zero,) * (4 * pairs), unroll=2)

    in_step = lax.broadcasted_iota(jnp.int32, (k, k), 0) // pw
    out_step = lax.broadcasted_iota(jnp.int32, (k, k), 1) // pw
    for j in range(pairs):
        sc = sc_ref[j]
        kern = None
        woutt = [pair_rows(woutt_ref, c, j) for c in range(4)]
        for direction in range(2):
            wr, wi = pair_rows(win_ref, 2 * direction, j), pair_rows(win_ref, 2 * direction + 1, j)
            pr, pi = sc[2 * direction:2 * direction + 1, :], sc[2 * direction + 1:2 * direction + 2, :]
            wp = jnp.concatenate([wr * pr - wi * pi, wr * pi + wi * pr], axis=1).astype(BF16)
            wt = jnp.concatenate(woutt[2 * direction:2 * direction + 2], axis=1)
            kd = lax.dot_general(wp, wt, _NT, preferred_element_type=F32)
            kd = jnp.where(out_step >= in_step if direction == 0 else out_step <= in_step, kd, 0.0)
            kern = kd if kern is None else kern + kd
        dj = d_scr[:, j * k:(j + 1) * k]
        wt = jnp.concatenate(woutt, axis=1)
        y = jnp.dot(dj, kern.astype(BF16), preferred_element_type=F32)
        y = y + lax.dot_general(st_scr[:, j * k:(j + 1) * k].astype(BF16), wt, _NT, preferred_element_type=F32)
        y = y + dj.astype(F32) * dv_ref[j]
        st_scr[:, j * k:(j + 1) * k] = y
    for m in range(t_steps // pairs):
        moved = piece_transpose([st_scr[:, j * k + m * LANES:j * k + (m + 1) * LANES] for j in range(pairs)])
        for i in range(pairs):
            y_ref[pl.ds(pairs * m + i, rows, stride=t_steps), :] = moved[i]


def _s5_operands(lam_re, lam_im, log_step, b_re, b_im, c_re, c_im, d_skip):
    _, groups, p = lam_re.shape
    hh, t, gp = S5_GROUP, S5_CHUNK, S5_GROUPS_PER_PAIR
    npair = groups // gp
    rep = lambda v: jnp.concatenate([v] * gp, axis=-1)
    own = (jnp.arange(gp * p)[None, :] // p == jnp.arange(groups)[:, None] % gp).astype(F32)
    lam_re, lam_im = rep(lam_re), rep(lam_im)
    dt = jnp.exp(log_step)[..., None]
    xr, xi = lam_re * dt, lam_im * dt
    lbr, lbi = jnp.exp(xr) * jnp.cos(xi), jnp.exp(xr) * jnp.sin(xi)
    nr, ni = lbr - 1.0, lbi
    den = lam_re * lam_re + lam_im * lam_im
    zr, zi = (nr * lam_re + ni * lam_im) / den * own, (ni * lam_re - nr * lam_im) / den * own
    btr, bti = rep(jnp.swapaxes(b_re, -1, -2)), rep(jnp.swapaxes(b_im, -1, -2))
    bbr = zr[:, :, None, :] * btr - zi[:, :, None, :] * bti
    bbi = zr[:, :, None, :] * bti + zi[:, :, None, :] * btr
    ccr, cci = rep(c_re) * own[None, :, None, :], rep(c_im) * own[None, :, None, :]

    def powers(e):
        kk = e[:, None, :, None]
        mag, ang = jnp.exp(kk * xr[:, :, None, :]), kk * xi[:, :, None, :]
        return (mag * jnp.cos(ang))[:, :, :, None, :], (mag * jnp.sin(ang))[:, :, :, None, :]

    tt = jnp.arange(t, dtype=F32)
    pir, pii = powers(jnp.stack([t - 1 - tt, tt]))
    por, poi = powers(jnp.stack([tt + 1, t - tt]))
    b5r, b5i = bbr[:, :, None], bbi[:, :, None]
    c5r, c5i = ccr[:, :, None], cci[:, :, None]
    win_r, win_i = pir * b5r - pii * b5i, pir * b5i + pii * b5r
    out_r, out_i = por * c5r - poi * c5i, por * c5i + poi * c5r

    def pair_layout(x_re, x_im):
        return jnp.stack([x_re[0], x_im[0], x_re[1], x_im[1]]).reshape(4, groups, t * hh, gp * p)

    def pair_vec(e):
        mag, ang = jnp.exp(e * xr) * own, e * xi
        v = jnp.stack([(mag * jnp.cos(ang))[0], (mag * jnp.sin(ang))[0],
                       (mag * jnp.cos(ang))[1], (mag * jnp.sin(ang))[1]])
        return jnp.transpose(v.reshape(4, npair, gp, gp * p).sum(axis=2), (1, 0, 2))

    dvec = jnp.broadcast_to(d_skip.reshape(npair, 1, gp, hh), (npair, t, gp, hh)).reshape(npair, 1, t * gp * hh)
    return (pair_layout(win_r, win_i), pair_layout(out_r, -out_i).astype(BF16),
            pair_vec(float(t)), pair_vec(-float(t)), dvec)


def _s5_scan(u_cm, operands, layer, batch, seq):
    win, woutt, a, sc, dvec = operands
    gp = S5_GROUPS_PER_PAIR
    _, _, groups, group_rows, sl = win.shape
    npair, k = groups // gp, gp * group_rows
    t = S5_CHUNK
    pw = k // t
    pairs = LANES // pw
    chunks = seq // t
    rows = chunks * batch
    width = npair * pw
    nblk = width // LANES
    body = functools.partial(_s5_body, batch=batch, chunks=chunks, pairs=pairs, pw=pw, sl=sl)
    per_blk = lambda shape: pl.BlockSpec((None, pairs) + shape, lambda q: (layer, q, 0, 0))
    comp_blk = pl.BlockSpec((None, 4, gp * pairs, group_rows, sl), lambda q: (layer, 0, q, 0, 0),
                            pipeline_mode=pl.Buffered(1))
    x2d = u_cm.reshape(rows, t * width)
    y = pl.pallas_call(
        body,
        grid=(nblk,),
        in_specs=[pl.BlockSpec((rows, LANES), functools.partial(lambda q, s: (0, s * nblk + q), s=s))
                  for s in range(t)]
        + [comp_blk, comp_blk, per_blk((4, sl)), per_blk((4, sl)), per_blk((1, k))],
        out_specs=pl.BlockSpec((rows * t, LANES), lambda q: (0, q), pipeline_mode=pl.Buffered(1)),
        out_shape=jax.ShapeDtypeStruct((rows * t, width), F32),
        scratch_shapes=[pltpu.VMEM((rows, pairs * k), BF16), pltpu.VMEM((rows, pairs * k), F32)],
        compiler_params=pltpu.CompilerParams(dimension_semantics=("parallel",),
                                             vmem_limit_bytes=VMEM_LIMIT_BYTES),
    )(*([x2d] * t), win, woutt, a, sc, dvec)
    return y.reshape(chunks, batch, t, width)


def _dft_tables(blk):
    n = 2 * blk
    f = np.arange(blk, dtype=np.int64)
    sym = ((2 * f[:, None] + 1) * (2 * f[None, :] + 1)) % (4 * n)
    ang_sym = sym.astype(np.float64) * (2.0 * np.pi / (4 * n))
    lag = np.arange(2 * blk, dtype=np.int64) - blk
    ang_lag = (((2 * f[:, None] + 1) * lag[None, :]) % (2 * n)).astype(np.float64) * (2.0 * np.pi / (2 * n))
    tc, ts = np.cos(ang_lag), np.sin(ang_lag)
    tc[:, 0] = 0.0
    ts[:, 0] = 0.0
    as_bf = lambda x: jnp.asarray(x.astype(np.float32)).astype(BF16)
    return as_bf(np.cos(ang_sym)), as_bf(np.sin(ang_sym)), as_bf(tc), as_bf(ts)


def _position_features(seq, pad_to):
    t = np.linspace(0.0, 1.0, seq)[:, None]
    pos = np.arange(seq, dtype=np.float64)[:, None]
    bands = np.linspace(1e-4, HY_BANDS - 1, HY_BANDS)[None, :]
    ang = bands * pos * (2.0 * math.pi / seq)
    feats = np.zeros((seq, pad_to), np.float64)
    feats[:, :1 + 2 * HY_BANDS] = np.concatenate([t, np.cos(ang), -np.sin(ang)], axis=-1)
    mirror = lambda x: np.roll(x[::-1], 1, axis=0)
    f32 = lambda x: jnp.asarray(x.astype(np.float32))
    return f32(feats), f32(t), f32(mirror(feats)), f32(mirror(t))


_HY_COMBO_TERMS = (
    ((0, 1),), ((-1, 1), (0, -1)), ((1, 1), (0, -1)),
    ((-2, 1), (0, -1)), ((-3, 1), (-1, -1), (-2, -1), (0, 1)), ((-1, 1), (1, -1), (-2, -1), (0, 1)),
    ((2, 1), (0, -1)), ((1, 1), (-1, -1), (2, -1), (0, 1)), ((3, 1), (1, -1), (2, -1), (0, 1)),
)


def _hy_filter_body(feats_ref, t_ref, featsm_ref, tm_ref, w1_ref, b1_ref, fr_ref, w2_ref, b2_ref, w3f_ref,
                    w3b_ref, decf_ref, decb_ref, tc_ref, ts_ref, gr_ref, gi_ref, h2_scr, h2m_scr, k2_scr,
                    kr_scr, ki_scr, *, seq, nb):
    hi = lax.Precision.HIGHEST
    blk = seq // nb

    @pl.when(pl.program_id(0) == 0)
    def _():
        fr = fr_ref[...]

        def mlp(feats):
            h1 = jnp.sin(fr * (jnp.dot(feats, w1_ref[...], precision=hi, preferred_element_type=F32) + b1_ref[...]))
            return jnp.sin(fr * (jnp.dot(h1, w2_ref[...], precision=hi, preferred_element_type=F32) + b2_ref[...]))

        def split(h):
            top = h.astype(BF16).astype(F32)
            return jnp.concatenate([top, h - top, top, h - top], axis=1).astype(BF16)

        h2_scr[...] = split(mlp(feats_ref[...]))
        h2m_scr[...] = split(mlp(featsm_ref[...]))

    def split_rows(w):
        top = w.astype(BF16).astype(F32)
        return jnp.concatenate([top, top, w - top, w - top], axis=0).astype(BF16)

    hf = jnp.dot(h2_scr[...], split_rows(w3f_ref[...]), preferred_element_type=F32)
    hf = hf * (jnp.exp(-t_ref[...] * jnp.abs(decf_ref[...])) + HY_SHIFT)
    hb = jnp.dot(h2m_scr[...], split_rows(w3b_ref[...]), preferred_element_type=F32)
    hb = hb * (jnp.exp(-tm_ref[...] * jnp.abs(decb_ref[...])) + HY_SHIFT)
    row = lax.broadcasted_iota(jnp.int32, hb.shape, 0)
    hb = jnp.where(row == 0, 0.0, hb)
    norm = jnp.sum(hf * hf, axis=0, keepdims=True) + jnp.sum(hb * hb, axis=0, keepdims=True)
    scale = lax.rsqrt(norm + HY_EPS) * (1.0 / blk)
    k2_scr[0:seq, :] = hb.astype(BF16)
    k2_scr[seq:2 * seq, :] = hf.astype(BF16)
    for d in range(2 * nb - 1):
        seg = k2_scr[d * blk:(d + 2) * blk, :]
        kr_scr[d] = jnp.dot(tc_ref[...], seg, preferred_element_type=F32) * scale
        ki_scr[d] = jnp.dot(ts_ref[...], seg, preferred_element_type=F32) * scale
    for idx, terms in enumerate(_HY_COMBO_TERMS):
        for src, dst in ((kr_scr, gr_ref), (ki_scr, gi_ref)):
            acc = None
            for lag, weight in terms:
                term = src[lag + nb - 1]
                acc = (term if weight > 0 else -term) if acc is None else (acc + term if weight > 0 else acc - term)
            dst[0, idx] = acc


def _hy_filters(seq, w1, b1, freq, w2, b2, w3, decay, width, tables, nb, *, col_block=256):
    _, _, tc, ts = tables
    blk = seq // nb
    ffn = w2.shape[0]
    feats, t, featsm, tm = _position_features(seq, LANES)
    w1p = jnp.zeros((LANES, ffn), F32).at[:w1.shape[0]].set(w1)
    nblk = width // col_block
    side = lambda rows, direction: pl.BlockSpec(
        (rows, col_block), lambda i: (0, (2 * (i // nblk) + direction) * nblk + i % nblk))
    out = pl.BlockSpec((1, HY_COMBOS, blk, col_block), lambda i: (i // nblk, 0, 0, i % nblk))
    body = functools.partial(_hy_filter_body, seq=seq, nb=nb)
    row2 = lambda v: v.reshape(1, -1)
    return pl.pallas_call(
        body,
        grid=(HY_ORDER * nblk,),
        in_specs=[_resident(feats.shape), _resident(t.shape), _resident(feats.shape), _resident(t.shape),
                  _resident(w1p.shape), _resident((1, ffn)), _resident((1, ffn)), _resident(w2.shape),
                  _resident((1, ffn)), side(ffn, 0), side(ffn, 1), side(1, 0), side(1, 1),
                  _resident(tc.shape), _resident(ts.shape)],
        out_specs=[out, out],
        out_shape=[jax.ShapeDtypeStruct((HY_ORDER, HY_COMBOS, blk, width), F32)] * 2,
        scratch_shapes=[pltpu.VMEM((seq, 4 * ffn), BF16), pltpu.VMEM((seq, 4 * ffn), BF16),
                        pltpu.VMEM((2 * seq, col_block), BF16),
                        pltpu.VMEM((2 * nb - 1, blk, col_block), F32), pltpu.VMEM((2 * nb - 1, blk, col_block), F32)],
        compiler_params=pltpu.CompilerParams(dimension_semantics=("arbitrary",),
                                             vmem_limit_bytes=VMEM_LIMIT_BYTES),
    )(feats, t, featsm, tm, w1p, row2(b1), row2(freq), w2, row2(b2), w3, w3, row2(decay), row2(decay), tc, ts)


def _hy_conv_body(pv_ref, p1_ref, p2_ref, g_ref, cw_ref, cb_ref, gr_ref, gi_ref, d_ref, mc_ref, ms_ref, o_ref,
                  z_scr, a_scr, b_scr, xs_scr, *, seq, nb, cw, row_chunk, problems):
    blk = seq // nb
    gate_refs = (p1_ref, p2_ref)
    lanes = lambda j: slice(j * cw, (j + 1) * cw)
    cadd = lambda x, y: (x[0] + y[0], x[1] + y[1])
    cmul = lambda x, y: (x[0] * y[0] - x[1] * y[1], x[0] * y[1] + x[1] * y[0])

    def short_conv(k, p_ref, part):
        x = p_ref[k].astype(F32)
        pad = xs_scr.at[k, part]
        pad[0:SUBLANES, :] = jnp.zeros((SUBLANES, cw), F32)
        pad[SUBLANES + seq:2 * SUBLANES + seq, :] = jnp.zeros((SUBLANES, cw), F32)
        pad[SUBLANES:SUBLANES + seq, :] = x
        prev, nxt = pad[pl.ds(SUBLANES - 1, seq), :], pad[pl.ds(SUBLANES + 1, seq), :]
        w = cw_ref[:, part, :]
        return cb_ref[part:part + 1, :] + prev * w[0:1] + x * w[1:2] + nxt * w[2:3]

    def load_v(k):
        v = short_conv(k, pv_ref, 0)
        for j in range(nb):
            z_scr[k, :, lanes(j)] = v[j * blk:(j + 1) * blk, :]

    def forward(k, o):
        zb = z_scr[k].astype(BF16)
        a_scr[k] = jnp.dot(mc_ref[...], zb, preferred_element_type=F32)
        b_scr[k] = jnp.dot(ms_ref[...], zb, preferred_element_type=F32)

    def pointwise(k, o):
        def toeplitz2(base, x0, x1, r):
            g = lambda idx: (gr_ref[o, idx, r, :], gi_ref[o, idx, r, :])
            q = cmul(g(base), cadd(x0, x1))
            return cadd(q, cmul(g(base + 1), x1)), cadd(q, cmul(g(base + 2), x0))

        for c in range(blk // row_chunk):
            r = slice(c * row_chunk, (c + 1) * row_chunk)
            z = [(a_scr[k, r, lanes(j)], b_scr[k, r, lanes(j)]) for j in range(nb)]
            p1 = toeplitz2(0, cadd(z[0], z[2]), cadd(z[1], z[3]), r)
            p2 = toeplitz2(3, z[2], z[3], r)
            p3 = toeplitz2(6, z[0], z[1], r)
            for i, y in enumerate((cadd(p1[0], p2[0]), cadd(p1[1], p2[1]), cadd(p1[0], p3[0]), cadd(p1[1], p3[1]))):
                a_scr[k, r, lanes(i)], b_scr[k, r, lanes(i)] = y

    def inverse(k, o):
        conv = (jnp.dot(mc_ref[...], a_scr[k].astype(BF16), preferred_element_type=F32)
                + jnp.dot(ms_ref[...], b_scr[k].astype(BF16), preferred_element_type=F32))
        gate = short_conv(k, gate_refs[o], o + 1)
        dsk = d_ref[o:o + 1, :]
        for j in range(nb):
            rows = slice(j * blk, (j + 1) * blk)
            z = gate[rows, :] * (conv[:, lanes(j)] + dsk * z_scr[k, :, lanes(j)])
            if o == HY_ORDER - 1:
                o_ref[k, rows, :] = (z * g_ref[k, rows, :].astype(F32)).astype(o_ref.dtype)
            else:
                z_scr[k, :, lanes(j)] = z

    stages = [load_v]
    for o in range(HY_ORDER):
        stages += [functools.partial(forward, o=o), functools.partial(pointwise, o=o),
                   functools.partial(inverse, o=o)]
    for s in range(len(stages) + problems - 1):
        for k in range(problems):
            if 0 <= s - k < len(stages):
                stages[s - k](k)


def _hy_conv(p_hy, gate_silu, conv_w, conv_b, gr, gi, d_skip, tables, batch, seq, width, nb, *, col_block=LANES,
             row_chunk=SUBLANES, problems=2):
    assert nb == HY_TIME_BLOCKS
    mc, ms, _, _ = tables
    blk = seq // nb
    nblk = width // col_block
    part = lambda k: pl.BlockSpec((problems, seq, col_block), lambda c, b: (b, 0, k * nblk + c))
    cw = conv_w.reshape(conv_w.shape[0], HY_ORDER + 1, width)
    cb = conv_b.reshape(HY_ORDER + 1, width)
    filt = pl.BlockSpec((HY_ORDER, HY_COMBOS, blk, col_block), lambda c, b: (0, 0, 0, c),
                        pipeline_mode=pl.Buffered(1))
    body = functools.partial(_hy_conv_body, seq=seq, nb=nb, cw=col_block, row_chunk=row_chunk, problems=problems)
    wide = nb * col_block
    scratch = lambda dtype: pltpu.VMEM((problems, blk, wide), dtype)
    return pl.pallas_call(
        body,
        grid=(nblk, batch // problems),
        in_specs=[part(0), part(1), part(2),
                  pl.BlockSpec((problems, seq, col_block), lambda c, b: (b, 0, c)),
                  pl.BlockSpec((cw.shape[0], HY_ORDER + 1, col_block), lambda c, b: (0, 0, c)),
                  pl.BlockSpec((HY_ORDER + 1, col_block), lambda c, b: (0, c)),
                  filt, filt,
                  pl.BlockSpec((HY_ORDER, col_block), lambda c, b: (0, c)),
                  _resident(mc.shape), _resident(ms.shape)],
        out_specs=pl.BlockSpec((problems, seq, col_block), lambda c, b: (b, 0, c)),
        out_shape=jax.ShapeDtypeStruct((batch, seq, width), BF16),
        scratch_shapes=[scratch(F32), scratch(F32), scratch(F32),
                        pltpu.VMEM((problems, HY_ORDER + 1, seq + 2 * SUBLANES, col_block), F32)],
        compiler_params=pltpu.CompilerParams(dimension_semantics=("parallel", "parallel"),
                                             vmem_limit_bytes=VMEM_LIMIT_BYTES),
    )(p_hy, p_hy, p_hy, gate_silu, cw, cb, gr, gi, d_skip, mc, ms)


def _merge_body(ys_ref, gs_ref, yh_ref, m_ref, h_ref, wg_ref, bg_ref, wbs_ref, wbh_ref, wo_ref, fw_ref, o_ref,
                *, d_model, final_norm):
    ys = ys_ref[...]
    y = _gelu_tanh(ys.reshape(ys.shape[0] * ys.shape[1], ys.shape[2]))
    glu = y * _sigmoid(jnp.dot(y.astype(BF16), wg_ref[...], preferred_element_type=F32) + bg_ref[...])
    s5 = (glu * gs_ref[...].astype(F32)).astype(BF16)
    y_s5 = jnp.dot(s5, wbs_ref[...], preferred_element_type=F32)
    y_hy = jnp.dot(yh_ref[...], wbh_ref[...], preferred_element_type=F32)
    merged = m_ref[:, :d_model].astype(F32) * y_s5 + m_ref[:, d_model:].astype(F32) * y_hy
    h = h_ref[...] + jnp.dot(merged.astype(BF16), wo_ref[...], preferred_element_type=F32)
    if final_norm:
        ms = jnp.mean(h * h, axis=-1, keepdims=True)
        h = h * lax.rsqrt(ms + RMS_EPS) * fw_ref[...]
    o_ref[...] = h


def _merge(y_s5_cm, gs, y_hypre, m, h2d, w_glu, b_glu, w_bs, w_bh, w_out, layer, final_w, final_norm, seq, *,
           row_tile=512):
    rows, d = h2d.shape
    sw = y_s5_cm.shape[-1]
    hw = y_hypre.shape[1]
    tile = lambda w: pl.BlockSpec((row_tile, w), lambda i: (i, 0))
    body = functools.partial(_merge_body, d_model=d, final_norm=final_norm)
    return pl.pallas_call(
        body,
        grid=(rows // row_tile,),
        in_specs=[_chunk_major_spec(row_tile, seq, sw), tile(sw), tile(hw), tile(2 * d), tile(d),
                  _resident_layer(w_glu.shape, layer), _resident_layer((b_glu.shape[0], 1, sw), layer),
                  _resident_layer(w_bs.shape, layer), _resident_layer(w_bh.shape, layer),
                  _resident_layer(w_out.shape, layer), _resident((1, d))],
        out_specs=tile(d),
        out_shape=jax.ShapeDtypeStruct((rows, d), F32),
        compiler_params=pltpu.CompilerParams(dimension_semantics=("parallel",),
                                             vmem_limit_bytes=VMEM_LIMIT_BYTES),
    )(y_s5_cm, gs, y_hypre, m, h2d, w_glu, b_glu.reshape(-1, 1, sw), w_bs, w_bh, w_out, final_w.reshape(1, d))


def kernel(x, norm_w, w_in, s5_lam_re, s5_lam_im, s5_log_step, s5_b_re, s5_b_im, s5_c_re, s5_c_im, s5_d, s5_w_glu, s5_b_glu, hy_conv_w, hy_conv_b, hy_w1, hy_b1, hy_freq, hy_w2, hy_b2, hy_w3, hy_decay, hy_d, w_branch_s5, w_branch_hy, w_out, final_norm_w):
    batch, seq, d_model = x.shape
    depth = w_in.shape[0]
    s5_width = s5_d.shape[1]
    hy_width = hy_d.shape[2]
    widths = (s5_width, s5_width, (HY_ORDER + 1) * hy_width, hy_width, 2 * d_model)
    nb = HY_TIME_BLOCKS
    tables = _dft_tables(seq // nb)
    h = x.reshape(batch * seq, d_model)
    w_in_bf, w_glu_bf, w_bs_bf, w_bh_bf, w_out_bf = (w.astype(BF16) for w in (w_in, s5_w_glu, w_branch_s5,
                                                                              w_branch_hy, w_out))
    s5_ops = jax.vmap(_s5_operands)(s5_lam_re, s5_lam_im, s5_log_step, s5_b_re, s5_b_im, s5_c_re, s5_c_im, s5_d)
    for l in range(depth):
        u, gs, p_hy, gh, m = _inproj(h, norm_w, w_in_bf, l, widths, batch, seq)
        y_s5 = _s5_scan(u, s5_ops, l, batch, seq)
        gr, gi = _hy_filters(seq, hy_w1[l], hy_b1[l], hy_freq[l], hy_w2[l], hy_b2[l], hy_w3[l], hy_decay[l],
                             hy_width, tables, nb)
        y_hypre = _hy_conv(p_hy.reshape(batch, seq, -1), gh.reshape(batch, seq, -1), hy_conv_w[l], hy_conv_b[l],
                           gr, gi, hy_d[l], tables, batch, seq, hy_width, nb)
        h = _merge(y_s5, gs, y_hypre.reshape(batch * seq, hy_width), m, h, w_glu_bf, s5_b_glu, w_bs_bf, w_bh_bf,
                   w_out_bf, l, final_norm_w, l == depth - 1, seq)
    return h.reshape(batch, seq, d_model)
```

```python
import functools
import math

import numpy as np
import jax
import jax.numpy as jnp
from jax import lax
from jax.experimental import pallas as pl
from jax.experimental.pallas import tpu as pltpu

F32 = jnp.float32
BF16 = jnp.bfloat16

RMS_EPS = 1e-6
S5_GROUP = 16
S5_CHUNK = 16
S5_GROUPS_PER_PAIR = 2
LANES = 128
SUBLANES = 8
HY_ORDER = 2
HY_BANDS = 16
HY_SHIFT = 0.05
HY_EPS = 1e-6
HY_TIME_BLOCKS = 4
HY_LAGS = 2 * HY_TIME_BLOCKS - 1
HY_COMBOS = 9
VMEM_LIMIT_BYTES = 56 * 1024 * 1024

_NT = (((1,), (1,)), ((), ()))


def _sigmoid(x):
    return 1.0 / (1.0 + jnp.exp(-x))


def _silu(x):
    return x * _sigmoid(x)


def _gelu_tanh(x):
    return 0.5 * x * (1.0 + jnp.tanh(math.sqrt(2.0 / math.pi) * (x + 0.044715 * (x * x * x))))


def _resident(shape):
    zeros = (0,) * len(shape)
    return pl.BlockSpec(shape, lambda *_: zeros, pipeline_mode=pl.Buffered(1))


def _resident_layer(stacked_shape, layer):
    zeros = (0,) * (len(stacked_shape) - 1)
    return pl.BlockSpec((None,) + tuple(stacked_shape[1:]), lambda *_: (layer,) + zeros,
                        pipeline_mode=pl.Buffered(1))


def _chunk_major_spec(row_tile, seq, width):
    tiles_per_seq = seq // row_tile
    return pl.BlockSpec((row_tile // S5_CHUNK, None, S5_CHUNK, width),
                        lambda i: (i % tiles_per_seq, i // tiles_per_seq, 0, 0))


def _inproj_body(x_ref, nw_ref, w_ref, u_ref, gs_ref, p_ref, gh_ref, m_ref, *, bounds, col_chunk):
    x = x_ref[...]
    ms = jnp.mean(x * x, axis=-1, keepdims=True)
    xb = (x * lax.rsqrt(ms + RMS_EPS) * nw_ref[...]).astype(BF16)
    outs = ((u_ref, None), (gs_ref, _silu), (p_ref, None), (gh_ref, _silu), (m_ref, _sigmoid))
    for (out_ref, act), lo, hi in zip(outs, bounds[:-1], bounds[1:]):
        for c0 in range(lo, hi, col_chunk):
            y = jnp.dot(xb, w_ref[:, c0:c0 + col_chunk], preferred_element_type=F32)
            if act is not None:
                y = act(y)
            if out_ref is u_ref:
                out_ref[:, :, c0 - lo:c0 - lo + col_chunk] = y.astype(out_ref.dtype).reshape(
                    y.shape[0] // S5_CHUNK, S5_CHUNK, col_chunk)
            else:
                out_ref[:, c0 - lo:c0 - lo + col_chunk] = y.astype(out_ref.dtype)


def _inproj(h2d, norm_w, w_in_bf, layer, widths, batch, seq, *, row_tile=256, col_chunk=512):
    rows, d = h2d.shape
    bounds = [0]
    for w in widths:
        bounds.append(bounds[-1] + w)
    body = functools.partial(_inproj_body, bounds=tuple(bounds), col_chunk=col_chunk)
    tile = lambda w: pl.BlockSpec((row_tile, w), lambda i: (i, 0))
    return pl.pallas_call(
        body,
        grid=(rows // row_tile,),
        in_specs=[tile(d), _resident_layer((norm_w.shape[0], 1, d), layer), _resident_layer(w_in_bf.shape, layer)],
        out_specs=[_chunk_major_spec(row_tile, seq, widths[0])] + [tile(w) for w in widths[1:]],
        out_shape=[jax.ShapeDtypeStruct((seq // S5_CHUNK, batch, S5_CHUNK, widths[0]), BF16)]
        + [jax.ShapeDtypeStruct((rows, w), BF16) for w in widths[1:]],
        compiler_params=pltpu.CompilerParams(dimension_semantics=("parallel",),
                                             vmem_limit_bytes=VMEM_LIMIT_BYTES),
    )(h2d, norm_w.reshape(-1, 1, d), w_in_bf)


def _s5_body(*refs, batch, chunks, pairs, pw, sl):
    t_steps = S5_CHUNK
    x_refs = refs[:t_steps]
    win_ref, woutt_ref, a_ref, sc_ref, dv_ref, y_ref, d_scr, st_scr = refs[t_steps:]
    k = t_steps * pw
    rows = batch * chunks
    assert k == 4 * sl

    assert pairs == 4 and pairs * pw == LANES and t_steps % pairs == 0

    def piece_transpose(v):
        lane = lax.broadcasted_iota(jnp.int32, v[0].shape, 1)
        upper, odd = lane >= 2 * pw, (lane // pw) % 2 == 1
        t0 = jnp.where(upper, pltpu.roll(v[2], 2 * pw, 1), v[0])
        t2 = jnp.where(upper, v[2], pltpu.roll(v[0], 2 * pw, 1))
        t1 = jnp.where(upper, pltpu.roll(v[3], 2 * pw, 1), v[1])
        t3 = jnp.where(upper, v[3], pltpu.roll(v[1], 2 * pw, 1))
        return (jnp.where(odd, pltpu.roll(t1, pw, 1), t0), jnp.where(odd, t1, pltpu.roll(t0, LANES - pw, 1)),
                jnp.where(odd, pltpu.roll(t3, pw, 1), t2), jnp.where(odd, t3, pltpu.roll(t2, LANES - pw, 1)))

    def pair_rows(ref, c, j):
        g = [ref[c, S5_GROUPS_PER_PAIR * j + a] for a in range(S5_GROUPS_PER_PAIR)]
        return jnp.concatenate([ga[t * S5_GROUP:(t + 1) * S5_GROUP] for t in range(t_steps) for ga in g], axis=0)

    for j in range(pairs):
        d_scr[:, j * k:(j + 1) * k] = jnp.concatenate(
            [x_refs[t][:, j * pw:(j + 1) * pw] for t in range(t_steps)], axis=1)
        win = jnp.concatenate([pair_rows(win_ref, c, j) for c in range(4)], axis=1).astype(BF16)
        st_scr[:, j * k:(j + 1) * k] = jnp.dot(d_scr[:, j * k:(j + 1) * k], win, preferred_element_type=F32)

    coef = [[jnp.broadcast_to(a_ref[j, c:c + 1, :], (batch, sl)) for c in range(4)] for j in range(pairs)]

    per_iter = 2
    assert chunks % per_iter == 0

    def step(i, carry):
        state = list(carry)
        rows_of, loaded = [], []
        for u in range(per_iter):
            c = i * per_iter + u
            rows_of.append((pl.ds(pl.multiple_of(c * batch, batch), batch),
                            pl.ds(pl.multiple_of((chunks - 1 - c) * batch, batch), batch)))
            loaded.append([st_scr[rows_of[u][col // 2 % 2], col * sl:(col + 1) * sl] for col in range(4 * pairs)])
        stores = []
        for u in range(per_iter):
            for j in range(pairs):
                for direction in range(2):
                    re, im = 4 * j + 2 * direction, 4 * j + 2 * direction + 1
                    sr, si = state[re], state[im]
                    ar, ai = coef[j][2 * direction], coef[j][2 * direction + 1]
                    stores += [(u, re, sr), (u, im, si)]
                    state[re] = ar * sr - ai * si + loaded[u][re]
                    state[im] = ar * si + ai * sr + loaded[u][im]
        for u, col, value in stores:
            st_scr[rows_of[u][col // 2 % 2], col * sl:(col + 1) * sl] = value
        return tuple(state)

    zero = jnp.zeros((batch, sl), F32)
    lax.fori_loop(0, chunks // per_iter, step, (zero,) * (4 * pairs))

    in_step = lax.broadcasted_iota(jnp.int32, (k, k), 0) // pw
    out_step = lax.broadcasted_iota(jnp.int32, (k, k), 1) // pw
    for j in range(pairs):
        sc = sc_ref[j]
        kern = None
        woutt = [pair_rows(woutt_ref, c, j) for c in range(4)]
        for direction in range(2):
            wr, wi = pair_rows(win_ref, 2 * direction, j), pair_rows(win_ref, 2 * direction + 1, j)
            pr, pi = sc[2 * direction:2 * direction + 1, :], sc[2 * direction + 1:2 * direction + 2, :]
            wp = jnp.concatenate([wr * pr - wi * pi, wr * pi + wi * pr], axis=1).astype(BF16)
            wt = jnp.concatenate(woutt[2 * direction:2 * direction + 2], axis=1)
            kd = lax.dot_general(wp, wt, _NT, preferred_element_type=F32)
            kd = jnp.where(out_step >= in_step if direction == 0 else out_step <= in_step, kd, 0.0)
            kern = kd if kern is None else kern + kd
        dj = d_scr[:, j * k:(j + 1) * k]
        wt = jnp.concatenate(woutt, axis=1)
        y = jnp.dot(dj, kern.astype(BF16), preferred_element_type=F32)
        y = y + lax.dot_general(st_scr[:, j * k:(j + 1) * k].astype(BF16), wt, _NT, preferred_element_type=F32)
        y = y + dj.astype(F32) * dv_ref[j]
        st_scr[:, j * k:(j + 1) * k] = y
    for m in range(t_steps // pairs):
        moved = piece_transpose([st_scr[:, j * k + m * LANES:j * k + (m + 1) * LANES] for j in range(pairs)])
        for i in range(pairs):
            y_ref[pl.ds(pairs * m + i, rows, stride=t_steps), :] = moved[i]


def _s5_operands(lam_re, lam_im, log_step, b_re, b_im, c_re, c_im, d_skip):
    _, groups, p = lam_re.shape
    hh, t, gp = S5_GROUP, S5_CHUNK, S5_GROUPS_PER_PAIR
    npair = groups // gp
    rep = lambda v: jnp.concatenate([v] * gp, axis=-1)
    own = (jnp.arange(gp * p)[None, :] // p == jnp.arange(groups)[:, None] % gp).astype(F32)
    lam_re, lam_im = rep(lam_re), rep(lam_im)
    dt = jnp.exp(log_step)[..., None]
    xr, xi = lam_re * dt, lam_im * dt
    lbr, lbi = jnp.exp(xr) * jnp.cos(xi), jnp.exp(xr) * jnp.sin(xi)
    nr, ni = lbr - 1.0, lbi
    den = lam_re * lam_re + lam_im * lam_im
    zr, zi = (nr * lam_re + ni * lam_im) / den * own, (ni * lam_re - nr * lam_im) / den * own
    btr, bti = rep(jnp.swapaxes(b_re, -1, -2)), rep(jnp.swapaxes(b_im, -1, -2))
    bbr = zr[:, :, None, :] * btr - zi[:, :, None, :] * bti
    bbi = zr[:, :, None, :] * bti + zi[:, :, None, :] * btr
    ccr, cci = rep(c_re) * own[None, :, None, :], rep(c_im) * own[None, :, None, :]

    def powers(e):
        kk = e[:, None, :, None]
        mag, ang = jnp.exp(kk * xr[:, :, None, :]), kk * xi[:, :, None, :]
        return (mag * jnp.cos(ang))[:, :, :, None, :], (mag * jnp.sin(ang))[:, :, :, None, :]

    tt = jnp.arange(t, dtype=F32)
    pir, pii = powers(jnp.stack([t - 1 - tt, tt]))
    por, poi = powers(jnp.stack([tt + 1, t - tt]))
    b5r, b5i = bbr[:, :, None], bbi[:, :, None]
    c5r, c5i = ccr[:, :, None], cci[:, :, None]
    win_r, win_i = pir * b5r - pii * b5i, pir * b5i + pii * b5r
    out_r, out_i = por * c5r - poi * c5i, por * c5i + poi * c5r

    def pair_layout(x_re, x_im):
        return jnp.stack([x_re[0], x_im[0], x_re[1], x_im[1]]).reshape(4, groups, t * hh, gp * p)

    def pair_vec(e):
        mag, ang = jnp.exp(e * xr) * own, e * xi
        v = jnp.stack([(mag * jnp.cos(ang))[0], (mag * jnp.sin(ang))[0],
                       (mag * jnp.cos(ang))[1], (mag * jnp.sin(ang))[1]])
        return jnp.transpose(v.reshape(4, npair, gp, gp * p).sum(axis=2), (1, 0, 2))

    dvec = jnp.broadcast_to(d_skip.reshape(npair, 1, gp, hh), (npair, t, gp, hh)).reshape(npair, 1, t * gp * hh)
    return (pair_layout(win_r, win_i), pair_layout(out_r, -out_i).astype(BF16),
            pair_vec(float(t)), pair_vec(-float(t)), dvec)


def _s5_scan(u_cm, operands, layer, batch, seq):
    win, woutt, a, sc, dvec = operands
    gp = S5_GROUPS_PER_PAIR
    _, _, groups, group_rows, sl = win.shape
    npair, k = groups // gp, gp * group_rows
    t = S5_CHUNK
    pw = k // t
    pairs = LANES // pw
    chunks = seq // t
    rows = chunks * batch
    width = npair * pw
    nblk = width // LANES
    body = functools.partial(_s5_body, batch=batch, chunks=chunks, pairs=pairs, pw=pw, sl=sl)
    per_blk = lambda shape: pl.BlockSpec((None, pairs) + shape, lambda q: (layer, q, 0, 0))
    comp_blk = pl.BlockSpec((None, 4, gp * pairs, group_rows, sl), lambda q: (layer, 0, q, 0, 0),
                            pipeline_mode=pl.Buffered(1))
    x2d = u_cm.reshape(rows, t * width)
    y = pl.pallas_call(
        body,
        grid=(nblk,),
        in_specs=[pl.BlockSpec((rows, LANES), functools.partial(lambda q, s: (0, s * nblk + q), s=s))
                  for s in range(t)]
        + [comp_blk, comp_blk, per_blk((4, sl)), per_blk((4, sl)), per_blk((1, k))],
        out_specs=pl.BlockSpec((rows * t, LANES), lambda q: (0, q), pipeline_mode=pl.Buffered(1)),
        out_shape=jax.ShapeDtypeStruct((rows * t, width), F32),
        scratch_shapes=[pltpu.VMEM((rows, pairs * k), BF16), pltpu.VMEM((rows, pairs * k), F32)],
        compiler_params=pltpu.CompilerParams(dimension_semantics=("parallel",),
                                             vmem_limit_bytes=VMEM_LIMIT_BYTES),
    )(*([x2d] * t), win, woutt, a, sc, dvec)
    return y.reshape(chunks, batch, t, width)


def _dft_tables(blk):
    n = 2 * blk
    f = np.arange(blk, dtype=np.int64)
    sym = ((2 * f[:, None] + 1) * (2 * f[None, :] + 1)) % (4 * n)
    ang_sym = sym.astype(np.float64) * (2.0 * np.pi / (4 * n))
    lag = np.arange(2 * blk, dtype=np.int64) - blk
    ang_lag = (((2 * f[:, None] + 1) * lag[None, :]) % (2 * n)).astype(np.float64) * (2.0 * np.pi / (2 * n))
    tc, ts = np.cos(ang_lag), np.sin(ang_lag)
    tc[:, 0] = 0.0
    ts[:, 0] = 0.0
    as_bf = lambda x: jnp.asarray(x.astype(np.float32)).astype(BF16)
    return as_bf(np.cos(ang_sym)), as_bf(np.sin(ang_sym)), as_bf(tc), as_bf(ts)


def _position_features(seq, pad_to):
    t = np.linspace(0.0, 1.0, seq)[:, None]
    pos = np.arange(seq, dtype=np.float64)[:, None]
    bands = np.linspace(1e-4, HY_BANDS - 1, HY_BANDS)[None, :]
    ang = bands * pos * (2.0 * math.pi / seq)
    feats = np.zeros((seq, pad_to), np.float64)
    feats[:, :1 + 2 * HY_BANDS] = np.concatenate([t, np.cos(ang), -np.sin(ang)], axis=-1)
    mirror = lambda x: np.roll(x[::-1], 1, axis=0)
    f32 = lambda x: jnp.asarray(x.astype(np.float32))
    return f32(feats), f32(t), f32(mirror(feats)), f32(mirror(t))


_HY_COMBO_TERMS = (
    ((0, 1),), ((-1, 1), (0, -1)), ((1, 1), (0, -1)),
    ((-2, 1), (0, -1)), ((-3, 1), (-1, -1), (-2, -1), (0, 1)), ((-1, 1), (1, -1), (-2, -1), (0, 1)),
    ((2, 1), (0, -1)), ((1, 1), (-1, -1), (2, -1), (0, 1)), ((3, 1), (1, -1), (2, -1), (0, 1)),
)


def _hy_filter_body(feats_ref, t_ref, featsm_ref, tm_ref, w1_ref, b1_ref, fr_ref, w2_ref, b2_ref, w3f_ref,
                    w3b_ref, decf_ref, decb_ref, tc_ref, ts_ref, gr_ref, gi_ref, h2_scr, h2m_scr, k2_scr,
                    kr_scr, ki_scr, *, seq, nb):
    hi = lax.Precision.HIGHEST
    blk = seq // nb

    @pl.when(pl.program_id(0) == 0)
    def _():
        fr = fr_ref[...]

        def mlp(feats):
            h1 = jnp.sin(fr * (jnp.dot(feats, w1_ref[...], precision=hi, preferred_element_type=F32) + b1_ref[...]))
            return jnp.sin(fr * (jnp.dot(h1, w2_ref[...], precision=hi, preferred_element_type=F32) + b2_ref[...]))

        def split(h):
            top = h.astype(BF16).astype(F32)
            return jnp.concatenate([top, h - top, top, h - top], axis=1).astype(BF16)

        h2_scr[...] = split(mlp(feats_ref[...]))
        h2m_scr[...] = split(mlp(featsm_ref[...]))

    def split_rows(w):
        top = w.astype(BF16).astype(F32)
        return jnp.concatenate([top, top, w - top, w - top], axis=0).astype(BF16)

    hf = jnp.dot(h2_scr[...], split_rows(w3f_ref[...]), preferred_element_type=F32)
    hf = hf * (jnp.exp(-t_ref[...] * jnp.abs(decf_ref[...])) + HY_SHIFT)
    hb = jnp.dot(h2m_scr[...], split_rows(w3b_ref[...]), preferred_element_type=F32)
    hb = hb * (jnp.exp(-tm_ref[...] * jnp.abs(decb_ref[...])) + HY_SHIFT)
    row = lax.broadcasted_iota(jnp.int32, hb.shape, 0)
    hb = jnp.where(row == 0, 0.0, hb)
    norm = jnp.sum(hf * hf, axis=0, keepdims=True) + jnp.sum(hb * hb, axis=0, keepdims=True)
    scale = lax.rsqrt(norm + HY_EPS) * (1.0 / blk)
    k2_scr[0:seq, :] = hb.astype(BF16)
    k2_scr[seq:2 * seq, :] = hf.astype(BF16)
    for d in range(2 * nb - 1):
        seg = k2_scr[d * blk:(d + 2) * blk, :]
        kr_scr[d] = jnp.dot(tc_ref[...], seg, preferred_element_type=F32) * scale
        ki_scr[d] = jnp.dot(ts_ref[...], seg, preferred_element_type=F32) * scale
    for idx, terms in enumerate(_HY_COMBO_TERMS):
        for src, dst in ((kr_scr, gr_ref), (ki_scr, gi_ref)):
            acc = None
            for lag, weight in terms:
                term = src[lag + nb - 1]
                acc = (term if weight > 0 else -term) if acc is None else (acc + term if weight > 0 else acc - term)
            dst[0, idx] = acc


def _hy_filters(seq, w1, b1, freq, w2, b2, w3, decay, width, tables, nb, *, col_block=256):
    _, _, tc, ts = tables
    blk = seq // nb
    ffn = w2.shape[0]
    feats, t, featsm, tm = _position_features(seq, LANES)
    w1p = jnp.zeros((LANES, ffn), F32).at[:w1.shape[0]].set(w1)
    nblk = width // col_block
    side = lambda rows, direction: pl.BlockSpec(
        (rows, col_block), lambda i: (0, (2 * (i // nblk) + direction) * nblk + i % nblk))
    out = pl.BlockSpec((1, HY_COMBOS, blk, col_block), lambda i: (i // nblk, 0, 0, i % nblk))
    body = functools.partial(_hy_filter_body, seq=seq, nb=nb)
    row2 = lambda v: v.reshape(1, -1)
    return pl.pallas_call(
        body,
        grid=(HY_ORDER * nblk,),
        in_specs=[_resident(feats.shape), _resident(t.shape), _resident(feats.shape), _resident(t.shape),
                  _resident(w1p.shape), _resident((1, ffn)), _resident((1, ffn)), _resident(w2.shape),
                  _resident((1, ffn)), side(ffn, 0), side(ffn, 1), side(1, 0), side(1, 1),
                  _resident(tc.shape), _resident(ts.shape)],
        out_specs=[out, out],
        out_shape=[jax.ShapeDtypeStruct((HY_ORDER, HY_COMBOS, blk, width), F32)] * 2,
        scratch_shapes=[pltpu.VMEM((seq, 4 * ffn), BF16), pltpu.VMEM((seq, 4 * ffn), BF16),
                        pltpu.VMEM((2 * seq, col_block), BF16),
                        pltpu.VMEM((2 * nb - 1, blk, col_block), F32), pltpu.VMEM((2 * nb - 1, blk, col_block), F32)],
        compiler_params=pltpu.CompilerParams(dimension_semantics=("arbitrary",),
                                             vmem_limit_bytes=VMEM_LIMIT_BYTES),
    )(feats, t, featsm, tm, w1p, row2(b1), row2(freq), w2, row2(b2), w3, w3, row2(decay), row2(decay), tc, ts)


def _hy_conv_body(pv_ref, p1_ref, p2_ref, g_ref, cw_ref, cb_ref, gr_ref, gi_ref, d_ref, mc_ref, ms_ref, o_ref,
                  z_scr, a_scr, b_scr, xs_scr, *, seq, nb, cw, row_chunk, problems):
    blk = seq // nb
    gate_refs = (p1_ref, p2_ref)
    lanes = lambda j: slice(j * cw, (j + 1) * cw)
    cadd = lambda x, y: (x[0] + y[0], x[1] + y[1])
    cmul = lambda x, y: (x[0] * y[0] - x[1] * y[1], x[0] * y[1] + x[1] * y[0])

    def short_conv(k, p_ref, part):
        x = p_ref[k].astype(F32)
        pad = xs_scr.at[k, part]
        pad[0:SUBLANES, :] = jnp.zeros((SUBLANES, cw), F32)
        pad[SUBLANES + seq:2 * SUBLANES + seq, :] = jnp.zeros((SUBLANES, cw), F32)
        pad[SUBLANES:SUBLANES + seq, :] = x
        prev, nxt = pad[pl.ds(SUBLANES - 1, seq), :], pad[pl.ds(SUBLANES + 1, seq), :]
        w = cw_ref[:, part, :]
        return cb_ref[part:part + 1, :] + prev * w[0:1] + x * w[1:2] + nxt * w[2:3]

    def load_v(k):
        v = short_conv(k, pv_ref, 0)
        for j in range(nb):
            z_scr[k, :, lanes(j)] = v[j * blk:(j + 1) * blk, :]

    def forward(k, o):
        zb = z_scr[k].astype(BF16)
        a_scr[k] = jnp.dot(mc_ref[...], zb, preferred_element_type=F32)
        b_scr[k] = jnp.dot(ms_ref[...], zb, preferred_element_type=F32)

    def pointwise(k, o):
        def toeplitz2(base, x0, x1, r):
            g = lambda idx: (gr_ref[o, idx, r, :], gi_ref[o, idx, r, :])
            q = cmul(g(base), cadd(x0, x1))
            return cadd(q, cmul(g(base + 1), x1)), cadd(q, cmul(g(base + 2), x0))

        for c in range(blk // row_chunk):
            r = slice(c * row_chunk, (c + 1) * row_chunk)
            z = [(a_scr[k, r, lanes(j)], b_scr[k, r, lanes(j)]) for j in range(nb)]
            p1 = toeplitz2(0, cadd(z[0], z[2]), cadd(z[1], z[3]), r)
            p2 = toeplitz2(3, z[2], z[3], r)
            p3 = toeplitz2(6, z[0], z[1], r)
            for i, y in enumerate((cadd(p1[0], p2[0]), cadd(p1[1], p2[1]), cadd(p1[0], p3[0]), cadd(p1[1], p3[1]))):
                a_scr[k, r, lanes(i)], b_scr[k, r, lanes(i)] = y

    def inverse(k, o):
        conv = (jnp.dot(mc_ref[...], a_scr[k].astype(BF16), preferred_element_type=F32)
                + jnp.dot(ms_ref[...], b_scr[k].astype(BF16), preferred_element_type=F32))
        gate = short_conv(k, gate_refs[o], o + 1)
        dsk = d_ref[o:o + 1, :]
        for j in range(nb):
            rows = slice(j * blk, (j + 1) * blk)
            z = gate[rows, :] * (conv[:, lanes(j)] + dsk * z_scr[k, :, lanes(j)])
            if o == HY_ORDER - 1:
                o_ref[k, rows, :] = (z * g_ref[k, rows, :].astype(F32)).astype(o_ref.dtype)
            else:
                z_scr[k, :, lanes(j)] = z

    stages = [load_v]
    for o in range(HY_ORDER):
        stages += [functools.partial(forward, o=o), functools.partial(pointwise, o=o),
                   functools.partial(inverse, o=o)]
    for s in range(len(stages) + problems - 1):
        for k in range(problems):
            if 0 <= s - k < len(stages):
                stages[s - k](k)


def _hy_conv(p_hy, gate_silu, conv_w, conv_b, gr, gi, d_skip, tables, batch, seq, width, nb, *, col_block=LANES,
             row_chunk=SUBLANES, problems=2):
    assert nb == HY_TIME_BLOCKS
    mc, ms, _, _ = tables
    blk = seq // nb
    nblk = width // col_block
    part = lambda k: pl.BlockSpec((problems, seq, col_block), lambda c, b: (b, 0, k * nblk + c))
    cw = conv_w.reshape(conv_w.shape[0], HY_ORDER + 1, width)
    cb = conv_b.reshape(HY_ORDER + 1, width)
    filt = pl.BlockSpec((HY_ORDER, HY_COMBOS, blk, col_block), lambda c, b: (0, 0, 0, c),
                        pipeline_mode=pl.Buffered(1))
    body = functools.partial(_hy_conv_body, seq=seq, nb=nb, cw=col_block, row_chunk=row_chunk, problems=problems)
    wide = nb * col_block
    scratch = lambda dtype: pltpu.VMEM((problems, blk, wide), dtype)
    return pl.pallas_call(
        body,
        grid=(nblk, batch // problems),
        in_specs=[part(0), part(1), part(2),
                  pl.BlockSpec((problems, seq, col_block), lambda c, b: (b, 0, c)),
                  pl.BlockSpec((cw.shape[0], HY_ORDER + 1, col_block), lambda c, b: (0, 0, c)),
                  pl.BlockSpec((HY_ORDER + 1, col_block), lambda c, b: (0, c)),
                  filt, filt,
                  pl.BlockSpec((HY_ORDER, col_block), lambda c, b: (0, c)),
                  _resident(mc.shape), _resident(ms.shape)],
        out_specs=pl.BlockSpec((problems, seq, col_block), lambda c, b: (b, 0, c)),
        out_shape=jax.ShapeDtypeStruct((batch, seq, width), BF16),
        scratch_shapes=[scratch(F32), scratch(F32), scratch(F32),
                        pltpu.VMEM((problems, HY_ORDER + 1, seq + 2 * SUBLANES, col_block), F32)],
        compiler_params=pltpu.CompilerParams(dimension_semantics=("parallel", "parallel"),
                                             vmem_limit_bytes=VMEM_LIMIT_BYTES),
    )(p_hy, p_hy, p_hy, gate_silu, cw, cb, gr, gi, d_skip, mc, ms)


def _merge_body(ys_ref, gs_ref, yh_ref, m_ref, h_ref, wg_ref, bg_ref, wbs_ref, wbh_ref, wo_ref, fw_ref, o_ref,
                *, d_model, final_norm):
    ys = ys_ref[...]
    y = _gelu_tanh(ys.reshape(ys.shape[0] * ys.shape[1], ys.shape[2]))
    glu = y * _sigmoid(jnp.dot(y.astype(BF16), wg_ref[...], preferred_element_type=F32) + bg_ref[...])
    s5 = (glu * gs_ref[...].astype(F32)).astype(BF16)
    y_s5 = jnp.dot(s5, wbs_ref[...], preferred_element_type=F32)
    y_hy = jnp.dot(yh_ref[...], wbh_ref[...], preferred_element_type=F32)
    merged = m_ref[:, :d_model].astype(F32) * y_s5 + m_ref[:, d_model:].astype(F32) * y_hy
    h = h_ref[...] + jnp.dot(merged.astype(BF16), wo_ref[...], preferred_element_type=F32)
    if final_norm:
        ms = jnp.mean(h * h, axis=-1, keepdims=True)
        h = h * lax.rsqrt(ms + RMS_EPS) * fw_ref[...]
    o_ref[...] = h


def _merge(y_s5_cm, gs, y_hypre, m, h2d, w_glu, b_glu, w_bs, w_bh, w_out, layer, final_w, final_norm, seq, *,
           row_tile=1024):
    rows, d = h2d.shape
    sw = y_s5_cm.shape[-1]
    hw = y_hypre.shape[1]
    tile = lambda w: pl.BlockSpec((row_tile, w), lambda i: (i, 0))
    body = functools.partial(_merge_body, d_model=d, final_norm=final_norm)
    return pl.pallas_call(
        body,
        grid=(rows // row_tile,),
        in_specs=[_chunk_major_spec(row_tile, seq, sw), tile(sw), tile(hw), tile(2 * d), tile(d),
                  _resident_layer(w_glu.shape, layer), _resident_layer((b_glu.shape[0], 1, sw), layer),
                  _resident_layer(w_bs.shape, layer), _resident_layer(w_bh.shape, layer),
                  _resident_layer(w_out.shape, layer), _resident((1, d))],
        out_specs=tile(d),
        out_shape=jax.ShapeDtypeStruct((rows, d), F32),
        compiler_params=pltpu.CompilerParams(dimension_semantics=("parallel",),
                                             vmem_limit_bytes=VMEM_LIMIT_BYTES),
    )(y_s5_cm, gs, y_hypre, m, h2d, w_glu, b_glu.reshape(-1, 1, sw), w_bs, w_bh, w_out, final_w.reshape(1, d))


def kernel(x, norm_w, w_in, s5_lam_re, s5_lam_im, s5_log_step, s5_b_re, s5_b_im, s5_c_re, s5_c_im, s5_d, s5_w_glu, s5_b_glu, hy_conv_w, hy_conv_b, hy_w1, hy_b1, hy_freq, hy_w2, hy_b2, hy_w3, hy_decay, hy_d, w_branch_s5, w_branch_hy, w_out, final_norm_w):
    batch, seq, d_model = x.shape
    depth = w_in.shape[0]
    s5_width = s5_d.shape[1]
    hy_width = hy_d.shape[2]
    widths = (s5_width, s5_width, (HY_ORDER + 1) * hy_width, hy_width, 2 * d_model)
    nb = HY_TIME_BLOCKS
    tables = _dft_tables(seq // nb)
    h = x.reshape(batch * seq, d_model)
    w_in_bf, w_glu_bf, w_bs_bf, w_bh_bf, w_out_bf = (w.astype(BF16) for w in (w_in, s5_w_glu, w_branch_s5,
                                                                              w_branch_hy, w_out))
    s5_ops = jax.vmap(_s5_operands)(s5_lam_re, s5_lam_im, s5_log_step, s5_b_re, s5_b_im, s5_c_re, s5_c_im, s5_d)
    for l in range(depth):
        u, gs, p_hy, gh, m = _inproj(h, norm_w, w_in_bf, l, widths, batch, seq)
        y_s5 = _s5_scan(u, s5_ops, l, batch, seq)
        gr, gi = _hy_filters(seq, hy_w1[l], hy_b1[l], hy_freq[l], hy_w2[l], hy_b2[l], hy_w3[l], hy_decay[l],
                             hy_width, tables, nb)
        y_hypre = _hy_conv(p_hy.reshape(batch, seq, -1), gh.reshape(batch, seq, -1), hy_conv_w[l], hy_conv_b[l],
                           gr, gi, hy_d[l], tables, batch, seq, hy_width, nb)
        h = _merge(y_s5, gs, y_hypre.reshape(batch * seq, hy_width), m, h, w_glu_bf, s5_b_glu, w_bs_bf, w_bh_bf,
                   w_out_bf, l, final_norm_w, l == depth - 1, seq)
    return h.reshape(batch, seq, d_model)
```

```python
import functools
import math

import numpy as np
import jax
import jax.numpy as jnp
from jax import lax
from jax.experimental import pallas as pl
from jax.experimental.pallas import tpu as pltpu

F32 = jnp.float32
BF16 = jnp.bfloat16

RMS_EPS = 1e-6
S5_GROUP = 16
S5_CHUNK = 16
S5_GROUPS_PER_PAIR = 2
LANES = 128
SUBLANES = 8
HY_ORDER = 2
HY_BANDS = 16
HY_SHIFT = 0.05
HY_EPS = 1e-6
HY_TIME_BLOCKS = 4
HY_LAGS = 2 * HY_TIME_BLOCKS - 1
HY_COMBOS = 9
VMEM_LIMIT_BYTES = 56 * 1024 * 1024

_NT = (((1,), (1,)), ((), ()))


def _sigmoid(x):
    return 1.0 / (1.0 + jnp.exp(-x))


def _silu(x):
    return x * _sigmoid(x)


def _gelu_tanh(x):
    return 0.5 * x * (1.0 + jnp.tanh(math.sqrt(2.0 / math.pi) * (x + 0.044715 * (x * x * x))))


def _resident(shape):
    zeros = (0,) * len(shape)
    return pl.BlockSpec(shape, lambda *_: zeros, pipeline_mode=pl.Buffered(1))


def _resident_layer(stacked_shape, layer):
    zeros = (0,) * (len(stacked_shape) - 1)
    return pl.BlockSpec((None,) + tuple(stacked_shape[1:]), lambda *_: (layer,) + zeros,
                        pipeline_mode=pl.Buffered(1))


def _chunk_major_spec(row_tile, seq, width):
    tiles_per_seq = seq // row_tile
    return pl.BlockSpec((row_tile // S5_CHUNK, None, S5_CHUNK, width),
                        lambda i: (i % tiles_per_seq, i // tiles_per_seq, 0, 0))


def _inproj_body(x_ref, nw_ref, w_ref, u_ref, gs_ref, p_ref, gh_ref, m_ref, *, bounds, col_chunk):
    x = x_ref[...]
    ms = jnp.mean(x * x, axis=-1, keepdims=True)
    xb = (x * lax.rsqrt(ms + RMS_EPS) * nw_ref[...]).astype(BF16)
    outs = ((u_ref, None), (gs_ref, _silu), (p_ref, None), (gh_ref, _silu), (m_ref, _sigmoid))
    for (out_ref, act), lo, hi in zip(outs, bounds[:-1], bounds[1:]):
        for c0 in range(lo, hi, col_chunk):
            y = jnp.dot(xb, w_ref[:, c0:c0 + col_chunk], preferred_element_type=F32)
            if act is not None:
                y = act(y)
            if out_ref is u_ref:
                out_ref[:, :, c0 - lo:c0 - lo + col_chunk] = y.astype(out_ref.dtype).reshape(
                    y.shape[0] // S5_CHUNK, S5_CHUNK, col_chunk)
            else:
                out_ref[:, c0 - lo:c0 - lo + col_chunk] = y.astype(out_ref.dtype)


def _inproj(h2d, norm_w, w_in_bf, layer, widths, batch, seq, *, row_tile=256, col_chunk=512):
    rows, d = h2d.shape
    bounds = [0]
    for w in widths:
        bounds.append(bounds[-1] + w)
    body = functools.partial(_inproj_body, bounds=tuple(bounds), col_chunk=col_chunk)
    tile = lambda w: pl.BlockSpec((row_tile, w), lambda i: (i, 0))
    return pl.pallas_call(
        body,
        grid=(rows // row_tile,),
        in_specs=[tile(d), _resident_layer((norm_w.shape[0], 1, d), layer), _resident_layer(w_in_bf.shape, layer)],
        out_specs=[_chunk_major_spec(row_tile, seq, widths[0])] + [tile(w) for w in widths[1:]],
        out_shape=[jax.ShapeDtypeStruct((seq // S5_CHUNK, batch, S5_CHUNK, widths[0]), BF16)]
        + [jax.ShapeDtypeStruct((rows, w), BF16) for w in widths[1:]],
        compiler_params=pltpu.CompilerParams(dimension_semantics=("parallel",),
                                             vmem_limit_bytes=VMEM_LIMIT_BYTES),
    )(h2d, norm_w.reshape(-1, 1, d), w_in_bf)


def _s5_body(*refs, batch, chunks, pairs, pw, sl):
    t_steps = S5_CHUNK
    x_refs = refs[:t_steps]
    bb_ref, cc_ref, pin_ref, pout_ref, a_ref, sc_ref, dv_ref, y_ref, d_scr, st_scr = refs[t_steps:]
    k = t_steps * pw
    rows = batch * chunks
    assert k == 4 * sl

    assert pairs == 4 and pairs * pw == LANES and t_steps % pairs == 0

    def piece_transpose(v):
        lane = lax.broadcasted_iota(jnp.int32, v[0].shape, 1)
        upper, odd = lane >= 2 * pw, (lane // pw) % 2 == 1
        t0 = jnp.where(upper, pltpu.roll(v[2], 2 * pw, 1), v[0])
        t2 = jnp.where(upper, v[2], pltpu.roll(v[0], 2 * pw, 1))
        t1 = jnp.where(upper, pltpu.roll(v[3], 2 * pw, 1), v[1])
        t3 = jnp.where(upper, v[3], pltpu.roll(v[1], 2 * pw, 1))
        return (jnp.where(odd, pltpu.roll(t1, pw, 1), t0), jnp.where(odd, t1, pltpu.roll(t0, LANES - pw, 1)),
                jnp.where(odd, pltpu.roll(t3, pw, 1), t2), jnp.where(odd, t3, pltpu.roll(t2, LANES - pw, 1)))

    def pair_operand(fac_ref, pow_ref, direction, j):
        re, im = [], []
        for t in range(t_steps):
            for a in range(S5_GROUPS_PER_PAIR):
                g = S5_GROUPS_PER_PAIR * j + a
                fr, fi = fac_ref[2 * direction, g], fac_ref[2 * direction + 1, g]
                pr, pi = pow_ref[2 * direction, g, t:t + 1, :], pow_ref[2 * direction + 1, g, t:t + 1, :]
                re.append(pr * fr - pi * fi)
                im.append(pr * fi + pi * fr)
        return jnp.concatenate(re, axis=0), jnp.concatenate(im, axis=0)

    for j in range(pairs):
        d_scr[:, j * k:(j + 1) * k] = jnp.concatenate(
            [x_refs[t][:, j * pw:(j + 1) * pw] for t in range(t_steps)], axis=1)
        win = jnp.concatenate(pair_operand(bb_ref, pin_ref, 0, j) + pair_operand(bb_ref, pin_ref, 1, j),
                              axis=1).astype(BF16)
        st_scr[:, j * k:(j + 1) * k] = jnp.dot(d_scr[:, j * k:(j + 1) * k], win, preferred_element_type=F32)

    coef = [[jnp.broadcast_to(a_ref[j, c:c + 1, :], (batch, sl)) for c in range(4)] for j in range(pairs)]

    per_iter = 2
    assert chunks % per_iter == 0

    def step(i, carry):
        state = list(carry)
        rows_of, loaded = [], []
        for u in range(per_iter):
            c = i * per_iter + u
            rows_of.append((pl.ds(pl.multiple_of(c * batch, batch), batch),
                            pl.ds(pl.multiple_of((chunks - 1 - c) * batch, batch), batch)))
            loaded.append([st_scr[rows_of[u][col // 2 % 2], col * sl:(col + 1) * sl] for col in range(4 * pairs)])
        stores = []
        for u in range(per_iter):
            for j in range(pairs):
                for direction in range(2):
                    re, im = 4 * j + 2 * direction, 4 * j + 2 * direction + 1
                    sr, si = state[re], state[im]
                    ar, ai = coef[j][2 * direction], coef[j][2 * direction + 1]
                    stores += [(u, re, sr), (u, im, si)]
                    state[re] = ar * sr - ai * si + loaded[u][re]
                    state[im] = ar * si + ai * sr + loaded[u][im]
        for u, col, value in stores:
            st_scr[rows_of[u][col // 2 % 2], col * sl:(col + 1) * sl] = value
        return tuple(state)

    zero = jnp.zeros((batch, sl), F32)
    lax.fori_loop(0, chunks // per_iter, step, (zero,) * (4 * pairs))

    in_step = lax.broadcasted_iota(jnp.int32, (k, k), 0) // pw
    out_step = lax.broadcasted_iota(jnp.int32, (k, k), 1) // pw
    for j in range(pairs):
        sc = sc_ref[j]
        kern = None
        woutt = []
        for direction in range(2):
            wr, wi = pair_operand(bb_ref, pin_ref, direction, j)
            out_re, out_im = pair_operand(cc_ref, pout_ref, direction, j)
            woutt += [out_re.astype(BF16), (-out_im).astype(BF16)]
            pr, pi = sc[2 * direction:2 * direction + 1, :], sc[2 * direction + 1:2 * direction + 2, :]
            wp = jnp.concatenate([wr * pr - wi * pi, wr * pi + wi * pr], axis=1).astype(BF16)
            wt = jnp.concatenate(woutt[2 * direction:2 * direction + 2], axis=1)
            kd = lax.dot_general(wp, wt, _NT, preferred_element_type=F32)
            kd = jnp.where(out_step >= in_step if direction == 0 else out_step <= in_step, kd, 0.0)
            kern = kd if kern is None else kern + kd
        dj = d_scr[:, j * k:(j + 1) * k]
        wt = jnp.concatenate(woutt, axis=1)
        y = jnp.dot(dj, kern.astype(BF16), preferred_element_type=F32)
        y = y + lax.dot_general(st_scr[:, j * k:(j + 1) * k].astype(BF16), wt, _NT, preferred_element_type=F32)
        y = y + dj.astype(F32) * dv_ref[j]
        st_scr[:, j * k:(j + 1) * k] = y
    for m in range(t_steps // pairs):
        moved = piece_transpose([st_scr[:, j * k + m * LANES:j * k + (m + 1) * LANES] for j in range(pairs)])
        for i in range(pairs):
            y_ref[pl.ds(pairs * m + i, rows, stride=t_steps), :] = moved[i]


def _s5_operands(lam_re, lam_im, log_step, b_re, b_im, c_re, c_im, d_skip):
    _, groups, p = lam_re.shape
    hh, t, gp = S5_GROUP, S5_CHUNK, S5_GROUPS_PER_PAIR
    npair = groups // gp
    rep = lambda v: jnp.concatenate([v] * gp, axis=-1)
    own = (jnp.arange(gp * p)[None, :] // p == jnp.arange(groups)[:, None] % gp).astype(F32)
    lam_re, lam_im = rep(lam_re), rep(lam_im)
    dt = jnp.exp(log_step)[..., None]
    xr, xi = lam_re * dt, lam_im * dt
    lbr, lbi = jnp.exp(xr) * jnp.cos(xi), jnp.exp(xr) * jnp.sin(xi)
    nr, ni = lbr - 1.0, lbi
    den = lam_re * lam_re + lam_im * lam_im
    zr, zi = (nr * lam_re + ni * lam_im) / den * own, (ni * lam_re - nr * lam_im) / den * own
    btr, bti = rep(jnp.swapaxes(b_re, -1, -2)), rep(jnp.swapaxes(b_im, -1, -2))
    bbr = zr[:, :, None, :] * btr - zi[:, :, None, :] * bti
    bbi = zr[:, :, None, :] * bti + zi[:, :, None, :] * btr
    ccr, cci = rep(c_re) * own[None, :, None, :], rep(c_im) * own[None, :, None, :]

    def powers(e):
        kk = e[:, None, :, None]
        mag, ang = jnp.exp(kk * xr[:, :, None, :]), kk * xi[:, :, None, :]
        return mag * jnp.cos(ang), mag * jnp.sin(ang)

    components = lambda x_re, x_im: jnp.stack([x_re[0], x_im[0], x_re[1], x_im[1]])
    tt = jnp.arange(t, dtype=F32)
    pin = components(*powers(jnp.stack([t - 1 - tt, tt])))
    pout = components(*powers(jnp.stack([tt + 1, t - tt])))

    def pair_vec(e):
        mag, ang = jnp.exp(e * xr) * own, e * xi
        v = jnp.stack([(mag * jnp.cos(ang))[0], (mag * jnp.sin(ang))[0],
                       (mag * jnp.cos(ang))[1], (mag * jnp.sin(ang))[1]])
        return jnp.transpose(v.reshape(4, npair, gp, gp * p).sum(axis=2), (1, 0, 2))

    dvec = jnp.broadcast_to(d_skip.reshape(npair, 1, gp, hh), (npair, t, gp, hh)).reshape(npair, 1, t * gp * hh)
    return components(bbr, bbi), components(ccr, cci), pin, pout, pair_vec(float(t)), pair_vec(-float(t)), dvec


def _s5_scan(u_cm, operands, layer, batch, seq):
    bb, cc, pin, pout, a, sc, dvec = operands
    gp, t = S5_GROUPS_PER_PAIR, S5_CHUNK
    _, _, groups, hh, sl = bb.shape
    npair, pw = groups // gp, gp * hh
    k = t * pw
    pairs = LANES // pw
    chunks = seq // t
    rows = chunks * batch
    width = npair * pw
    nblk = width // LANES
    body = functools.partial(_s5_body, batch=batch, chunks=chunks, pairs=pairs, pw=pw, sl=sl)
    per_blk = lambda shape: pl.BlockSpec((None, pairs) + shape, lambda q: (layer, q, 0, 0))
    per_group = lambda group_rows: pl.BlockSpec((None, 4, gp * pairs, group_rows, sl), lambda q: (layer, 0, q, 0, 0))
    x2d = u_cm.reshape(rows, t * width)
    y = pl.pallas_call(
        body,
        grid=(nblk,),
        in_specs=[pl.BlockSpec((rows, LANES), functools.partial(lambda q, s: (0, s * nblk + q), s=s))
                  for s in range(t)]
        + [per_group(hh), per_group(hh), per_group(t), per_group(t), per_blk((4, sl)), per_blk((4, sl)),
           per_blk((1, k))],
        out_specs=pl.BlockSpec((rows * t, LANES), lambda q: (0, q), pipeline_mode=pl.Buffered(1)),
        out_shape=jax.ShapeDtypeStruct((rows * t, width), F32),
        scratch_shapes=[pltpu.VMEM((rows, pairs * k), BF16), pltpu.VMEM((rows, pairs * k), F32)],
        compiler_params=pltpu.CompilerParams(dimension_semantics=("parallel",),
                                             vmem_limit_bytes=VMEM_LIMIT_BYTES),
    )(*([x2d] * t), bb, cc, pin, pout, a, sc, dvec)
    return y.reshape(chunks, batch, t, width)


def _dft_tables(blk):
    n = 2 * blk
    f = np.arange(blk, dtype=np.int64)
    sym = ((2 * f[:, None] + 1) * (2 * f[None, :] + 1)) % (4 * n)
    ang_sym = sym.astype(np.float64) * (2.0 * np.pi / (4 * n))
    lag = np.arange(2 * blk, dtype=np.int64) - blk
    ang_lag = (((2 * f[:, None] + 1) * lag[None, :]) % (2 * n)).astype(np.float64) * (2.0 * np.pi / (2 * n))
    tc, ts = np.cos(ang_lag), np.sin(ang_lag)
    tc[:, 0] = 0.0
    ts[:, 0] = 0.0
    as_bf = lambda x: jnp.asarray(x.astype(np.float32)).astype(BF16)
    return as_bf(np.cos(ang_sym)), as_bf(np.sin(ang_sym)), as_bf(tc), as_bf(ts)


def _position_features(seq, pad_to):
    t = np.linspace(0.0, 1.0, seq)[:, None]
    pos = np.arange(seq, dtype=np.float64)[:, None]
    bands = np.linspace(1e-4, HY_BANDS - 1, HY_BANDS)[None, :]
    ang = bands * pos * (2.0 * math.pi / seq)
    feats = np.zeros((seq, pad_to), np.float64)
    feats[:, :1 + 2 * HY_BANDS] = np.concatenate([t, np.cos(ang), -np.sin(ang)], axis=-1)
    mirror = lambda x: np.roll(x[::-1], 1, axis=0)
    f32 = lambda x: jnp.asarray(x.astype(np.float32))
    return f32(feats), f32(t), f32(mirror(feats)), f32(mirror(t))


_HY_COMBO_TERMS = (
    ((0, 1),), ((-1, 1), (0, -1)), ((1, 1), (0, -1)),
    ((-2, 1), (0, -1)), ((-3, 1), (-1, -1), (-2, -1), (0, 1)), ((-1, 1), (1, -1), (-2, -1), (0, 1)),
    ((2, 1), (0, -1)), ((1, 1), (-1, -1), (2, -1), (0, 1)), ((3, 1), (1, -1), (2, -1), (0, 1)),
)


def _hy_filter_body(feats_ref, t_ref, featsm_ref, tm_ref, w1_ref, b1_ref, fr_ref, w2_ref, b2_ref, w3f_ref,
                    w3b_ref, decf_ref, decb_ref, tc_ref, ts_ref, gr_ref, gi_ref, h2_scr, h2m_scr, k2_scr,
                    kr_scr, ki_scr, *, seq, nb):
    hi = lax.Precision.HIGHEST
    blk = seq // nb

    @pl.when(pl.program_id(0) == 0)
    def _():
        fr = fr_ref[...]

        def mlp(feats):
            h1 = jnp.sin(fr * (jnp.dot(feats, w1_ref[...], precision=hi, preferred_element_type=F32) + b1_ref[...]))
            return jnp.sin(fr * (jnp.dot(h1, w2_ref[...], precision=hi, preferred_element_type=F32) + b2_ref[...]))

        def split(h):
            top = h.astype(BF16).astype(F32)
            return jnp.concatenate([top, h - top, top, h - top], axis=1).astype(BF16)

        h2_scr[...] = split(mlp(feats_ref[...]))
        h2m_scr[...] = split(mlp(featsm_ref[...]))

    def split_rows(w):
        top = w.astype(BF16).astype(F32)
        return jnp.concatenate([top, top, w - top, w - top], axis=0).astype(BF16)

    hf = jnp.dot(h2_scr[...], split_rows(w3f_ref[...]), preferred_element_type=F32)
    hf = hf * (jnp.exp(-t_ref[...] * jnp.abs(decf_ref[...])) + HY_SHIFT)
    hb = jnp.dot(h2m_scr[...], split_rows(w3b_ref[...]), preferred_element_type=F32)
    hb = hb * (jnp.exp(-tm_ref[...] * jnp.abs(decb_ref[...])) + HY_SHIFT)
    row = lax.broadcasted_iota(jnp.int32, hb.shape, 0)
    hb = jnp.where(row == 0, 0.0, hb)
    norm = jnp.sum(hf * hf, axis=0, keepdims=True) + jnp.sum(hb * hb, axis=0, keepdims=True)
    scale = lax.rsqrt(norm + HY_EPS) * (1.0 / blk)
    k2_scr[0:seq, :] = hb.astype(BF16)
    k2_scr[seq:2 * seq, :] = hf.astype(BF16)
    for d in range(2 * nb - 1):
        seg = k2_scr[d * blk:(d + 2) * blk, :]
        kr_scr[d] = jnp.dot(tc_ref[...], seg, preferred_element_type=F32) * scale
        ki_scr[d] = jnp.dot(ts_ref[...], seg, preferred_element_type=F32) * scale
    for idx, terms in enumerate(_HY_COMBO_TERMS):
        for src, dst in ((kr_scr, gr_ref), (ki_scr, gi_ref)):
            acc = None
            for lag, weight in terms:
                term = src[lag + nb - 1]
                acc = (term if weight > 0 else -term) if acc is None else (acc + term if weight > 0 else acc - term)
            dst[0, idx] = acc


def _hy_filters(seq, w1, b1, freq, w2, b2, w3, decay, width, tables, nb, *, col_block=256):
    _, _, tc, ts = tables
    blk = seq // nb
    ffn = w2.shape[0]
    feats, t, featsm, tm = _position_features(seq, LANES)
    w1p = jnp.zeros((LANES, ffn), F32).at[:w1.shape[0]].set(w1)
    nblk = width // col_block
    side = lambda rows, direction: pl.BlockSpec(
        (rows, col_block), lambda i: (0, (2 * (i // nblk) + direction) * nblk + i % nblk))
    out = pl.BlockSpec((1, HY_COMBOS, blk, col_block), lambda i: (i // nblk, 0, 0, i % nblk))
    body = functools.partial(_hy_filter_body, seq=seq, nb=nb)
    row2 = lambda v: v.reshape(1, -1)
    return pl.pallas_call(
        body,
        grid=(HY_ORDER * nblk,),
        in_specs=[_resident(feats.shape), _resident(t.shape), _resident(feats.shape), _resident(t.shape),
                  _resident(w1p.shape), _resident((1, ffn)), _resident((1, ffn)), _resident(w2.shape),
                  _resident((1, ffn)), side(ffn, 0), side(ffn, 1), side(1, 0), side(1, 1),
                  _resident(tc.shape), _resident(ts.shape)],
        out_specs=[out, out],
        out_shape=[jax.ShapeDtypeStruct((HY_ORDER, HY_COMBOS, blk, width), F32)] * 2,
        scratch_shapes=[pltpu.VMEM((seq, 4 * ffn), BF16), pltpu.VMEM((seq, 4 * ffn), BF16),
                        pltpu.VMEM((2 * seq, col_block), BF16),
                        pltpu.VMEM((2 * nb - 1, blk, col_block), F32), pltpu.VMEM((2 * nb - 1, blk, col_block), F32)],
        compiler_params=pltpu.CompilerParams(dimension_semantics=("arbitrary",),
                                             vmem_limit_bytes=VMEM_LIMIT_BYTES),
    )(feats, t, featsm, tm, w1p, row2(b1), row2(freq), w2, row2(b2), w3, w3, row2(decay), row2(decay), tc, ts)


def _hy_conv_body(pv_ref, p1_ref, p2_ref, g_ref, cw_ref, cb_ref, gr_ref, gi_ref, d_ref, mc_ref, ms_ref, o_ref,
                  z_scr, a_scr, b_scr, xs_scr, *, seq, nb, cw, row_chunk, problems):
    blk = seq // nb
    gate_refs = (p1_ref, p2_ref)
    lanes = lambda j: slice(j * cw, (j + 1) * cw)
    cadd = lambda x, y: (x[0] + y[0], x[1] + y[1])
    cmul = lambda x, y: (x[0] * y[0] - x[1] * y[1], x[0] * y[1] + x[1] * y[0])

    def short_conv(k, p_ref, part):
        x = p_ref[k].astype(F32)
        pad = xs_scr.at[k, part]
        pad[0:SUBLANES, :] = jnp.zeros((SUBLANES, cw), F32)
        pad[SUBLANES + seq:2 * SUBLANES + seq, :] = jnp.zeros((SUBLANES, cw), F32)
        pad[SUBLANES:SUBLANES + seq, :] = x
        prev, nxt = pad[pl.ds(SUBLANES - 1, seq), :], pad[pl.ds(SUBLANES + 1, seq), :]
        w = cw_ref[:, part, :]
        return cb_ref[part:part + 1, :] + prev * w[0:1] + x * w[1:2] + nxt * w[2:3]

    def load_v(k):
        v = short_conv(k, pv_ref, 0)
        for j in range(nb):
            z_scr[k, :, lanes(j)] = v[j * blk:(j + 1) * blk, :]

    def forward(k, o):
        zb = z_scr[k].astype(BF16)
        a_scr[k] = jnp.dot(mc_ref[...], zb, preferred_element_type=F32)
        b_scr[k] = jnp.dot(ms_ref[...], zb, preferred_element_type=F32)

    def pointwise(k, o):
        def toeplitz2(base, x0, x1, r):
            g = lambda idx: (gr_ref[o, idx, r, :], gi_ref[o, idx, r, :])
            q = cmul(g(base), cadd(x0, x1))
            return cadd(q, cmul(g(base + 1), x1)), cadd(q, cmul(g(base + 2), x0))

        for c in range(blk // row_chunk):
            r = slice(c * row_chunk, (c + 1) * row_chunk)
            z = [(a_scr[k, r, lanes(j)], b_scr[k, r, lanes(j)]) for j in range(nb)]
            p1 = toeplitz2(0, cadd(z[0], z[2]), cadd(z[1], z[3]), r)
            p2 = toeplitz2(3, z[2], z[3], r)
            p3 = toeplitz2(6, z[0], z[1], r)
            for i, y in enumerate((cadd(p1[0], p2[0]), cadd(p1[1], p2[1]), cadd(p1[0], p3[0]), cadd(p1[1], p3[1]))):
                a_scr[k, r, lanes(i)], b_scr[k, r, lanes(i)] = y

    def inverse(k, o):
        conv = (jnp.dot(mc_ref[...], a_scr[k].astype(BF16), preferred_element_type=F32)
                + jnp.dot(ms_ref[...], b_scr[k].astype(BF16), preferred_element_type=F32))
        gate = short_conv(k, gate_refs[o], o + 1)
        dsk = d_ref[o:o + 1, :]
        for j in range(nb):
            rows = slice(j * blk, (j + 1) * blk)
            z = gate[rows, :] * (conv[:, lanes(j)] + dsk * z_scr[k, :, lanes(j)])
            if o == HY_ORDER - 1:
                o_ref[k, rows, :] = (z * g_ref[k, rows, :].astype(F32)).astype(o_ref.dtype)
            else:
                z_scr[k, :, lanes(j)] = z

    stages = [load_v]
    for o in range(HY_ORDER):
        stages += [functools.partial(forward, o=o), functools.partial(pointwise, o=o),
                   functools.partial(inverse, o=o)]
    for s in range(len(stages) + problems - 1):
        for k in range(problems):
            if 0 <= s - k < len(stages):
                stages[s - k](k)


def _hy_conv(p_hy, gate_silu, conv_w, conv_b, gr, gi, d_skip, tables, batch, seq, width, nb, *, col_block=LANES,
             row_chunk=SUBLANES, problems=2):
    assert nb == HY_TIME_BLOCKS
    mc, ms, _, _ = tables
    blk = seq // nb
    nblk = width // col_block
    part = lambda k: pl.BlockSpec((problems, seq, col_block), lambda c, b: (b, 0, k * nblk + c))
    cw = conv_w.reshape(conv_w.shape[0], HY_ORDER + 1, width)
    cb = conv_b.reshape(HY_ORDER + 1, width)
    filt = pl.BlockSpec((HY_ORDER, HY_COMBOS, blk, col_block), lambda c, b: (0, 0, 0, c),
                        pipeline_mode=pl.Buffered(1))
    body = functools.partial(_hy_conv_body, seq=seq, nb=nb, cw=col_block, row_chunk=row_chunk, problems=problems)
    wide = nb * col_block
    scratch = lambda dtype: pltpu.VMEM((problems, blk, wide), dtype)
    return pl.pallas_call(
        body,
        grid=(nblk, batch // problems),
        in_specs=[part(0), part(1), part(2),
                  pl.BlockSpec((problems, seq, col_block), lambda c, b: (b, 0, c)),
                  pl.BlockSpec((cw.shape[0], HY_ORDER + 1, col_block), lambda c, b: (0, 0, c)),
                  pl.BlockSpec((HY_ORDER + 1, col_block), lambda c, b: (0, c)),
                  filt, filt,
                  pl.BlockSpec((HY_ORDER, col_block), lambda c, b: (0, c)),
                  _resident(mc.shape), _resident(ms.shape)],
        out_specs=pl.BlockSpec((problems, seq, col_block), lambda c, b: (b, 0, c)),
        out_shape=jax.ShapeDtypeStruct((batch, seq, width), BF16),
        scratch_shapes=[scratch(F32), scratch(F32), scratch(F32),
                        pltpu.VMEM((problems, HY_ORDER + 1, seq + 2 * SUBLANES, col_block), F32)],
        compiler_params=pltpu.CompilerParams(dimension_semantics=("parallel", "parallel"),
                                             vmem_limit_bytes=VMEM_LIMIT_BYTES),
    )(p_hy, p_hy, p_hy, gate_silu, cw, cb, gr, gi, d_skip, mc, ms)


def _merge_body(ys_ref, gs_ref, yh_ref, m_ref, h_ref, wg_ref, bg_ref, wbs_ref, wbh_ref, wo_ref, fw_ref, o_ref,
                *, d_model, final_norm):
    ys = ys_ref[...]
    y = _gelu_tanh(ys.reshape(ys.shape[0] * ys.shape[1], ys.shape[2]))
    glu = y * _sigmoid(jnp.dot(y.astype(BF16), wg_ref[...], preferred_element_type=F32) + bg_ref[...])
    s5 = (glu * gs_ref[...].astype(F32)).astype(BF16)
    y_s5 = jnp.dot(s5, wbs_ref[...], preferred_element_type=F32)
    y_hy = jnp.dot(yh_ref[...], wbh_ref[...], preferred_element_type=F32)
    merged = m_ref[:, :d_model].astype(F32) * y_s5 + m_ref[:, d_model:].astype(F32) * y_hy
    h = h_ref[...] + jnp.dot(merged.astype(BF16), wo_ref[...], preferred_element_type=F32)
    if final_norm:
        ms = jnp.mean(h * h, axis=-1, keepdims=True)
        h = h * lax.rsqrt(ms + RMS_EPS) * fw_ref[...]
    o_ref[...] = h


def _merge(y_s5_cm, gs, y_hypre, m, h2d, w_glu, b_glu, w_bs, w_bh, w_out, layer, final_w, final_norm, seq, *,
           row_tile=1024):
    rows, d = h2d.shape
    sw = y_s5_cm.shape[-1]
    hw = y_hypre.shape[1]
    tile = lambda w: pl.BlockSpec((row_tile, w), lambda i: (i, 0))
    body = functools.partial(_merge_body, d_model=d, final_norm=final_norm)
    return pl.pallas_call(
        body,
        grid=(rows // row_tile,),
        in_specs=[_chunk_major_spec(row_tile, seq, sw), tile(sw), tile(hw), tile(2 * d), tile(d),
                  _resident_layer(w_glu.shape, layer), _resident_layer((b_glu.shape[0], 1, sw), layer),
                  _resident_layer(w_bs.shape, layer), _resident_layer(w_bh.shape, layer),
                  _resident_layer(w_out.shape, layer), _resident((1, d))],
        out_specs=tile(d),
        out_shape=jax.ShapeDtypeStruct((rows, d), F32),
        compiler_params=pltpu.CompilerParams(dimension_semantics=("parallel",),
                                             vmem_limit_bytes=VMEM_LIMIT_BYTES),
    )(y_s5_cm, gs, y_hypre, m, h2d, w_glu, b_glu.reshape(-1, 1, sw), w_bs, w_bh, w_out, final_w.reshape(1, d))


def kernel(x, norm_w, w_in, s5_lam_re, s5_lam_im, s5_log_step, s5_b_re, s5_b_im, s5_c_re, s5_c_im, s5_d, s5_w_glu, s5_b_glu, hy_conv_w, hy_conv_b, hy_w1, hy_b1, hy_freq, hy_w2, hy_b2, hy_w3, hy_decay, hy_d, w_branch_s5, w_branch_hy, w_out, final_norm_w):
    batch, seq, d_model = x.shape
    depth = w_in.shape[0]
    s5_width = s5_d.shape[1]
    hy_width = hy_d.shape[2]
    widths = (s5_width, s5_width, (HY_ORDER + 1) * hy_width, hy_width, 2 * d_model)
    nb = HY_TIME_BLOCKS
    tables = _dft_tables(seq // nb)
    h = x.reshape(batch * seq, d_model)
    w_in_bf, w_glu_bf, w_bs_bf, w_bh_bf, w_out_bf = (w.astype(BF16) for w in (w_in, s5_w_glu, w_branch_s5,
                                                                              w_branch_hy, w_out))
    s5_ops = jax.vmap(_s5_operands)(s5_lam_re, s5_lam_im, s5_log_step, s5_b_re, s5_b_im, s5_c_re, s5_c_im, s5_d)
    for l in range(depth):
        u, gs, p_hy, gh, m = _inproj(h, norm_w, w_in_bf, l, widths, batch, seq)
        y_s5 = _s5_scan(u, s5_ops, l, batch, seq)
        gr, gi = _hy_filters(seq, hy_w1[l], hy_b1[l], hy_freq[l], hy_w2[l], hy_b2[l], hy_w3[l], hy_decay[l],
                             hy_width, tables, nb)
        y_hypre = _hy_conv(p_hy.reshape(batch, seq, -1), gh.reshape(batch, seq, -1), hy_conv_w[l], hy_conv_b[l],
                           gr, gi, hy_d[l], tables, batch, seq, hy_width, nb)
        h = _merge(y_s5, gs, y_hypre.reshape(batch * seq, hy_width), m, h, w_glu_bf, s5_b_glu, w_bs_bf, w_bh_bf,
                   w_out_bf, l, final_norm_w, l == depth - 1, seq)
    return h.reshape(batch, seq, d_model)
```

```python
import functools
import math

import numpy as np
import jax
import jax.numpy as jnp
from jax import lax
from jax.experimental import pallas as pl
from jax.experimental.pallas import tpu as pltpu

F32 = jnp.float32
BF16 = jnp.bfloat16

RMS_EPS = 1e-6
S5_GROUP = 16
S5_CHUNK = 16
S5_GROUPS_PER_PAIR = 2
LANES = 128
SUBLANES = 8
HY_ORDER = 2
HY_BANDS = 16
HY_SHIFT = 0.05
HY_EPS = 1e-6
HY_TIME_BLOCKS = 4
HY_LAGS = 2 * HY_TIME_BLOCKS - 1
HY_COMBOS = 9
VMEM_LIMIT_BYTES = 56 * 1024 * 1024

_NT = (((1,), (1,)), ((), ()))


def _sigmoid(x):
    return 1.0 / (1.0 + jnp.exp2(x * (-1.0 / math.log(2.0))))


def _silu(x):
    return x * _sigmoid(x)


def _gelu_tanh(x):
    return 0.5 * x * (1.0 + jnp.tanh(math.sqrt(2.0 / math.pi) * (x + 0.044715 * (x * x * x))))


def _resident(shape):
    zeros = (0,) * len(shape)
    return pl.BlockSpec(shape, lambda *_: zeros, pipeline_mode=pl.Buffered(1))


def _resident_layer(stacked_shape, layer):
    zeros = (0,) * (len(stacked_shape) - 1)
    return pl.BlockSpec((None,) + tuple(stacked_shape[1:]), lambda *_: (layer,) + zeros,
                        pipeline_mode=pl.Buffered(1))


def _chunk_major_spec(row_tile, seq, width):
    tiles_per_seq = seq // row_tile
    return pl.BlockSpec((row_tile // S5_CHUNK, None, S5_CHUNK, width),
                        lambda i: (i % tiles_per_seq, i // tiles_per_seq, 0, 0))


def _inproj_body(x_ref, nw_ref, w_ref, u_ref, gs_ref, p_ref, gh_ref, m_ref, *, bounds, col_chunk):
    x = x_ref[...]
    ms = jnp.mean(x * x, axis=-1, keepdims=True)
    xb = (x * lax.rsqrt(ms + RMS_EPS) * nw_ref[...]).astype(BF16)
    outs = ((u_ref, None), (gs_ref, _silu), (p_ref, None), (gh_ref, _silu), (m_ref, _sigmoid))
    for (out_ref, act), lo, hi in zip(outs, bounds[:-1], bounds[1:]):
        for c0 in range(lo, hi, col_chunk):
            y = jnp.dot(xb, w_ref[:, c0:c0 + col_chunk], preferred_element_type=F32)
            if act is not None:
                y = act(y)
            if out_ref is u_ref:
                out_ref[:, :, c0 - lo:c0 - lo + col_chunk] = y.astype(out_ref.dtype).reshape(
                    y.shape[0] // S5_CHUNK, S5_CHUNK, col_chunk)
            else:
                out_ref[:, c0 - lo:c0 - lo + col_chunk] = y.astype(out_ref.dtype)


def _inproj(h2d, norm_w, w_in_bf, layer, widths, batch, seq, *, row_tile=256, col_chunk=512):
    rows, d = h2d.shape
    bounds = [0]
    for w in widths:
        bounds.append(bounds[-1] + w)
    body = functools.partial(_inproj_body, bounds=tuple(bounds), col_chunk=col_chunk)
    tile = lambda w: pl.BlockSpec((row_tile, w), lambda i: (i, 0))
    return pl.pallas_call(
        body,
        grid=(rows // row_tile,),
        in_specs=[tile(d), _resident_layer((norm_w.shape[0], 1, d), layer), _resident_layer(w_in_bf.shape, layer)],
        out_specs=[_chunk_major_spec(row_tile, seq, widths[0])] + [tile(w) for w in widths[1:]],
        out_shape=[jax.ShapeDtypeStruct((seq // S5_CHUNK, batch, S5_CHUNK, widths[0]), BF16)]
        + [jax.ShapeDtypeStruct((rows, w), BF16) for w in widths[1:]],
        compiler_params=pltpu.CompilerParams(dimension_semantics=("parallel",),
                                             vmem_limit_bytes=VMEM_LIMIT_BYTES),
    )(h2d, norm_w.reshape(-1, 1, d), w_in_bf)


def _s5_body(*refs, batch, chunks, pairs, pw, sl):
    t_steps = S5_CHUNK
    x_refs = refs[:t_steps]
    bb_ref, cc_ref, pin_ref, pout_ref, a_ref, sc_ref, dv_ref, y_ref, d_scr, st_scr = refs[t_steps:]
    k = t_steps * pw
    rows = batch * chunks
    assert k == 4 * sl

    assert pairs == 4 and pairs * pw == LANES and t_steps % pairs == 0

    def piece_transpose(v):
        lane = lax.broadcasted_iota(jnp.int32, v[0].shape, 1)
        upper, odd = lane >= 2 * pw, (lane // pw) % 2 == 1
        t0 = jnp.where(upper, pltpu.roll(v[2], 2 * pw, 1), v[0])
        t2 = jnp.where(upper, v[2], pltpu.roll(v[0], 2 * pw, 1))
        t1 = jnp.where(upper, pltpu.roll(v[3], 2 * pw, 1), v[1])
        t3 = jnp.where(upper, v[3], pltpu.roll(v[1], 2 * pw, 1))
        return (jnp.where(odd, pltpu.roll(t1, pw, 1), t0), jnp.where(odd, t1, pltpu.roll(t0, LANES - pw, 1)),
                jnp.where(odd, pltpu.roll(t3, pw, 1), t2), jnp.where(odd, t3, pltpu.roll(t2, LANES - pw, 1)))

    def pair_operand(fac_ref, pow_ref, direction, j):
        re, im = [], []
        for t in range(t_steps):
            for a in range(S5_GROUPS_PER_PAIR):
                g = S5_GROUPS_PER_PAIR * j + a
                fr, fi = fac_ref[2 * direction, g], fac_ref[2 * direction + 1, g]
                pr, pi = pow_ref[2 * direction, g, t:t + 1, :], pow_ref[2 * direction + 1, g, t:t + 1, :]
                re.append(pr * fr - pi * fi)
                im.append(pr * fi + pi * fr)
        return jnp.concatenate(re, axis=0), jnp.concatenate(im, axis=0)

    for j in range(pairs):
        d_scr[:, j * k:(j + 1) * k] = jnp.concatenate(
            [x_refs[t][:, j * pw:(j + 1) * pw] for t in range(t_steps)], axis=1)
        win = jnp.concatenate(pair_operand(bb_ref, pin_ref, 0, j) + pair_operand(bb_ref, pin_ref, 1, j),
                              axis=1).astype(BF16)
        st_scr[:, j * k:(j + 1) * k] = jnp.dot(d_scr[:, j * k:(j + 1) * k], win, preferred_element_type=F32)

    coef = [[jnp.broadcast_to(a_ref[j, c:c + 1, :], (batch, sl)) for c in range(4)] for j in range(pairs)]

    per_iter = 2
    assert chunks % per_iter == 0

    def step(i, carry):
        state = list(carry)
        rows_of, loaded = [], []
        for u in range(per_iter):
            c = i * per_iter + u
            rows_of.append((pl.ds(pl.multiple_of(c * batch, batch), batch),
                            pl.ds(pl.multiple_of((chunks - 1 - c) * batch, batch), batch)))
            loaded.append([st_scr[rows_of[u][col // 2 % 2], col * sl:(col + 1) * sl] for col in range(4 * pairs)])
        stores = []
        for u in range(per_iter):
            for j in range(pairs):
                for direction in range(2):
                    re, im = 4 * j + 2 * direction, 4 * j + 2 * direction + 1
                    sr, si = state[re], state[im]
                    ar, ai = coef[j][2 * direction], coef[j][2 * direction + 1]
                    stores += [(u, re, sr), (u, im, si)]
                    state[re] = ar * sr - ai * si + loaded[u][re]
                    state[im] = ar * si + ai * sr + loaded[u][im]
        for u, col, value in stores:
            st_scr[rows_of[u][col // 2 % 2], col * sl:(col + 1) * sl] = value
        return tuple(state)

    zero = jnp.zeros((batch, sl), F32)
    lax.fori_loop(0, chunks // per_iter, step, (zero,) * (4 * pairs))

    in_step = lax.broadcasted_iota(jnp.int32, (k, k), 0) // pw
    out_step = lax.broadcasted_iota(jnp.int32, (k, k), 1) // pw
    for j in range(pairs):
        sc = sc_ref[j]
        kern = None
        woutt = []
        for direction in range(2):
            wr, wi = pair_operand(bb_ref, pin_ref, direction, j)
            out_re, out_im = pair_operand(cc_ref, pout_ref, direction, j)
            woutt += [out_re.astype(BF16), (-out_im).astype(BF16)]
            pr, pi = sc[2 * direction:2 * direction + 1, :], sc[2 * direction + 1:2 * direction + 2, :]
            wp = jnp.concatenate([wr * pr - wi * pi, wr * pi + wi * pr], axis=1).astype(BF16)
            wt = jnp.concatenate(woutt[2 * direction:2 * direction + 2], axis=1)
            kd = lax.dot_general(wp, wt, _NT, preferred_element_type=F32)
            kd = jnp.where(out_step >= in_step if direction == 0 else out_step <= in_step, kd, 0.0)
            kern = kd if kern is None else kern + kd
        dj = d_scr[:, j * k:(j + 1) * k]
        wt = jnp.concatenate(woutt, axis=1)
        y = jnp.dot(dj, kern.astype(BF16), preferred_element_type=F32)
        y = y + lax.dot_general(st_scr[:, j * k:(j + 1) * k].astype(BF16), wt, _NT, preferred_element_type=F32)
        y = y + dj.astype(F32) * dv_ref[j]
        st_scr[:, j * k:(j + 1) * k] = y
    for m in range(t_steps // pairs):
        moved = piece_transpose([st_scr[:, j * k + m * LANES:j * k + (m + 1) * LANES] for j in range(pairs)])
        for i in range(pairs):
            y_ref[pl.ds(pairs * m + i, rows, stride=t_steps), :] = moved[i]


def _s5_operands(lam_re, lam_im, log_step, b_re, b_im, c_re, c_im, d_skip):
    _, groups, p = lam_re.shape
    hh, t, gp = S5_GROUP, S5_CHUNK, S5_GROUPS_PER_PAIR
    npair = groups // gp
    rep = lambda v: jnp.concatenate([v] * gp, axis=-1)
    own = (jnp.arange(gp * p)[None, :] // p == jnp.arange(groups)[:, None] % gp).astype(F32)
    lam_re, lam_im = rep(lam_re), rep(lam_im)
    dt = jnp.exp(log_step)[..., None]
    xr, xi = lam_re * dt, lam_im * dt
    lbr, lbi = jnp.exp(xr) * jnp.cos(xi), jnp.exp(xr) * jnp.sin(xi)
    nr, ni = lbr - 1.0, lbi
    den = lam_re * lam_re + lam_im * lam_im
    zr, zi = (nr * lam_re + ni * lam_im) / den * own, (ni * lam_re - nr * lam_im) / den * own
    btr, bti = rep(jnp.swapaxes(b_re, -1, -2)), rep(jnp.swapaxes(b_im, -1, -2))
    bbr = zr[:, :, None, :] * btr - zi[:, :, None, :] * bti
    bbi = zr[:, :, None, :] * bti + zi[:, :, None, :] * btr
    ccr, cci = rep(c_re) * own[None, :, None, :], rep(c_im) * own[None, :, None, :]

    def powers(e):
        kk = e[:, None, :, None]
        mag, ang = jnp.exp(kk * xr[:, :, None, :]), kk * xi[:, :, None, :]
        return mag * jnp.cos(ang), mag * jnp.sin(ang)

    components = lambda x_re, x_im: jnp.stack([x_re[0], x_im[0], x_re[1], x_im[1]])
    tt = jnp.arange(t, dtype=F32)
    pin = components(*powers(jnp.stack([t - 1 - tt, tt])))
    pout = components(*powers(jnp.stack([tt + 1, t - tt])))

    def pair_vec(e):
        mag, ang = jnp.exp(e * xr) * own, e * xi
        v = jnp.stack([(mag * jnp.cos(ang))[0], (mag * jnp.sin(ang))[0],
                       (mag * jnp.cos(ang))[1], (mag * jnp.sin(ang))[1]])
        return jnp.transpose(v.reshape(4, npair, gp, gp * p).sum(axis=2), (1, 0, 2))

    dvec = jnp.broadcast_to(d_skip.reshape(npair, 1, gp, hh), (npair, t, gp, hh)).reshape(npair, 1, t * gp * hh)
    return components(bbr, bbi), components(ccr, cci), pin, pout, pair_vec(float(t)), pair_vec(-float(t)), dvec


def _s5_scan(u_cm, operands, layer, batch, seq):
    bb, cc, pin, pout, a, sc, dvec = operands
    gp, t = S5_GROUPS_PER_PAIR, S5_CHUNK
    _, _, groups, hh, sl = bb.shape
    npair, pw = groups // gp, gp * hh
    k = t * pw
    pairs = LANES // pw
    chunks = seq // t
    rows = chunks * batch
    width = npair * pw
    nblk = width // LANES
    body = functools.partial(_s5_body, batch=batch, chunks=chunks, pairs=pairs, pw=pw, sl=sl)
    per_blk = lambda shape: pl.BlockSpec((None, pairs) + shape, lambda q: (layer, q, 0, 0))
    per_group = lambda group_rows: pl.BlockSpec((None, 4, gp * pairs, group_rows, sl), lambda q: (layer, 0, q, 0, 0))
    x2d = u_cm.reshape(rows, t * width)
    y = pl.pallas_call(
        body,
        grid=(nblk,),
        in_specs=[pl.BlockSpec((rows, LANES), functools.partial(lambda q, s: (0, s * nblk + q), s=s))
                  for s in range(t)]
        + [per_group(hh), per_group(hh), per_group(t), per_group(t), per_blk((4, sl)), per_blk((4, sl)),
           per_blk((1, k))],
        out_specs=pl.BlockSpec((rows * t, LANES), lambda q: (0, q)),
        out_shape=jax.ShapeDtypeStruct((rows * t, width), F32),
        scratch_shapes=[pltpu.VMEM((rows, pairs * k), BF16), pltpu.VMEM((rows, pairs * k), F32)],
        compiler_params=pltpu.CompilerParams(dimension_semantics=("parallel",),
                                             vmem_limit_bytes=VMEM_LIMIT_BYTES),
    )(*([x2d] * t), bb, cc, pin, pout, a, sc, dvec)
    return y.reshape(chunks, batch, t, width)


def _dft_tables(blk):
    n = 2 * blk
    f = np.arange(blk, dtype=np.int64)
    sym = ((2 * f[:, None] + 1) * (2 * f[None, :] + 1)) % (4 * n)
    ang_sym = sym.astype(np.float64) * (2.0 * np.pi / (4 * n))
    lag = np.arange(2 * blk, dtype=np.int64) - blk
    ang_lag = (((2 * f[:, None] + 1) * lag[None, :]) % (2 * n)).astype(np.float64) * (2.0 * np.pi / (2 * n))
    tc, ts = np.cos(ang_lag), np.sin(ang_lag)
    tc[:, 0] = 0.0
    ts[:, 0] = 0.0
    as_bf = lambda x: jnp.asarray(x.astype(np.float32)).astype(BF16)
    return as_bf(np.cos(ang_sym)), as_bf(np.sin(ang_sym)), as_bf(tc), as_bf(ts)


def _position_features(seq, pad_to):
    t = np.linspace(0.0, 1.0, seq)[:, None]
    pos = np.arange(seq, dtype=np.float64)[:, None]
    bands = np.linspace(1e-4, HY_BANDS - 1, HY_BANDS)[None, :]
    ang = bands * pos * (2.0 * math.pi / seq)
    feats = np.zeros((seq, pad_to), np.float64)
    feats[:, :1 + 2 * HY_BANDS] = np.concatenate([t, np.cos(ang), -np.sin(ang)], axis=-1)
    mirror = lambda x: np.roll(x[::-1], 1, axis=0)
    f32 = lambda x: jnp.asarray(x.astype(np.float32))
    return f32(feats), f32(t), f32(mirror(feats)), f32(mirror(t))


_HY_COMBO_TERMS = (
    ((0, 1),), ((-1, 1), (0, -1)), ((1, 1), (0, -1)),
    ((-2, 1), (0, -1)), ((-3, 1), (-1, -1), (-2, -1), (0, 1)), ((-1, 1), (1, -1), (-2, -1), (0, 1)),
    ((2, 1), (0, -1)), ((1, 1), (-1, -1), (2, -1), (0, 1)), ((3, 1), (1, -1), (2, -1), (0, 1)),
)


def _hy_filter_body(feats_ref, t_ref, featsm_ref, tm_ref, w1_ref, b1_ref, fr_ref, w2_ref, b2_ref, w3f_ref,
                    w3b_ref, decf_ref, decb_ref, tc_ref, ts_ref, gr_ref, gi_ref, h2_scr, h2m_scr, k2_scr,
                    kr_scr, ki_scr, *, seq, nb):
    hi = lax.Precision.HIGHEST
    blk = seq // nb

    @pl.when(pl.program_id(0) == 0)
    def _():
        fr = fr_ref[...]

        def mlp(feats):
            h1 = jnp.sin(fr * (jnp.dot(feats, w1_ref[...], precision=hi, preferred_element_type=F32) + b1_ref[...]))
            return jnp.sin(fr * (jnp.dot(h1, w2_ref[...], precision=hi, preferred_element_type=F32) + b2_ref[...]))

        def split(h):
            top = h.astype(BF16).astype(F32)
            return jnp.concatenate([top, h - top, top, h - top], axis=1).astype(BF16)

        h2_scr[...] = split(mlp(feats_ref[...]))
        h2m_scr[...] = split(mlp(featsm_ref[...]))

    def split_rows(w):
        top = w.astype(BF16).astype(F32)
        return jnp.concatenate([top, top, w - top, w - top], axis=0).astype(BF16)

    hf = jnp.dot(h2_scr[...], split_rows(w3f_ref[...]), preferred_element_type=F32)
    hf = hf * (jnp.exp(-t_ref[...] * jnp.abs(decf_ref[...])) + HY_SHIFT)
    hb = jnp.dot(h2m_scr[...], split_rows(w3b_ref[...]), preferred_element_type=F32)
    hb = hb * (jnp.exp(-tm_ref[...] * jnp.abs(decb_ref[...])) + HY_SHIFT)
    row = lax.broadcasted_iota(jnp.int32, hb.shape, 0)
    hb = jnp.where(row == 0, 0.0, hb)
    norm = jnp.sum(hf * hf, axis=0, keepdims=True) + jnp.sum(hb * hb, axis=0, keepdims=True)
    scale = lax.rsqrt(norm + HY_EPS) * (1.0 / blk)
    k2_scr[0:seq, :] = hb.astype(BF16)
    k2_scr[seq:2 * seq, :] = hf.astype(BF16)
    for d in range(2 * nb - 1):
        seg = k2_scr[d * blk:(d + 2) * blk, :]
        kr_scr[d] = jnp.dot(tc_ref[...], seg, preferred_element_type=F32) * scale
        ki_scr[d] = jnp.dot(ts_ref[...], seg, preferred_element_type=F32) * scale
    for idx, terms in enumerate(_HY_COMBO_TERMS):
        for src, dst in ((kr_scr, gr_ref), (ki_scr, gi_ref)):
            acc = None
            for lag, weight in terms:
                term = src[lag + nb - 1]
                acc = (term if weight > 0 else -term) if acc is None else (acc + term if weight > 0 else acc - term)
            dst[0, idx] = acc


def _hy_filters(seq, w1, b1, freq, w2, b2, w3, decay, width, tables, nb, *, col_block=256):
    _, _, tc, ts = tables
    blk = seq // nb
    ffn = w2.shape[0]
    feats, t, featsm, tm = _position_features(seq, LANES)
    w1p = jnp.zeros((LANES, ffn), F32).at[:w1.shape[0]].set(w1)
    nblk = width // col_block
    side = lambda rows, direction: pl.BlockSpec(
        (rows, col_block), lambda i: (0, (2 * (i // nblk) + direction) * nblk + i % nblk))
    out = pl.BlockSpec((1, HY_COMBOS, blk, col_block), lambda i: (i // nblk, 0, 0, i % nblk))
    body = functools.partial(_hy_filter_body, seq=seq, nb=nb)
    row2 = lambda v: v.reshape(1, -1)
    return pl.pallas_call(
        body,
        grid=(HY_ORDER * nblk,),
        in_specs=[_resident(feats.shape), _resident(t.shape), _resident(feats.shape), _resident(t.shape),
                  _resident(w1p.shape), _resident((1, ffn)), _resident((1, ffn)), _resident(w2.shape),
                  _resident((1, ffn)), side(ffn, 0), side(ffn, 1), side(1, 0), side(1, 1),
                  _resident(tc.shape), _resident(ts.shape)],
        out_specs=[out, out],
        out_shape=[jax.ShapeDtypeStruct((HY_ORDER, HY_COMBOS, blk, width), F32)] * 2,
        scratch_shapes=[pltpu.VMEM((seq, 4 * ffn), BF16), pltpu.VMEM((seq, 4 * ffn), BF16),
                        pltpu.VMEM((2 * seq, col_block), BF16),
                        pltpu.VMEM((2 * nb - 1, blk, col_block), F32), pltpu.VMEM((2 * nb - 1, blk, col_block), F32)],
        compiler_params=pltpu.CompilerParams(dimension_semantics=("arbitrary",),
                                             vmem_limit_bytes=VMEM_LIMIT_BYTES),
    )(feats, t, featsm, tm, w1p, row2(b1), row2(freq), w2, row2(b2), w3, w3, row2(decay), row2(decay), tc, ts)


def _hy_conv_body(pv_ref, p1_ref, p2_ref, g_ref, cw_ref, cb_ref, gr_ref, gi_ref, d_ref, mc_ref, ms_ref, o_ref,
                  z_scr, a_scr, b_scr, xs_scr, *, seq, nb, cw, row_chunk, problems):
    blk = seq // nb
    gate_refs = (p1_ref, p2_ref)
    lanes = lambda j: slice(j * cw, (j + 1) * cw)
    cadd = lambda x, y: (x[0] + y[0], x[1] + y[1])
    cmul = lambda x, y: (x[0] * y[0] - x[1] * y[1], x[0] * y[1] + x[1] * y[0])

    def short_conv(k, p_ref, part):
        x = p_ref[k].astype(F32)
        pad = xs_scr.at[k, part]
        pad[0:SUBLANES, :] = jnp.zeros((SUBLANES, cw), F32)
        pad[SUBLANES + seq:2 * SUBLANES + seq, :] = jnp.zeros((SUBLANES, cw), F32)
        pad[SUBLANES:SUBLANES + seq, :] = x
        prev, nxt = pad[pl.ds(SUBLANES - 1, seq), :], pad[pl.ds(SUBLANES + 1, seq), :]
        w = cw_ref[:, part, :]
        return cb_ref[part:part + 1, :] + prev * w[0:1] + x * w[1:2] + nxt * w[2:3]

    def load_v(k):
        v = short_conv(k, pv_ref, 0)
        for j in range(nb):
            z_scr[k, :, lanes(j)] = v[j * blk:(j + 1) * blk, :]

    def forward(k, o):
        zb = z_scr[k].astype(BF16)
        a_scr[k] = jnp.dot(mc_ref[...], zb, preferred_element_type=F32)
        b_scr[k] = jnp.dot(ms_ref[...], zb, preferred_element_type=F32)

    def pointwise(k, o):
        def toeplitz2(base, x0, x1, r):
            g = lambda idx: (gr_ref[o, idx, r, :], gi_ref[o, idx, r, :])
            q = cmul(g(base), cadd(x0, x1))
            return cadd(q, cmul(g(base + 1), x1)), cadd(q, cmul(g(base + 2), x0))

        for c in range(blk // row_chunk):
            r = slice(c * row_chunk, (c + 1) * row_chunk)
            z = [(a_scr[k, r, lanes(j)], b_scr[k, r, lanes(j)]) for j in range(nb)]
            p1 = toeplitz2(0, cadd(z[0], z[2]), cadd(z[1], z[3]), r)
            p2 = toeplitz2(3, z[2], z[3], r)
            p3 = toeplitz2(6, z[0], z[1], r)
            for i, y in enumerate((cadd(p1[0], p2[0]), cadd(p1[1], p2[1]), cadd(p1[0], p3[0]), cadd(p1[1], p3[1]))):
                a_scr[k, r, lanes(i)], b_scr[k, r, lanes(i)] = y

    def inverse(k, o):
        conv = (jnp.dot(mc_ref[...], a_scr[k].astype(BF16), preferred_element_type=F32)
                + jnp.dot(ms_ref[...], b_scr[k].astype(BF16), preferred_element_type=F32))
        gate = short_conv(k, gate_refs[o], o + 1)
        dsk = d_ref[o:o + 1, :]
        for j in range(nb):
            rows = slice(j * blk, (j + 1) * blk)
            z = gate[rows, :] * (conv[:, lanes(j)] + dsk * z_scr[k, :, lanes(j)])
            if o == HY_ORDER - 1:
                o_ref[k, rows, :] = (z * g_ref[k, rows, :].astype(F32)).astype(o_ref.dtype)
            else:
                z_scr[k, :, lanes(j)] = z

    stages = [load_v]
    for o in range(HY_ORDER):
        stages += [functools.partial(forward, o=o), functools.partial(pointwise, o=o),
                   functools.partial(inverse, o=o)]
    for s in range(len(stages) + problems - 1):
        for k in range(problems):
            if 0 <= s - k < len(stages):
                stages[s - k](k)


def _hy_conv(p_hy, gate_silu, conv_w, conv_b, gr, gi, d_skip, tables, batch, seq, width, nb, *, col_block=LANES,
             row_chunk=SUBLANES, problems=2):
    assert nb == HY_TIME_BLOCKS
    mc, ms, _, _ = tables
    blk = seq // nb
    nblk = width // col_block
    part = lambda k: pl.BlockSpec((problems, seq, col_block), lambda c, b: (b, 0, k * nblk + c))
    cw = conv_w.reshape(conv_w.shape[0], HY_ORDER + 1, width)
    cb = conv_b.reshape(HY_ORDER + 1, width)
    filt = pl.BlockSpec((HY_ORDER, HY_COMBOS, blk, col_block), lambda c, b: (0, 0, 0, c))
    body = functools.partial(_hy_conv_body, seq=seq, nb=nb, cw=col_block, row_chunk=row_chunk, problems=problems)
    wide = nb * col_block
    scratch = lambda dtype: pltpu.VMEM((problems, blk, wide), dtype)
    return pl.pallas_call(
        body,
        grid=(nblk, batch // problems),
        in_specs=[part(0), part(1), part(2),
                  pl.BlockSpec((problems, seq, col_block), lambda c, b: (b, 0, c)),
                  pl.BlockSpec((cw.shape[0], HY_ORDER + 1, col_block), lambda c, b: (0, 0, c)),
                  pl.BlockSpec((HY_ORDER + 1, col_block), lambda c, b: (0, c)),
                  filt, filt,
                  pl.BlockSpec((HY_ORDER, col_block), lambda c, b: (0, c)),
                  _resident(mc.shape), _resident(ms.shape)],
        out_specs=pl.BlockSpec((problems, seq, col_block), lambda c, b: (b, 0, c)),
        out_shape=jax.ShapeDtypeStruct((batch, seq, width), BF16),
        scratch_shapes=[scratch(F32), scratch(F32), scratch(F32),
                        pltpu.VMEM((problems, HY_ORDER + 1, seq + 2 * SUBLANES, col_block), F32)],
        compiler_params=pltpu.CompilerParams(dimension_semantics=("parallel", "parallel"),
                                             vmem_limit_bytes=VMEM_LIMIT_BYTES),
    )(p_hy, p_hy, p_hy, gate_silu, cw, cb, gr, gi, d_skip, mc, ms)


def _merge_body(ys_ref, gs_ref, yh_ref, m_ref, h_ref, wg_ref, bg_ref, wbs_ref, wbh_ref, wo_ref, fw_ref, o_ref,
                *, d_model, final_norm):
    ys = ys_ref[...]
    y = _gelu_tanh(ys.reshape(ys.shape[0] * ys.shape[1], ys.shape[2]))
    glu = y * _sigmoid(jnp.dot(y.astype(BF16), wg_ref[...], preferred_element_type=F32) + bg_ref[...])
    s5 = (glu * gs_ref[...].astype(F32)).astype(BF16)
    y_s5 = jnp.dot(s5, wbs_ref[...], preferred_element_type=F32)
    y_hy = jnp.dot(yh_ref[...], wbh_ref[...], preferred_element_type=F32)
    merged = m_ref[:, :d_model].astype(F32) * y_s5 + m_ref[:, d_model:].astype(F32) * y_hy
    h = h_ref[...] + jnp.dot(merged.astype(BF16), wo_ref[...], preferred_element_type=F32)
    if final_norm:
        ms = jnp.mean(h * h, axis=-1, keepdims=True)
        h = h * lax.rsqrt(ms + RMS_EPS) * fw_ref[...]
    o_ref[...] = h


def _merge(y_s5_cm, gs, y_hypre, m, h2d, w_glu, b_glu, w_bs, w_bh, w_out, layer, final_w, final_norm, seq, *,
           row_tile=1024):
    rows, d = h2d.shape
    sw = y_s5_cm.shape[-1]
    hw = y_hypre.shape[1]
    tile = lambda w: pl.BlockSpec((row_tile, w), lambda i: (i, 0))
    body = functools.partial(_merge_body, d_model=d, final_norm=final_norm)
    return pl.pallas_call(
        body,
        grid=(rows // row_tile,),
        in_specs=[_chunk_major_spec(row_tile, seq, sw), tile(sw), tile(hw), tile(2 * d), tile(d),
                  _resident_layer(w_glu.shape, layer), _resident_layer((b_glu.shape[0], 1, sw), layer),
                  _resident_layer(w_bs.shape, layer), _resident_layer(w_bh.shape, layer),
                  _resident_layer(w_out.shape, layer), _resident((1, d))],
        out_specs=tile(d),
        out_shape=jax.ShapeDtypeStruct((rows, d), F32),
        compiler_params=pltpu.CompilerParams(dimension_semantics=("parallel",),
                                             vmem_limit_bytes=VMEM_LIMIT_BYTES),
    )(y_s5_cm, gs, y_hypre, m, h2d, w_glu, b_glu.reshape(-1, 1, sw), w_bs, w_bh, w_out, final_w.reshape(1, d))


def kernel(x, norm_w, w_in, s5_lam_re, s5_lam_im, s5_log_step, s5_b_re, s5_b_im, s5_c_re, s5_c_im, s5_d, s5_w_glu, s5_b_glu, hy_conv_w, hy_conv_b, hy_w1, hy_b1, hy_freq, hy_w2, hy_b2, hy_w3, hy_decay, hy_d, w_branch_s5, w_branch_hy, w_out, final_norm_w):
    batch, seq, d_model = x.shape
    depth = w_in.shape[0]
    s5_width = s5_d.shape[1]
    hy_width = hy_d.shape[2]
    widths = (s5_width, s5_width, (HY_ORDER + 1) * hy_width, hy_width, 2 * d_model)
    nb = HY_TIME_BLOCKS
    tables = _dft_tables(seq // nb)
    h = x.reshape(batch * seq, d_model)
    w_in_bf, w_glu_bf, w_bs_bf, w_bh_bf, w_out_bf = (w.astype(BF16) for w in (w_in, s5_w_glu, w_branch_s5,
                                                                              w_branch_hy, w_out))
    s5_ops = jax.vmap(_s5_operands)(s5_lam_re, s5_lam_im, s5_log_step, s5_b_re, s5_b_im, s5_c_re, s5_c_im, s5_d)
    for l in range(depth):
        u, gs, p_hy, gh, m = _inproj(h, norm_w, w_in_bf, l, widths, batch, seq)
        y_s5 = _s5_scan(u, s5_ops, l, batch, seq)
        gr, gi = _hy_filters(seq, hy_w1[l], hy_b1[l], hy_freq[l], hy_w2[l], hy_b2[l], hy_w3[l], hy_decay[l],
                             hy_width, tables, nb)
        y_hypre = _hy_conv(p_hy.reshape(batch, seq, -1), gh.reshape(batch, seq, -1), hy_conv_w[l], hy_conv_b[l],
                           gr, gi, hy_d[l], tables, batch, seq, hy_width, nb)
        h = _merge(y_s5, gs, y_hypre.reshape(batch * seq, hy_width), m, h, w_glu_bf, s5_b_glu, w_bs_bf, w_bh_bf,
                   w_out_bf, l, final_norm_w, l == depth - 1, seq)
    return h.reshape(batch, seq, d_model)
```

```python
import functools
import math

import numpy as np
import jax
import jax.numpy as jnp
from jax import lax
from jax.experimental import pallas as pl
from jax.experimental.pallas import tpu as pltpu

F32 = jnp.float32
BF16 = jnp.bfloat16

RMS_EPS = 1e-6
S5_GROUP = 16
S5_CHUNK = 16
S5_GROUPS_PER_PAIR = 2
LANES = 128
SUBLANES = 8
NARROW = 512
HY_ORDER = 2
HY_BANDS = 16
HY_SHIFT = 0.05
HY_EPS = 1e-6
HY_TIME_BLOCKS = 4
HY_LAGS = 2 * HY_TIME_BLOCKS - 1
HY_COMBOS = 9
VMEM_LIMIT_BYTES = 56 * 1024 * 1024

_NT = (((1,), (1,)), ((), ()))


def _sigmoid(x):
    return 1.0 / (1.0 + jnp.exp2(x * (-1.0 / math.log(2.0))))


def _silu(x):
    return x * _sigmoid(x)


def _gelu_tanh(x):
    return 0.5 * x * (1.0 + jnp.tanh(math.sqrt(2.0 / math.pi) * (x + 0.044715 * (x * x * x))))


def _resident(shape):
    zeros = (0,) * len(shape)
    return pl.BlockSpec(shape, lambda *_: zeros, pipeline_mode=pl.Buffered(1))


def _resident_layer(stacked_shape, layer):
    zeros = (0,) * (len(stacked_shape) - 1)
    return pl.BlockSpec((None,) + tuple(stacked_shape[1:]), lambda *_: (layer,) + zeros,
                        pipeline_mode=pl.Buffered(1))


def _chunk_major_spec(row_tile, seq, width):
    tiles_per_seq = seq // row_tile
    return pl.BlockSpec((row_tile // S5_CHUNK, None, S5_CHUNK, width),
                        lambda i: (i % tiles_per_seq, i // tiles_per_seq, 0, 0))


def _inproj_body(x_ref, nw_ref, w_ref, u_ref, gs_ref, p_ref, gh_ref, m_ref, *, bounds, col_chunk):
    x = x_ref[...]
    ms = jnp.mean(x * x, axis=-1, keepdims=True)
    xb = (x * lax.rsqrt(ms + RMS_EPS) * nw_ref[...]).astype(BF16)
    outs = ((u_ref, None), (gs_ref, _silu), (p_ref, None), (gh_ref, _silu), (m_ref, _sigmoid))
    for (out_ref, act), lo, hi in zip(outs, bounds[:-1], bounds[1:]):
        for c0 in range(lo, hi, col_chunk):
            y = jnp.dot(xb, w_ref[c0 // col_chunk], preferred_element_type=F32)
            if act is not None:
                y = act(y)
            if out_ref is u_ref:
                out_ref[:, :, c0 - lo:c0 - lo + col_chunk] = y.astype(out_ref.dtype).reshape(
                    y.shape[0] // S5_CHUNK, S5_CHUNK, col_chunk)
            else:
                out_ref[:, c0 - lo:c0 - lo + col_chunk] = y.astype(out_ref.dtype)


def _inproj(h2d, norm_w, w_in_bf, layer, widths, batch, seq, *, row_tile=256, col_chunk=NARROW):
    rows, d = h2d.shape
    bounds = [0]
    for w in widths:
        bounds.append(bounds[-1] + w)
    body = functools.partial(_inproj_body, bounds=tuple(bounds), col_chunk=col_chunk)
    tile = lambda w: pl.BlockSpec((row_tile, w), lambda i: (i, 0))
    assert all(b % col_chunk == 0 for b in bounds) and w_in_bf.shape[2:] == (d, col_chunk)
    return pl.pallas_call(
        body,
        grid=(rows // row_tile,),
        in_specs=[tile(d), _resident_layer((norm_w.shape[0], 1, d), layer), _resident_layer(w_in_bf.shape, layer)],
        out_specs=[_chunk_major_spec(row_tile, seq, widths[0])] + [tile(w) for w in widths[1:]],
        out_shape=[jax.ShapeDtypeStruct((seq // S5_CHUNK, batch, S5_CHUNK, widths[0]), BF16)]
        + [jax.ShapeDtypeStruct((rows, w), BF16) for w in widths[1:]],
        compiler_params=pltpu.CompilerParams(dimension_semantics=("parallel",),
                                             vmem_limit_bytes=VMEM_LIMIT_BYTES),
    )(h2d, norm_w.reshape(-1, 1, d), w_in_bf)


def _s5_body(*refs, batch, chunks, pairs, pw, sl):
    t_steps = S5_CHUNK
    x_refs = refs[:t_steps]
    bb_ref, cc_ref, pin_ref, pout_ref, a_ref, sc_ref, dv_ref, y_ref, d_scr, st_scr = refs[t_steps:]
    k = t_steps * pw
    rows = batch * chunks
    assert k == 4 * sl

    assert pairs == 4 and pairs * pw == LANES and t_steps % pairs == 0

    def piece_transpose(v):
        lane = lax.broadcasted_iota(jnp.int32, v[0].shape, 1)
        upper, odd = lane >= 2 * pw, (lane // pw) % 2 == 1
        t0 = jnp.where(upper, pltpu.roll(v[2], 2 * pw, 1), v[0])
        t2 = jnp.where(upper, v[2], pltpu.roll(v[0], 2 * pw, 1))
        t1 = jnp.where(upper, pltpu.roll(v[3], 2 * pw, 1), v[1])
        t3 = jnp.where(upper, v[3], pltpu.roll(v[1], 2 * pw, 1))
        return (jnp.where(odd, pltpu.roll(t1, pw, 1), t0), jnp.where(odd, t1, pltpu.roll(t0, LANES - pw, 1)),
                jnp.where(odd, pltpu.roll(t3, pw, 1), t2), jnp.where(odd, t3, pltpu.roll(t2, LANES - pw, 1)))

    def pair_operand(fac_ref, pow_ref, direction, j):
        re, im = [], []
        for t in range(t_steps):
            for a in range(S5_GROUPS_PER_PAIR):
                g = S5_GROUPS_PER_PAIR * j + a
                fr, fi = fac_ref[2 * direction, g], fac_ref[2 * direction + 1, g]
                pr, pi = pow_ref[2 * direction, g, t:t + 1, :], pow_ref[2 * direction + 1, g, t:t + 1, :]
                re.append(pr * fr - pi * fi)
                im.append(pr * fi + pi * fr)
        return jnp.concatenate(re, axis=0), jnp.concatenate(im, axis=0)

    for j in range(pairs):
        d_scr[:, j * k:(j + 1) * k] = jnp.concatenate(
            [x_refs[t][:, j * pw:(j + 1) * pw] for t in range(t_steps)], axis=1)
        win = jnp.concatenate(pair_operand(bb_ref, pin_ref, 0, j) + pair_operand(bb_ref, pin_ref, 1, j),
                              axis=1).astype(BF16)
        st_scr[:, j * k:(j + 1) * k] = jnp.dot(d_scr[:, j * k:(j + 1) * k], win, preferred_element_type=F32)

    coef = [[jnp.broadcast_to(a_ref[j, c:c + 1, :], (batch, sl)) for c in range(4)] for j in range(pairs)]

    per_iter = 2
    assert chunks % per_iter == 0

    def step(i, carry):
        state = list(carry)
        rows_of, loaded = [], []
        for u in range(per_iter):
            c = i * per_iter + u
            rows_of.append((pl.ds(pl.multiple_of(c * batch, batch), batch),
                            pl.ds(pl.multiple_of((chunks - 1 - c) * batch, batch), batch)))
            loaded.append([st_scr[rows_of[u][col // 2 % 2], col * sl:(col + 1) * sl] for col in range(4 * pairs)])
        stores = []
        for u in range(per_iter):
            for j in range(pairs):
                for direction in range(2):
                    re, im = 4 * j + 2 * direction, 4 * j + 2 * direction + 1
                    sr, si = state[re], state[im]
                    ar, ai = coef[j][2 * direction], coef[j][2 * direction + 1]
                    stores += [(u, re, sr), (u, im, si)]
                    state[re] = ar * sr - ai * si + loaded[u][re]
                    state[im] = ar * si + ai * sr + loaded[u][im]
        for u, col, value in stores:
            st_scr[rows_of[u][col // 2 % 2], col * sl:(col + 1) * sl] = value
        return tuple(state)

    zero = jnp.zeros((batch, sl), F32)
    lax.fori_loop(0, chunks // per_iter, step, (zero,) * (4 * pairs))

    in_step = lax.broadcasted_iota(jnp.int32, (k, k), 0) // pw
    out_step = lax.broadcasted_iota(jnp.int32, (k, k), 1) // pw
    for j in range(pairs):
        sc = sc_ref[j]
        kern = None
        woutt = []
        for direction in range(2):
            wr, wi = pair_operand(bb_ref, pin_ref, direction, j)
            out_re, out_im = pair_operand(cc_ref, pout_ref, direction, j)
            woutt += [out_re.astype(BF16), (-out_im).astype(BF16)]
            pr, pi = sc[2 * direction:2 * direction + 1, :], sc[2 * direction + 1:2 * direction + 2, :]
            wp = jnp.concatenate([wr * pr - wi * pi, wr * pi + wi * pr], axis=1).astype(BF16)
            wt = jnp.concatenate(woutt[2 * direction:2 * direction + 2], axis=1)
            kd = lax.dot_general(wp, wt, _NT, preferred_element_type=F32)
            kd = jnp.where(out_step >= in_step if direction == 0 else out_step <= in_step, kd, 0.0)
            kern = kd if kern is None else kern + kd
        dj = d_scr[:, j * k:(j + 1) * k]
        wt = jnp.concatenate(woutt, axis=1)
        y = jnp.dot(dj, kern.astype(BF16), preferred_element_type=F32)
        y = y + lax.dot_general(st_scr[:, j * k:(j + 1) * k].astype(BF16), wt, _NT, preferred_element_type=F32)
        y = y + dj.astype(F32) * dv_ref[j]
        st_scr[:, j * k:(j + 1) * k] = y
    for m in range(t_steps // pairs):
        moved = piece_transpose([st_scr[:, j * k + m * LANES:j * k + (m + 1) * LANES] for j in range(pairs)])
        for i in range(pairs):
            y_ref[pl.ds(pairs * m + i, rows, stride=t_steps), :] = moved[i]


def _s5_operands(lam_re, lam_im, log_step, b_re, b_im, c_re, c_im, d_skip):
    _, groups, p = lam_re.shape
    hh, t, gp = S5_GROUP, S5_CHUNK, S5_GROUPS_PER_PAIR
    npair = groups // gp
    rep = lambda v: jnp.concatenate([v] * gp, axis=-1)
    own = (jnp.arange(gp * p)[None, :] // p == jnp.arange(groups)[:, None] % gp).astype(F32)
    lam_re, lam_im = rep(lam_re), rep(lam_im)
    dt = jnp.exp(log_step)[..., None]
    xr, xi = lam_re * dt, lam_im * dt
    lbr, lbi = jnp.exp(xr) * jnp.cos(xi), jnp.exp(xr) * jnp.sin(xi)
    nr, ni = lbr - 1.0, lbi
    den = lam_re * lam_re + lam_im * lam_im
    zr, zi = (nr * lam_re + ni * lam_im) / den * own, (ni * lam_re - nr * lam_im) / den * own
    btr, bti = rep(jnp.swapaxes(b_re, -1, -2)), rep(jnp.swapaxes(b_im, -1, -2))
    bbr = zr[:, :, None, :] * btr - zi[:, :, None, :] * bti
    bbi = zr[:, :, None, :] * bti + zi[:, :, None, :] * btr
    ccr, cci = rep(c_re) * own[None, :, None, :], rep(c_im) * own[None, :, None, :]

    def powers(e):
        kk = e[:, None, :, None]
        mag, ang = jnp.exp(kk * xr[:, :, None, :]), kk * xi[:, :, None, :]
        return mag * jnp.cos(ang), mag * jnp.sin(ang)

    components = lambda x_re, x_im: jnp.stack([x_re[0], x_im[0], x_re[1], x_im[1]])
    tt = jnp.arange(t, dtype=F32)
    pin = components(*powers(jnp.stack([t - 1 - tt, tt])))
    pout = components(*powers(jnp.stack([tt + 1, t - tt])))

    def pair_vec(e):
        mag, ang = jnp.exp(e * xr) * own, e * xi
        v = jnp.stack([(mag * jnp.cos(ang))[0], (mag * jnp.sin(ang))[0],
                       (mag * jnp.cos(ang))[1], (mag * jnp.sin(ang))[1]])
        return jnp.transpose(v.reshape(4, npair, gp, gp * p).sum(axis=2), (1, 0, 2))

    dvec = jnp.broadcast_to(d_skip.reshape(npair, 1, gp, hh), (npair, t, gp, hh)).reshape(npair, 1, t * gp * hh)
    return components(bbr, bbi), components(ccr, cci), pin, pout, pair_vec(float(t)), pair_vec(-float(t)), dvec


def _s5_scan(u_cm, operands, layer, batch, seq):
    bb, cc, pin, pout, a, sc, dvec = operands
    gp, t = S5_GROUPS_PER_PAIR, S5_CHUNK
    _, _, groups, hh, sl = bb.shape
    npair, pw = groups // gp, gp * hh
    k = t * pw
    pairs = LANES // pw
    chunks = seq // t
    rows = chunks * batch
    width = npair * pw
    nblk = width // LANES
    body = functools.partial(_s5_body, batch=batch, chunks=chunks, pairs=pairs, pw=pw, sl=sl)
    per_blk = lambda shape: pl.BlockSpec((None, pairs) + shape, lambda q: (layer, q, 0, 0))
    per_group = lambda group_rows: pl.BlockSpec((None, 4, gp * pairs, group_rows, sl), lambda q: (layer, 0, q, 0, 0))
    x2d = u_cm.reshape(rows, t * width)
    y = pl.pallas_call(
        body,
        grid=(nblk,),
        in_specs=[pl.BlockSpec((rows, LANES), functools.partial(lambda q, s: (0, s * nblk + q), s=s))
                  for s in range(t)]
        + [per_group(hh), per_group(hh), per_group(t), per_group(t), per_blk((4, sl)), per_blk((4, sl)),
           per_blk((1, k))],
        out_specs=pl.BlockSpec((rows * t, LANES), lambda q: (0, q)),
        out_shape=jax.ShapeDtypeStruct((rows * t, width), F32),
        scratch_shapes=[pltpu.VMEM((rows, pairs * k), BF16), pltpu.VMEM((rows, pairs * k), F32)],
        compiler_params=pltpu.CompilerParams(dimension_semantics=("parallel",),
                                             vmem_limit_bytes=VMEM_LIMIT_BYTES),
    )(*([x2d] * t), bb, cc, pin, pout, a, sc, dvec)
    return y.reshape(chunks, batch, t, width)


def _dft_tables(blk):
    n = 2 * blk
    f = np.arange(blk, dtype=np.int64)
    sym = ((2 * f[:, None] + 1) * (2 * f[None, :] + 1)) % (4 * n)
    ang_sym = sym.astype(np.float64) * (2.0 * np.pi / (4 * n))
    lag = np.arange(2 * blk, dtype=np.int64) - blk
    ang_lag = (((2 * f[:, None] + 1) * lag[None, :]) % (2 * n)).astype(np.float64) * (2.0 * np.pi / (2 * n))
    tc, ts = np.cos(ang_lag), np.sin(ang_lag)
    tc[:, 0] = 0.0
    ts[:, 0] = 0.0
    as_bf = lambda x: jnp.asarray(x.astype(np.float32)).astype(BF16)
    chunks = lambda x: np.transpose(x.reshape(blk, -1, min(NARROW, x.shape[1])), (1, 0, 2))
    return as_bf(np.cos(ang_sym)), as_bf(np.sin(ang_sym)), as_bf(chunks(tc)), as_bf(chunks(ts))


def _position_features(seq, pad_to):
    t = np.linspace(0.0, 1.0, seq)[:, None]
    pos = np.arange(seq, dtype=np.float64)[:, None]
    bands = np.linspace(1e-4, HY_BANDS - 1, HY_BANDS)[None, :]
    ang = bands * pos * (2.0 * math.pi / seq)
    feats = np.zeros((seq, pad_to), np.float64)
    feats[:, :1 + 2 * HY_BANDS] = np.concatenate([t, np.cos(ang), -np.sin(ang)], axis=-1)
    mirror = lambda x: np.roll(x[::-1], 1, axis=0)
    f32 = lambda x: jnp.asarray(x.astype(np.float32))
    return f32(feats), f32(t), f32(mirror(feats)), f32(mirror(t))


_HY_COMBO_TERMS = (
    ((0, 1),), ((-1, 1), (0, -1)), ((1, 1), (0, -1)),
    ((-2, 1), (0, -1)), ((-3, 1), (-1, -1), (-2, -1), (0, 1)), ((-1, 1), (1, -1), (-2, -1), (0, 1)),
    ((2, 1), (0, -1)), ((1, 1), (-1, -1), (2, -1), (0, 1)), ((3, 1), (1, -1), (2, -1), (0, 1)),
)


def _hy_filter_body(feats_ref, t_ref, featsm_ref, tm_ref, w1_ref, b1_ref, fr_ref, w2_ref, b2_ref, w3f_ref,
                    w3b_ref, decf_ref, decb_ref, tc_ref, ts_ref, gr_ref, gi_ref, h2_scr, h2m_scr, k2_scr,
                    kr_scr, ki_scr, *, seq, nb):
    hi = lax.Precision.HIGHEST
    blk = seq // nb

    @pl.when(pl.program_id(0) == 0)
    def _():
        fr = fr_ref[...]

        def mlp(feats):
            h1 = jnp.sin(fr * (jnp.dot(feats, w1_ref[...], precision=hi, preferred_element_type=F32) + b1_ref[...]))
            return jnp.sin(fr * (jnp.dot(h1, w2_ref[...], precision=hi, preferred_element_type=F32) + b2_ref[...]))

        def split(h):
            top = h.astype(BF16).astype(F32)
            return jnp.concatenate([top, h - top, top, h - top], axis=1).astype(BF16)

        h2_scr[...] = split(mlp(feats_ref[...]))
        h2m_scr[...] = split(mlp(featsm_ref[...]))

    def split_rows(w):
        top = w.astype(BF16).astype(F32)
        return jnp.concatenate([top, top, w - top, w - top], axis=0).astype(BF16)

    hf = jnp.dot(h2_scr[...], split_rows(w3f_ref[...]), preferred_element_type=F32)
    hf = hf * (jnp.exp(-t_ref[...] * jnp.abs(decf_ref[...])) + HY_SHIFT)
    hb = jnp.dot(h2m_scr[...], split_rows(w3b_ref[...]), preferred_element_type=F32)
    hb = hb * (jnp.exp(-tm_ref[...] * jnp.abs(decb_ref[...])) + HY_SHIFT)
    row = lax.broadcasted_iota(jnp.int32, hb.shape, 0)
    hb = jnp.where(row == 0, 0.0, hb)
    norm = jnp.sum(hf * hf, axis=0, keepdims=True) + jnp.sum(hb * hb, axis=0, keepdims=True)
    scale = lax.rsqrt(norm + HY_EPS) * (1.0 / blk)
    k2_scr[0:seq, :] = hb.astype(BF16)
    k2_scr[seq:2 * seq, :] = hf.astype(BF16)
    for d in range(2 * nb - 1):
        seg = k2_scr[d * blk:(d + 2) * blk, :]
        tc, ts = (jnp.concatenate([ref[c] for c in range(ref.shape[0])], axis=1) for ref in (tc_ref, ts_ref))
        kr_scr[d] = jnp.dot(tc, seg, preferred_element_type=F32) * scale
        ki_scr[d] = jnp.dot(ts, seg, preferred_element_type=F32) * scale
    for idx, terms in enumerate(_HY_COMBO_TERMS):
        for src, dst in ((kr_scr, gr_ref), (ki_scr, gi_ref)):
            acc = None
            for lag, weight in terms:
                term = src[lag + nb - 1]
                acc = (term if weight > 0 else -term) if acc is None else (acc + term if weight > 0 else acc - term)
            dst[0, idx] = acc


def _hy_filters(seq, w1, b1, freq, w2, b2, w3, decay, width, tables, nb, *, col_block=256):
    _, _, tc, ts = tables
    blk = seq // nb
    ffn = w2.shape[0]
    feats, t, featsm, tm = _position_features(seq, LANES)
    w1p = jnp.zeros((LANES, ffn), F32).at[:w1.shape[0]].set(w1)
    nblk = width // col_block
    side = lambda rows, direction: pl.BlockSpec(
        (rows, col_block), lambda i: (0, (2 * (i // nblk) + direction) * nblk + i % nblk))
    out = pl.BlockSpec((1, HY_COMBOS, blk, col_block), lambda i: (i // nblk, 0, 0, i % nblk))
    body = functools.partial(_hy_filter_body, seq=seq, nb=nb)
    row2 = lambda v: v.reshape(1, -1)
    return pl.pallas_call(
        body,
        grid=(HY_ORDER * nblk,),
        in_specs=[_resident(feats.shape), _resident(t.shape), _resident(feats.shape), _resident(t.shape),
                  _resident(w1p.shape), _resident((1, ffn)), _resident((1, ffn)), _resident(w2.shape),
                  _resident((1, ffn)), side(ffn, 0), side(ffn, 1), side(1, 0), side(1, 1),
                  _resident(tc.shape), _resident(ts.shape)],
        out_specs=[out, out],
        out_shape=[jax.ShapeDtypeStruct((HY_ORDER, HY_COMBOS, blk, width), F32)] * 2,
        scratch_shapes=[pltpu.VMEM((seq, 4 * ffn), BF16), pltpu.VMEM((seq, 4 * ffn), BF16),
                        pltpu.VMEM((2 * seq, col_block), BF16),
                        pltpu.VMEM((2 * nb - 1, blk, col_block), F32), pltpu.VMEM((2 * nb - 1, blk, col_block), F32)],
        compiler_params=pltpu.CompilerParams(dimension_semantics=("arbitrary",),
                                             vmem_limit_bytes=VMEM_LIMIT_BYTES),
    )(feats, t, featsm, tm, w1p, row2(b1), row2(freq), w2, row2(b2), w3, w3, row2(decay), row2(decay), tc, ts)


def _hy_conv_body(pv_ref, p1_ref, p2_ref, g_ref, cw_ref, cb_ref, gr_ref, gi_ref, d_ref, mc_ref, ms_ref, o_ref,
                  z_scr, a_scr, b_scr, xs_scr, *, seq, nb, cw, row_chunk, problems):
    blk = seq // nb
    gate_refs = (p1_ref, p2_ref)
    lanes = lambda j: slice(j * cw, (j + 1) * cw)
    cadd = lambda x, y: (x[0] + y[0], x[1] + y[1])
    cmul = lambda x, y: (x[0] * y[0] - x[1] * y[1], x[0] * y[1] + x[1] * y[0])

    def short_conv(k, p_ref, part):
        x = p_ref[k].astype(F32)
        pad = xs_scr.at[k, part]
        pad[0:SUBLANES, :] = jnp.zeros((SUBLANES, cw), F32)
        pad[SUBLANES + seq:2 * SUBLANES + seq, :] = jnp.zeros((SUBLANES, cw), F32)
        pad[SUBLANES:SUBLANES + seq, :] = x
        prev, nxt = pad[pl.ds(SUBLANES - 1, seq), :], pad[pl.ds(SUBLANES + 1, seq), :]
        w = cw_ref[:, part, :]
        return cb_ref[part:part + 1, :] + prev * w[0:1] + x * w[1:2] + nxt * w[2:3]

    def load_v(k):
        v = short_conv(k, pv_ref, 0)
        for j in range(nb):
            z_scr[k, :, lanes(j)] = v[j * blk:(j + 1) * blk, :]

    def forward(k, o):
        zb = z_scr[k].astype(BF16)
        a_scr[k] = jnp.dot(mc_ref[...], zb, preferred_element_type=F32)
        b_scr[k] = jnp.dot(ms_ref[...], zb, preferred_element_type=F32)

    def pointwise(k, o):
        def toeplitz2(base, x0, x1, r):
            g = lambda idx: (gr_ref[o, idx, r, :], gi_ref[o, idx, r, :])
            q = cmul(g(base), cadd(x0, x1))
            return cadd(q, cmul(g(base + 1), x1)), cadd(q, cmul(g(base + 2), x0))

        for c in range(blk // row_chunk):
            r = slice(c * row_chunk, (c + 1) * row_chunk)
            z = [(a_scr[k, r, lanes(j)], b_scr[k, r, lanes(j)]) for j in range(nb)]
            p1 = toeplitz2(0, cadd(z[0], z[2]), cadd(z[1], z[3]), r)
            p2 = toeplitz2(3, z[2], z[3], r)
            p3 = toeplitz2(6, z[0], z[1], r)
            for i, y in enumerate((cadd(p1[0], p2[0]), cadd(p1[1], p2[1]), cadd(p1[0], p3[0]), cadd(p1[1], p3[1]))):
                a_scr[k, r, lanes(i)], b_scr[k, r, lanes(i)] = y

    def inverse(k, o):
        conv = (jnp.dot(mc_ref[...], a_scr[k].astype(BF16), preferred_element_type=F32)
                + jnp.dot(ms_ref[...], b_scr[k].astype(BF16), preferred_element_type=F32))
        gate = short_conv(k, gate_refs[o], o + 1)
        dsk = d_ref[o:o + 1, :]
        for j in range(nb):
            rows = slice(j * blk, (j + 1) * blk)
            z = gate[rows, :] * (conv[:, lanes(j)] + dsk * z_scr[k, :, lanes(j)])
            if o == HY_ORDER - 1:
                o_ref[k, rows, :] = (z * g_ref[k, rows, :].astype(F32)).astype(o_ref.dtype)
            else:
                z_scr[k, :, lanes(j)] = z

    stages = [load_v]
    for o in range(HY_ORDER):
        stages += [functools.partial(forward, o=o), functools.partial(pointwise, o=o),
                   functools.partial(inverse, o=o)]
    for s in range(len(stages) + problems - 1):
        for k in range(problems):
            if 0 <= s - k < len(stages):
                stages[s - k](k)


def _hy_conv(p_hy, gate_silu, conv_w, conv_b, gr, gi, d_skip, tables, batch, seq, width, nb, *, col_block=LANES,
             row_chunk=SUBLANES, problems=2):
    assert nb == HY_TIME_BLOCKS
    mc, ms, _, _ = tables
    blk = seq // nb
    nblk = width // col_block
    part = lambda k: pl.BlockSpec((problems, seq, col_block), lambda c, b: (b, 0, k * nblk + c))
    cw = conv_w.reshape(conv_w.shape[0], HY_ORDER + 1, width)
    cb = conv_b.reshape(HY_ORDER + 1, width)
    filt = pl.BlockSpec((HY_ORDER, HY_COMBOS, blk, col_block), lambda c, b: (0, 0, 0, c))
    body = functools.partial(_hy_conv_body, seq=seq, nb=nb, cw=col_block, row_chunk=row_chunk, problems=problems)
    wide = nb * col_block
    scratch = lambda dtype: pltpu.VMEM((problems, blk, wide), dtype)
    return pl.pallas_call(
        body,
        grid=(nblk, batch // problems),
        in_specs=[part(0), part(1), part(2),
                  pl.BlockSpec((problems, seq, col_block), lambda c, b: (b, 0, c)),
                  pl.BlockSpec((cw.shape[0], HY_ORDER + 1, col_block), lambda c, b: (0, 0, c)),
                  pl.BlockSpec((HY_ORDER + 1, col_block), lambda c, b: (0, c)),
                  filt, filt,
                  pl.BlockSpec((HY_ORDER, col_block), lambda c, b: (0, c)),
                  _resident(mc.shape), _resident(ms.shape)],
        out_specs=pl.BlockSpec((problems, seq, col_block), lambda c, b: (b, 0, c)),
        out_shape=jax.ShapeDtypeStruct((batch, seq, width), BF16),
        scratch_shapes=[scratch(F32), scratch(F32), scratch(F32),
                        pltpu.VMEM((problems, HY_ORDER + 1, seq + 2 * SUBLANES, col_block), F32)],
        compiler_params=pltpu.CompilerParams(dimension_semantics=("parallel", "parallel"),
                                             vmem_limit_bytes=VMEM_LIMIT_BYTES),
    )(p_hy, p_hy, p_hy, gate_silu, cw, cb, gr, gi, d_skip, mc, ms)


def _merge_body(ys_ref, gs_ref, *refs, d_model, final_norm, yh_parts):
    yh_refs = refs[:yh_parts]
    m_ref, h_ref, wg_ref, bg_ref, wbs_ref, wbh_ref, wo_ref, fw_ref, o_ref = refs[yh_parts:]

    def dot_chunks(x, w_ref):
        return jnp.concatenate([jnp.dot(x, w_ref[c], preferred_element_type=F32) for c in range(w_ref.shape[0])],
                               axis=1)

    ys = ys_ref[...]
    y = _gelu_tanh(ys.reshape(ys.shape[0] * ys.shape[1], ys.shape[2]))
    glu = y * _sigmoid(dot_chunks(y.astype(BF16), wg_ref) + bg_ref[...])
    s5 = (glu * gs_ref[...].astype(F32)).astype(BF16)
    y_s5 = dot_chunks(s5, wbs_ref)
    y_hy = dot_chunks(jnp.concatenate([yh_ref[...] for yh_ref in yh_refs], axis=1), wbh_ref)
    merged = m_ref[:, :d_model].astype(F32) * y_s5 + m_ref[:, d_model:].astype(F32) * y_hy
    h = h_ref[...] + dot_chunks(merged.astype(BF16), wo_ref)
    if final_norm:
        ms = jnp.mean(h * h, axis=-1, keepdims=True)
        h = h * lax.rsqrt(ms + RMS_EPS) * fw_ref[...]
    o_ref[...] = h


def _merge(y_s5_cm, gs, y_hypre, m, h2d, w_glu, b_glu, w_bs, w_bh, w_out, layer, final_w, final_norm, seq, *,
           row_tile=1024):
    rows, d = h2d.shape
    sw = y_s5_cm.shape[-1]
    hw = y_hypre.shape[1]
    tile = lambda w: pl.BlockSpec((row_tile, w), lambda i: (i, 0))
    yh_parts = hw // NARROW
    body = functools.partial(_merge_body, d_model=d, final_norm=final_norm, yh_parts=yh_parts)
    return pl.pallas_call(
        body,
        grid=(rows // row_tile,),
        in_specs=[_chunk_major_spec(row_tile, seq, sw), tile(sw)]
        + [pl.BlockSpec((row_tile, NARROW), functools.partial(lambda i, c: (i, c), c=c)) for c in range(yh_parts)]
        + [tile(2 * d), tile(d),
                  _resident_layer(w_glu.shape, layer), _resident_layer((b_glu.shape[0], 1, sw), layer),
                  _resident_layer(w_bs.shape, layer), _resident_layer(w_bh.shape, layer),
                  _resident_layer(w_out.shape, layer), _resident((1, d))],
        out_specs=tile(d),
        out_shape=jax.ShapeDtypeStruct((rows, d), F32),
        compiler_params=pltpu.CompilerParams(dimension_semantics=("parallel",),
                                             vmem_limit_bytes=VMEM_LIMIT_BYTES),
    )(y_s5_cm, gs, *([y_hypre] * yh_parts), m, h2d, w_glu, b_glu.reshape(-1, 1, sw), w_bs, w_bh, w_out,
      final_w.reshape(1, d))


def kernel(x, norm_w, w_in, s5_lam_re, s5_lam_im, s5_log_step, s5_b_re, s5_b_im, s5_c_re, s5_c_im, s5_d, s5_w_glu, s5_b_glu, hy_conv_w, hy_conv_b, hy_w1, hy_b1, hy_freq, hy_w2, hy_b2, hy_w3, hy_decay, hy_d, w_branch_s5, w_branch_hy, w_out, final_norm_w):
    batch, seq, d_model = x.shape
    depth = w_in.shape[0]
    s5_width = s5_d.shape[1]
    hy_width = hy_d.shape[2]
    widths = (s5_width, s5_width, (HY_ORDER + 1) * hy_width, hy_width, 2 * d_model)
    nb = HY_TIME_BLOCKS
    tables = _dft_tables(seq // nb)
    h = x.reshape(batch * seq, d_model)
    w_in_bf, w_glu_bf, w_bs_bf, w_bh_bf, w_out_bf = (
        jnp.transpose(w.astype(BF16).reshape(depth, w.shape[1], -1, NARROW), (0, 2, 1, 3))
        for w in (w_in, s5_w_glu, w_branch_s5, w_branch_hy, w_out))
    s5_ops = jax.vmap(_s5_operands)(s5_lam_re, s5_lam_im, s5_log_step, s5_b_re, s5_b_im, s5_c_re, s5_c_im, s5_d)
    for l in range(depth):
        u, gs, p_hy, gh, m = _inproj(h, norm_w, w_in_bf, l, widths, batch, seq)
        y_s5 = _s5_scan(u, s5_ops, l, batch, seq)
        gr, gi = _hy_filters(seq, hy_w1[l], hy_b1[l], hy_freq[l], hy_w2[l], hy_b2[l], hy_w3[l], hy_decay[l],
                             hy_width, tables, nb)
        y_hypre = _hy_conv(p_hy.reshape(batch, seq, -1), gh.reshape(batch, seq, -1), hy_conv_w[l], hy_conv_b[l],
                           gr, gi, hy_d[l], tables, batch, seq, hy_width, nb)
        h = _merge(y_s5, gs, y_hypre.reshape(batch * seq, hy_width), m, h, w_glu_bf, s5_b_glu, w_bs_bf, w_bh_bf,
                   w_out_bf, l, final_norm_w, l == depth - 1, seq)
    return h.reshape(batch, seq, d_model)
```

```python
import functools
import math

import numpy as np
import jax
import jax.numpy as jnp
from jax import lax
from jax.experimental import pallas as pl
from jax.experimental.pallas import tpu as pltpu

F32 = jnp.float32
BF16 = jnp.bfloat16

RMS_EPS = 1e-6
S5_GROUP = 16
S5_CHUNK = 16
S5_GROUPS_PER_PAIR = 2
LANES = 128
SUBLANES = 8
HY_ORDER = 2
HY_BANDS = 16
HY_SHIFT = 0.05
HY_EPS = 1e-6
HY_TIME_BLOCKS = 4
HY_LAGS = 2 * HY_TIME_BLOCKS - 1
HY_COMBOS = 9
VMEM_LIMIT_BYTES = 56 * 1024 * 1024

_NT = (((1,), (1,)), ((), ()))


def _sigmoid(x):
    return 1.0 / (1.0 + jnp.exp2(x * (-1.0 / math.log(2.0))))


def _silu(x):
    return x * _sigmoid(x)


def _gelu_tanh(x):
    return 0.5 * x * (1.0 + jnp.tanh(math.sqrt(2.0 / math.pi) * (x + 0.044715 * (x * x * x))))


def _resident(shape):
    zeros = (0,) * len(shape)
    return pl.BlockSpec(shape, lambda *_: zeros, pipeline_mode=pl.Buffered(1))


def _resident_layer(stacked_shape, layer):
    zeros = (0,) * (len(stacked_shape) - 1)
    return pl.BlockSpec((None,) + tuple(stacked_shape[1:]), lambda *_: (layer,) + zeros,
                        pipeline_mode=pl.Buffered(1))


def _chunk_major_spec(row_tile, seq, width):
    tiles_per_seq = seq // row_tile
    return pl.BlockSpec((row_tile // S5_CHUNK, None, S5_CHUNK, width),
                        lambda i: (i % tiles_per_seq, i // tiles_per_seq, 0, 0))


def _inproj_body(x_ref, nw_ref, w_ref, u_ref, gs_ref, p_ref, gh_ref, m_ref, *, bounds, col_chunk, sub_rows):
    outs = ((u_ref, None), (gs_ref, _silu), (p_ref, None), (gh_ref, _silu), (m_ref, _sigmoid))
    for r0 in range(0, x_ref.shape[0], sub_rows):
        x = x_ref[r0:r0 + sub_rows, :]
        ms = jnp.mean(x * x, axis=-1, keepdims=True)
        xb = (x * lax.rsqrt(ms + RMS_EPS) * nw_ref[...]).astype(BF16)
        for (out_ref, act), lo, hi in zip(outs, bounds[:-1], bounds[1:]):
            for c0 in range(lo, hi, col_chunk):
                y = jnp.dot(xb, w_ref[:, c0:c0 + col_chunk], preferred_element_type=F32)
                if act is not None:
                    y = act(y)
                cols = slice(c0 - lo, c0 - lo + col_chunk)
                if out_ref is u_ref:
                    out_ref[r0 // S5_CHUNK:(r0 + sub_rows) // S5_CHUNK, :, cols] = y.astype(out_ref.dtype).reshape(
                        sub_rows // S5_CHUNK, S5_CHUNK, col_chunk)
                else:
                    out_ref[r0:r0 + sub_rows, cols] = y.astype(out_ref.dtype)


def _inproj(h2d, norm_w, w_in_bf, layer, widths, batch, seq, *, row_tile=512, sub_rows=256, col_chunk=512):
    rows, d = h2d.shape
    bounds = [0]
    for w in widths:
        bounds.append(bounds[-1] + w)
    body = functools.partial(_inproj_body, bounds=tuple(bounds), col_chunk=col_chunk, sub_rows=sub_rows)
    tile = lambda w: pl.BlockSpec((row_tile, w), lambda i: (i, 0))
    return pl.pallas_call(
        body,
        grid=(rows // row_tile,),
        in_specs=[tile(d), _resident_layer((norm_w.shape[0], 1, d), layer), _resident_layer(w_in_bf.shape, layer)],
        out_specs=[_chunk_major_spec(row_tile, seq, widths[0])] + [tile(w) for w in widths[1:]],
        out_shape=[jax.ShapeDtypeStruct((seq // S5_CHUNK, batch, S5_CHUNK, widths[0]), BF16)]
        + [jax.ShapeDtypeStruct((rows, w), BF16) for w in widths[1:]],
        compiler_params=pltpu.CompilerParams(dimension_semantics=("parallel",),
                                             vmem_limit_bytes=VMEM_LIMIT_BYTES),
    )(h2d, norm_w.reshape(-1, 1, d), w_in_bf)


def _s5_body(*refs, batch, chunks, pairs, pw, sl):
    t_steps = S5_CHUNK
    x_refs = refs[:t_steps]
    bb_ref, cc_ref, pin_ref, pout_ref, a_ref, sc_ref, dv_ref, y_ref, d_scr, st_scr = refs[t_steps:]
    k = t_steps * pw
    rows = batch * chunks
    assert k == 4 * sl

    assert pairs == 4 and pairs * pw == LANES and t_steps % pairs == 0

    def piece_transpose(v):
        lane = lax.broadcasted_iota(jnp.int32, v[0].shape, 1)
        upper, odd = lane >= 2 * pw, (lane // pw) % 2 == 1
        t0 = jnp.where(upper, pltpu.roll(v[2], 2 * pw, 1), v[0])
        t2 = jnp.where(upper, v[2], pltpu.roll(v[0], 2 * pw, 1))
        t1 = jnp.where(upper, pltpu.roll(v[3], 2 * pw, 1), v[1])
        t3 = jnp.where(upper, v[3], pltpu.roll(v[1], 2 * pw, 1))
        return (jnp.where(odd, pltpu.roll(t1, pw, 1), t0), jnp.where(odd, t1, pltpu.roll(t0, LANES - pw, 1)),
                jnp.where(odd, pltpu.roll(t3, pw, 1), t2), jnp.where(odd, t3, pltpu.roll(t2, LANES - pw, 1)))

    def pair_operand(fac_ref, pow_ref, direction, j):
        re, im = [], []
        for t in range(t_steps):
            for a in range(S5_GROUPS_PER_PAIR):
                g = S5_GROUPS_PER_PAIR * j + a
                fr, fi = fac_ref[2 * direction, g], fac_ref[2 * direction + 1, g]
                pr, pi = pow_ref[2 * direction, g, t:t + 1, :], pow_ref[2 * direction + 1, g, t:t + 1, :]
                re.append(pr * fr - pi * fi)
                im.append(pr * fi + pi * fr)
        return jnp.concatenate(re, axis=0), jnp.concatenate(im, axis=0)

    for j in range(pairs):
        d_scr[:, j * k:(j + 1) * k] = jnp.concatenate(
            [x_refs[t][:, j * pw:(j + 1) * pw] for t in range(t_steps)], axis=1)
        win = jnp.concatenate(pair_operand(bb_ref, pin_ref, 0, j) + pair_operand(bb_ref, pin_ref, 1, j),
                              axis=1).astype(BF16)
        st_scr[:, j * k:(j + 1) * k] = jnp.dot(d_scr[:, j * k:(j + 1) * k], win, preferred_element_type=F32)

    coef = [[jnp.broadcast_to(a_ref[j, c:c + 1, :], (batch, sl)) for c in range(4)] for j in range(pairs)]

    per_iter = 2
    assert chunks % per_iter == 0

    def step(i, carry):
        state = list(carry)
        rows_of, loaded = [], []
        for u in range(per_iter):
            c = i * per_iter + u
            rows_of.append((pl.ds(pl.multiple_of(c * batch, batch), batch),
                            pl.ds(pl.multiple_of((chunks - 1 - c) * batch, batch), batch)))
            loaded.append([st_scr[rows_of[u][col // 2 % 2], col * sl:(col + 1) * sl] for col in range(4 * pairs)])
        stores = []
        for u in range(per_iter):
            for j in range(pairs):
                for direction in range(2):
                    re, im = 4 * j + 2 * direction, 4 * j + 2 * direction + 1
                    sr, si = state[re], state[im]
                    ar, ai = coef[j][2 * direction], coef[j][2 * direction + 1]
                    stores += [(u, re, sr), (u, im, si)]
                    state[re] = ar * sr - ai * si + loaded[u][re]
                    state[im] = ar * si + ai * sr + loaded[u][im]
        for u, col, value in stores:
            st_scr[rows_of[u][col // 2 % 2], col * sl:(col + 1) * sl] = value
        return tuple(state)

    zero = jnp.zeros((batch, sl), F32)
    lax.fori_loop(0, chunks // per_iter, step, (zero,) * (4 * pairs))

    in_step = lax.broadcasted_iota(jnp.int32, (k, k), 0) // pw
    out_step = lax.broadcasted_iota(jnp.int32, (k, k), 1) // pw
    for j in range(pairs):
        sc = sc_ref[j]
        kern = None
        woutt = []
        for direction in range(2):
            wr, wi = pair_operand(bb_ref, pin_ref, direction, j)
            out_re, out_im = pair_operand(cc_ref, pout_ref, direction, j)
            woutt += [out_re.astype(BF16), (-out_im).astype(BF16)]
            pr, pi = sc[2 * direction:2 * direction + 1, :], sc[2 * direction + 1:2 * direction + 2, :]
            wp = jnp.concatenate([wr * pr - wi * pi, wr * pi + wi * pr], axis=1).astype(BF16)
            wt = jnp.concatenate(woutt[2 * direction:2 * direction + 2], axis=1)
            kd = lax.dot_general(wp, wt, _NT, preferred_element_type=F32)
            kd = jnp.where(out_step >= in_step if direction == 0 else out_step <= in_step, kd, 0.0)
            kern = kd if kern is None else kern + kd
        dj = d_scr[:, j * k:(j + 1) * k]
        wt = jnp.concatenate(woutt, axis=1)
        y = jnp.dot(dj, kern.astype(BF16), preferred_element_type=F32)
        y = y + lax.dot_general(st_scr[:, j * k:(j + 1) * k].astype(BF16), wt, _NT, preferred_element_type=F32)
        y = y + dj.astype(F32) * dv_ref[j]
        st_scr[:, j * k:(j + 1) * k] = y
    for m in range(t_steps // pairs):
        moved = piece_transpose([st_scr[:, j * k + m * LANES:j * k + (m + 1) * LANES] for j in range(pairs)])
        for i in range(pairs):
            y_ref[pl.ds(pairs * m + i, rows, stride=t_steps), :] = moved[i]


def _s5_operands(lam_re, lam_im, log_step, b_re, b_im, c_re, c_im, d_skip):
    _, groups, p = lam_re.shape
    hh, t, gp = S5_GROUP, S5_CHUNK, S5_GROUPS_PER_PAIR
    npair = groups // gp
    rep = lambda v: jnp.concatenate([v] * gp, axis=-1)
    own = (jnp.arange(gp * p)[None, :] // p == jnp.arange(groups)[:, None] % gp).astype(F32)
    lam_re, lam_im = rep(lam_re), rep(lam_im)
    dt = jnp.exp(log_step)[..., None]
    xr, xi = lam_re * dt, lam_im * dt
    lbr, lbi = jnp.exp(xr) * jnp.cos(xi), jnp.exp(xr) * jnp.sin(xi)
    nr, ni = lbr - 1.0, lbi
    den = lam_re * lam_re + lam_im * lam_im
    zr, zi = (nr * lam_re + ni * lam_im) / den * own, (ni * lam_re - nr * lam_im) / den * own
    btr, bti = rep(jnp.swapaxes(b_re, -1, -2)), rep(jnp.swapaxes(b_im, -1, -2))
    bbr = zr[:, :, None, :] * btr - zi[:, :, None, :] * bti
    bbi = zr[:, :, None, :] * bti + zi[:, :, None, :] * btr
    ccr, cci = rep(c_re) * own[None, :, None, :], rep(c_im) * own[None, :, None, :]

    def powers(e):
        kk = e[:, None, :, None]
        mag, ang = jnp.exp(kk * xr[:, :, None, :]), kk * xi[:, :, None, :]
        return mag * jnp.cos(ang), mag * jnp.sin(ang)

    components = lambda x_re, x_im: jnp.stack([x_re[0], x_im[0], x_re[1], x_im[1]])
    tt = jnp.arange(t, dtype=F32)
    pin = components(*powers(jnp.stack([t - 1 - tt, tt])))
    pout = components(*powers(jnp.stack([tt + 1, t - tt])))

    def pair_vec(e):
        mag, ang = jnp.exp(e * xr) * own, e * xi
        v = jnp.stack([(mag * jnp.cos(ang))[0], (mag * jnp.sin(ang))[0],
                       (mag * jnp.cos(ang))[1], (mag * jnp.sin(ang))[1]])
        return jnp.transpose(v.reshape(4, npair, gp, gp * p).sum(axis=2), (1, 0, 2))

    dvec = jnp.broadcast_to(d_skip.reshape(npair, 1, gp, hh), (npair, t, gp, hh)).reshape(npair, 1, t * gp * hh)
    return components(bbr, bbi), components(ccr, cci), pin, pout, pair_vec(float(t)), pair_vec(-float(t)), dvec


def _s5_scan(u_cm, operands, layer, batch, seq):
    bb, cc, pin, pout, a, sc, dvec = operands
    gp, t = S5_GROUPS_PER_PAIR, S5_CHUNK
    _, _, groups, hh, sl = bb.shape
    npair, pw = groups // gp, gp * hh
    k = t * pw
    pairs = LANES // pw
    chunks = seq // t
    rows = chunks * batch
    width = npair * pw
    nblk = width // LANES
    body = functools.partial(_s5_body, batch=batch, chunks=chunks, pairs=pairs, pw=pw, sl=sl)
    per_blk = lambda shape: pl.BlockSpec((None, pairs) + shape, lambda q: (layer, q, 0, 0))
    per_group = lambda group_rows: pl.BlockSpec((None, 4, gp * pairs, group_rows, sl), lambda q: (layer, 0, q, 0, 0))
    x2d = u_cm.reshape(rows, t * width)
    y = pl.pallas_call(
        body,
        grid=(nblk,),
        in_specs=[pl.BlockSpec((rows, LANES), functools.partial(lambda q, s: (0, s * nblk + q), s=s))
                  for s in range(t)]
        + [per_group(hh), per_group(hh), per_group(t), per_group(t), per_blk((4, sl)), per_blk((4, sl)),
           per_blk((1, k))],
        out_specs=pl.BlockSpec((rows * t, LANES), lambda q: (0, q)),
        out_shape=jax.ShapeDtypeStruct((rows * t, width), F32),
        scratch_shapes=[pltpu.VMEM((rows, pairs * k), BF16), pltpu.VMEM((rows, pairs * k), F32)],
        compiler_params=pltpu.CompilerParams(dimension_semantics=("parallel",),
                                             vmem_limit_bytes=VMEM_LIMIT_BYTES),
    )(*([x2d] * t), bb, cc, pin, pout, a, sc, dvec)
    return y.reshape(chunks, batch, t, width)


def _dft_tables(blk):
    n = 2 * blk
    f = np.arange(blk, dtype=np.int64)
    sym = ((2 * f[:, None] + 1) * (2 * f[None, :] + 1)) % (4 * n)
    ang_sym = sym.astype(np.float64) * (2.0 * np.pi / (4 * n))
    lag = np.arange(2 * blk, dtype=np.int64) - blk
    ang_lag = (((2 * f[:, None] + 1) * lag[None, :]) % (2 * n)).astype(np.float64) * (2.0 * np.pi / (2 * n))
    tc, ts = np.cos(ang_lag), np.sin(ang_lag)
    tc[:, 0] = 0.0
    ts[:, 0] = 0.0
    as_bf = lambda x: jnp.asarray(x.astype(np.float32)).astype(BF16)
    return as_bf(np.cos(ang_sym)), as_bf(np.sin(ang_sym)), as_bf(tc), as_bf(ts)


def _position_features(seq, pad_to):
    t = np.linspace(0.0, 1.0, seq)[:, None]
    pos = np.arange(seq, dtype=np.float64)[:, None]
    bands = np.linspace(1e-4, HY_BANDS - 1, HY_BANDS)[None, :]
    ang = bands * pos * (2.0 * math.pi / seq)
    feats = np.zeros((seq, pad_to), np.float64)
    feats[:, :1 + 2 * HY_BANDS] = np.concatenate([t, np.cos(ang), -np.sin(ang)], axis=-1)
    mirror = lambda x: np.roll(x[::-1], 1, axis=0)
    f32 = lambda x: jnp.asarray(x.astype(np.float32))
    return f32(feats), f32(t), f32(mirror(feats)), f32(mirror(t))


_HY_COMBO_TERMS = (
    ((0, 1),), ((-1, 1), (0, -1)), ((1, 1), (0, -1)),
    ((-2, 1), (0, -1)), ((-3, 1), (-1, -1), (-2, -1), (0, 1)), ((-1, 1), (1, -1), (-2, -1), (0, 1)),
    ((2, 1), (0, -1)), ((1, 1), (-1, -1), (2, -1), (0, 1)), ((3, 1), (1, -1), (2, -1), (0, 1)),
)


def _hy_filter_body(feats_ref, t_ref, featsm_ref, tm_ref, w1_ref, b1_ref, fr_ref, w2_ref, b2_ref, w3f_ref,
                    w3b_ref, decf_ref, decb_ref, tc_ref, ts_ref, gr_ref, gi_ref, h2_scr, h2m_scr, k2_scr,
                    kr_scr, ki_scr, *, seq, nb):
    hi = lax.Precision.HIGHEST
    blk = seq // nb

    @pl.when(pl.program_id(0) == 0)
    def _():
        fr = fr_ref[...]

        def mlp(feats):
            h1 = jnp.sin(fr * (jnp.dot(feats, w1_ref[...], precision=hi, preferred_element_type=F32) + b1_ref[...]))
            return jnp.sin(fr * (jnp.dot(h1, w2_ref[...], precision=hi, preferred_element_type=F32) + b2_ref[...]))

        def split(h):
            top = h.astype(BF16).astype(F32)
            return jnp.concatenate([top, h - top, top, h - top], axis=1).astype(BF16)

        h2_scr[...] = split(mlp(feats_ref[...]))
        h2m_scr[...] = split(mlp(featsm_ref[...]))

    def split_rows(w):
        top = w.astype(BF16).astype(F32)
        return jnp.concatenate([top, top, w - top, w - top], axis=0).astype(BF16)

    hf = jnp.dot(h2_scr[...], split_rows(w3f_ref[...]), preferred_element_type=F32)
    hf = hf * (jnp.exp(-t_ref[...] * jnp.abs(decf_ref[...])) + HY_SHIFT)
    hb = jnp.dot(h2m_scr[...], split_rows(w3b_ref[...]), preferred_element_type=F32)
    hb = hb * (jnp.exp(-tm_ref[...] * jnp.abs(decb_ref[...])) + HY_SHIFT)
    row = lax.broadcasted_iota(jnp.int32, hb.shape, 0)
    hb = jnp.where(row == 0, 0.0, hb)
    norm = jnp.sum(hf * hf, axis=0, keepdims=True) + jnp.sum(hb * hb, axis=0, keepdims=True)
    scale = lax.rsqrt(norm + HY_EPS) * (1.0 / blk)
    k2_scr[0:seq, :] = hb.astype(BF16)
    k2_scr[seq:2 * seq, :] = hf.astype(BF16)
    for d in range(2 * nb - 1):
        seg = k2_scr[d * blk:(d + 2) * blk, :]
        kr_scr[d] = jnp.dot(tc_ref[...], seg, preferred_element_type=F32) * scale
        ki_scr[d] = jnp.dot(ts_ref[...], seg, preferred_element_type=F32) * scale
    for idx, terms in enumerate(_HY_COMBO_TERMS):
        for src, dst in ((kr_scr, gr_ref), (ki_scr, gi_ref)):
            acc = None
            for lag, weight in terms:
                term = src[lag + nb - 1]
                acc = (term if weight > 0 else -term) if acc is None else (acc + term if weight > 0 else acc - term)
            dst[0, idx] = acc


def _hy_filters(seq, w1, b1, freq, w2, b2, w3, decay, width, tables, nb, *, col_block=256):
    _, _, tc, ts = tables
    blk = seq // nb
    ffn = w2.shape[0]
    feats, t, featsm, tm = _position_features(seq, LANES)
    w1p = jnp.zeros((LANES, ffn), F32).at[:w1.shape[0]].set(w1)
    nblk = width // col_block
    side = lambda rows, direction: pl.BlockSpec(
        (rows, col_block), lambda i: (0, (2 * (i // nblk) + direction) * nblk + i % nblk))
    out = pl.BlockSpec((1, HY_COMBOS, blk, col_block), lambda i: (i // nblk, 0, 0, i % nblk))
    body = functools.partial(_hy_filter_body, seq=seq, nb=nb)
    row2 = lambda v: v.reshape(1, -1)
    return pl.pallas_call(
        body,
        grid=(HY_ORDER * nblk,),
        in_specs=[_resident(feats.shape), _resident(t.shape), _resident(feats.shape), _resident(t.shape),
                  _resident(w1p.shape), _resident((1, ffn)), _resident((1, ffn)), _resident(w2.shape),
                  _resident((1, ffn)), side(ffn, 0), side(ffn, 1), side(1, 0), side(1, 1),
                  _resident(tc.shape), _resident(ts.shape)],
        out_specs=[out, out],
        out_shape=[jax.ShapeDtypeStruct((HY_ORDER, HY_COMBOS, blk, width), F32)] * 2,
        scratch_shapes=[pltpu.VMEM((seq, 4 * ffn), BF16), pltpu.VMEM((seq, 4 * ffn), BF16),
                        pltpu.VMEM((2 * seq, col_block), BF16),
                        pltpu.VMEM((2 * nb - 1, blk, col_block), F32), pltpu.VMEM((2 * nb - 1, blk, col_block), F32)],
        compiler_params=pltpu.CompilerParams(dimension_semantics=("arbitrary",),
                                             vmem_limit_bytes=VMEM_LIMIT_BYTES),
    )(feats, t, featsm, tm, w1p, row2(b1), row2(freq), w2, row2(b2), w3, w3, row2(decay), row2(decay), tc, ts)


def _hy_conv_body(pv_ref, p1_ref, p2_ref, g_ref, cw_ref, cb_ref, gr_ref, gi_ref, d_ref, mc_ref, ms_ref, o_ref,
                  z_scr, a_scr, b_scr, xs_scr, *, seq, nb, cw, row_chunk, problems):
    blk = seq // nb
    gate_refs = (p1_ref, p2_ref)
    lanes = lambda j: slice(j * cw, (j + 1) * cw)
    cadd = lambda x, y: (x[0] + y[0], x[1] + y[1])
    cmul = lambda x, y: (x[0] * y[0] - x[1] * y[1], x[0] * y[1] + x[1] * y[0])

    def short_conv(k, p_ref, part):
        x = p_ref[k].astype(F32)
        pad = xs_scr.at[k, part]
        pad[0:SUBLANES, :] = jnp.zeros((SUBLANES, cw), F32)
        pad[SUBLANES + seq:2 * SUBLANES + seq, :] = jnp.zeros((SUBLANES, cw), F32)
        pad[SUBLANES:SUBLANES + seq, :] = x
        prev, nxt = pad[pl.ds(SUBLANES - 1, seq), :], pad[pl.ds(SUBLANES + 1, seq), :]
        w = cw_ref[:, part, :]
        return cb_ref[part:part + 1, :] + prev * w[0:1] + x * w[1:2] + nxt * w[2:3]

    def load_v(k):
        v = short_conv(k, pv_ref, 0)
        for j in range(nb):
            z_scr[k, :, lanes(j)] = v[j * blk:(j + 1) * blk, :]

    def forward(k, o):
        zb = z_scr[k].astype(BF16)
        a_scr[k] = jnp.dot(mc_ref[...], zb, preferred_element_type=F32)
        b_scr[k] = jnp.dot(ms_ref[...], zb, preferred_element_type=F32)

    def pointwise(k, o):
        def toeplitz2(base, x0, x1, r):
            g = lambda idx: (gr_ref[o, idx, r, :], gi_ref[o, idx, r, :])
            q = cmul(g(base), cadd(x0, x1))
            return cadd(q, cmul(g(base + 1), x1)), cadd(q, cmul(g(base + 2), x0))

        for c in range(blk // row_chunk):
            r = slice(c * row_chunk, (c + 1) * row_chunk)
            z = [(a_scr[k, r, lanes(j)], b_scr[k, r, lanes(j)]) for j in range(nb)]
            p1 = toeplitz2(0, cadd(z[0], z[2]), cadd(z[1], z[3]), r)
            p2 = toeplitz2(3, z[2], z[3], r)
            p3 = toeplitz2(6, z[0], z[1], r)
            for i, y in enumerate((cadd(p1[0], p2[0]), cadd(p1[1], p2[1]), cadd(p1[0], p3[0]), cadd(p1[1], p3[1]))):
                a_scr[k, r, lanes(i)], b_scr[k, r, lanes(i)] = y

    def inverse(k, o):
        conv = (jnp.dot(mc_ref[...], a_scr[k].astype(BF16), preferred_element_type=F32)
                + jnp.dot(ms_ref[...], b_scr[k].astype(BF16), preferred_element_type=F32))
        gate = short_conv(k, gate_refs[o], o + 1)
        dsk = d_ref[o:o + 1, :]
        for j in range(nb):
            rows = slice(j * blk, (j + 1) * blk)
            z = gate[rows, :] * (conv[:, lanes(j)] + dsk * z_scr[k, :, lanes(j)])
            if o == HY_ORDER - 1:
                o_ref[k, rows, :] = (z * g_ref[k, rows, :].astype(F32)).astype(o_ref.dtype)
            else:
                z_scr[k, :, lanes(j)] = z

    stages = [load_v]
    for o in range(HY_ORDER):
        stages += [functools.partial(forward, o=o), functools.partial(pointwise, o=o),
                   functools.partial(inverse, o=o)]
    for s in range(len(stages) + problems - 1):
        for k in range(problems):
            if 0 <= s - k < len(stages):
                stages[s - k](k)


def _hy_conv(p_hy, gate_silu, conv_w, conv_b, gr, gi, d_skip, tables, batch, seq, width, nb, *, col_block=LANES,
             row_chunk=SUBLANES, problems=2):
    assert nb == HY_TIME_BLOCKS
    mc, ms, _, _ = tables
    blk = seq // nb
    nblk = width // col_block
    part = lambda k: pl.BlockSpec((problems, seq, col_block), lambda c, b: (b, 0, k * nblk + c))
    cw = conv_w.reshape(conv_w.shape[0], HY_ORDER + 1, width)
    cb = conv_b.reshape(HY_ORDER + 1, width)
    filt = pl.BlockSpec((HY_ORDER, HY_COMBOS, blk, col_block), lambda c, b: (0, 0, 0, c))
    body = functools.partial(_hy_conv_body, seq=seq, nb=nb, cw=col_block, row_chunk=row_chunk, problems=problems)
    wide = nb * col_block
    scratch = lambda dtype: pltpu.VMEM((problems, blk, wide), dtype)
    return pl.pallas_call(
        body,
        grid=(nblk, batch // problems),
        in_specs=[part(0), part(1), part(2),
                  pl.BlockSpec((problems, seq, col_block), lambda c, b: (b, 0, c)),
                  pl.BlockSpec((cw.shape[0], HY_ORDER + 1, col_block), lambda c, b: (0, 0, c)),
                  pl.BlockSpec((HY_ORDER + 1, col_block), lambda c, b: (0, c)),
                  filt, filt,
                  pl.BlockSpec((HY_ORDER, col_block), lambda c, b: (0, c)),
                  _resident(mc.shape), _resident(ms.shape)],
        out_specs=pl.BlockSpec((problems, seq, col_block), lambda c, b: (b, 0, c)),
        out_shape=jax.ShapeDtypeStruct((batch, seq, width), BF16),
        scratch_shapes=[scratch(F32), scratch(F32), scratch(F32),
                        pltpu.VMEM((problems, HY_ORDER + 1, seq + 2 * SUBLANES, col_block), F32)],
        compiler_params=pltpu.CompilerParams(dimension_semantics=("parallel", "parallel"),
                                             vmem_limit_bytes=VMEM_LIMIT_BYTES),
    )(p_hy, p_hy, p_hy, gate_silu, cw, cb, gr, gi, d_skip, mc, ms)


def _merge_body(ys_ref, gs_ref, yh_ref, m_ref, h_ref, wg_ref, bg_ref, wbs_ref, wbh_ref, wo_ref, fw_ref, o_ref,
                *, d_model, final_norm):
    ys = ys_ref[...]
    y = _gelu_tanh(ys.reshape(ys.shape[0] * ys.shape[1], ys.shape[2]))
    glu = y * _sigmoid(jnp.dot(y.astype(BF16), wg_ref[...], preferred_element_type=F32) + bg_ref[...])
    s5 = (glu * gs_ref[...].astype(F32)).astype(BF16)
    y_s5 = jnp.dot(s5, wbs_ref[...], preferred_element_type=F32)
    y_hy = jnp.dot(yh_ref[...], wbh_ref[...], preferred_element_type=F32)
    merged = m_ref[:, :d_model].astype(F32) * y_s5 + m_ref[:, d_model:].astype(F32) * y_hy
    h = h_ref[...] + jnp.dot(merged.astype(BF16), wo_ref[...], preferred_element_type=F32)
    if final_norm:
        ms = jnp.mean(h * h, axis=-1, keepdims=True)
        h = h * lax.rsqrt(ms + RMS_EPS) * fw_ref[...]
    o_ref[...] = h


def _merge(y_s5_cm, gs, y_hypre, m, h2d, w_glu, b_glu, w_bs, w_bh, w_out, layer, final_w, final_norm, seq, *,
           row_tile=1024):
    rows, d = h2d.shape
    sw = y_s5_cm.shape[-1]
    hw = y_hypre.shape[1]
    tile = lambda w: pl.BlockSpec((row_tile, w), lambda i: (i, 0))
    body = functools.partial(_merge_body, d_model=d, final_norm=final_norm)
    return pl.pallas_call(
        body,
        grid=(rows // row_tile,),
        in_specs=[_chunk_major_spec(row_tile, seq, sw), tile(sw), tile(hw), tile(2 * d), tile(d),
                  _resident_layer(w_glu.shape, layer), _resident_layer((b_glu.shape[0], 1, sw), layer),
                  _resident_layer(w_bs.shape, layer), _resident_layer(w_bh.shape, layer),
                  _resident_layer(w_out.shape, layer), _resident((1, d))],
        out_specs=tile(d),
        out_shape=jax.ShapeDtypeStruct((rows, d), F32),
        compiler_params=pltpu.CompilerParams(dimension_semantics=("parallel",),
                                             vmem_limit_bytes=VMEM_LIMIT_BYTES),
    )(y_s5_cm, gs, y_hypre, m, h2d, w_glu, b_glu.reshape(-1, 1, sw), w_bs, w_bh, w_out, final_w.reshape(1, d))


def kernel(x, norm_w, w_in, s5_lam_re, s5_lam_im, s5_log_step, s5_b_re, s5_b_im, s5_c_re, s5_c_im, s5_d, s5_w_glu, s5_b_glu, hy_conv_w, hy_conv_b, hy_w1, hy_b1, hy_freq, hy_w2, hy_b2, hy_w3, hy_decay, hy_d, w_branch_s5, w_branch_hy, w_out, final_norm_w):
    batch, seq, d_model = x.shape
    depth = w_in.shape[0]
    s5_width = s5_d.shape[1]
    hy_width = hy_d.shape[2]
    widths = (s5_width, s5_width, (HY_ORDER + 1) * hy_width, hy_width, 2 * d_model)
    nb = HY_TIME_BLOCKS
    tables = _dft_tables(seq // nb)
    h = x.reshape(batch * seq, d_model)
    w_in_bf, w_glu_bf, w_bs_bf, w_bh_bf, w_out_bf = (w.astype(BF16) for w in (w_in, s5_w_glu, w_branch_s5,
                                                                              w_branch_hy, w_out))
    s5_ops = jax.vmap(_s5_operands)(s5_lam_re, s5_lam_im, s5_log_step, s5_b_re, s5_b_im, s5_c_re, s5_c_im, s5_d)
    for l in range(depth):
        u, gs, p_hy, gh, m = _inproj(h, norm_w, w_in_bf, l, widths, batch, seq)
        y_s5 = _s5_scan(u, s5_ops, l, batch, seq)
        gr, gi = _hy_filters(seq, hy_w1[l], hy_b1[l], hy_freq[l], hy_w2[l], hy_b2[l], hy_w3[l], hy_decay[l],
                             hy_width, tables, nb)
        y_hypre = _hy_conv(p_hy.reshape(batch, seq, -1), gh.reshape(batch, seq, -1), hy_conv_w[l], hy_conv_b[l],
                           gr, gi, hy_d[l], tables, batch, seq, hy_width, nb)
        h = _merge(y_s5, gs, y_hypre.reshape(batch * seq, hy_width), m, h, w_glu_bf, s5_b_glu, w_bs_bf, w_bh_bf,
                   w_out_bf, l, final_norm_w, l == depth - 1, seq)
    return h.reshape(batch, seq, d_model)
```

```python
import functools
import math

import numpy as np
import jax
import jax.numpy as jnp
from jax import lax
from jax.experimental import pallas as pl
from jax.experimental.pallas import tpu as pltpu

F32 = jnp.float32
BF16 = jnp.bfloat16

RMS_EPS = 1e-6
S5_GROUP = 16
S5_CHUNK = 16
S5_GROUPS_PER_PAIR = 2
LANES = 128
SUBLANES = 8
HY_ORDER = 2
HY_BANDS = 16
HY_SHIFT = 0.05
HY_EPS = 1e-6
HY_TIME_BLOCKS = 4
HY_LAGS = 2 * HY_TIME_BLOCKS - 1
HY_COMBOS = 9
VMEM_LIMIT_BYTES = 56 * 1024 * 1024

_NT = (((1,), (1,)), ((), ()))


def _sigmoid(x):
    return 1.0 / (1.0 + jnp.exp2(x * (-1.0 / math.log(2.0))))


def _silu(x):
    return x * _sigmoid(x)


def _gelu_tanh(x):
    return 0.5 * x * (1.0 + jnp.tanh(math.sqrt(2.0 / math.pi) * (x + 0.044715 * (x * x * x))))


def _resident(shape):
    zeros = (0,) * len(shape)
    return pl.BlockSpec(shape, lambda *_: zeros, pipeline_mode=pl.Buffered(1))


def _resident_layer(stacked_shape, layer):
    zeros = (0,) * (len(stacked_shape) - 1)
    return pl.BlockSpec((None,) + tuple(stacked_shape[1:]), lambda *_: (layer,) + zeros,
                        pipeline_mode=pl.Buffered(1))


def _chunk_major_spec(row_tile, seq, width):
    tiles_per_seq = seq // row_tile
    return pl.BlockSpec((row_tile // S5_CHUNK, None, S5_CHUNK, width),
                        lambda i: (i % tiles_per_seq, i // tiles_per_seq, 0, 0))


def _inproj_body(x_ref, nw_ref, w_ref, u_ref, gs_ref, p_ref, gh_ref, m_ref, *, bounds, col_chunk, sub_rows):
    outs = ((u_ref, None), (gs_ref, _silu), (p_ref, None), (gh_ref, _silu), (m_ref, _sigmoid))
    for r0 in range(0, x_ref.shape[0], sub_rows):
        x = x_ref[r0:r0 + sub_rows, :]
        ms = jnp.mean(x * x, axis=-1, keepdims=True)
        xb = (x * lax.rsqrt(ms + RMS_EPS) * nw_ref[...]).astype(BF16)
        for (out_ref, act), lo, hi in zip(outs, bounds[:-1], bounds[1:]):
            for c0 in range(lo, hi, col_chunk):
                y = jnp.dot(xb, w_ref[:, c0:c0 + col_chunk].astype(BF16), preferred_element_type=F32)
                if act is not None:
                    y = act(y)
                cols = slice(c0 - lo, c0 - lo + col_chunk)
                if out_ref is u_ref:
                    out_ref[r0 // S5_CHUNK:(r0 + sub_rows) // S5_CHUNK, :, cols] = y.astype(out_ref.dtype).reshape(
                        sub_rows // S5_CHUNK, S5_CHUNK, col_chunk)
                else:
                    out_ref[r0:r0 + sub_rows, cols] = y.astype(out_ref.dtype)


def _inproj(h2d, norm_w, w_in_bf, layer, widths, batch, seq, *, row_tile=512, sub_rows=256, col_chunk=512):
    rows, d = h2d.shape
    bounds = [0]
    for w in widths:
        bounds.append(bounds[-1] + w)
    body = functools.partial(_inproj_body, bounds=tuple(bounds), col_chunk=col_chunk, sub_rows=sub_rows)
    tile = lambda w: pl.BlockSpec((row_tile, w), lambda i: (i, 0))
    return pl.pallas_call(
        body,
        grid=(rows // row_tile,),
        in_specs=[tile(d), _resident_layer((norm_w.shape[0], 1, d), layer), _resident_layer(w_in_bf.shape, layer)],
        out_specs=[_chunk_major_spec(row_tile, seq, widths[0])] + [tile(w) for w in widths[1:]],
        out_shape=[jax.ShapeDtypeStruct((seq // S5_CHUNK, batch, S5_CHUNK, widths[0]), BF16)]
        + [jax.ShapeDtypeStruct((rows, w), BF16) for w in widths[1:]],
        compiler_params=pltpu.CompilerParams(dimension_semantics=("parallel",),
                                             vmem_limit_bytes=VMEM_LIMIT_BYTES),
    )(h2d, norm_w.reshape(-1, 1, d), w_in_bf)


def _s5_body(*refs, batch, chunks, pairs, pw, sl):
    t_steps = S5_CHUNK
    x_refs = refs[:t_steps]
    bb_ref, cc_ref, pin_ref, pout_ref, a_ref, sc_ref, dv_ref, y_ref, d_scr, st_scr = refs[t_steps:]
    k = t_steps * pw
    rows = batch * chunks
    assert k == 4 * sl

    assert pairs == 4 and pairs * pw == LANES and t_steps % pairs == 0

    def piece_transpose(v):
        lane = lax.broadcasted_iota(jnp.int32, v[0].shape, 1)
        upper, odd = lane >= 2 * pw, (lane // pw) % 2 == 1
        t0 = jnp.where(upper, pltpu.roll(v[2], 2 * pw, 1), v[0])
        t2 = jnp.where(upper, v[2], pltpu.roll(v[0], 2 * pw, 1))
        t1 = jnp.where(upper, pltpu.roll(v[3], 2 * pw, 1), v[1])
        t3 = jnp.where(upper, v[3], pltpu.roll(v[1], 2 * pw, 1))
        return (jnp.where(odd, pltpu.roll(t1, pw, 1), t0), jnp.where(odd, t1, pltpu.roll(t0, LANES - pw, 1)),
                jnp.where(odd, pltpu.roll(t3, pw, 1), t2), jnp.where(odd, t3, pltpu.roll(t2, LANES - pw, 1)))

    def pair_operand(fac_ref, pow_ref, direction, j):
        re, im = [], []
        for t in range(t_steps):
            for a in range(S5_GROUPS_PER_PAIR):
                g = S5_GROUPS_PER_PAIR * j + a
                fr, fi = fac_ref[2 * direction, g], fac_ref[2 * direction + 1, g]
                pr, pi = pow_ref[2 * direction, g, t:t + 1, :], pow_ref[2 * direction + 1, g, t:t + 1, :]
                re.append(pr * fr - pi * fi)
                im.append(pr * fi + pi * fr)
        return jnp.concatenate(re, axis=0), jnp.concatenate(im, axis=0)

    for j in range(pairs):
        d_scr[:, j * k:(j + 1) * k] = jnp.concatenate(
            [x_refs[t][:, j * pw:(j + 1) * pw] for t in range(t_steps)], axis=1)
        win = jnp.concatenate(pair_operand(bb_ref, pin_ref, 0, j) + pair_operand(bb_ref, pin_ref, 1, j),
                              axis=1).astype(BF16)
        st_scr[:, j * k:(j + 1) * k] = jnp.dot(d_scr[:, j * k:(j + 1) * k], win, preferred_element_type=F32)

    coef = [[jnp.broadcast_to(a_ref[j, c:c + 1, :], (batch, sl)) for c in range(4)] for j in range(pairs)]

    per_iter = 2
    assert chunks % per_iter == 0

    def step(i, carry):
        state = list(carry)
        rows_of, loaded = [], []
        for u in range(per_iter):
            c = i * per_iter + u
            rows_of.append((pl.ds(pl.multiple_of(c * batch, batch), batch),
                            pl.ds(pl.multiple_of((chunks - 1 - c) * batch, batch), batch)))
            loaded.append([st_scr[rows_of[u][col // 2 % 2], col * sl:(col + 1) * sl] for col in range(4 * pairs)])
        stores = []
        for u in range(per_iter):
            for j in range(pairs):
                for direction in range(2):
                    re, im = 4 * j + 2 * direction, 4 * j + 2 * direction + 1
                    sr, si = state[re], state[im]
                    ar, ai = coef[j][2 * direction], coef[j][2 * direction + 1]
                    stores += [(u, re, sr), (u, im, si)]
                    state[re] = ar * sr - ai * si + loaded[u][re]
                    state[im] = ar * si + ai * sr + loaded[u][im]
        for u, col, value in stores:
            st_scr[rows_of[u][col // 2 % 2], col * sl:(col + 1) * sl] = value
        return tuple(state)

    zero = jnp.zeros((batch, sl), F32)
    lax.fori_loop(0, chunks // per_iter, step, (zero,) * (4 * pairs))

    in_step = lax.broadcasted_iota(jnp.int32, (k, k), 0) // pw
    out_step = lax.broadcasted_iota(jnp.int32, (k, k), 1) // pw
    for j in range(pairs):
        sc = sc_ref[j]
        kern = None
        woutt = []
        for direction in range(2):
            wr, wi = pair_operand(bb_ref, pin_ref, direction, j)
            out_re, out_im = pair_operand(cc_ref, pout_ref, direction, j)
            woutt += [out_re.astype(BF16), (-out_im).astype(BF16)]
            pr, pi = sc[2 * direction:2 * direction + 1, :], sc[2 * direction + 1:2 * direction + 2, :]
            wp = jnp.concatenate([wr * pr - wi * pi, wr * pi + wi * pr], axis=1).astype(BF16)
            wt = jnp.concatenate(woutt[2 * direction:2 * direction + 2], axis=1)
            kd = lax.dot_general(wp, wt, _NT, preferred_element_type=F32)
            kd = jnp.where(out_step >= in_step if direction == 0 else out_step <= in_step, kd, 0.0)
            kern = kd if kern is None else kern + kd
        dj = d_scr[:, j * k:(j + 1) * k]
        wt = jnp.concatenate(woutt, axis=1)
        y = jnp.dot(dj, kern.astype(BF16), preferred_element_type=F32)
        y = y + lax.dot_general(st_scr[:, j * k:(j + 1) * k].astype(BF16), wt, _NT, preferred_element_type=F32)
        y = y + dj.astype(F32) * dv_ref[j]
        st_scr[:, j * k:(j + 1) * k] = y
    for m in range(t_steps // pairs):
        moved = piece_transpose([st_scr[:, j * k + m * LANES:j * k + (m + 1) * LANES] for j in range(pairs)])
        for i in range(pairs):
            y_ref[pl.ds(pairs * m + i, rows, stride=t_steps), :] = moved[i]


def _s5_operands(lam_re, lam_im, log_step, b_re, b_im, c_re, c_im, d_skip):
    _, groups, p = lam_re.shape
    hh, t, gp = S5_GROUP, S5_CHUNK, S5_GROUPS_PER_PAIR
    npair = groups // gp
    rep = lambda v: jnp.concatenate([v] * gp, axis=-1)
    own = (jnp.arange(gp * p)[None, :] // p == jnp.arange(groups)[:, None] % gp).astype(F32)
    lam_re, lam_im = rep(lam_re), rep(lam_im)
    dt = jnp.exp(log_step)[..., None]
    xr, xi = lam_re * dt, lam_im * dt
    lbr, lbi = jnp.exp(xr) * jnp.cos(xi), jnp.exp(xr) * jnp.sin(xi)
    nr, ni = lbr - 1.0, lbi
    den = lam_re * lam_re + lam_im * lam_im
    zr, zi = (nr * lam_re + ni * lam_im) / den * own, (ni * lam_re - nr * lam_im) / den * own
    btr, bti = rep(jnp.swapaxes(b_re, -1, -2)), rep(jnp.swapaxes(b_im, -1, -2))
    bbr = zr[:, :, None, :] * btr - zi[:, :, None, :] * bti
    bbi = zr[:, :, None, :] * bti + zi[:, :, None, :] * btr
    ccr, cci = rep(c_re) * own[None, :, None, :], rep(c_im) * own[None, :, None, :]

    def powers(e):
        kk = e[:, None, :, None]
        mag, ang = jnp.exp(kk * xr[:, :, None, :]), kk * xi[:, :, None, :]
        return mag * jnp.cos(ang), mag * jnp.sin(ang)

    components = lambda x_re, x_im: jnp.stack([x_re[0], x_im[0], x_re[1], x_im[1]])
    tt = jnp.arange(t, dtype=F32)
    pin = components(*powers(jnp.stack([t - 1 - tt, tt])))
    pout = components(*powers(jnp.stack([tt + 1, t - tt])))

    def pair_vec(e):
        mag, ang = jnp.exp(e * xr) * own, e * xi
        v = jnp.stack([(mag * jnp.cos(ang))[0], (mag * jnp.sin(ang))[0],
                       (mag * jnp.cos(ang))[1], (mag * jnp.sin(ang))[1]])
        return jnp.transpose(v.reshape(4, npair, gp, gp * p).sum(axis=2), (1, 0, 2))

    dvec = jnp.broadcast_to(d_skip.reshape(npair, 1, gp, hh), (npair, t, gp, hh)).reshape(npair, 1, t * gp * hh)
    return components(bbr, bbi), components(ccr, cci), pin, pout, pair_vec(float(t)), pair_vec(-float(t)), dvec


def _s5_scan(u_cm, operands, layer, batch, seq):
    bb, cc, pin, pout, a, sc, dvec = operands
    gp, t = S5_GROUPS_PER_PAIR, S5_CHUNK
    _, _, groups, hh, sl = bb.shape
    npair, pw = groups // gp, gp * hh
    k = t * pw
    pairs = LANES // pw
    chunks = seq // t
    rows = chunks * batch
    width = npair * pw
    nblk = width // LANES
    body = functools.partial(_s5_body, batch=batch, chunks=chunks, pairs=pairs, pw=pw, sl=sl)
    per_blk = lambda shape: pl.BlockSpec((None, pairs) + shape, lambda q: (layer, q, 0, 0))
    per_group = lambda group_rows: pl.BlockSpec((None, 4, gp * pairs, group_rows, sl), lambda q: (layer, 0, q, 0, 0))
    x2d = u_cm.reshape(rows, t * width)
    y = pl.pallas_call(
        body,
        grid=(nblk,),
        in_specs=[pl.BlockSpec((rows, LANES), functools.partial(lambda q, s: (0, s * nblk + q), s=s))
                  for s in range(t)]
        + [per_group(hh), per_group(hh), per_group(t), per_group(t), per_blk((4, sl)), per_blk((4, sl)),
           per_blk((1, k))],
        out_specs=pl.BlockSpec((rows * t, LANES), lambda q: (0, q)),
        out_shape=jax.ShapeDtypeStruct((rows * t, width), F32),
        scratch_shapes=[pltpu.VMEM((rows, pairs * k), BF16), pltpu.VMEM((rows, pairs * k), F32)],
        compiler_params=pltpu.CompilerParams(dimension_semantics=("parallel",),
                                             vmem_limit_bytes=VMEM_LIMIT_BYTES),
    )(*([x2d] * t), bb, cc, pin, pout, a, sc, dvec)
    return y.reshape(chunks, batch, t, width)


def _dft_tables(blk):
    n = 2 * blk
    f = np.arange(blk, dtype=np.int64)
    sym = ((2 * f[:, None] + 1) * (2 * f[None, :] + 1)) % (4 * n)
    ang_sym = sym.astype(np.float64) * (2.0 * np.pi / (4 * n))
    lag = np.arange(2 * blk, dtype=np.int64) - blk
    ang_lag = (((2 * f[:, None] + 1) * lag[None, :]) % (2 * n)).astype(np.float64) * (2.0 * np.pi / (2 * n))
    tc, ts = np.cos(ang_lag), np.sin(ang_lag)
    tc[:, 0] = 0.0
    ts[:, 0] = 0.0
    as_bf = lambda x: jnp.asarray(x.astype(np.float32)).astype(BF16)
    return as_bf(np.cos(ang_sym)), as_bf(np.sin(ang_sym)), as_bf(tc), as_bf(ts)


def _position_features(seq, pad_to):
    t = np.linspace(0.0, 1.0, seq)[:, None]
    pos = np.arange(seq, dtype=np.float64)[:, None]
    bands = np.linspace(1e-4, HY_BANDS - 1, HY_BANDS)[None, :]
    ang = bands * pos * (2.0 * math.pi / seq)
    feats = np.zeros((seq, pad_to), np.float64)
    feats[:, :1 + 2 * HY_BANDS] = np.concatenate([t, np.cos(ang), -np.sin(ang)], axis=-1)
    mirror = lambda x: np.roll(x[::-1], 1, axis=0)
    f32 = lambda x: jnp.asarray(x.astype(np.float32))
    return f32(feats), f32(t), f32(mirror(feats)), f32(mirror(t))


_HY_COMBO_TERMS = (
    ((0, 1),), ((-1, 1), (0, -1)), ((1, 1), (0, -1)),
    ((-2, 1), (0, -1)), ((-3, 1), (-1, -1), (-2, -1), (0, 1)), ((-1, 1), (1, -1), (-2, -1), (0, 1)),
    ((2, 1), (0, -1)), ((1, 1), (-1, -1), (2, -1), (0, 1)), ((3, 1), (1, -1), (2, -1), (0, 1)),
)


def _hy_filter_body(feats_ref, t_ref, featsm_ref, tm_ref, w1_ref, b1_ref, fr_ref, w2_ref, b2_ref, w3f_ref,
                    w3b_ref, decf_ref, decb_ref, tc_ref, ts_ref, gr_ref, gi_ref, h2_scr, h2m_scr, k2_scr,
                    kr_scr, ki_scr, *, seq, nb):
    hi = lax.Precision.HIGHEST
    blk = seq // nb

    @pl.when(pl.program_id(0) == 0)
    def _():
        fr = fr_ref[...]

        def mlp(feats):
            h1 = jnp.sin(fr * (jnp.dot(feats, w1_ref[...], precision=hi, preferred_element_type=F32) + b1_ref[...]))
            return jnp.sin(fr * (jnp.dot(h1, w2_ref[...], precision=hi, preferred_element_type=F32) + b2_ref[...]))

        def split(h):
            top = h.astype(BF16).astype(F32)
            return jnp.concatenate([top, h - top, top, h - top], axis=1).astype(BF16)

        h2_scr[...] = split(mlp(feats_ref[...]))
        h2m_scr[...] = split(mlp(featsm_ref[...]))

    def split_rows(w):
        top = w.astype(BF16).astype(F32)
        return jnp.concatenate([top, top, w - top, w - top], axis=0).astype(BF16)

    hf = jnp.dot(h2_scr[...], split_rows(w3f_ref[...]), preferred_element_type=F32)
    hf = hf * (jnp.exp(-t_ref[...] * jnp.abs(decf_ref[...])) + HY_SHIFT)
    hb = jnp.dot(h2m_scr[...], split_rows(w3b_ref[...]), preferred_element_type=F32)
    hb = hb * (jnp.exp(-tm_ref[...] * jnp.abs(decb_ref[...])) + HY_SHIFT)
    row = lax.broadcasted_iota(jnp.int32, hb.shape, 0)
    hb = jnp.where(row == 0, 0.0, hb)
    norm = jnp.sum(hf * hf, axis=0, keepdims=True) + jnp.sum(hb * hb, axis=0, keepdims=True)
    scale = lax.rsqrt(norm + HY_EPS) * (1.0 / blk)
    k2_scr[0:seq, :] = hb.astype(BF16)
    k2_scr[seq:2 * seq, :] = hf.astype(BF16)
    for d in range(2 * nb - 1):
        seg = k2_scr[d * blk:(d + 2) * blk, :]
        kr_scr[d] = jnp.dot(tc_ref[...], seg, preferred_element_type=F32) * scale
        ki_scr[d] = jnp.dot(ts_ref[...], seg, preferred_element_type=F32) * scale
    for idx, terms in enumerate(_HY_COMBO_TERMS):
        for src, dst in ((kr_scr, gr_ref), (ki_scr, gi_ref)):
            acc = None
            for lag, weight in terms:
                term = src[lag + nb - 1]
                acc = (term if weight > 0 else -term) if acc is None else (acc + term if weight > 0 else acc - term)
            dst[0, idx] = acc


def _hy_filters(seq, w1, b1, freq, w2, b2, w3, decay, width, tables, nb, *, col_block=256):
    _, _, tc, ts = tables
    blk = seq // nb
    ffn = w2.shape[0]
    feats, t, featsm, tm = _position_features(seq, LANES)
    w1p = jnp.zeros((LANES, ffn), F32).at[:w1.shape[0]].set(w1)
    nblk = width // col_block
    side = lambda rows, direction: pl.BlockSpec(
        (rows, col_block), lambda i: (0, (2 * (i // nblk) + direction) * nblk + i % nblk))
    out = pl.BlockSpec((1, HY_COMBOS, blk, col_block), lambda i: (i // nblk, 0, 0, i % nblk))
    body = functools.partial(_hy_filter_body, seq=seq, nb=nb)
    row2 = lambda v: v.reshape(1, -1)
    return pl.pallas_call(
        body,
        grid=(HY_ORDER * nblk,),
        in_specs=[_resident(feats.shape), _resident(t.shape), _resident(feats.shape), _resident(t.shape),
                  _resident(w1p.shape), _resident((1, ffn)), _resident((1, ffn)), _resident(w2.shape),
                  _resident((1, ffn)), side(ffn, 0), side(ffn, 1), side(1, 0), side(1, 1),
                  _resident(tc.shape), _resident(ts.shape)],
        out_specs=[out, out],
        out_shape=[jax.ShapeDtypeStruct((HY_ORDER, HY_COMBOS, blk, width), F32)] * 2,
        scratch_shapes=[pltpu.VMEM((seq, 4 * ffn), BF16), pltpu.VMEM((seq, 4 * ffn), BF16),
                        pltpu.VMEM((2 * seq, col_block), BF16),
                        pltpu.VMEM((2 * nb - 1, blk, col_block), F32), pltpu.VMEM((2 * nb - 1, blk, col_block), F32)],
        compiler_params=pltpu.CompilerParams(dimension_semantics=("arbitrary",),
                                             vmem_limit_bytes=VMEM_LIMIT_BYTES),
    )(feats, t, featsm, tm, w1p, row2(b1), row2(freq), w2, row2(b2), w3, w3, row2(decay), row2(decay), tc, ts)


def _hy_conv_body(pv_ref, p1_ref, p2_ref, g_ref, cw_ref, cb_ref, gr_ref, gi_ref, d_ref, mc_ref, ms_ref, o_ref,
                  z_scr, a_scr, b_scr, xs_scr, *, seq, nb, cw, row_chunk, problems):
    blk = seq // nb
    gate_refs = (p1_ref, p2_ref)
    lanes = lambda j: slice(j * cw, (j + 1) * cw)
    cadd = lambda x, y: (x[0] + y[0], x[1] + y[1])
    cmul = lambda x, y: (x[0] * y[0] - x[1] * y[1], x[0] * y[1] + x[1] * y[0])

    def short_conv(k, p_ref, part):
        x = p_ref[k].astype(F32)
        pad = xs_scr.at[k, part]
        pad[0:SUBLANES, :] = jnp.zeros((SUBLANES, cw), F32)
        pad[SUBLANES + seq:2 * SUBLANES + seq, :] = jnp.zeros((SUBLANES, cw), F32)
        pad[SUBLANES:SUBLANES + seq, :] = x
        prev, nxt = pad[pl.ds(SUBLANES - 1, seq), :], pad[pl.ds(SUBLANES + 1, seq), :]
        w = cw_ref[:, part, :]
        return cb_ref[part:part + 1, :] + prev * w[0:1] + x * w[1:2] + nxt * w[2:3]

    def load_v(k):
        v = short_conv(k, pv_ref, 0)
        for j in range(nb):
            z_scr[k, :, lanes(j)] = v[j * blk:(j + 1) * blk, :]

    def forward(k, o):
        zb = z_scr[k].astype(BF16)
        a_scr[k] = jnp.dot(mc_ref[...], zb, preferred_element_type=F32)
        b_scr[k] = jnp.dot(ms_ref[...], zb, preferred_element_type=F32)

    def pointwise(k, o):
        def toeplitz2(base, x0, x1, r):
            g = lambda idx: (gr_ref[o, idx, r, :], gi_ref[o, idx, r, :])
            q = cmul(g(base), cadd(x0, x1))
            return cadd(q, cmul(g(base + 1), x1)), cadd(q, cmul(g(base + 2), x0))

        for c in range(blk // row_chunk):
            r = slice(c * row_chunk, (c + 1) * row_chunk)
            z = [(a_scr[k, r, lanes(j)], b_scr[k, r, lanes(j)]) for j in range(nb)]
            p1 = toeplitz2(0, cadd(z[0], z[2]), cadd(z[1], z[3]), r)
            p2 = toeplitz2(3, z[2], z[3], r)
            p3 = toeplitz2(6, z[0], z[1], r)
            for i, y in enumerate((cadd(p1[0], p2[0]), cadd(p1[1], p2[1]), cadd(p1[0], p3[0]), cadd(p1[1], p3[1]))):
                a_scr[k, r, lanes(i)], b_scr[k, r, lanes(i)] = y

    def inverse(k, o):
        conv = (jnp.dot(mc_ref[...], a_scr[k].astype(BF16), preferred_element_type=F32)
                + jnp.dot(ms_ref[...], b_scr[k].astype(BF16), preferred_element_type=F32))
        gate = short_conv(k, gate_refs[o], o + 1)
        dsk = d_ref[o:o + 1, :]
        for j in range(nb):
            rows = slice(j * blk, (j + 1) * blk)
            z = gate[rows, :] * (conv[:, lanes(j)] + dsk * z_scr[k, :, lanes(j)])
            if o == HY_ORDER - 1:
                o_ref[k, rows, :] = (z * g_ref[k, rows, :].astype(F32)).astype(o_ref.dtype)
            else:
                z_scr[k, :, lanes(j)] = z

    stages = [load_v]
    for o in range(HY_ORDER):
        stages += [functools.partial(forward, o=o), functools.partial(pointwise, o=o),
                   functools.partial(inverse, o=o)]
    for s in range(len(stages) + problems - 1):
        for k in range(problems):
            if 0 <= s - k < len(stages):
                stages[s - k](k)


def _hy_conv(p_hy, gate_silu, conv_w, conv_b, gr, gi, d_skip, tables, batch, seq, width, nb, *, col_block=LANES,
             row_chunk=SUBLANES, problems=2):
    assert nb == HY_TIME_BLOCKS
    mc, ms, _, _ = tables
    blk = seq // nb
    nblk = width // col_block
    part = lambda k: pl.BlockSpec((problems, seq, col_block), lambda c, b: (b, 0, k * nblk + c))
    cw = conv_w.reshape(conv_w.shape[0], HY_ORDER + 1, width)
    cb = conv_b.reshape(HY_ORDER + 1, width)
    filt = pl.BlockSpec((HY_ORDER, HY_COMBOS, blk, col_block), lambda c, b: (0, 0, 0, c))
    body = functools.partial(_hy_conv_body, seq=seq, nb=nb, cw=col_block, row_chunk=row_chunk, problems=problems)
    wide = nb * col_block
    scratch = lambda dtype: pltpu.VMEM((problems, blk, wide), dtype)
    return pl.pallas_call(
        body,
        grid=(nblk, batch // problems),
        in_specs=[part(0), part(1), part(2),
                  pl.BlockSpec((problems, seq, col_block), lambda c, b: (b, 0, c)),
                  pl.BlockSpec((cw.shape[0], HY_ORDER + 1, col_block), lambda c, b: (0, 0, c)),
                  pl.BlockSpec((HY_ORDER + 1, col_block), lambda c, b: (0, c)),
                  filt, filt,
                  pl.BlockSpec((HY_ORDER, col_block), lambda c, b: (0, c)),
                  _resident(mc.shape), _resident(ms.shape)],
        out_specs=pl.BlockSpec((problems, seq, col_block), lambda c, b: (b, 0, c)),
        out_shape=jax.ShapeDtypeStruct((batch, seq, width), BF16),
        scratch_shapes=[scratch(F32), scratch(F32), scratch(F32),
                        pltpu.VMEM((problems, HY_ORDER + 1, seq + 2 * SUBLANES, col_block), F32)],
        compiler_params=pltpu.CompilerParams(dimension_semantics=("parallel", "parallel"),
                                             vmem_limit_bytes=VMEM_LIMIT_BYTES),
    )(p_hy, p_hy, p_hy, gate_silu, cw, cb, gr, gi, d_skip, mc, ms)


def _merge_body(ys_ref, gs_ref, yh_ref, m_ref, h_ref, wg_ref, bg_ref, wbs_ref, wbh_ref, wo_ref, fw_ref, o_ref,
                *, d_model, final_norm):
    ys = ys_ref[...]
    y = _gelu_tanh(ys.reshape(ys.shape[0] * ys.shape[1], ys.shape[2]))
    glu = y * _sigmoid(jnp.dot(y.astype(BF16), wg_ref[...], preferred_element_type=F32) + bg_ref[...])
    s5 = (glu * gs_ref[...].astype(F32)).astype(BF16)
    y_s5 = jnp.dot(s5, wbs_ref[...], preferred_element_type=F32)
    y_hy = jnp.dot(yh_ref[...], wbh_ref[...], preferred_element_type=F32)
    merged = m_ref[:, :d_model].astype(F32) * y_s5 + m_ref[:, d_model:].astype(F32) * y_hy
    h = h_ref[...] + jnp.dot(merged.astype(BF16), wo_ref[...], preferred_element_type=F32)
    if final_norm:
        ms = jnp.mean(h * h, axis=-1, keepdims=True)
        h = h * lax.rsqrt(ms + RMS_EPS) * fw_ref[...]
    o_ref[...] = h


def _merge(y_s5_cm, gs, y_hypre, m, h2d, w_glu, b_glu, w_bs, w_bh, w_out, layer, final_w, final_norm, seq, *,
           row_tile=1024):
    rows, d = h2d.shape
    sw = y_s5_cm.shape[-1]
    hw = y_hypre.shape[1]
    tile = lambda w: pl.BlockSpec((row_tile, w), lambda i: (i, 0))
    body = functools.partial(_merge_body, d_model=d, final_norm=final_norm)
    return pl.pallas_call(
        body,
        grid=(rows // row_tile,),
        in_specs=[_chunk_major_spec(row_tile, seq, sw), tile(sw), tile(hw), tile(2 * d), tile(d),
                  _resident_layer(w_glu.shape, layer), _resident_layer((b_glu.shape[0], 1, sw), layer),
                  _resident_layer(w_bs.shape, layer), _resident_layer(w_bh.shape, layer),
                  _resident_layer(w_out.shape, layer), _resident((1, d))],
        out_specs=tile(d),
        out_shape=jax.ShapeDtypeStruct((rows, d), F32),
        compiler_params=pltpu.CompilerParams(dimension_semantics=("parallel",),
                                             vmem_limit_bytes=VMEM_LIMIT_BYTES),
    )(y_s5_cm, gs, y_hypre, m, h2d, w_glu, b_glu.reshape(-1, 1, sw), w_bs, w_bh, w_out, final_w.reshape(1, d))


def kernel(x, norm_w, w_in, s5_lam_re, s5_lam_im, s5_log_step, s5_b_re, s5_b_im, s5_c_re, s5_c_im, s5_d, s5_w_glu, s5_b_glu, hy_conv_w, hy_conv_b, hy_w1, hy_b1, hy_freq, hy_w2, hy_b2, hy_w3, hy_decay, hy_d, w_branch_s5, w_branch_hy, w_out, final_norm_w):
    batch, seq, d_model = x.shape
    depth = w_in.shape[0]
    s5_width = s5_d.shape[1]
    hy_width = hy_d.shape[2]
    widths = (s5_width, s5_width, (HY_ORDER + 1) * hy_width, hy_width, 2 * d_model)
    nb = HY_TIME_BLOCKS
    tables = _dft_tables(seq // nb)
    h = x.reshape(batch * seq, d_model)
    w_in_bf, w_glu_bf, w_bs_bf, w_bh_bf, w_out_bf = (w.astype(BF16) for w in (w_in, s5_w_glu, w_branch_s5,
                                                                              w_branch_hy, w_out))
    s5_ops = jax.vmap(_s5_operands)(s5_lam_re, s5_lam_im, s5_log_step, s5_b_re, s5_b_im, s5_c_re, s5_c_im, s5_d)
    for l in range(depth):
        u, gs, p_hy, gh, m = _inproj(h, norm_w, w_in, l, widths, batch, seq)
        y_s5 = _s5_scan(u, s5_ops, l, batch, seq)
        gr, gi = _hy_filters(seq, hy_w1[l], hy_b1[l], hy_freq[l], hy_w2[l], hy_b2[l], hy_w3[l], hy_decay[l],
                             hy_width, tables, nb)
        y_hypre = _hy_conv(p_hy.reshape(batch, seq, -1), gh.reshape(batch, seq, -1), hy_conv_w[l], hy_conv_b[l],
                           gr, gi, hy_d[l], tables, batch, seq, hy_width, nb)
        h = _merge(y_s5, gs, y_hypre.reshape(batch * seq, hy_width), m, h, w_glu_bf, s5_b_glu, w_bs_bf, w_bh_bf,
                   w_out_bf, l, final_norm_w, l == depth - 1, seq)
    return h.reshape(batch, seq, d_model)
```

```python
import functools
import math

import numpy as np
import jax
import jax.numpy as jnp
from jax import lax
from jax.experimental import pallas as pl
from jax.experimental.pallas import tpu as pltpu

F32 = jnp.float32
BF16 = jnp.bfloat16

RMS_EPS = 1e-6
S5_GROUP = 16
S5_CHUNK = 16
S5_GROUPS_PER_PAIR = 2
LANES = 128
SUBLANES = 8
HY_ORDER = 2
HY_BANDS = 16
HY_SHIFT = 0.05
HY_EPS = 1e-6
HY_TIME_BLOCKS = 4
HY_LAGS = 2 * HY_TIME_BLOCKS - 1
HY_COMBOS = 9
VMEM_LIMIT_BYTES = 56 * 1024 * 1024

_NT = (((1,), (1,)), ((), ()))


def _sigmoid(x):
    return 1.0 / (1.0 + jnp.exp2(x * (-1.0 / math.log(2.0))))


def _silu(x):
    return x * _sigmoid(x)


def _gelu_tanh(x):
    return 0.5 * x * (1.0 + jnp.tanh(math.sqrt(2.0 / math.pi) * (x + 0.044715 * (x * x * x))))


def _resident(shape):
    zeros = (0,) * len(shape)
    return pl.BlockSpec(shape, lambda *_: zeros, pipeline_mode=pl.Buffered(1))


def _resident_layer(stacked_shape, layer):
    zeros = (0,) * (len(stacked_shape) - 1)
    return pl.BlockSpec((None,) + tuple(stacked_shape[1:]), lambda *_: (layer,) + zeros,
                        pipeline_mode=pl.Buffered(1))


def _chunk_major_spec(row_tile, seq, width):
    tiles_per_seq = seq // row_tile
    return pl.BlockSpec((row_tile // S5_CHUNK, None, S5_CHUNK, width),
                        lambda i: (i % tiles_per_seq, i // tiles_per_seq, 0, 0))


def _inproj_body(x_ref, nw_ref, w_ref, u_ref, gs_ref, p_ref, gh_ref, m_ref, *, bounds, col_chunk, sub_rows):
    outs = ((u_ref, None), (gs_ref, _silu), (p_ref, None), (gh_ref, _silu), (m_ref, _sigmoid))
    for r0 in range(0, x_ref.shape[0], sub_rows):
        x = x_ref[r0:r0 + sub_rows, :]
        ms = jnp.mean(x * x, axis=-1, keepdims=True)
        xb = (x * lax.rsqrt(ms + RMS_EPS) * nw_ref[...]).astype(BF16)
        for (out_ref, act), lo, hi in zip(outs, bounds[:-1], bounds[1:]):
            for c0 in range(lo, hi, col_chunk):
                y = jnp.dot(xb, w_ref[:, c0:c0 + col_chunk].astype(BF16), preferred_element_type=F32)
                if act is not None:
                    y = act(y)
                cols = slice(c0 - lo, c0 - lo + col_chunk)
                if out_ref is u_ref:
                    out_ref[r0 // S5_CHUNK:(r0 + sub_rows) // S5_CHUNK, :, cols] = y.astype(out_ref.dtype).reshape(
                        sub_rows // S5_CHUNK, S5_CHUNK, col_chunk)
                else:
                    out_ref[r0:r0 + sub_rows, cols] = y.astype(out_ref.dtype)


def _inproj(h2d, norm_w, w_in, layer, widths, batch, seq, *, row_tile=512, sub_rows=256, col_chunk=512):
    rows, d = h2d.shape
    bounds = [0]
    for w in widths:
        bounds.append(bounds[-1] + w)
    body = functools.partial(_inproj_body, bounds=tuple(bounds), col_chunk=col_chunk, sub_rows=sub_rows)
    tile = lambda w: pl.BlockSpec((row_tile, w), lambda i: (i, 0))
    return pl.pallas_call(
        body,
        grid=(rows // row_tile,),
        in_specs=[tile(d), _resident_layer((norm_w.shape[0], 1, d), layer), _resident_layer(w_in.shape, layer)],
        out_specs=[_chunk_major_spec(row_tile, seq, widths[0])] + [tile(w) for w in widths[1:]],
        out_shape=[jax.ShapeDtypeStruct((seq // S5_CHUNK, batch, S5_CHUNK, widths[0]), BF16)]
        + [jax.ShapeDtypeStruct((rows, w), BF16) for w in widths[1:]],
        compiler_params=pltpu.CompilerParams(dimension_semantics=("parallel",),
                                             vmem_limit_bytes=VMEM_LIMIT_BYTES),
    )(h2d, norm_w.reshape(-1, 1, d), w_in)


def _s5_body(*refs, batch, chunks, pairs, pw, sl):
    t_steps = S5_CHUNK
    x_refs = refs[:t_steps]
    bb_ref, cc_ref, pin_ref, pout_ref, a_ref, sc_ref, dv_ref, y_ref, d_scr, st_scr = refs[t_steps:]
    k = t_steps * pw
    rows = batch * chunks
    assert k == 4 * sl

    assert pairs == 4 and pairs * pw == LANES and t_steps % pairs == 0

    def piece_transpose(v):
        lane = lax.broadcasted_iota(jnp.int32, v[0].shape, 1)
        upper, odd = lane >= 2 * pw, (lane // pw) % 2 == 1
        t0 = jnp.where(upper, pltpu.roll(v[2], 2 * pw, 1), v[0])
        t2 = jnp.where(upper, v[2], pltpu.roll(v[0], 2 * pw, 1))
        t1 = jnp.where(upper, pltpu.roll(v[3], 2 * pw, 1), v[1])
        t3 = jnp.where(upper, v[3], pltpu.roll(v[1], 2 * pw, 1))
        return (jnp.where(odd, pltpu.roll(t1, pw, 1), t0), jnp.where(odd, t1, pltpu.roll(t0, LANES - pw, 1)),
                jnp.where(odd, pltpu.roll(t3, pw, 1), t2), jnp.where(odd, t3, pltpu.roll(t2, LANES - pw, 1)))

    def pair_operand(fac_ref, pow_ref, direction, j):
        re, im = [], []
        for t in range(t_steps):
            for a in range(S5_GROUPS_PER_PAIR):
                g = S5_GROUPS_PER_PAIR * j + a
                fr, fi = fac_ref[2 * direction, g], fac_ref[2 * direction + 1, g]
                pr, pi = pow_ref[2 * direction, g, t:t + 1, :], pow_ref[2 * direction + 1, g, t:t + 1, :]
                re.append(pr * fr - pi * fi)
                im.append(pr * fi + pi * fr)
        return jnp.concatenate(re, axis=0), jnp.concatenate(im, axis=0)

    for j in range(pairs):
        d_scr[:, j * k:(j + 1) * k] = jnp.concatenate(
            [x_refs[t][:, j * pw:(j + 1) * pw] for t in range(t_steps)], axis=1)
        win = jnp.concatenate(pair_operand(bb_ref, pin_ref, 0, j) + pair_operand(bb_ref, pin_ref, 1, j),
                              axis=1).astype(BF16)
        st_scr[:, j * k:(j + 1) * k] = jnp.dot(d_scr[:, j * k:(j + 1) * k], win, preferred_element_type=F32)

    coef = [[jnp.broadcast_to(a_ref[j, c:c + 1, :], (batch, sl)) for c in range(4)] for j in range(pairs)]

    per_iter = 2
    assert chunks % per_iter == 0

    def step(i, carry):
        state = list(carry)
        rows_of, loaded = [], []
        for u in range(per_iter):
            c = i * per_iter + u
            rows_of.append((pl.ds(pl.multiple_of(c * batch, batch), batch),
                            pl.ds(pl.multiple_of((chunks - 1 - c) * batch, batch), batch)))
            loaded.append([st_scr[rows_of[u][col // 2 % 2], col * sl:(col + 1) * sl] for col in range(4 * pairs)])
        stores = []
        for u in range(per_iter):
            for j in range(pairs):
                for direction in range(2):
                    re, im = 4 * j + 2 * direction, 4 * j + 2 * direction + 1
                    sr, si = state[re], state[im]
                    ar, ai = coef[j][2 * direction], coef[j][2 * direction + 1]
                    stores += [(u, re, sr), (u, im, si)]
                    state[re] = ar * sr - ai * si + loaded[u][re]
                    state[im] = ar * si + ai * sr + loaded[u][im]
        for u, col, value in stores:
            st_scr[rows_of[u][col // 2 % 2], col * sl:(col + 1) * sl] = value
        return tuple(state)

    zero = jnp.zeros((batch, sl), F32)
    lax.fori_loop(0, chunks // per_iter, step, (zero,) * (4 * pairs))

    in_step = lax.broadcasted_iota(jnp.int32, (k, k), 0) // pw
    out_step = lax.broadcasted_iota(jnp.int32, (k, k), 1) // pw
    for j in range(pairs):
        sc = sc_ref[j]
        kern = None
        woutt = []
        for direction in range(2):
            wr, wi = pair_operand(bb_ref, pin_ref, direction, j)
            out_re, out_im = pair_operand(cc_ref, pout_ref, direction, j)
            woutt += [out_re.astype(BF16), (-out_im).astype(BF16)]
            pr, pi = sc[2 * direction:2 * direction + 1, :], sc[2 * direction + 1:2 * direction + 2, :]
            wp = jnp.concatenate([wr * pr - wi * pi, wr * pi + wi * pr], axis=1).astype(BF16)
            wt = jnp.concatenate(woutt[2 * direction:2 * direction + 2], axis=1)
            kd = lax.dot_general(wp, wt, _NT, preferred_element_type=F32)
            kd = jnp.where(out_step >= in_step if direction == 0 else out_step <= in_step, kd, 0.0)
            kern = kd if kern is None else kern + kd
        dj = d_scr[:, j * k:(j + 1) * k]
        wt = jnp.concatenate(woutt, axis=1)
        y = jnp.dot(dj, kern.astype(BF16), preferred_element_type=F32)
        y = y + lax.dot_general(st_scr[:, j * k:(j + 1) * k].astype(BF16), wt, _NT, preferred_element_type=F32)
        y = y + dj.astype(F32) * dv_ref[j]
        st_scr[:, j * k:(j + 1) * k] = y
    for m in range(t_steps // pairs):
        moved = piece_transpose([st_scr[:, j * k + m * LANES:j * k + (m + 1) * LANES] for j in range(pairs)])
        for i in range(pairs):
            y_ref[pl.ds(pairs * m + i, rows, stride=t_steps), :] = moved[i]


def _s5_operands(lam_re, lam_im, log_step, b_re, b_im, c_re, c_im, d_skip):
    _, groups, p = lam_re.shape
    hh, t, gp = S5_GROUP, S5_CHUNK, S5_GROUPS_PER_PAIR
    npair = groups // gp
    rep = lambda v: jnp.concatenate([v] * gp, axis=-1)
    own = (jnp.arange(gp * p)[None, :] // p == jnp.arange(groups)[:, None] % gp).astype(F32)
    lam_re, lam_im = rep(lam_re), rep(lam_im)
    dt = jnp.exp(log_step)[..., None]
    xr, xi = lam_re * dt, lam_im * dt
    lbr, lbi = jnp.exp(xr) * jnp.cos(xi), jnp.exp(xr) * jnp.sin(xi)
    nr, ni = lbr - 1.0, lbi
    den = lam_re * lam_re + lam_im * lam_im
    zr, zi = (nr * lam_re + ni * lam_im) / den * own, (ni * lam_re - nr * lam_im) / den * own
    btr, bti = rep(jnp.swapaxes(b_re, -1, -2)), rep(jnp.swapaxes(b_im, -1, -2))
    bbr = zr[:, :, None, :] * btr - zi[:, :, None, :] * bti
    bbi = zr[:, :, None, :] * bti + zi[:, :, None, :] * btr
    ccr, cci = rep(c_re) * own[None, :, None, :], rep(c_im) * own[None, :, None, :]

    def powers(e):
        kk = e[:, None, :, None]
        mag, ang = jnp.exp(kk * xr[:, :, None, :]), kk * xi[:, :, None, :]
        return mag * jnp.cos(ang), mag * jnp.sin(ang)

    components = lambda x_re, x_im: jnp.stack([x_re[0], x_im[0], x_re[1], x_im[1]])
    tt = jnp.arange(t, dtype=F32)
    pin = components(*powers(jnp.stack([t - 1 - tt, tt])))
    pout = components(*powers(jnp.stack([tt + 1, t - tt])))

    def pair_vec(e):
        mag, ang = jnp.exp(e * xr) * own, e * xi
        v = jnp.stack([(mag * jnp.cos(ang))[0], (mag * jnp.sin(ang))[0],
                       (mag * jnp.cos(ang))[1], (mag * jnp.sin(ang))[1]])
        return jnp.transpose(v.reshape(4, npair, gp, gp * p).sum(axis=2), (1, 0, 2))

    dvec = jnp.broadcast_to(d_skip.reshape(npair, 1, gp, hh), (npair, t, gp, hh)).reshape(npair, 1, t * gp * hh)
    return components(bbr, bbi), components(ccr, cci), pin, pout, pair_vec(float(t)), pair_vec(-float(t)), dvec


def _s5_scan(u_cm, operands, layer, batch, seq):
    bb, cc, pin, pout, a, sc, dvec = operands
    gp, t = S5_GROUPS_PER_PAIR, S5_CHUNK
    _, _, groups, hh, sl = bb.shape
    npair, pw = groups // gp, gp * hh
    k = t * pw
    pairs = LANES // pw
    chunks = seq // t
    rows = chunks * batch
    width = npair * pw
    nblk = width // LANES
    body = functools.partial(_s5_body, batch=batch, chunks=chunks, pairs=pairs, pw=pw, sl=sl)
    per_blk = lambda shape: pl.BlockSpec((None, pairs) + shape, lambda q: (layer, q, 0, 0))
    per_group = lambda group_rows: pl.BlockSpec((None, 4, gp * pairs, group_rows, sl), lambda q: (layer, 0, q, 0, 0))
    x2d = u_cm.reshape(rows, t * width)
    y = pl.pallas_call(
        body,
        grid=(nblk,),
        in_specs=[pl.BlockSpec((rows, LANES), functools.partial(lambda q, s: (0, s * nblk + q), s=s))
                  for s in range(t)]
        + [per_group(hh), per_group(hh), per_group(t), per_group(t), per_blk((4, sl)), per_blk((4, sl)),
           per_blk((1, k))],
        out_specs=pl.BlockSpec((rows * t, LANES), lambda q: (0, q)),
        out_shape=jax.ShapeDtypeStruct((rows * t, width), F32),
        scratch_shapes=[pltpu.VMEM((rows, pairs * k), BF16), pltpu.VMEM((rows, pairs * k), F32)],
        compiler_params=pltpu.CompilerParams(dimension_semantics=("parallel",),
                                             vmem_limit_bytes=VMEM_LIMIT_BYTES),
    )(*([x2d] * t), bb, cc, pin, pout, a, sc, dvec)
    return y.reshape(chunks, batch, t, width)


def _dft_tables(blk):
    n = 2 * blk
    f = np.arange(blk, dtype=np.int64)
    sym = ((2 * f[:, None] + 1) * (2 * f[None, :] + 1)) % (4 * n)
    ang_sym = sym.astype(np.float64) * (2.0 * np.pi / (4 * n))
    lag = np.arange(2 * blk, dtype=np.int64) - blk
    ang_lag = (((2 * f[:, None] + 1) * lag[None, :]) % (2 * n)).astype(np.float64) * (2.0 * np.pi / (2 * n))
    tc, ts = np.cos(ang_lag), np.sin(ang_lag)
    tc[:, 0] = 0.0
    ts[:, 0] = 0.0
    as_bf = lambda x: jnp.asarray(x.astype(np.float32)).astype(BF16)
    return as_bf(np.cos(ang_sym)), as_bf(np.sin(ang_sym)), as_bf(tc), as_bf(ts)


def _position_features(seq, pad_to):
    t = np.linspace(0.0, 1.0, seq)[:, None]
    pos = np.arange(seq, dtype=np.float64)[:, None]
    bands = np.linspace(1e-4, HY_BANDS - 1, HY_BANDS)[None, :]
    ang = bands * pos * (2.0 * math.pi / seq)
    feats = np.zeros((seq, pad_to), np.float64)
    feats[:, :1 + 2 * HY_BANDS] = np.concatenate([t, np.cos(ang), -np.sin(ang)], axis=-1)
    mirror = lambda x: np.roll(x[::-1], 1, axis=0)
    f32 = lambda x: jnp.asarray(x.astype(np.float32))
    return f32(feats), f32(t), f32(mirror(feats)), f32(mirror(t))


_HY_COMBO_TERMS = (
    ((0, 1),), ((-1, 1), (0, -1)), ((1, 1), (0, -1)),
    ((-2, 1), (0, -1)), ((-3, 1), (-1, -1), (-2, -1), (0, 1)), ((-1, 1), (1, -1), (-2, -1), (0, 1)),
    ((2, 1), (0, -1)), ((1, 1), (-1, -1), (2, -1), (0, 1)), ((3, 1), (1, -1), (2, -1), (0, 1)),
)


def _hy_filter_body(feats_ref, t_ref, featsm_ref, tm_ref, w1_ref, b1_ref, fr_ref, w2_ref, b2_ref, w3f_ref,
                    w3b_ref, decf_ref, decb_ref, tc_ref, ts_ref, gr_ref, gi_ref, h2_scr, h2m_scr, k2_scr,
                    kr_scr, ki_scr, *, seq, nb):
    hi = lax.Precision.HIGHEST
    blk = seq // nb

    @pl.when(pl.program_id(0) == 0)
    def _():
        fr = fr_ref[...]

        def mlp(feats):
            h1 = jnp.sin(fr * (jnp.dot(feats, w1_ref[...], precision=hi, preferred_element_type=F32) + b1_ref[...]))
            return jnp.sin(fr * (jnp.dot(h1, w2_ref[...], precision=hi, preferred_element_type=F32) + b2_ref[...]))

        def split(h):
            top = h.astype(BF16).astype(F32)
            return jnp.concatenate([top, h - top, top, h - top], axis=1).astype(BF16)

        h2_scr[...] = split(mlp(feats_ref[...]))
        h2m_scr[...] = split(mlp(featsm_ref[...]))

    def split_rows(w):
        top = w.astype(BF16).astype(F32)
        return jnp.concatenate([top, top, w - top, w - top], axis=0).astype(BF16)

    hf = jnp.dot(h2_scr[...], split_rows(w3f_ref[...]), preferred_element_type=F32)
    hf = hf * (jnp.exp(-t_ref[...] * jnp.abs(decf_ref[...])) + HY_SHIFT)
    hb = jnp.dot(h2m_scr[...], split_rows(w3b_ref[...]), preferred_element_type=F32)
    hb = hb * (jnp.exp(-tm_ref[...] * jnp.abs(decb_ref[...])) + HY_SHIFT)
    row = lax.broadcasted_iota(jnp.int32, hb.shape, 0)
    hb = jnp.where(row == 0, 0.0, hb)
    norm = jnp.sum(hf * hf, axis=0, keepdims=True) + jnp.sum(hb * hb, axis=0, keepdims=True)
    scale = lax.rsqrt(norm + HY_EPS) * (1.0 / blk)
    k2_scr[0:seq, :] = hb.astype(BF16)
    k2_scr[seq:2 * seq, :] = hf.astype(BF16)
    for d in range(2 * nb - 1):
        seg = k2_scr[d * blk:(d + 2) * blk, :]
        kr_scr[d] = jnp.dot(tc_ref[...], seg, preferred_element_type=F32) * scale
        ki_scr[d] = jnp.dot(ts_ref[...], seg, preferred_element_type=F32) * scale
    for idx, terms in enumerate(_HY_COMBO_TERMS):
        for src, dst in ((kr_scr, gr_ref), (ki_scr, gi_ref)):
            acc = None
            for lag, weight in terms:
                term = src[lag + nb - 1]
                acc = (term if weight > 0 else -term) if acc is None else (acc + term if weight > 0 else acc - term)
            dst[0, idx] = acc


def _hy_filters(seq, w1, b1, freq, w2, b2, w3, decay, width, tables, nb, *, col_block=256):
    _, _, tc, ts = tables
    blk = seq // nb
    ffn = w2.shape[0]
    feats, t, featsm, tm = _position_features(seq, LANES)
    w1p = jnp.zeros((LANES, ffn), F32).at[:w1.shape[0]].set(w1)
    nblk = width // col_block
    side = lambda rows, direction: pl.BlockSpec(
        (rows, col_block), lambda i: (0, (2 * (i // nblk) + direction) * nblk + i % nblk))
    out = pl.BlockSpec((1, HY_COMBOS, blk, col_block), lambda i: (i // nblk, 0, 0, i % nblk))
    body = functools.partial(_hy_filter_body, seq=seq, nb=nb)
    row2 = lambda v: v.reshape(1, -1)
    return pl.pallas_call(
        body,
        grid=(HY_ORDER * nblk,),
        in_specs=[_resident(feats.shape), _resident(t.shape), _resident(feats.shape), _resident(t.shape),
                  _resident(w1p.shape), _resident((1, ffn)), _resident((1, ffn)), _resident(w2.shape),
                  _resident((1, ffn)), side(ffn, 0), side(ffn, 1), side(1, 0), side(1, 1),
                  _resident(tc.shape), _resident(ts.shape)],
        out_specs=[out, out],
        out_shape=[jax.ShapeDtypeStruct((HY_ORDER, HY_COMBOS, blk, width), F32)] * 2,
        scratch_shapes=[pltpu.VMEM((seq, 4 * ffn), BF16), pltpu.VMEM((seq, 4 * ffn), BF16),
                        pltpu.VMEM((2 * seq, col_block), BF16),
                        pltpu.VMEM((2 * nb - 1, blk, col_block), F32), pltpu.VMEM((2 * nb - 1, blk, col_block), F32)],
        compiler_params=pltpu.CompilerParams(dimension_semantics=("arbitrary",),
                                             vmem_limit_bytes=VMEM_LIMIT_BYTES),
    )(feats, t, featsm, tm, w1p, row2(b1), row2(freq), w2, row2(b2), w3, w3, row2(decay), row2(decay), tc, ts)


def _hy_conv_body(pv_ref, p1_ref, p2_ref, g_ref, cw_ref, cb_ref, gr_ref, gi_ref, d_ref, mc_ref, ms_ref, o_ref,
                  z_scr, a_scr, b_scr, xs_scr, *, seq, nb, cw, row_chunk, problems):
    blk = seq // nb
    gate_refs = (p1_ref, p2_ref)
    lanes = lambda j: slice(j * cw, (j + 1) * cw)
    cadd = lambda x, y: (x[0] + y[0], x[1] + y[1])
    cmul = lambda x, y: (x[0] * y[0] - x[1] * y[1], x[0] * y[1] + x[1] * y[0])

    def short_conv(k, p_ref, part):
        x = p_ref[k].astype(F32)
        pad = xs_scr.at[k, part]
        pad[0:SUBLANES, :] = jnp.zeros((SUBLANES, cw), F32)
        pad[SUBLANES + seq:2 * SUBLANES + seq, :] = jnp.zeros((SUBLANES, cw), F32)
        pad[SUBLANES:SUBLANES + seq, :] = x
        prev, nxt = pad[pl.ds(SUBLANES - 1, seq), :], pad[pl.ds(SUBLANES + 1, seq), :]
        w = cw_ref[:, part, :]
        return cb_ref[part:part + 1, :] + prev * w[0:1] + x * w[1:2] + nxt * w[2:3]

    def load_v(k):
        v = short_conv(k, pv_ref, 0)
        for j in range(nb):
            z_scr[k, :, lanes(j)] = v[j * blk:(j + 1) * blk, :]

    def forward(k, o):
        zb = z_scr[k].astype(BF16)
        a_scr[k] = jnp.dot(mc_ref[...], zb, preferred_element_type=F32)
        b_scr[k] = jnp.dot(ms_ref[...], zb, preferred_element_type=F32)

    def pointwise(k, o):
        def toeplitz2(base, x0, x1, r):
            g = lambda idx: (gr_ref[o, idx, r, :], gi_ref[o, idx, r, :])
            q = cmul(g(base), cadd(x0, x1))
            return cadd(q, cmul(g(base + 1), x1)), cadd(q, cmul(g(base + 2), x0))

        for c in range(blk // row_chunk):
            r = slice(c * row_chunk, (c + 1) * row_chunk)
            z = [(a_scr[k, r, lanes(j)], b_scr[k, r, lanes(j)]) for j in range(nb)]
            p1 = toeplitz2(0, cadd(z[0], z[2]), cadd(z[1], z[3]), r)
            p2 = toeplitz2(3, z[2], z[3], r)
            p3 = toeplitz2(6, z[0], z[1], r)
            for i, y in enumerate((cadd(p1[0], p2[0]), cadd(p1[1], p2[1]), cadd(p1[0], p3[0]), cadd(p1[1], p3[1]))):
                a_scr[k, r, lanes(i)], b_scr[k, r, lanes(i)] = y

    def inverse(k, o):
        conv = (jnp.dot(mc_ref[...], a_scr[k].astype(BF16), preferred_element_type=F32)
                + jnp.dot(ms_ref[...], b_scr[k].astype(BF16), preferred_element_type=F32))
        gate = short_conv(k, gate_refs[o], o + 1)
        dsk = d_ref[o:o + 1, :]
        for j in range(nb):
            rows = slice(j * blk, (j + 1) * blk)
            z = gate[rows, :] * (conv[:, lanes(j)] + dsk * z_scr[k, :, lanes(j)])
            if o == HY_ORDER - 1:
                o_ref[k, rows, :] = (z * g_ref[k, rows, :].astype(F32)).astype(o_ref.dtype)
            else:
                z_scr[k, :, lanes(j)] = z

    stages = [load_v]
    for o in range(HY_ORDER):
        stages += [functools.partial(forward, o=o), functools.partial(pointwise, o=o),
                   functools.partial(inverse, o=o)]
    for s in range(len(stages) + problems - 1):
        for k in range(problems):
            if 0 <= s - k < len(stages):
                stages[s - k](k)


def _hy_conv(p_hy, gate_silu, conv_w, conv_b, gr, gi, d_skip, tables, batch, seq, width, nb, *, col_block=LANES,
             row_chunk=SUBLANES, problems=2):
    assert nb == HY_TIME_BLOCKS
    mc, ms, _, _ = tables
    blk = seq // nb
    nblk = width // col_block
    part = lambda k: pl.BlockSpec((problems, seq, col_block), lambda c, b: (b, 0, k * nblk + c))
    cw = conv_w.reshape(conv_w.shape[0], HY_ORDER + 1, width)
    cb = conv_b.reshape(HY_ORDER + 1, width)
    filt = pl.BlockSpec((HY_ORDER, HY_COMBOS, blk, col_block), lambda c, b: (0, 0, 0, c))
    body = functools.partial(_hy_conv_body, seq=seq, nb=nb, cw=col_block, row_chunk=row_chunk, problems=problems)
    wide = nb * col_block
    scratch = lambda dtype: pltpu.VMEM((problems, blk, wide), dtype)
    return pl.pallas_call(
        body,
        grid=(nblk, batch // problems),
        in_specs=[part(0), part(1), part(2),
                  pl.BlockSpec((problems, seq, col_block), lambda c, b: (b, 0, c)),
                  pl.BlockSpec((cw.shape[0], HY_ORDER + 1, col_block), lambda c, b: (0, 0, c)),
                  pl.BlockSpec((HY_ORDER + 1, col_block), lambda c, b: (0, c)),
                  filt, filt,
                  pl.BlockSpec((HY_ORDER, col_block), lambda c, b: (0, c)),
                  _resident(mc.shape), _resident(ms.shape)],
        out_specs=pl.BlockSpec((problems, seq, col_block), lambda c, b: (b, 0, c)),
        out_shape=jax.ShapeDtypeStruct((batch, seq, width), BF16),
        scratch_shapes=[scratch(F32), scratch(F32), scratch(F32),
                        pltpu.VMEM((problems, HY_ORDER + 1, seq + 2 * SUBLANES, col_block), F32)],
        compiler_params=pltpu.CompilerParams(dimension_semantics=("parallel", "parallel"),
                                             vmem_limit_bytes=VMEM_LIMIT_BYTES),
    )(p_hy, p_hy, p_hy, gate_silu, cw, cb, gr, gi, d_skip, mc, ms)


def _merge_body(ys_ref, gs_ref, yh_ref, m_ref, h_ref, wg_ref, bg_ref, wbs_ref, wbh_ref, wo_ref, fw_ref, o_ref,
                *, d_model, final_norm):
    ys = ys_ref[...]
    y = _gelu_tanh(ys.reshape(ys.shape[0] * ys.shape[1], ys.shape[2]))
    dot = lambda a, w_ref: jnp.dot(a, w_ref[...].astype(BF16), preferred_element_type=F32)
    glu = y * _sigmoid(dot(y.astype(BF16), wg_ref) + bg_ref[...])
    s5 = (glu * gs_ref[...].astype(F32)).astype(BF16)
    y_s5 = dot(s5, wbs_ref)
    y_hy = dot(yh_ref[...], wbh_ref)
    merged = m_ref[:, :d_model].astype(F32) * y_s5 + m_ref[:, d_model:].astype(F32) * y_hy
    h = h_ref[...] + dot(merged.astype(BF16), wo_ref)
    if final_norm:
        ms = jnp.mean(h * h, axis=-1, keepdims=True)
        h = h * lax.rsqrt(ms + RMS_EPS) * fw_ref[...]
    o_ref[...] = h


def _merge(y_s5_cm, gs, y_hypre, m, h2d, w_glu, b_glu, w_bs, w_bh, w_out, layer, final_w, final_norm, seq, *,
           row_tile=1024):
    rows, d = h2d.shape
    sw = y_s5_cm.shape[-1]
    hw = y_hypre.shape[1]
    tile = lambda w: pl.BlockSpec((row_tile, w), lambda i: (i, 0))
    body = functools.partial(_merge_body, d_model=d, final_norm=final_norm)
    return pl.pallas_call(
        body,
        grid=(rows // row_tile,),
        in_specs=[_chunk_major_spec(row_tile, seq, sw), tile(sw), tile(hw), tile(2 * d), tile(d),
                  _resident_layer(w_glu.shape, layer), _resident_layer((b_glu.shape[0], 1, sw), layer),
                  _resident_layer(w_bs.shape, layer), _resident_layer(w_bh.shape, layer),
                  _resident_layer(w_out.shape, layer), _resident((1, d))],
        out_specs=tile(d),
        out_shape=jax.ShapeDtypeStruct((rows, d), F32),
        compiler_params=pltpu.CompilerParams(dimension_semantics=("parallel",),
                                             vmem_limit_bytes=VMEM_LIMIT_BYTES),
    )(y_s5_cm, gs, y_hypre, m, h2d, w_glu, b_glu.reshape(-1, 1, sw), w_bs, w_bh, w_out, final_w.reshape(1, d))


def kernel(x, norm_w, w_in, s5_lam_re, s5_lam_im, s5_log_step, s5_b_re, s5_b_im, s5_c_re, s5_c_im, s5_d, s5_w_glu, s5_b_glu, hy_conv_w, hy_conv_b, hy_w1, hy_b1, hy_freq, hy_w2, hy_b2, hy_w3, hy_decay, hy_d, w_branch_s5, w_branch_hy, w_out, final_norm_w):
    batch, seq, d_model = x.shape
    depth = w_in.shape[0]
    s5_width = s5_d.shape[1]
    hy_width = hy_d.shape[2]
    widths = (s5_width, s5_width, (HY_ORDER + 1) * hy_width, hy_width, 2 * d_model)
    nb = HY_TIME_BLOCKS
    tables = _dft_tables(seq // nb)
    h = x.reshape(batch * seq, d_model)
    s5_ops = jax.vmap(_s5_operands)(s5_lam_re, s5_lam_im, s5_log_step, s5_b_re, s5_b_im, s5_c_re, s5_c_im, s5_d)
    for l in range(depth):
        u, gs, p_hy, gh, m = _inproj(h, norm_w, w_in, l, widths, batch, seq)
        y_s5 = _s5_scan(u, s5_ops, l, batch, seq)
        gr, gi = _hy_filters(seq, hy_w1[l], hy_b1[l], hy_freq[l], hy_w2[l], hy_b2[l], hy_w3[l], hy_decay[l],
                             hy_width, tables, nb)
        y_hypre = _hy_conv(p_hy.reshape(batch, seq, -1), gh.reshape(batch, seq, -1), hy_conv_w[l], hy_conv_b[l],
                           gr, gi, hy_d[l], tables, batch, seq, hy_width, nb)
        h = _merge(y_s5, gs, y_hypre.reshape(batch * seq, hy_width), m, h, s5_w_glu, s5_b_glu, w_branch_s5,
                   w_branch_hy, w_out, l, final_norm_w, l == depth - 1, seq)
    return h.reshape(batch, seq, d_model)
```

```python
import functools
import math

import numpy as np
import jax
import jax.numpy as jnp
from jax import lax
from jax.experimental import pallas as pl
from jax.experimental.pallas import tpu as pltpu

F32 = jnp.float32
BF16 = jnp.bfloat16

RMS_EPS = 1e-6
S5_GROUP = 16
S5_CHUNK = 16
S5_GROUPS_PER_PAIR = 2
LANES = 128
SUBLANES = 8
HY_ORDER = 2
HY_BANDS = 16
HY_SHIFT = 0.05
HY_EPS = 1e-6
HY_TIME_BLOCKS = 4
HY_LAGS = 2 * HY_TIME_BLOCKS - 1
HY_COMBOS = 9
VMEM_LIMIT_BYTES = 56 * 1024 * 1024

_NT = (((1,), (1,)), ((), ()))


def _sigmoid(x):
    return 1.0 / (1.0 + jnp.exp2(x * (-1.0 / math.log(2.0))))


def _silu(x):
    return x * _sigmoid(x)


def _gelu_tanh(x):
    return 0.5 * x * (1.0 + jnp.tanh(math.sqrt(2.0 / math.pi) * (x + 0.044715 * (x * x * x))))


def _resident(shape):
    zeros = (0,) * len(shape)
    return pl.BlockSpec(shape, lambda *_: zeros, pipeline_mode=pl.Buffered(1))


def _resident_layer(stacked_shape, layer):
    zeros = (0,) * (len(stacked_shape) - 1)
    return pl.BlockSpec((None,) + tuple(stacked_shape[1:]), lambda *_: (layer,) + zeros,
                        pipeline_mode=pl.Buffered(1))


def _chunk_major_spec(row_tile, seq, width):
    tiles_per_seq = seq // row_tile
    return pl.BlockSpec((row_tile // S5_CHUNK, None, S5_CHUNK, width),
                        lambda i: (i % tiles_per_seq, i // tiles_per_seq, 0, 0))


def _inproj_body(x_ref, nw_ref, w_ref, u_ref, gs_ref, p_ref, gh_ref, m_ref, *, bounds, col_chunk, sub_rows):
    outs = ((u_ref, None), (gs_ref, _silu), (p_ref, None), (gh_ref, _silu), (m_ref, _sigmoid))
    for r0 in range(0, x_ref.shape[0], sub_rows):
        x = x_ref[r0:r0 + sub_rows, :]
        ms = jnp.mean(x * x, axis=-1, keepdims=True)
        xb = (x * lax.rsqrt(ms + RMS_EPS) * nw_ref[...]).astype(BF16)
        for (out_ref, act), lo, hi in zip(outs, bounds[:-1], bounds[1:]):
            for c0 in range(lo, hi, col_chunk):
                y = jnp.dot(xb, w_ref[:, c0:c0 + col_chunk].astype(BF16), preferred_element_type=F32)
                if act is not None:
                    y = act(y)
                cols = slice(c0 - lo, c0 - lo + col_chunk)
                if out_ref is u_ref:
                    out_ref[r0 // S5_CHUNK:(r0 + sub_rows) // S5_CHUNK, :, cols] = y.astype(out_ref.dtype).reshape(
                        sub_rows // S5_CHUNK, S5_CHUNK, col_chunk)
                else:
                    out_ref[r0:r0 + sub_rows, cols] = y.astype(out_ref.dtype)


def _inproj(h2d, norm_w, w_in, layer, widths, batch, seq, *, row_tile=512, sub_rows=256, col_chunk=512):
    rows, d = h2d.shape
    bounds = [0]
    for w in widths:
        bounds.append(bounds[-1] + w)
    body = functools.partial(_inproj_body, bounds=tuple(bounds), col_chunk=col_chunk, sub_rows=sub_rows)
    tile = lambda w: pl.BlockSpec((row_tile, w), lambda i: (i, 0))
    return pl.pallas_call(
        body,
        grid=(rows // row_tile,),
        in_specs=[tile(d), _resident_layer((norm_w.shape[0], 1, d), layer), _resident_layer(w_in.shape, layer)],
        out_specs=[_chunk_major_spec(row_tile, seq, widths[0])] + [tile(w) for w in widths[1:]],
        out_shape=[jax.ShapeDtypeStruct((seq // S5_CHUNK, batch, S5_CHUNK, widths[0]), BF16)]
        + [jax.ShapeDtypeStruct((rows, w), BF16) for w in widths[1:]],
        compiler_params=pltpu.CompilerParams(dimension_semantics=("parallel",),
                                             vmem_limit_bytes=VMEM_LIMIT_BYTES),
    )(h2d, norm_w.reshape(-1, 1, d), w_in)


def _s5_body(*refs, batch, chunks, pairs, pw, sl):
    t_steps = S5_CHUNK
    x_refs = refs[:t_steps]
    bb_ref, cc_ref, pin_ref, pout_ref, a_ref, sc_ref, dv_ref, y_ref, d_scr, st_scr = refs[t_steps:]
    k = t_steps * pw
    rows = batch * chunks
    assert k == 4 * sl

    assert pairs == 4 and pairs * pw == LANES and t_steps % pairs == 0

    def piece_transpose(v):
        lane = lax.broadcasted_iota(jnp.int32, v[0].shape, 1)
        upper, odd = lane >= 2 * pw, (lane // pw) % 2 == 1
        t0 = jnp.where(upper, pltpu.roll(v[2], 2 * pw, 1), v[0])
        t2 = jnp.where(upper, v[2], pltpu.roll(v[0], 2 * pw, 1))
        t1 = jnp.where(upper, pltpu.roll(v[3], 2 * pw, 1), v[1])
        t3 = jnp.where(upper, v[3], pltpu.roll(v[1], 2 * pw, 1))
        return (jnp.where(odd, pltpu.roll(t1, pw, 1), t0), jnp.where(odd, t1, pltpu.roll(t0, LANES - pw, 1)),
                jnp.where(odd, pltpu.roll(t3, pw, 1), t2), jnp.where(odd, t3, pltpu.roll(t2, LANES - pw, 1)))

    def pair_operand(fac_ref, pow_ref, direction, j):
        re, im = [], []
        for t in range(t_steps):
            for a in range(S5_GROUPS_PER_PAIR):
                g = S5_GROUPS_PER_PAIR * j + a
                fr, fi = fac_ref[2 * direction, g], fac_ref[2 * direction + 1, g]
                pr, pi = pow_ref[2 * direction, g, t:t + 1, :], pow_ref[2 * direction + 1, g, t:t + 1, :]
                re.append(pr * fr - pi * fi)
                im.append(pr * fi + pi * fr)
        return jnp.concatenate(re, axis=0), jnp.concatenate(im, axis=0)

    for j in range(pairs):
        d_scr[:, j * k:(j + 1) * k] = jnp.concatenate(
            [x_refs[t][:, j * pw:(j + 1) * pw] for t in range(t_steps)], axis=1)
        win = jnp.concatenate(pair_operand(bb_ref, pin_ref, 0, j) + pair_operand(bb_ref, pin_ref, 1, j),
                              axis=1).astype(BF16)
        st_scr[:, j * k:(j + 1) * k] = jnp.dot(d_scr[:, j * k:(j + 1) * k], win, preferred_element_type=F32)

    coef = [[jnp.broadcast_to(a_ref[j, c:c + 1, :], (batch, sl)) for c in range(4)] for j in range(pairs)]

    per_iter = 2
    assert chunks % per_iter == 0

    def step(i, carry):
        state = list(carry)
        rows_of, loaded = [], []
        for u in range(per_iter):
            c = i * per_iter + u
            rows_of.append((pl.ds(pl.multiple_of(c * batch, batch), batch),
                            pl.ds(pl.multiple_of((chunks - 1 - c) * batch, batch), batch)))
            loaded.append([st_scr[rows_of[u][col // 2 % 2], col * sl:(col + 1) * sl] for col in range(4 * pairs)])
        stores = []
        for u in range(per_iter):
            for j in range(pairs):
                for direction in range(2):
                    re, im = 4 * j + 2 * direction, 4 * j + 2 * direction + 1
                    sr, si = state[re], state[im]
                    ar, ai = coef[j][2 * direction], coef[j][2 * direction + 1]
                    stores += [(u, re, sr), (u, im, si)]
                    state[re] = ar * sr - ai * si + loaded[u][re]
                    state[im] = ar * si + ai * sr + loaded[u][im]
        for u, col, value in stores:
            st_scr[rows_of[u][col // 2 % 2], col * sl:(col + 1) * sl] = value
        return tuple(state)

    zero = jnp.zeros((batch, sl), F32)
    lax.fori_loop(0, chunks // per_iter, step, (zero,) * (4 * pairs))

    in_step = lax.broadcasted_iota(jnp.int32, (k, k), 0) // pw
    out_step = lax.broadcasted_iota(jnp.int32, (k, k), 1) // pw
    for j in range(pairs):
        sc = sc_ref[j]
        kern = None
        woutt = []
        for direction in range(2):
            wr, wi = pair_operand(bb_ref, pin_ref, direction, j)
            out_re, out_im = pair_operand(cc_ref, pout_ref, direction, j)
            woutt += [out_re.astype(BF16), (-out_im).astype(BF16)]
            pr, pi = sc[2 * direction:2 * direction + 1, :], sc[2 * direction + 1:2 * direction + 2, :]
            wp = jnp.concatenate([wr * pr - wi * pi, wr * pi + wi * pr], axis=1).astype(BF16)
            wt = jnp.concatenate(woutt[2 * direction:2 * direction + 2], axis=1)
            kd = lax.dot_general(wp, wt, _NT, preferred_element_type=F32)
            kd = jnp.where(out_step >= in_step if direction == 0 else out_step <= in_step, kd, 0.0)
            kern = kd if kern is None else kern + kd
        dj = d_scr[:, j * k:(j + 1) * k]
        wt = jnp.concatenate(woutt, axis=1)
        y = jnp.dot(dj, kern.astype(BF16), preferred_element_type=F32)
        y = y + lax.dot_general(st_scr[:, j * k:(j + 1) * k].astype(BF16), wt, _NT, preferred_element_type=F32)
        y = y + dj.astype(F32) * dv_ref[j]
        st_scr[:, j * k:(j + 1) * k] = y
    for m in range(t_steps // pairs):
        moved = piece_transpose([st_scr[:, j * k + m * LANES:j * k + (m + 1) * LANES] for j in range(pairs)])
        for i in range(pairs):
            y_ref[pl.ds(pairs * m + i, rows, stride=t_steps), :] = moved[i]


def _s5_operands(lam_re, lam_im, log_step, b_re, b_im, c_re, c_im, d_skip):
    _, groups, p = lam_re.shape
    hh, t, gp = S5_GROUP, S5_CHUNK, S5_GROUPS_PER_PAIR
    npair = groups // gp
    rep = lambda v: jnp.concatenate([v] * gp, axis=-1)
    own = (jnp.arange(gp * p)[None, :] // p == jnp.arange(groups)[:, None] % gp).astype(F32)
    lam_re, lam_im = rep(lam_re), rep(lam_im)
    dt = jnp.exp(log_step)[..., None]
    xr, xi = lam_re * dt, lam_im * dt
    lbr, lbi = jnp.exp(xr) * jnp.cos(xi), jnp.exp(xr) * jnp.sin(xi)
    nr, ni = lbr - 1.0, lbi
    den = lam_re * lam_re + lam_im * lam_im
    zr, zi = (nr * lam_re + ni * lam_im) / den * own, (ni * lam_re - nr * lam_im) / den * own
    btr, bti = rep(jnp.swapaxes(b_re, -1, -2)), rep(jnp.swapaxes(b_im, -1, -2))
    bbr = zr[:, :, None, :] * btr - zi[:, :, None, :] * bti
    bbi = zr[:, :, None, :] * bti + zi[:, :, None, :] * btr
    ccr, cci = rep(c_re) * own[None, :, None, :], rep(c_im) * own[None, :, None, :]

    def powers(e):
        kk = e[:, None, :, None]
        mag, ang = jnp.exp(kk * xr[:, :, None, :]), kk * xi[:, :, None, :]
        return mag * jnp.cos(ang), mag * jnp.sin(ang)

    components = lambda x_re, x_im: jnp.stack([x_re[0], x_im[0], x_re[1], x_im[1]])
    tt = jnp.arange(t, dtype=F32)
    pin = components(*powers(jnp.stack([t - 1 - tt, tt])))
    pout = components(*powers(jnp.stack([tt + 1, t - tt])))

    def pair_vec(e):
        mag, ang = jnp.exp(e * xr) * own, e * xi
        v = jnp.stack([(mag * jnp.cos(ang))[0], (mag * jnp.sin(ang))[0],
                       (mag * jnp.cos(ang))[1], (mag * jnp.sin(ang))[1]])
        return jnp.transpose(v.reshape(4, npair, gp, gp * p).sum(axis=2), (1, 0, 2))

    dvec = jnp.broadcast_to(d_skip.reshape(npair, 1, gp, hh), (npair, t, gp, hh)).reshape(npair, 1, t * gp * hh)
    return components(bbr, bbi), components(ccr, cci), pin, pout, pair_vec(float(t)), pair_vec(-float(t)), dvec


def _s5_scan(u_cm, operands, layer, batch, seq):
    bb, cc, pin, pout, a, sc, dvec = operands
    gp, t = S5_GROUPS_PER_PAIR, S5_CHUNK
    _, _, groups, hh, sl = bb.shape
    npair, pw = groups // gp, gp * hh
    k = t * pw
    pairs = LANES // pw
    chunks = seq // t
    rows = chunks * batch
    width = npair * pw
    nblk = width // LANES
    body = functools.partial(_s5_body, batch=batch, chunks=chunks, pairs=pairs, pw=pw, sl=sl)
    per_blk = lambda shape: pl.BlockSpec((None, pairs) + shape, lambda q: (layer, q, 0, 0))
    per_group = lambda group_rows: pl.BlockSpec((None, 4, gp * pairs, group_rows, sl), lambda q: (layer, 0, q, 0, 0))
    x2d = u_cm.reshape(rows, t * width)
    y = pl.pallas_call(
        body,
        grid=(nblk,),
        in_specs=[pl.BlockSpec((rows, LANES), functools.partial(lambda q, s: (0, s * nblk + q), s=s))
                  for s in range(t)]
        + [per_group(hh), per_group(hh), per_group(t), per_group(t), per_blk((4, sl)), per_blk((4, sl)),
           per_blk((1, k))],
        out_specs=pl.BlockSpec((rows * t, LANES), lambda q: (0, q)),
        out_shape=jax.ShapeDtypeStruct((rows * t, width), F32),
        scratch_shapes=[pltpu.VMEM((rows, pairs * k), BF16), pltpu.VMEM((rows, pairs * k), F32)],
        compiler_params=pltpu.CompilerParams(dimension_semantics=("parallel",),
                                             vmem_limit_bytes=VMEM_LIMIT_BYTES),
    )(*([x2d] * t), bb, cc, pin, pout, a, sc, dvec)
    return y.reshape(chunks, batch, t, width)


def _dft_tables(blk):
    n = 2 * blk
    f = np.arange(blk, dtype=np.int64)
    sym = ((2 * f[:, None] + 1) * (2 * f[None, :] + 1)) % (4 * n)
    ang_sym = sym.astype(np.float64) * (2.0 * np.pi / (4 * n))
    lag = np.arange(2 * blk, dtype=np.int64) - blk
    ang_lag = (((2 * f[:, None] + 1) * lag[None, :]) % (2 * n)).astype(np.float64) * (2.0 * np.pi / (2 * n))
    tc, ts = np.cos(ang_lag), np.sin(ang_lag)
    tc[:, 0] = 0.0
    ts[:, 0] = 0.0
    as_bf = lambda x: jnp.asarray(x.astype(np.float32)).astype(BF16)
    return as_bf(np.cos(ang_sym)), as_bf(np.sin(ang_sym)), as_bf(tc), as_bf(ts)


def _position_features(seq, pad_to):
    t = np.linspace(0.0, 1.0, seq)[:, None]
    pos = np.arange(seq, dtype=np.float64)[:, None]
    bands = np.linspace(1e-4, HY_BANDS - 1, HY_BANDS)[None, :]
    ang = bands * pos * (2.0 * math.pi / seq)
    feats = np.zeros((seq, pad_to), np.float64)
    feats[:, :1 + 2 * HY_BANDS] = np.concatenate([t, np.cos(ang), -np.sin(ang)], axis=-1)
    mirror = lambda x: np.roll(x[::-1], 1, axis=0)
    f32 = lambda x: jnp.asarray(x.astype(np.float32))
    return f32(feats), f32(t), f32(mirror(feats)), f32(mirror(t))


_HY_COMBO_TERMS = (
    ((0, 1),), ((-1, 1), (0, -1)), ((1, 1), (0, -1)),
    ((-2, 1), (0, -1)), ((-3, 1), (-1, -1), (-2, -1), (0, 1)), ((-1, 1), (1, -1), (-2, -1), (0, 1)),
    ((2, 1), (0, -1)), ((1, 1), (-1, -1), (2, -1), (0, 1)), ((3, 1), (1, -1), (2, -1), (0, 1)),
)


def _hy_filter_body(feats_ref, t_ref, featsm_ref, tm_ref, w1_ref, b1_ref, fr_ref, w2_ref, b2_ref, w3f_ref,
                    w3b_ref, decf_ref, decb_ref, tc_ref, ts_ref, gr_ref, gi_ref, h2_scr, h2m_scr, k2_scr,
                    kr_scr, ki_scr, *, seq, nb):
    blk = seq // nb

    def split(h):
        top = h.astype(BF16).astype(F32)
        return jnp.concatenate([top, h - top, top, h - top], axis=1).astype(BF16)

    def split_rows(w):
        top = w.astype(BF16).astype(F32)
        return jnp.concatenate([top, top, w - top, w - top], axis=0).astype(BF16)

    @pl.when(pl.program_id(0) == 0)
    def _():
        fr = fr_ref[...]

        def mlp(feats):
            h1 = jnp.sin(fr * (jnp.dot(split(feats), split_rows(w1_ref[...]), preferred_element_type=F32)
                               + b1_ref[...]))
            return jnp.sin(fr * (jnp.dot(split(h1), split_rows(w2_ref[...]), preferred_element_type=F32)
                                 + b2_ref[...]))

        h2_scr[...] = split(mlp(feats_ref[...]))
        h2m_scr[...] = split(mlp(featsm_ref[...]))

    hf = jnp.dot(h2_scr[...], split_rows(w3f_ref[...]), preferred_element_type=F32)
    hf = hf * (jnp.exp(-t_ref[...] * jnp.abs(decf_ref[...])) + HY_SHIFT)
    hb = jnp.dot(h2m_scr[...], split_rows(w3b_ref[...]), preferred_element_type=F32)
    hb = hb * (jnp.exp(-tm_ref[...] * jnp.abs(decb_ref[...])) + HY_SHIFT)
    row = lax.broadcasted_iota(jnp.int32, hb.shape, 0)
    hb = jnp.where(row == 0, 0.0, hb)
    norm = jnp.sum(hf * hf, axis=0, keepdims=True) + jnp.sum(hb * hb, axis=0, keepdims=True)
    scale = lax.rsqrt(norm + HY_EPS) * (1.0 / blk)
    k2_scr[0:seq, :] = hb.astype(BF16)
    k2_scr[seq:2 * seq, :] = hf.astype(BF16)
    for d in range(2 * nb - 1):
        seg = k2_scr[d * blk:(d + 2) * blk, :]
        kr_scr[d] = jnp.dot(tc_ref[...], seg, preferred_element_type=F32) * scale
        ki_scr[d] = jnp.dot(ts_ref[...], seg, preferred_element_type=F32) * scale
    for idx, terms in enumerate(_HY_COMBO_TERMS):
        for src, dst in ((kr_scr, gr_ref), (ki_scr, gi_ref)):
            acc = None
            for lag, weight in terms:
                term = src[lag + nb - 1]
                acc = (term if weight > 0 else -term) if acc is None else (acc + term if weight > 0 else acc - term)
            dst[0, idx] = acc


def _hy_filters(seq, w1, b1, freq, w2, b2, w3, decay, width, tables, nb, *, col_block=256):
    _, _, tc, ts = tables
    blk = seq // nb
    ffn = w2.shape[0]
    feats, t, featsm, tm = _position_features(seq, LANES)
    w1p = jnp.zeros((LANES, ffn), F32).at[:w1.shape[0]].set(w1)
    nblk = width // col_block
    side = lambda rows, direction: pl.BlockSpec(
        (rows, col_block), lambda i: (0, (2 * (i // nblk) + direction) * nblk + i % nblk))
    out = pl.BlockSpec((1, HY_COMBOS, blk, col_block), lambda i: (i // nblk, 0, 0, i % nblk))
    body = functools.partial(_hy_filter_body, seq=seq, nb=nb)
    row2 = lambda v: v.reshape(1, -1)
    return pl.pallas_call(
        body,
        grid=(HY_ORDER * nblk,),
        in_specs=[_resident(feats.shape), _resident(t.shape), _resident(feats.shape), _resident(t.shape),
                  _resident(w1p.shape), _resident((1, ffn)), _resident((1, ffn)), _resident(w2.shape),
                  _resident((1, ffn)), side(ffn, 0), side(ffn, 1), side(1, 0), side(1, 1),
                  _resident(tc.shape), _resident(ts.shape)],
        out_specs=[out, out],
        out_shape=[jax.ShapeDtypeStruct((HY_ORDER, HY_COMBOS, blk, width), F32)] * 2,
        scratch_shapes=[pltpu.VMEM((seq, 4 * ffn), BF16), pltpu.VMEM((seq, 4 * ffn), BF16),
                        pltpu.VMEM((2 * seq, col_block), BF16),
                        pltpu.VMEM((2 * nb - 1, blk, col_block), F32), pltpu.VMEM((2 * nb - 1, blk, col_block), F32)],
        compiler_params=pltpu.CompilerParams(dimension_semantics=("arbitrary",),
                                             vmem_limit_bytes=VMEM_LIMIT_BYTES),
    )(feats, t, featsm, tm, w1p, row2(b1), row2(freq), w2, row2(b2), w3, w3, row2(decay), row2(decay), tc, ts)


def _hy_conv_body(pv_ref, p1_ref, p2_ref, g_ref, cw_ref, cb_ref, gr_ref, gi_ref, d_ref, mc_ref, ms_ref, o_ref,
                  z_scr, a_scr, b_scr, xs_scr, *, seq, nb, cw, row_chunk, problems):
    blk = seq // nb
    gate_refs = (p1_ref, p2_ref)
    lanes = lambda j: slice(j * cw, (j + 1) * cw)
    cadd = lambda x, y: (x[0] + y[0], x[1] + y[1])
    cmul = lambda x, y: (x[0] * y[0] - x[1] * y[1], x[0] * y[1] + x[1] * y[0])

    def short_conv(k, p_ref, part):
        x = p_ref[k].astype(F32)
        pad = xs_scr.at[k, part]
        pad[0:SUBLANES, :] = jnp.zeros((SUBLANES, cw), F32)
        pad[SUBLANES + seq:2 * SUBLANES + seq, :] = jnp.zeros((SUBLANES, cw), F32)
        pad[SUBLANES:SUBLANES + seq, :] = x
        prev, nxt = pad[pl.ds(SUBLANES - 1, seq), :], pad[pl.ds(SUBLANES + 1, seq), :]
        w = cw_ref[:, part, :]
        return cb_ref[part:part + 1, :] + prev * w[0:1] + x * w[1:2] + nxt * w[2:3]

    def load_v(k):
        v = short_conv(k, pv_ref, 0)
        for j in range(nb):
            z_scr[k, :, lanes(j)] = v[j * blk:(j + 1) * blk, :]

    def forward(k, o):
        zb = z_scr[k].astype(BF16)
        a_scr[k] = jnp.dot(mc_ref[...], zb, preferred_element_type=F32)
        b_scr[k] = jnp.dot(ms_ref[...], zb, preferred_element_type=F32)

    def pointwise(k, o):
        def toeplitz2(base, x0, x1, r):
            g = lambda idx: (gr_ref[o, idx, r, :], gi_ref[o, idx, r, :])
            q = cmul(g(base), cadd(x0, x1))
            return cadd(q, cmul(g(base + 1), x1)), cadd(q, cmul(g(base + 2), x0))

        for c in range(blk // row_chunk):
            r = slice(c * row_chunk, (c + 1) * row_chunk)
            z = [(a_scr[k, r, lanes(j)], b_scr[k, r, lanes(j)]) for j in range(nb)]
            p1 = toeplitz2(0, cadd(z[0], z[2]), cadd(z[1], z[3]), r)
            p2 = toeplitz2(3, z[2], z[3], r)
            p3 = toeplitz2(6, z[0], z[1], r)
            for i, y in enumerate((cadd(p1[0], p2[0]), cadd(p1[1], p2[1]), cadd(p1[0], p3[0]), cadd(p1[1], p3[1]))):
                a_scr[k, r, lanes(i)], b_scr[k, r, lanes(i)] = y

    def inverse(k, o):
        conv = (jnp.dot(mc_ref[...], a_scr[k].astype(BF16), preferred_element_type=F32)
                + jnp.dot(ms_ref[...], b_scr[k].astype(BF16), preferred_element_type=F32))
        gate = short_conv(k, gate_refs[o], o + 1)
        dsk = d_ref[o:o + 1, :]
        for j in range(nb):
            rows = slice(j * blk, (j + 1) * blk)
            z = gate[rows, :] * (conv[:, lanes(j)] + dsk * z_scr[k, :, lanes(j)])
            if o == HY_ORDER - 1:
                o_ref[k, rows, :] = (z * g_ref[k, rows, :].astype(F32)).astype(o_ref.dtype)
            else:
                z_scr[k, :, lanes(j)] = z

    stages = [load_v]
    for o in range(HY_ORDER):
        stages += [functools.partial(forward, o=o), functools.partial(pointwise, o=o),
                   functools.partial(inverse, o=o)]
    for s in range(len(stages) + problems - 1):
        for k in range(problems):
            if 0 <= s - k < len(stages):
                stages[s - k](k)


def _hy_conv(p_hy, gate_silu, conv_w, conv_b, gr, gi, d_skip, tables, batch, seq, width, nb, *, col_block=LANES,
             row_chunk=SUBLANES, problems=2):
    assert nb == HY_TIME_BLOCKS
    mc, ms, _, _ = tables
    blk = seq // nb
    nblk = width // col_block
    part = lambda k: pl.BlockSpec((problems, seq, col_block), lambda c, b: (b, 0, k * nblk + c))
    cw = conv_w.reshape(conv_w.shape[0], HY_ORDER + 1, width)
    cb = conv_b.reshape(HY_ORDER + 1, width)
    filt = pl.BlockSpec((HY_ORDER, HY_COMBOS, blk, col_block), lambda c, b: (0, 0, 0, c))
    body = functools.partial(_hy_conv_body, seq=seq, nb=nb, cw=col_block, row_chunk=row_chunk, problems=problems)
    wide = nb * col_block
    scratch = lambda dtype: pltpu.VMEM((problems, blk, wide), dtype)
    return pl.pallas_call(
        body,
        grid=(nblk, batch // problems),
        in_specs=[part(0), part(1), part(2),
                  pl.BlockSpec((problems, seq, col_block), lambda c, b: (b, 0, c)),
                  pl.BlockSpec((cw.shape[0], HY_ORDER + 1, col_block), lambda c, b: (0, 0, c)),
                  pl.BlockSpec((HY_ORDER + 1, col_block), lambda c, b: (0, c)),
                  filt, filt,
                  pl.BlockSpec((HY_ORDER, col_block), lambda c, b: (0, c)),
                  _resident(mc.shape), _resident(ms.shape)],
        out_specs=pl.BlockSpec((problems, seq, col_block), lambda c, b: (b, 0, c)),
        out_shape=jax.ShapeDtypeStruct((batch, seq, width), BF16),
        scratch_shapes=[scratch(F32), scratch(F32), scratch(F32),
                        pltpu.VMEM((problems, HY_ORDER + 1, seq + 2 * SUBLANES, col_block), F32)],
        compiler_params=pltpu.CompilerParams(dimension_semantics=("parallel", "parallel"),
                                             vmem_limit_bytes=VMEM_LIMIT_BYTES),
    )(p_hy, p_hy, p_hy, gate_silu, cw, cb, gr, gi, d_skip, mc, ms)


def _merge_body(ys_ref, gs_ref, yh_ref, m_ref, h_ref, wg_ref, bg_ref, wbs_ref, wbh_ref, wo_ref, fw_ref, o_ref,
                *, d_model, final_norm):
    ys = ys_ref[...]
    y = _gelu_tanh(ys.reshape(ys.shape[0] * ys.shape[1], ys.shape[2]))
    dot = lambda a, w_ref: jnp.dot(a, w_ref[...].astype(BF16), preferred_element_type=F32)
    glu = y * _sigmoid(dot(y.astype(BF16), wg_ref) + bg_ref[...])
    s5 = (glu * gs_ref[...].astype(F32)).astype(BF16)
    y_s5 = dot(s5, wbs_ref)
    y_hy = dot(yh_ref[...], wbh_ref)
    merged = m_ref[:, :d_model].astype(F32) * y_s5 + m_ref[:, d_model:].astype(F32) * y_hy
    h = h_ref[...] + dot(merged.astype(BF16), wo_ref)
    if final_norm:
        ms = jnp.mean(h * h, axis=-1, keepdims=True)
        h = h * lax.rsqrt(ms + RMS_EPS) * fw_ref[...]
    o_ref[...] = h


def _merge(y_s5_cm, gs, y_hypre, m, h2d, w_glu, b_glu, w_bs, w_bh, w_out, layer, final_w, final_norm, seq, *,
           row_tile=1024):
    rows, d = h2d.shape
    sw = y_s5_cm.shape[-1]
    hw = y_hypre.shape[1]
    tile = lambda w: pl.BlockSpec((row_tile, w), lambda i: (i, 0))
    body = functools.partial(_merge_body, d_model=d, final_norm=final_norm)
    return pl.pallas_call(
        body,
        grid=(rows // row_tile,),
        in_specs=[_chunk_major_spec(row_tile, seq, sw), tile(sw), tile(hw), tile(2 * d), tile(d),
                  _resident_layer(w_glu.shape, layer), _resident_layer((b_glu.shape[0], 1, sw), layer),
                  _resident_layer(w_bs.shape, layer), _resident_layer(w_bh.shape, layer),
                  _resident_layer(w_out.shape, layer), _resident((1, d))],
        out_specs=tile(d),
        out_shape=jax.ShapeDtypeStruct((rows, d), F32),
        compiler_params=pltpu.CompilerParams(dimension_semantics=("parallel",),
                                             vmem_limit_bytes=VMEM_LIMIT_BYTES),
    )(y_s5_cm, gs, y_hypre, m, h2d, w_glu, b_glu.reshape(-1, 1, sw), w_bs, w_bh, w_out, final_w.reshape(1, d))


def kernel(x, norm_w, w_in, s5_lam_re, s5_lam_im, s5_log_step, s5_b_re, s5_b_im, s5_c_re, s5_c_im, s5_d, s5_w_glu, s5_b_glu, hy_conv_w, hy_conv_b, hy_w1, hy_b1, hy_freq, hy_w2, hy_b2, hy_w3, hy_decay, hy_d, w_branch_s5, w_branch_hy, w_out, final_norm_w):
    batch, seq, d_model = x.shape
    depth = w_in.shape[0]
    s5_width = s5_d.shape[1]
    hy_width = hy_d.shape[2]
    widths = (s5_width, s5_width, (HY_ORDER + 1) * hy_width, hy_width, 2 * d_model)
    nb = HY_TIME_BLOCKS
    tables = _dft_tables(seq // nb)
    h = x.reshape(batch * seq, d_model)
    s5_ops = jax.vmap(_s5_operands)(s5_lam_re, s5_lam_im, s5_log_step, s5_b_re, s5_b_im, s5_c_re, s5_c_im, s5_d)
    for l in range(depth):
        u, gs, p_hy, gh, m = _inproj(h, norm_w, w_in, l, widths, batch, seq)
        y_s5 = _s5_scan(u, s5_ops, l, batch, seq)
        gr, gi = _hy_filters(seq, hy_w1[l], hy_b1[l], hy_freq[l], hy_w2[l], hy_b2[l], hy_w3[l], hy_decay[l],
                             hy_width, tables, nb)
        y_hypre = _hy_conv(p_hy.reshape(batch, seq, -1), gh.reshape(batch, seq, -1), hy_conv_w[l], hy_conv_b[l],
                           gr, gi, hy_d[l], tables, batch, seq, hy_width, nb)
        h = _merge(y_s5, gs, y_hypre.reshape(batch * seq, hy_width), m, h, s5_w_glu, s5_b_glu, w_branch_s5,
                   w_branch_hy, w_out, l, final_norm_w, l == depth - 1, seq)
    return h.reshape(batch, seq, d_model)
```

```python
import functools
import math

import numpy as np
import jax
import jax.numpy as jnp
from jax import lax
from jax.experimental import pallas as pl
from jax.experimental.pallas import tpu as pltpu

F32 = jnp.float32
BF16 = jnp.bfloat16

RMS_EPS = 1e-6
S5_GROUP = 16
S5_CHUNK = 16
S5_GROUPS_PER_PAIR = 2
LANES = 128
SUBLANES = 8
HY_ORDER = 2
HY_BANDS = 16
HY_SHIFT = 0.05
HY_EPS = 1e-6
HY_TIME_BLOCKS = 4
HY_COMBOS = 9
VMEM_LIMIT_BYTES = 56 * 1024 * 1024

_NT = (((1,), (1,)), ((), ()))


def _sigmoid(x):
    return 1.0 / (1.0 + jnp.exp2(x * (-1.0 / math.log(2.0))))


def _silu(x):
    return x * _sigmoid(x)


def _gelu_tanh(x):
    return 0.5 * x * (1.0 + jnp.tanh(math.sqrt(2.0 / math.pi) * (x + 0.044715 * (x * x * x))))


def _resident(shape):
    zeros = (0,) * len(shape)
    return pl.BlockSpec(shape, lambda *_: zeros, pipeline_mode=pl.Buffered(1))


def _resident_layer(stacked_shape, layer):
    zeros = (0,) * (len(stacked_shape) - 1)
    return pl.BlockSpec((None,) + tuple(stacked_shape[1:]), lambda *_: (layer,) + zeros,
                        pipeline_mode=pl.Buffered(1))


def _chunk_major_spec(row_tile, seq, width):
    tiles_per_seq = seq // row_tile
    return pl.BlockSpec((row_tile // S5_CHUNK, None, S5_CHUNK, width),
                        lambda i: (i % tiles_per_seq, i // tiles_per_seq, 0, 0))


def _inproj_body(x_ref, nw_ref, w_ref, u_ref, gs_ref, p_ref, gh_ref, m_ref, *, bounds, col_chunk, sub_rows):
    outs = ((u_ref, None), (gs_ref, _silu), (p_ref, None), (gh_ref, _silu), (m_ref, _sigmoid))
    for r0 in range(0, x_ref.shape[0], sub_rows):
        x = x_ref[r0:r0 + sub_rows, :]
        ms = jnp.mean(x * x, axis=-1, keepdims=True)
        xb = (x * lax.rsqrt(ms + RMS_EPS) * nw_ref[...]).astype(BF16)
        for (out_ref, act), lo, hi in zip(outs, bounds[:-1], bounds[1:]):
            for c0 in range(lo, hi, col_chunk):
                y = jnp.dot(xb, w_ref[:, c0:c0 + col_chunk].astype(BF16), preferred_element_type=F32)
                if act is not None:
                    y = act(y)
                cols = slice(c0 - lo, c0 - lo + col_chunk)
                if out_ref is u_ref:
                    out_ref[r0 // S5_CHUNK:(r0 + sub_rows) // S5_CHUNK, :, cols] = y.astype(out_ref.dtype).reshape(
                        sub_rows // S5_CHUNK, S5_CHUNK, col_chunk)
                else:
                    out_ref[r0:r0 + sub_rows, cols] = y.astype(out_ref.dtype)


def _inproj(h2d, norm_w, w_in, layer, widths, batch, seq, *, row_tile=512, sub_rows=256, col_chunk=512):
    rows, d = h2d.shape
    bounds = [0]
    for w in widths:
        bounds.append(bounds[-1] + w)
    body = functools.partial(_inproj_body, bounds=tuple(bounds), col_chunk=col_chunk, sub_rows=sub_rows)
    tile = lambda w: pl.BlockSpec((row_tile, w), lambda i: (i, 0))
    return pl.pallas_call(
        body,
        grid=(rows // row_tile,),
        in_specs=[tile(d), _resident_layer((norm_w.shape[0], 1, d), layer), _resident_layer(w_in.shape, layer)],
        out_specs=[_chunk_major_spec(row_tile, seq, widths[0])] + [tile(w) for w in widths[1:]],
        out_shape=[jax.ShapeDtypeStruct((seq // S5_CHUNK, batch, S5_CHUNK, widths[0]), BF16)]
        + [jax.ShapeDtypeStruct((rows, w), BF16) for w in widths[1:]],
        compiler_params=pltpu.CompilerParams(dimension_semantics=("parallel",),
                                             vmem_limit_bytes=VMEM_LIMIT_BYTES),
    )(h2d, norm_w.reshape(-1, 1, d), w_in)


def _s5_body(*refs, batch, chunks, pairs, pw, sl):
    t_steps = S5_CHUNK
    x_refs = refs[:t_steps]
    bb_ref, cc_ref, pin_ref, pout_ref, a_ref, sc_ref, dv_ref, y_ref, d_scr, st_scr = refs[t_steps:]
    k = t_steps * pw
    rows = batch * chunks
    assert k == 4 * sl

    assert pairs == 4 and pairs * pw == LANES and t_steps % pairs == 0

    def piece_transpose(v):
        lane = lax.broadcasted_iota(jnp.int32, v[0].shape, 1)
        upper, odd = lane >= 2 * pw, (lane // pw) % 2 == 1
        t0 = jnp.where(upper, pltpu.roll(v[2], 2 * pw, 1), v[0])
        t2 = jnp.where(upper, v[2], pltpu.roll(v[0], 2 * pw, 1))
        t1 = jnp.where(upper, pltpu.roll(v[3], 2 * pw, 1), v[1])
        t3 = jnp.where(upper, v[3], pltpu.roll(v[1], 2 * pw, 1))
        return (jnp.where(odd, pltpu.roll(t1, pw, 1), t0), jnp.where(odd, t1, pltpu.roll(t0, LANES - pw, 1)),
                jnp.where(odd, pltpu.roll(t3, pw, 1), t2), jnp.where(odd, t3, pltpu.roll(t2, LANES - pw, 1)))

    def pair_operand(fac_ref, pow_ref, direction, j):
        re, im = [], []
        for t in range(t_steps):
            for a in range(S5_GROUPS_PER_PAIR):
                g = S5_GROUPS_PER_PAIR * j + a
                fr, fi = fac_ref[2 * direction, g], fac_ref[2 * direction + 1, g]
                pr, pi = pow_ref[2 * direction, g, t:t + 1, :], pow_ref[2 * direction + 1, g, t:t + 1, :]
                re.append(pr * fr - pi * fi)
                im.append(pr * fi + pi * fr)
        return jnp.concatenate(re, axis=0), jnp.concatenate(im, axis=0)

    for j in range(pairs):
        d_scr[:, j * k:(j + 1) * k] = jnp.concatenate(
            [x_refs[t][:, j * pw:(j + 1) * pw] for t in range(t_steps)], axis=1)
        win = jnp.concatenate(pair_operand(bb_ref, pin_ref, 0, j) + pair_operand(bb_ref, pin_ref, 1, j),
                              axis=1).astype(BF16)
        st_scr[:, j * k:(j + 1) * k] = jnp.dot(d_scr[:, j * k:(j + 1) * k], win, preferred_element_type=F32)

    coef = [[jnp.broadcast_to(a_ref[j, c:c + 1, :], (batch, sl)) for c in range(4)] for j in range(pairs)]

    per_iter = 2
    assert chunks % per_iter == 0

    def step(i, carry):
        state = list(carry)
        rows_of, loaded = [], []
        for u in range(per_iter):
            c = i * per_iter + u
            rows_of.append((pl.ds(pl.multiple_of(c * batch, batch), batch),
                            pl.ds(pl.multiple_of((chunks - 1 - c) * batch, batch), batch)))
            loaded.append([st_scr[rows_of[u][col // 2 % 2], col * sl:(col + 1) * sl] for col in range(4 * pairs)])
        stores = []
        for u in range(per_iter):
            for j in range(pairs):
                for direction in range(2):
                    re, im = 4 * j + 2 * direction, 4 * j + 2 * direction + 1
                    sr, si = state[re], state[im]
                    ar, ai = coef[j][2 * direction], coef[j][2 * direction + 1]
                    stores += [(u, re, sr), (u, im, si)]
                    state[re] = ar * sr - ai * si + loaded[u][re]
                    state[im] = ar * si + ai * sr + loaded[u][im]
        for u, col, value in stores:
            st_scr[rows_of[u][col // 2 % 2], col * sl:(col + 1) * sl] = value
        return tuple(state)

    zero = jnp.zeros((batch, sl), F32)
    lax.fori_loop(0, chunks // per_iter, step, (zero,) * (4 * pairs))

    in_step = lax.broadcasted_iota(jnp.int32, (k, k), 0) // pw
    out_step = lax.broadcasted_iota(jnp.int32, (k, k), 1) // pw
    for j in range(pairs):
        sc = sc_ref[j]
        kern = None
        woutt = []
        for direction in range(2):
            wr, wi = pair_operand(bb_ref, pin_ref, direction, j)
            out_re, out_im = pair_operand(cc_ref, pout_ref, direction, j)
            woutt += [out_re.astype(BF16), (-out_im).astype(BF16)]
            pr, pi = sc[2 * direction:2 * direction + 1, :], sc[2 * direction + 1:2 * direction + 2, :]
            wp = jnp.concatenate([wr * pr - wi * pi, wr * pi + wi * pr], axis=1).astype(BF16)
            wt = jnp.concatenate(woutt[2 * direction:2 * direction + 2], axis=1)
            kd = lax.dot_general(wp, wt, _NT, preferred_element_type=F32)
            kd = jnp.where(out_step >= in_step if direction == 0 else out_step <= in_step, kd, 0.0)
            kern = kd if kern is None else kern + kd
        dj = d_scr[:, j * k:(j + 1) * k]
        wt = jnp.concatenate(woutt, axis=1)
        y = jnp.dot(dj, kern.astype(BF16), preferred_element_type=F32)
        y = y + lax.dot_general(st_scr[:, j * k:(j + 1) * k].astype(BF16), wt, _NT, preferred_element_type=F32)
        y = y + dj.astype(F32) * dv_ref[j]
        st_scr[:, j * k:(j + 1) * k] = y
    for m in range(t_steps // pairs):
        moved = piece_transpose([st_scr[:, j * k + m * LANES:j * k + (m + 1) * LANES] for j in range(pairs)])
        for i in range(pairs):
            y_ref[pl.ds(pairs * m + i, rows, stride=t_steps), :] = moved[i]


def _s5_operands(lam_re, lam_im, log_step, b_re, b_im, c_re, c_im, d_skip):
    _, groups, p = lam_re.shape
    hh, t, gp = S5_GROUP, S5_CHUNK, S5_GROUPS_PER_PAIR
    npair = groups // gp
    rep = lambda v: jnp.concatenate([v] * gp, axis=-1)
    own = (jnp.arange(gp * p)[None, :] // p == jnp.arange(groups)[:, None] % gp).astype(F32)
    lam_re, lam_im = rep(lam_re), rep(lam_im)
    dt = jnp.exp(log_step)[..., None]
    xr, xi = lam_re * dt, lam_im * dt
    lbr, lbi = jnp.exp(xr) * jnp.cos(xi), jnp.exp(xr) * jnp.sin(xi)
    nr, ni = lbr - 1.0, lbi
    den = lam_re * lam_re + lam_im * lam_im
    zr, zi = (nr * lam_re + ni * lam_im) / den * own, (ni * lam_re - nr * lam_im) / den * own
    btr, bti = rep(jnp.swapaxes(b_re, -1, -2)), rep(jnp.swapaxes(b_im, -1, -2))
    bbr = zr[:, :, None, :] * btr - zi[:, :, None, :] * bti
    bbi = zr[:, :, None, :] * bti + zi[:, :, None, :] * btr
    ccr, cci = rep(c_re) * own[None, :, None, :], rep(c_im) * own[None, :, None, :]

    def powers(e):
        kk = e[:, None, :, None]
        mag, ang = jnp.exp(kk * xr[:, :, None, :]), kk * xi[:, :, None, :]
        return mag * jnp.cos(ang), mag * jnp.sin(ang)

    components = lambda x_re, x_im: jnp.stack([x_re[0], x_im[0], x_re[1], x_im[1]])
    tt = jnp.arange(t, dtype=F32)
    pin = components(*powers(jnp.stack([t - 1 - tt, tt])))
    pout = components(*powers(jnp.stack([tt + 1, t - tt])))

    def pair_vec(e):
        mag, ang = jnp.exp(e * xr) * own, e * xi
        v = jnp.stack([(mag * jnp.cos(ang))[0], (mag * jnp.sin(ang))[0],
                       (mag * jnp.cos(ang))[1], (mag * jnp.sin(ang))[1]])
        return jnp.transpose(v.reshape(4, npair, gp, gp * p).sum(axis=2), (1, 0, 2))

    dvec = jnp.broadcast_to(d_skip.reshape(npair, 1, gp, hh), (npair, t, gp, hh)).reshape(npair, 1, t * gp * hh)
    return components(bbr, bbi), components(ccr, cci), pin, pout, pair_vec(float(t)), pair_vec(-float(t)), dvec


def _s5_scan(u_cm, operands, layer, batch, seq):
    bb, cc, pin, pout, a, sc, dvec = operands
    gp, t = S5_GROUPS_PER_PAIR, S5_CHUNK
    _, _, groups, hh, sl = bb.shape
    npair, pw = groups // gp, gp * hh
    k = t * pw
    pairs = LANES // pw
    chunks = seq // t
    rows = chunks * batch
    width = npair * pw
    nblk = width // LANES
    body = functools.partial(_s5_body, batch=batch, chunks=chunks, pairs=pairs, pw=pw, sl=sl)
    per_blk = lambda shape: pl.BlockSpec((None, pairs) + shape, lambda q: (layer, q, 0, 0))
    per_group = lambda group_rows: pl.BlockSpec((None, 4, gp * pairs, group_rows, sl), lambda q: (layer, 0, q, 0, 0))
    x2d = u_cm.reshape(rows, t * width)
    y = pl.pallas_call(
        body,
        grid=(nblk,),
        in_specs=[pl.BlockSpec((rows, LANES), functools.partial(lambda q, s: (0, s * nblk + q), s=s))
                  for s in range(t)]
        + [per_group(hh), per_group(hh), per_group(t), per_group(t), per_blk((4, sl)), per_blk((4, sl)),
           per_blk((1, k))],
        out_specs=pl.BlockSpec((rows * t, LANES), lambda q: (0, q)),
        out_shape=jax.ShapeDtypeStruct((rows * t, width), F32),
        scratch_shapes=[pltpu.VMEM((rows, pairs * k), BF16), pltpu.VMEM((rows, pairs * k), F32)],
        compiler_params=pltpu.CompilerParams(dimension_semantics=("parallel",),
                                             vmem_limit_bytes=VMEM_LIMIT_BYTES),
    )(*([x2d] * t), bb, cc, pin, pout, a, sc, dvec)
    return y.reshape(chunks, batch, t, width)


def _dft_tables(blk):
    n = 2 * blk
    f = np.arange(blk, dtype=np.int64)
    sym = ((2 * f[:, None] + 1) * (2 * f[None, :] + 1)) % (4 * n)
    ang_sym = sym.astype(np.float64) * (2.0 * np.pi / (4 * n))
    lag = np.arange(2 * blk, dtype=np.int64) - blk
    ang_lag = (((2 * f[:, None] + 1) * lag[None, :]) % (2 * n)).astype(np.float64) * (2.0 * np.pi / (2 * n))
    tc, ts = np.cos(ang_lag), np.sin(ang_lag)
    tc[:, 0] = 0.0
    ts[:, 0] = 0.0
    as_bf = lambda x: jnp.asarray(x.astype(np.float32)).astype(BF16)
    return as_bf(np.cos(ang_sym)), as_bf(np.sin(ang_sym)), as_bf(tc), as_bf(ts)


def _position_features(seq, pad_to):
    t = np.linspace(0.0, 1.0, seq)[:, None]
    pos = np.arange(seq, dtype=np.float64)[:, None]
    bands = np.linspace(1e-4, HY_BANDS - 1, HY_BANDS)[None, :]
    ang = bands * pos * (2.0 * math.pi / seq)
    feats = np.zeros((seq, pad_to), np.float64)
    feats[:, :1 + 2 * HY_BANDS] = np.concatenate([t, np.cos(ang), -np.sin(ang)], axis=-1)
    mirror = lambda x: np.roll(x[::-1], 1, axis=0)
    f32 = lambda x: jnp.asarray(x.astype(np.float32))
    return f32(feats), f32(t), f32(mirror(feats)), f32(mirror(t))


_HY_COMBO_TERMS = (
    ((0, 1),), ((-1, 1), (0, -1)), ((1, 1), (0, -1)),
    ((-2, 1), (0, -1)), ((-3, 1), (-1, -1), (-2, -1), (0, 1)), ((-1, 1), (1, -1), (-2, -1), (0, 1)),
    ((2, 1), (0, -1)), ((1, 1), (-1, -1), (2, -1), (0, 1)), ((3, 1), (1, -1), (2, -1), (0, 1)),
)


def _hy_filter_body(feats_ref, t_ref, featsm_ref, tm_ref, w1_ref, b1_ref, fr_ref, w2_ref, b2_ref, w3f_ref,
                    w3b_ref, decf_ref, decb_ref, tc_ref, ts_ref, gr_ref, gi_ref, h2_scr, h2m_scr, k2_scr,
                    kr_scr, ki_scr, *, seq, nb):
    blk = seq // nb

    def split(h):
        top = h.astype(BF16).astype(F32)
        return jnp.concatenate([top, h - top, top, h - top], axis=1).astype(BF16)

    def split_rows(w):
        top = w.astype(BF16).astype(F32)
        return jnp.concatenate([top, top, w - top, w - top], axis=0).astype(BF16)

    @pl.when(pl.program_id(0) == 0)
    def _():
        fr = fr_ref[...]

        def mlp(feats):
            h1 = jnp.sin(fr * (jnp.dot(split(feats), split_rows(w1_ref[...]), preferred_element_type=F32)
                               + b1_ref[...]))
            return jnp.sin(fr * (jnp.dot(split(h1), split_rows(w2_ref[...]), preferred_element_type=F32)
                                 + b2_ref[...]))

        h2_scr[...] = split(mlp(feats_ref[...]))
        h2m_scr[...] = split(mlp(featsm_ref[...]))

    hf = jnp.dot(h2_scr[...], split_rows(w3f_ref[...]), preferred_element_type=F32)
    hf = hf * (jnp.exp(-t_ref[...] * jnp.abs(decf_ref[...])) + HY_SHIFT)
    hb = jnp.dot(h2m_scr[...], split_rows(w3b_ref[...]), preferred_element_type=F32)
    hb = hb * (jnp.exp(-tm_ref[...] * jnp.abs(decb_ref[...])) + HY_SHIFT)
    row = lax.broadcasted_iota(jnp.int32, hb.shape, 0)
    hb = jnp.where(row == 0, 0.0, hb)
    norm = jnp.sum(hf * hf, axis=0, keepdims=True) + jnp.sum(hb * hb, axis=0, keepdims=True)
    scale = lax.rsqrt(norm + HY_EPS) * (1.0 / blk)
    k2_scr[0:seq, :] = hb.astype(BF16)
    k2_scr[seq:2 * seq, :] = hf.astype(BF16)
    for d in range(2 * nb - 1):
        seg = k2_scr[d * blk:(d + 2) * blk, :]
        kr_scr[d] = jnp.dot(tc_ref[...], seg, preferred_element_type=F32) * scale
        ki_scr[d] = jnp.dot(ts_ref[...], seg, preferred_element_type=F32) * scale
    for idx, terms in enumerate(_HY_COMBO_TERMS):
        for src, dst in ((kr_scr, gr_ref), (ki_scr, gi_ref)):
            acc = None
            for lag, weight in terms:
                term = src[lag + nb - 1]
                acc = (term if weight > 0 else -term) if acc is None else (acc + term if weight > 0 else acc - term)
            dst[0, idx] = acc


def _hy_filters(seq, w1, b1, freq, w2, b2, w3, decay, width, tables, nb, *, col_block=256):
    _, _, tc, ts = tables
    blk = seq // nb
    ffn = w2.shape[0]
    feats, t, featsm, tm = _position_features(seq, LANES)
    w1p = jnp.zeros((LANES, ffn), F32).at[:w1.shape[0]].set(w1)
    nblk = width // col_block
    side = lambda rows, direction: pl.BlockSpec(
        (rows, col_block), lambda i: (0, (2 * (i // nblk) + direction) * nblk + i % nblk))
    out = pl.BlockSpec((1, HY_COMBOS, blk, col_block), lambda i: (i // nblk, 0, 0, i % nblk))
    body = functools.partial(_hy_filter_body, seq=seq, nb=nb)
    row2 = lambda v: v.reshape(1, -1)
    return pl.pallas_call(
        body,
        grid=(HY_ORDER * nblk,),
        in_specs=[_resident(feats.shape), _resident(t.shape), _resident(feats.shape), _resident(t.shape),
                  _resident(w1p.shape), _resident((1, ffn)), _resident((1, ffn)), _resident(w2.shape),
                  _resident((1, ffn)), side(ffn, 0), side(ffn, 1), side(1, 0), side(1, 1),
                  _resident(tc.shape), _resident(ts.shape)],
        out_specs=[out, out],
        out_shape=[jax.ShapeDtypeStruct((HY_ORDER, HY_COMBOS, blk, width), F32)] * 2,
        scratch_shapes=[pltpu.VMEM((seq, 4 * ffn), BF16), pltpu.VMEM((seq, 4 * ffn), BF16),
                        pltpu.VMEM((2 * seq, col_block), BF16),
                        pltpu.VMEM((2 * nb - 1, blk, col_block), F32), pltpu.VMEM((2 * nb - 1, blk, col_block), F32)],
        compiler_params=pltpu.CompilerParams(dimension_semantics=("arbitrary",),
                                             vmem_limit_bytes=VMEM_LIMIT_BYTES),
    )(feats, t, featsm, tm, w1p, row2(b1), row2(freq), w2, row2(b2), w3, w3, row2(decay), row2(decay), tc, ts)


def _hy_conv_body(pv_ref, p1_ref, p2_ref, g_ref, cw_ref, cb_ref, gr_ref, gi_ref, d_ref, mc_ref, ms_ref, o_ref,
                  z_scr, a_scr, b_scr, xs_scr, *, seq, nb, cw, row_chunk, problems):
    blk = seq // nb
    gate_refs = (p1_ref, p2_ref)
    lanes = lambda j: slice(j * cw, (j + 1) * cw)
    cadd = lambda x, y: (x[0] + y[0], x[1] + y[1])
    cmul = lambda x, y: (x[0] * y[0] - x[1] * y[1], x[0] * y[1] + x[1] * y[0])

    def short_conv(k, p_ref, part):
        x = p_ref[k].astype(F32)
        pad = xs_scr.at[k, part]
        pad[0:SUBLANES, :] = jnp.zeros((SUBLANES, cw), F32)
        pad[SUBLANES + seq:2 * SUBLANES + seq, :] = jnp.zeros((SUBLANES, cw), F32)
        pad[SUBLANES:SUBLANES + seq, :] = x
        prev, nxt = pad[pl.ds(SUBLANES - 1, seq), :], pad[pl.ds(SUBLANES + 1, seq), :]
        w = cw_ref[:, part, :]
        return cb_ref[part:part + 1, :] + prev * w[0:1] + x * w[1:2] + nxt * w[2:3]

    def load_v(k):
        v = short_conv(k, pv_ref, 0)
        for j in range(nb):
            z_scr[k, :, lanes(j)] = v[j * blk:(j + 1) * blk, :]

    def forward(k, o):
        zb = z_scr[k].astype(BF16)
        a_scr[k] = jnp.dot(mc_ref[...], zb, preferred_element_type=F32)
        b_scr[k] = jnp.dot(ms_ref[...], zb, preferred_element_type=F32)

    def pointwise(k, o):
        def toeplitz2(base, x0, x1, r):
            g = lambda idx: (gr_ref[o, idx, r, :], gi_ref[o, idx, r, :])
            q = cmul(g(base), cadd(x0, x1))
            return cadd(q, cmul(g(base + 1), x1)), cadd(q, cmul(g(base + 2), x0))

        for c in range(blk // row_chunk):
            r = slice(c * row_chunk, (c + 1) * row_chunk)
            z = [(a_scr[k, r, lanes(j)], b_scr[k, r, lanes(j)]) for j in range(nb)]
            p1 = toeplitz2(0, cadd(z[0], z[2]), cadd(z[1], z[3]), r)
            p2 = toeplitz2(3, z[2], z[3], r)
            p3 = toeplitz2(6, z[0], z[1], r)
            for i, y in enumerate((cadd(p1[0], p2[0]), cadd(p1[1], p2[1]), cadd(p1[0], p3[0]), cadd(p1[1], p3[1]))):
                a_scr[k, r, lanes(i)], b_scr[k, r, lanes(i)] = y

    def inverse(k, o):
        conv = (jnp.dot(mc_ref[...], a_scr[k].astype(BF16), preferred_element_type=F32)
                + jnp.dot(ms_ref[...], b_scr[k].astype(BF16), preferred_element_type=F32))
        gate = short_conv(k, gate_refs[o], o + 1)
        dsk = d_ref[o:o + 1, :]
        for j in range(nb):
            rows = slice(j * blk, (j + 1) * blk)
            z = gate[rows, :] * (conv[:, lanes(j)] + dsk * z_scr[k, :, lanes(j)])
            if o == HY_ORDER - 1:
                o_ref[k, rows, :] = (z * g_ref[k, rows, :].astype(F32)).astype(o_ref.dtype)
            else:
                z_scr[k, :, lanes(j)] = z

    stages = [load_v]
    for o in range(HY_ORDER):
        stages += [functools.partial(forward, o=o), functools.partial(pointwise, o=o),
                   functools.partial(inverse, o=o)]
    for s in range(len(stages) + problems - 1):
        for k in range(problems):
            if 0 <= s - k < len(stages):
                stages[s - k](k)


def _hy_conv(p_hy, gate_silu, conv_w, conv_b, gr, gi, d_skip, tables, batch, seq, width, nb, *, col_block=LANES,
             row_chunk=SUBLANES, problems=2):
    assert nb == HY_TIME_BLOCKS
    mc, ms, _, _ = tables
    blk = seq // nb
    nblk = width // col_block
    part = lambda k: pl.BlockSpec((problems, seq, col_block), lambda c, b: (b, 0, k * nblk + c))
    cw = conv_w.reshape(conv_w.shape[0], HY_ORDER + 1, width)
    cb = conv_b.reshape(HY_ORDER + 1, width)
    filt = pl.BlockSpec((HY_ORDER, HY_COMBOS, blk, col_block), lambda c, b: (0, 0, 0, c))
    body = functools.partial(_hy_conv_body, seq=seq, nb=nb, cw=col_block, row_chunk=row_chunk, problems=problems)
    wide = nb * col_block
    scratch = lambda dtype: pltpu.VMEM((problems, blk, wide), dtype)
    return pl.pallas_call(
        body,
        grid=(nblk, batch // problems),
        in_specs=[part(0), part(1), part(2),
                  pl.BlockSpec((problems, seq, col_block), lambda c, b: (b, 0, c)),
                  pl.BlockSpec((cw.shape[0], HY_ORDER + 1, col_block), lambda c, b: (0, 0, c)),
                  pl.BlockSpec((HY_ORDER + 1, col_block), lambda c, b: (0, c)),
                  filt, filt,
                  pl.BlockSpec((HY_ORDER, col_block), lambda c, b: (0, c)),
                  _resident(mc.shape), _resident(ms.shape)],
        out_specs=pl.BlockSpec((problems, seq, col_block), lambda c, b: (b, 0, c)),
        out_shape=jax.ShapeDtypeStruct((batch, seq, width), BF16),
        scratch_shapes=[scratch(F32), scratch(F32), scratch(F32),
                        pltpu.VMEM((problems, HY_ORDER + 1, seq + 2 * SUBLANES, col_block), F32)],
        compiler_params=pltpu.CompilerParams(dimension_semantics=("parallel", "parallel"),
                                             vmem_limit_bytes=VMEM_LIMIT_BYTES),
    )(p_hy, p_hy, p_hy, gate_silu, cw, cb, gr, gi, d_skip, mc, ms)


def _merge_body(ys_ref, gs_ref, yh_ref, m_ref, h_ref, wg_ref, bg_ref, wbs_ref, wbh_ref, wo_ref, fw_ref, o_ref,
                *, d_model, final_norm):
    ys = ys_ref[...]
    y = _gelu_tanh(ys.reshape(ys.shape[0] * ys.shape[1], ys.shape[2]))
    dot = lambda a, w_ref: jnp.dot(a, w_ref[...].astype(BF16), preferred_element_type=F32)
    glu = y * _sigmoid(dot(y.astype(BF16), wg_ref) + bg_ref[...])
    s5 = (glu * gs_ref[...].astype(F32)).astype(BF16)
    y_s5 = dot(s5, wbs_ref)
    y_hy = dot(yh_ref[...], wbh_ref)
    merged = m_ref[:, :d_model].astype(F32) * y_s5 + m_ref[:, d_model:].astype(F32) * y_hy
    h = h_ref[...] + dot(merged.astype(BF16), wo_ref)
    if final_norm:
        ms = jnp.mean(h * h, axis=-1, keepdims=True)
        h = h * lax.rsqrt(ms + RMS_EPS) * fw_ref[...]
    o_ref[...] = h


def _merge(y_s5_cm, gs, y_hypre, m, h2d, w_glu, b_glu, w_bs, w_bh, w_out, layer, final_w, final_norm, seq, *,
           row_tile=1024):
    rows, d = h2d.shape
    sw = y_s5_cm.shape[-1]
    hw = y_hypre.shape[1]
    tile = lambda w: pl.BlockSpec((row_tile, w), lambda i: (i, 0))
    body = functools.partial(_merge_body, d_model=d, final_norm=final_norm)
    return pl.pallas_call(
        body,
        grid=(rows // row_tile,),
        in_specs=[_chunk_major_spec(row_tile, seq, sw), tile(sw), tile(hw), tile(2 * d), tile(d),
                  _resident_layer(w_glu.shape, layer), _resident_layer((b_glu.shape[0], 1, sw), layer),
                  _resident_layer(w_bs.shape, layer), _resident_layer(w_bh.shape, layer),
                  _resident_layer(w_out.shape, layer), _resident((1, d))],
        out_specs=tile(d),
        out_shape=jax.ShapeDtypeStruct((rows, d), F32),
        compiler_params=pltpu.CompilerParams(dimension_semantics=("parallel",),
                                             vmem_limit_bytes=VMEM_LIMIT_BYTES),
    )(y_s5_cm, gs, y_hypre, m, h2d, w_glu, b_glu.reshape(-1, 1, sw), w_bs, w_bh, w_out, final_w.reshape(1, d))


def kernel(x, norm_w, w_in, s5_lam_re, s5_lam_im, s5_log_step, s5_b_re, s5_b_im, s5_c_re, s5_c_im, s5_d, s5_w_glu, s5_b_glu, hy_conv_w, hy_conv_b, hy_w1, hy_b1, hy_freq, hy_w2, hy_b2, hy_w3, hy_decay, hy_d, w_branch_s5, w_branch_hy, w_out, final_norm_w):
    batch, seq, d_model = x.shape
    depth = w_in.shape[0]
    s5_width = s5_d.shape[1]
    hy_width = hy_d.shape[2]
    widths = (s5_width, s5_width, (HY_ORDER + 1) * hy_width, hy_width, 2 * d_model)
    nb = HY_TIME_BLOCKS
    tables = _dft_tables(seq // nb)
    h = x.reshape(batch * seq, d_model)
    s5_ops = jax.vmap(_s5_operands)(s5_lam_re, s5_lam_im, s5_log_step, s5_b_re, s5_b_im, s5_c_re, s5_c_im, s5_d)
    for l in range(depth):
        u, gs, p_hy, gh, m = _inproj(h, norm_w, w_in, l, widths, batch, seq)
        y_s5 = _s5_scan(u, s5_ops, l, batch, seq)
        gr, gi = _hy_filters(seq, hy_w1[l], hy_b1[l], hy_freq[l], hy_w2[l], hy_b2[l], hy_w3[l], hy_decay[l],
                             hy_width, tables, nb)
        y_hypre = _hy_conv(p_hy.reshape(batch, seq, -1), gh.reshape(batch, seq, -1), hy_conv_w[l], hy_conv_b[l],
                           gr, gi, hy_d[l], tables, batch, seq, hy_width, nb)
        h = _merge(y_s5, gs, y_hypre.reshape(batch * seq, hy_width), m, h, s5_w_glu, s5_b_glu, w_branch_s5,
                   w_branch_hy, w_out, l, final_norm_w, l == depth - 1, seq)
    return h.reshape(batch, seq, d_model)
```
